```python
import math
import jax, jax.numpy as jnp
from jax import lax
import numpy as np

D_MODEL = 1024
BATCH = 8
SEQ = 2048
DEPTH = 1
DEC_BATCH = 128
DEC_SEQ = 1
PAST_LEN = 16384
PAGE_SIZE = 128

D_MIX = D_MODEL
R_WIDTH = D_MIX // 2
R_HEAD = 64
R_HEADS = R_WIDTH // R_HEAD
D_DECAY_LORA = 64
D_AAA_LORA = 64
D_GATE_LORA = 128
M_WIDTH = D_MIX - R_WIDTH
M_HEADDIM = 64
M_HEADS = M_WIDTH // M_HEADDIM
M_GROUPS = 2
M_HPG = M_HEADS // M_GROUPS
D_STATE = 128
CONV_K = 4
CONV_DIM = M_WIDTH + 2 * M_GROUPS * D_STATE
SSD_CHUNK = 256
D_FF = 4 * D_MODEL
RWKV_PROJ = 3 * R_WIDTH + D_DECAY_LORA + D_AAA_LORA + D_GATE_LORA
PROJ = RWKV_PROJ + M_WIDTH + CONV_DIM + M_HEADS
ALPHA = (2 * DEPTH) ** 0.25
BETA = (8 * DEPTH) ** -0.25
LN_EPS = 1e-5
LNX_EPS = 64e-5
RMS_EPS = 1e-5

kernel_name = 'rwkv7_mamba2_hybrid_step'


def layer_norm(x, g, b, eps):
    xf = x.astype(jnp.float32)
    mu = jnp.mean(xf, -1, keepdims=True)
    var = jnp.mean(jnp.square(xf - mu), -1, keepdims=True)
    return (xf - mu) * lax.rsqrt(var + eps) * g.astype(jnp.float32) + b.astype(jnp.float32)


def rwkv7_mix(u, wkv0, w0, w_up, a0, a_up, g_up, k_k, k_a, r_k, lnx_w, lnx_b):
    b, L, _ = u.shape
    cuts = [R_WIDTH, 2 * R_WIDTH, 3 * R_WIDTH, 3 * R_WIDTH + D_DECAY_LORA,
            3 * R_WIDTH + D_DECAY_LORA + D_AAA_LORA]
    r, k, v, uw, ua, ug = jnp.split(u, cuts, axis=-1)
    w_log = -jax.nn.softplus(-(w0 + jnp.tanh(uw) @ w_up)) - 0.5
    decay = jnp.exp(-jnp.exp(w_log))
    a = jax.nn.sigmoid(a0 + ua @ a_up)
    g = jax.nn.sigmoid(ug) @ g_up
    heads = lambda t: t.reshape(b, L, R_HEADS, R_HEAD)
    kk = heads(k * k_k)
    kk = kk * lax.rsqrt(jnp.maximum(jnp.sum(jnp.square(kk), -1, keepdims=True), 1e-24))
    k = heads(k * (1.0 + (a - 1.0) * k_a))
    r, v, decay, a = heads(r), heads(v), heads(decay), heads(a)

    def step(S, inp):
        r_t, w_t, k_t, v_t, ka_t, kb_t = inp
        Sa = jnp.einsum('bhvk,bhk->bhv', S, ka_t)
        S = (S * w_t[:, :, None, :] + Sa[..., None] * kb_t[:, :, None, :]
             + v_t[..., None] * k_t[:, :, None, :])
        return S, jnp.einsum('bhvk,bhk->bhv', S, r_t)

    seq = tuple(jnp.swapaxes(t, 0, 1) for t in (r, decay, k, v, -kk, kk * a))
    S_fin, y = lax.scan(step, wkv0, seq)
    y = jnp.swapaxes(y, 0, 1)
    y = layer_norm(y, lnx_w.reshape(R_HEADS, R_HEAD), lnx_b.reshape(R_HEADS, R_HEAD), LNX_EPS)
    y = y + jnp.sum(r * k * r_k, -1, keepdims=True) * v
    return y.reshape(b, L, R_WIDTH) * g, S_fin


def segsum(a):
    q = a.shape[-1]
    cs = jnp.cumsum(a, -1)
    diff = cs[..., :, None] - cs[..., None, :]
    mask = jnp.tril(jnp.ones((q, q), dtype=bool))
    return jnp.where(mask, diff, -jnp.inf)


def ssd_scan(X, Adt, Bm, Cm, h0):
    b, L = X.shape[:2]
    q = min(SSD_CHUNK, L)
    pad = (-L) % q
    if pad:
        padw = lambda t: jnp.pad(t, [(0, 0), (0, pad)] + [(0, 0)] * (t.ndim - 2))
        X, Adt, Bm, Cm = padw(X), padw(Adt), padw(Bm), padw(Cm)
    nc = (L + pad) // q
    X = X.reshape(b, nc, q, M_GROUPS, M_HPG, M_HEADDIM)
    Adt = Adt.reshape(b, nc, q, M_GROUPS, M_HPG).transpose(0, 3, 4, 1, 2)
    Bm = Bm.reshape(b, nc, q, M_GROUPS, D_STATE)
    Cm = Cm.reshape(b, nc, q, M_GROUPS, D_STATE)
    A_cs = jnp.cumsum(Adt, -1)
    Lmat = jnp.exp(segsum(Adt))
    CB = jnp.einsum('bclgn,bcsgn->bcgls', Cm, Bm)
    y_diag = jnp.einsum('bcgls,bgecls,bcsgep->bclgep', CB, Lmat, X)
    decay_states = jnp.exp(A_cs[..., -1:] - A_cs)
    states = jnp.einsum('bclgn,bgecl,bclgep->bcgepn', Bm, decay_states, X)
    states = jnp.concatenate([h0[:, None], states], axis=1)
    chunk_decay = jnp.exp(segsum(jnp.pad(A_cs[..., -1], ((0, 0), (0, 0), (0, 0), (1, 0)))))
    states = jnp.einsum('bgezc,bcgepn->bzgepn', chunk_decay, states)
    h_in, h_fin = states[:, :-1], states[:, -1]
    y_off = jnp.einsum('bclgn,bcgepn,bgecl->bclgep', Cm, h_in, jnp.exp(A_cs))
    y = (y_diag + y_off).reshape(b, nc * q, M_GROUPS, M_HPG, M_HEADDIM)[:, :L]
    return y, h_fin


def mamba2_mix(z, xbc, dt, conv_prev, ssm0, conv_w, conv_b, dt_bias, a_log, d_skip, gnorm_w):
    b, L, _ = xbc.shape
    xpad = jnp.concatenate([conv_prev, xbc], axis=1)
    conv = conv_b + sum(xpad[:, i:i + L] * conv_w[i] for i in range(CONV_K))
    new_conv = xpad[:, L:]
    xbc = jax.nn.silu(conv)
    xs, Bm, Cm = jnp.split(xbc, [M_WIDTH, M_WIDTH + M_GROUPS * D_STATE], axis=-1)
    xs = xs.reshape(b, L, M_GROUPS, M_HPG, M_HEADDIM)
    Bm = Bm.reshape(b, L, M_GROUPS, D_STATE)
    Cm = Cm.reshape(b, L, M_GROUPS, D_STATE)
    dt = jax.nn.softplus(dt + dt_bias).reshape(b, L, M_GROUPS, M_HPG)
    A = -jnp.exp(a_log).reshape(M_GROUPS, M_HPG)
    y, h_fin = ssd_scan(xs * dt[..., None], dt * A, Bm, Cm,
                        ssm0.reshape(b, M_GROUPS, M_HPG, M_HEADDIM, D_STATE))
    y = y + d_skip.reshape(M_GROUPS, M_HPG)[:, :, None] * xs
    gw = M_WIDTH // M_GROUPS
    y = y.reshape(b, L, M_GROUPS, gw) * jax.nn.silu(z).reshape(b, L, M_GROUPS, gw)
    y = y * lax.rsqrt(jnp.mean(jnp.square(y), -1, keepdims=True) + RMS_EPS)
    return (y.reshape(b, L, M_WIDTH) * gnorm_w, new_conv,
            h_fin.reshape(b, M_HEADS, M_HEADDIM, D_STATE))


def decoder_layer(x, shift_prev, wkv0, conv_prev, ssm0, w_in, mu_shift, w0, w_up, a0, a_up,
                  g_up, k_k, k_a, r_k, lnx_w, lnx_b, conv_w, conv_b, dt_bias, a_log, d_skip,
                  gnorm_w, w_out, ln1_g, ln1_b, w_ff1, w_ff2, ln2_g, ln2_b):
    L = x.shape[1]
    proj = (x @ w_in).astype(jnp.float32)
    pr = proj[..., :RWKV_PROJ]
    prev = jnp.concatenate([shift_prev[:, None].astype(jnp.float32), pr[:, :L - 1]], axis=1)
    u = pr + (prev - pr) * mu_shift
    o_r, wkv_fin = rwkv7_mix(u, wkv0.astype(jnp.float32), w0, w_up, a0, a_up, g_up,
                             k_k, k_a, r_k, lnx_w, lnx_b)
    m0 = RWKV_PROJ
    z = proj[..., m0:m0 + M_WIDTH]
    xbc = proj[..., m0 + M_WIDTH:m0 + M_WIDTH + CONV_DIM]
    dt = proj[..., m0 + M_WIDTH + CONV_DIM:]
    o_m, conv_new, ssm_fin = mamba2_mix(z, xbc, dt, conv_prev.astype(jnp.float32),
                                        ssm0.astype(jnp.float32), conv_w, conv_b,
                                        dt_bias, a_log, d_skip, gnorm_w)
    mix = jnp.concatenate([o_r, o_m], axis=-1).astype(x.dtype) @ w_out
    h = layer_norm(ALPHA * x + mix, ln1_g, ln1_b, LN_EPS).astype(x.dtype)
    ff = jnp.square(jax.nn.relu(h @ w_ff1)) @ w_ff2
    y = layer_norm(ALPHA * h + ff, ln2_g, ln2_b, LN_EPS).astype(x.dtype)
    return y, pr[:, -1], wkv_fin, conv_new, ssm_fin


def setup_inputs(seed: int = 0) -> dict:
    key = jax.random.key(seed)
    ks = jax.random.split(key, 32)
    nrm = lambda k, shape, s: jax.random.normal(k, shape, jnp.float32) * s
    dt = jnp.exp(jax.random.uniform(ks[20], (DEPTH, M_HEADS), jnp.float32,
                                    minval=math.log(1e-3), maxval=math.log(1e-1)))
    return {
        'x_prompt': nrm(ks[0], (BATCH, SEQ, D_MODEL), 1.0),
        'x_sample': nrm(ks[1], (DEC_BATCH, DEC_SEQ, D_MODEL), 1.0),
        'state_shift': nrm(ks[2], (DEPTH, DEC_BATCH, RWKV_PROJ), 1.0),
        'state_wkv': nrm(ks[3], (DEPTH, DEC_BATCH, R_HEADS, R_HEAD, R_HEAD), 0.3),
        'state_conv': nrm(ks[4], (DEPTH, DEC_BATCH, CONV_K - 1, CONV_DIM), 1.0),
        'state_ssm': nrm(ks[5], (DEPTH, DEC_BATCH, M_HEADS, M_HEADDIM, D_STATE), 0.3),
        'w_in': nrm(ks[6], (DEPTH, D_MODEL, PROJ), D_MODEL ** -0.5),
        'mu_shift': jax.random.uniform(ks[7], (DEPTH, RWKV_PROJ), jnp.float32),
        'w0': nrm(ks[8], (DEPTH, R_WIDTH), 0.5) - 1.0,
        'w_up': nrm(ks[9], (DEPTH, D_DECAY_LORA, R_WIDTH), 0.1),
        'a0': nrm(ks[10], (DEPTH, R_WIDTH), 0.1),
        'a_up': nrm(ks[11], (DEPTH, D_AAA_LORA, R_WIDTH), 0.1),
        'g_up': nrm(ks[12], (DEPTH, D_GATE_LORA, R_WIDTH), D_GATE_LORA ** -0.5),
        'k_k': 0.85 + nrm(ks[13], (DEPTH, R_WIDTH), 0.05),
        'k_a': 1.0 + nrm(ks[14], (DEPTH, R_WIDTH), 0.05),
        'r_k': nrm(ks[15], (DEPTH, R_HEADS, R_HEAD), 0.1),
        'lnx_w': 1.0 + nrm(ks[16], (DEPTH, R_WIDTH), 0.02),
        'lnx_b': nrm(ks[17], (DEPTH, R_WIDTH), 0.02),
        'conv_w': nrm(ks[18], (DEPTH, CONV_K, CONV_DIM), CONV_K ** -0.5),
        'conv_b': nrm(ks[19], (DEPTH, CONV_DIM), 0.02),
        'dt_bias': dt + jnp.log(-jnp.expm1(-dt)),
        'a_log': jnp.log(jax.random.uniform(ks[21], (DEPTH, M_HEADS), jnp.float32, minval=1.0, maxval=16.0)),
        'd_skip': 1.0 + nrm(ks[22], (DEPTH, M_HEADS), 0.1),
        'gnorm_w': 1.0 + nrm(ks[23], (DEPTH, M_WIDTH), 0.02),
        'w_out': nrm(ks[24], (DEPTH, D_MIX, D_MODEL), BETA * D_MIX ** -0.5),
        'ln1_g': 1.0 + nrm(ks[25], (DEPTH, D_MODEL), 0.02),
        'ln1_b': nrm(ks[26], (DEPTH, D_MODEL), 0.02),
        'w_ff1': nrm(ks[27], (DEPTH, D_MODEL, D_FF), D_MODEL ** -0.5),
        'w_ff2': nrm(ks[28], (DEPTH, D_FF, D_MODEL), BETA * D_FF ** -0.5),
        'ln2_g': 1.0 + nrm(ks[29], (DEPTH, D_MODEL), 0.02),
        'ln2_b': nrm(ks[30], (DEPTH, D_MODEL), 0.02),
    }


def reference(x_prompt, x_sample, state_shift, state_wkv, state_conv, state_ssm, w_in, mu_shift,
              w0, w_up, a0, a_up, g_up, k_k, k_a, r_k, lnx_w, lnx_b, conv_w, conv_b, dt_bias,
              a_log, d_skip, gnorm_w, w_out, ln1_g, ln1_b, w_ff1, w_ff2, ln2_g, ln2_b):
    h_p, h_s = x_prompt, x_sample
    nb = x_prompt.shape[0]
    new_p, new_s = [], []
    for l in range(DEPTH):
        lp = (w_in[l], mu_shift[l], w0[l], w_up[l], a0[l], a_up[l], g_up[l], k_k[l], k_a[l],
              r_k[l], lnx_w[l], lnx_b[l], conv_w[l], conv_b[l], dt_bias[l], a_log[l], d_skip[l],
              gnorm_w[l], w_out[l], ln1_g[l], ln1_b[l], w_ff1[l], w_ff2[l], ln2_g[l], ln2_b[l])
        h_p, *sp = decoder_layer(
            h_p,
            jnp.zeros((nb, RWKV_PROJ), jnp.float32),
            jnp.zeros((nb, R_HEADS, R_HEAD, R_HEAD), jnp.float32),
            jnp.zeros((nb, CONV_K - 1, CONV_DIM), jnp.float32),
            jnp.zeros((nb, M_HEADS, M_HEADDIM, D_STATE), jnp.float32),
            *lp)
        h_s, *ss = decoder_layer(h_s, state_shift[l], state_wkv[l], state_conv[l], state_ssm[l], *lp)
        new_p.append(sp)
        new_s.append(ss)
    st = lambda outs, i, ref: jnp.stack([o[i] for o in outs]).astype(ref.dtype)
    return (h_p, h_s,
            st(new_p, 0, state_shift), st(new_p, 1, state_wkv), st(new_p, 2, state_conv), st(new_p, 3, state_ssm),
            st(new_s, 0, state_shift), st(new_s, 1, state_wkv), st(new_s, 2, state_conv), st(new_s, 3, state_ssm))
```

```python
import functools
import math

import numpy as np
import jax
import jax.numpy as jnp
from jax import lax
from jax.experimental import pallas as pl
from jax.experimental.pallas import tpu as pltpu

F32 = jnp.float32
BF16 = jnp.bfloat16

D_MODEL = 1024
R_WIDTH = 512
R_HEAD = 64
R_HEADS = 8
D_DECAY_LORA = 64
D_AAA_LORA = 64
D_GATE_LORA = 128
M_WIDTH = 512
M_HEADDIM = 64
M_HEADS = 8
M_GROUPS = 2
D_STATE = 128
CONV_K = 4
CONV_DIM = M_WIDTH + 2 * M_GROUPS * D_STATE
D_FF = 4 * D_MODEL
RWKV_PROJ = 3 * R_WIDTH + D_DECAY_LORA + D_AAA_LORA + D_GATE_LORA
DT_PAD = 128
PROJ_PAD = RWKV_PROJ + M_WIDTH + CONV_DIM + DT_PAD
DEPTH = 1
ALPHA = (2 * DEPTH) ** 0.25
LN_EPS = 1e-5
LNX_EPS = 64e-5
RMS_EPS = 1e-5

RWKV_CHUNK = 64
SSD_CHUNK = 256
SUBLANES = 8
VMEM_LIMIT = 48 * 1024 * 1024

_NT = (((1,), (1,)), ((), ()))


def _mm(a, b):
    return jnp.dot(a.astype(BF16), b.astype(BF16), preferred_element_type=F32)


def _mm_nt(a, b):
    return lax.dot_general(a.astype(BF16), b.astype(BF16), _NT, preferred_element_type=F32)


def _split(x, passes):
    parts = []
    for _ in range(passes):
        p = x.astype(BF16)
        parts.append(p)
        x = x - p.astype(F32)
    return parts


def _mm_sel_r(x, sel, passes):
    return sum(jnp.dot(p, sel, preferred_element_type=F32) for p in _split(x, passes))


def _mm_sel_l(sel, x, passes):
    return sum(jnp.dot(sel, p, preferred_element_type=F32) for p in _split(x, passes))


def _softplus(x):
    return jnp.maximum(x, 0.0) + jnp.log(1.0 + jnp.exp(-jnp.abs(x)))


def _sigmoid(x):
    return 1.0 / (1.0 + jnp.exp(-x))


def _silu(x):
    return x * _sigmoid(x)


def _layer_norm(x, g, b, eps):
    mu = jnp.mean(x, axis=-1, keepdims=True)
    d = x - mu
    var = jnp.mean(d * d, axis=-1, keepdims=True)
    return d * lax.rsqrt(var + eps) * g + b


def _iota2(shape, axis):
    return lax.broadcasted_iota(jnp.int32, shape, axis)


def _proj_kernel(x_ref, w_ref, pr_ref, z_ref, xbc_ref, dt_ref):
    res = jnp.dot(x_ref[...].astype(BF16), w_ref[...], preferred_element_type=F32)
    c0 = RWKV_PROJ
    c1 = c0 + M_WIDTH
    c2 = c1 + CONV_DIM
    pr_ref[...] = res[:, :c0]
    z_ref[...] = res[:, c0:c1]
    xbc_ref[...] = res[:, c1:c2]
    dt_ref[...] = res[:, c2:]


def _proj(x2d, w_bf, tm):
    n = x2d.shape[0]
    widths = (RWKV_PROJ, M_WIDTH, CONV_DIM, DT_PAD)
    return pl.pallas_call(
        _proj_kernel,
        grid=(n // tm,),
        in_specs=[pl.BlockSpec((tm, D_MODEL), lambda i: (i, 0)),
                  pl.BlockSpec((D_MODEL, PROJ_PAD), lambda i: (0, 0))],
        out_specs=[pl.BlockSpec((tm, w), lambda i: (i, 0)) for w in widths],
        out_shape=[jax.ShapeDtypeStruct((n, w), F32) for w in widths],
        compiler_params=pltpu.CompilerParams(dimension_semantics=("arbitrary",),
                                             vmem_limit_bytes=VMEM_LIMIT),
        name="proj",
    )(x2d, w_bf)


def _rwkv_prep(pr, prev, mu, w0, wup, a0, aup, gup, k_k, k_a, bd):
    u = pr + (prev - pr) * mu
    r = u[:, 0:R_WIDTH]
    k = u[:, R_WIDTH:2 * R_WIDTH]
    v = u[:, 2 * R_WIDTH:3 * R_WIDTH]
    lora = u[:, 3 * R_WIDTH:3 * R_WIDTH + D_DECAY_LORA + D_AAA_LORA]
    ug = u[:, 3 * R_WIDTH + D_DECAY_LORA + D_AAA_LORA:]
    w_log = -_softplus(-(w0 + _mm(jnp.tanh(lora), wup))) - 0.5
    logw = -jnp.exp(w_log)
    a = _sigmoid(a0 + _mm(lora, aup))
    g = _mm(_sigmoid(ug), gup)
    kk = k * k_k
    ss = _mm_sel_r(kk * kk, bd, 2)
    kkn = kk * lax.rsqrt(jnp.maximum(ss, 1e-24))
    k2 = k * (1.0 + (a - 1.0) * k_a)
    return r, logw, k2, v, -kkn, kkn * a, g


def _rwkv_post(y, r, k2, v, g, r_k, lnx_w, lnx_b, bd):
    inv = 1.0 / R_HEAD
    mu = _mm_sel_r(y, bd, 2) * inv
    d = y - mu
    var = _mm_sel_r(d * d, bd, 2) * inv
    yn = d * lax.rsqrt(var + LNX_EPS) * lnx_w + lnx_b
    bonus = _mm_sel_r(r * k2 * r_k, bd, 2)
    return (yn + bonus * v) * g


def _unit_lower_inverse(n_strict):
    c = RWKV_CHUNK
    ii = _iota2((c, c), 0)
    jj = _iota2((c, c), 1)
    sh = 4
    nd = jnp.where((ii >> sh) == (jj >> sh), n_strict, 0.0)
    t = jnp.where(ii == jj, 1.0, 0.0) + nd
    npow = nd
    for _ in range(sh - 1):
        npow = _mm(npow, npow)
        t = t + _mm(t, npow)
    while (1 << sh) < c:
        off = ((ii >> (sh + 1)) == (jj >> (sh + 1))) & (((ii >> sh) & 1) > ((jj >> sh) & 1))
        t = t + _mm(_mm(t, jnp.where(off, n_strict, 0.0)), t)
        sh += 1
    return t


def _rwkv_chunk_kernel(pr_ref, prevblk_ref, mu_ref, w0_ref, wup_ref, a0_ref, aup_ref, gup_ref,
                       kk_ref, ka_ref, rk_ref, lnw_ref, lnb_ref, bd_ref,
                       o_ref, sfin_ref, g_scr):
    c = RWKV_CHUNK
    ci = pl.program_id(1)

    @pl.when(ci == 0)
    def _():
        g_scr[...] = jnp.zeros_like(g_scr)

    pr = pr_ref[...]
    prev_row = jnp.where(ci == 0, 0.0, prevblk_ref[SUBLANES - 1:SUBLANES, :])
    prev = jnp.where(_iota2((c, 1), 0) == 0, prev_row, pltpu.roll(pr, 1, axis=0))
    bd = bd_ref[...]
    r, logw, k2, v, av, bv, g = _rwkv_prep(pr, prev, mu_ref[...], w0_ref[...], wup_ref[...],
                                           a0_ref[...], aup_ref[...], gup_ref[...],
                                           kk_ref[...], ka_ref[...], bd)

    ii = _iota2((c, c), 0)
    jj = _iota2((c, c), 1)
    strict = ii > jj
    incl = ii >= jj
    tril = jnp.where(incl, 1.0, 0.0).astype(BF16)
    cum = _mm_sel_l(tril, logw, 3)
    cl = cum[c - 1:c, :]
    p_in = jnp.exp(cum)
    p_out = jnp.exp(-cum)
    p_end = jnp.exp(cl - cum)
    rt = r * p_in
    at = av * jnp.exp(cum - logw)
    bt = bv * p_out
    kt = k2 * p_out
    bh = bv * p_end
    kh = k2 * p_end
    pc = jnp.exp(cl)

    lane = _iota2((1, 2 * R_HEAD), 1)
    e128r = _iota2((2 * R_HEAD, 2 * R_HEAD), 0)
    e128c = _iota2((2 * R_HEAD, 2 * R_HEAD), 1)
    eye128 = e128r == e128c
    blockdiag = (e128r >= R_HEAD) == (e128c >= R_HEAD)

    y_pairs = []
    for p in range(R_HEADS // 2):
        sl = slice(2 * R_HEAD * p, 2 * R_HEAD * (p + 1))
        at_p, rt_p, bt_p, kt_p, v_p = at[:, sl], rt[:, sl], bt[:, sl], kt[:, sl], v[:, sl]
        at2_pair = jnp.zeros((c, 2 * R_HEAD), F32)
        u0_pair = jnp.zeros((c, 2 * R_HEAD), F32)
        rp_pair = jnp.zeros((c, 2 * R_HEAD), F32)
        y0_pair = jnp.zeros((c, 2 * R_HEAD), F32)
        for e in range(2):
            m = (lane < R_HEAD) if e == 0 else (lane >= R_HEAD)
            at_m = jnp.where(m, at_p, 0.0)
            rt_m = jnp.where(m, rt_p, 0.0)
            lhs = jnp.concatenate([at_m, rt_m], axis=0)
            sc_b = _mm_nt(lhs, bt_p)
            sc_k = _mm_nt(lhs, kt_p)
            a_ab = jnp.where(strict, sc_b[:c], 0.0)
            a_rb = jnp.where(incl, sc_b[c:], 0.0)
            a_ak = jnp.where(strict, sc_k[:c], 0.0)
            a_rk = jnp.where(incl, sc_k[c:], 0.0)
            t = _unit_lower_inverse(a_ab)
            at2 = _mm(t, at_m)
            u0 = _mm(t, _mm(a_ak, v_p))
            rp = rt_m + _mm(a_rb, at2)
            y0 = _mm(a_rb, u0) + _mm(a_rk, v_p)
            at2_pair = at2_pair + at2
            rp_pair = rp_pair + rp
            u0_pair = jnp.where(m, u0, u0_pair)
            y0_pair = jnp.where(m, y0, y0_pair)
        wt = jnp.concatenate([bh[:, sl], kh[:, sl]], axis=0).T
        m_ba = jnp.where(blockdiag, _mm(wt[:, :c], at2_pair), 0.0)
        h0c = jnp.where(blockdiag, _mm(wt, jnp.concatenate([u0_pair, v_p], axis=0)), 0.0)
        pc_col = jnp.sum(jnp.where(eye128, pc[:, sl], 0.0), axis=1, keepdims=True)
        g0 = g_scr[p]
        y_pairs.append(_mm(rp_pair, g0) + y0_pair)
        g_scr[p] = pc_col * g0 + _mm(m_ba, g0) + h0c

    y = jnp.concatenate(y_pairs, axis=1)
    o_ref[...] = _rwkv_post(y, r, k2, v, g, rk_ref[...], lnw_ref[...], lnb_ref[...], bd)

    @pl.when(ci == pl.num_programs(1) - 1)
    def _():
        for p in range(R_HEADS // 2):
            gt = g_scr[p].T
            sfin_ref[0, 2 * p] = gt[:R_HEAD, :R_HEAD]
            sfin_ref[0, 2 * p + 1] = pltpu.roll(gt, R_HEAD, axis=1)[R_HEAD:, :R_HEAD]


def _const_spec(shape):
    return pl.BlockSpec(shape, lambda *_: (0,) * len(shape))


def _rwkv_prompt(pr2d, batch, seq, wts):
    c = RWKV_CHUNK
    nchunk = seq // c
    rows8 = seq // SUBLANES

    def prev_map(b, i):
        return (b * rows8 + jnp.maximum(i * (c // SUBLANES) - 1, 0), 0)

    w_specs = [_const_spec(w.shape) for w in wts]
    return pl.pallas_call(
        _rwkv_chunk_kernel,
        grid=(batch, nchunk),
        in_specs=[pl.BlockSpec((c, RWKV_PROJ), lambda b, i: (b * nchunk + i, 0)),
                  pl.BlockSpec((SUBLANES, RWKV_PROJ), prev_map)] + w_specs,
        out_specs=[pl.BlockSpec((c, R_WIDTH), lambda b, i: (b * nchunk + i, 0)),
                   pl.BlockSpec((1, R_HEADS, R_HEAD, R_HEAD), lambda b, i: (b, 0, 0, 0))],
        out_shape=[jax.ShapeDtypeStruct((batch * seq, R_WIDTH), F32),
                   jax.ShapeDtypeStruct((batch, R_HEADS, R_HEAD, R_HEAD), F32)],
        scratch_shapes=[pltpu.VMEM((R_HEADS // 2, 2 * R_HEAD, 2 * R_HEAD), F32)],
        compiler_params=pltpu.CompilerParams(dimension_semantics=("arbitrary", "arbitrary"),
                                             vmem_limit_bytes=VMEM_LIMIT),
        name="rwkv_prompt",
    )(pr2d, pr2d, *wts)


def _rwkv_prep_kernel(pr_ref, prev_ref, mu_ref, w0_ref, wup_ref, a0_ref, aup_ref, gup_ref,
                      kk_ref, ka_ref, bd_ref,
                      r_ref, w_ref, k_ref, v_ref, a_ref, b_ref, g_ref):
    r, logw, k2, v, av, bv, g = _rwkv_prep(pr_ref[...], prev_ref[...], mu_ref[...], w0_ref[...],
                                           wup_ref[...], a0_ref[...], aup_ref[...], gup_ref[...],
                                           kk_ref[...], ka_ref[...], bd_ref[...])
    r_ref[...] = r
    w_ref[...] = jnp.exp(logw)
    k_ref[...] = k2
    v_ref[...] = v
    a_ref[...] = av
    b_ref[...] = bv
    g_ref[...] = g


def _rwkv_sample_prep(pr, prev, wts):
    n = pr.shape[0]
    args = (pr, prev) + tuple(wts)
    return pl.pallas_call(
        _rwkv_prep_kernel,
        grid=(1,),
        in_specs=[_const_spec(a.shape) for a in args],
        out_specs=[_const_spec((n, R_WIDTH))] * 7,
        out_shape=[jax.ShapeDtypeStruct((n, R_WIDTH), F32)] * 7,
        compiler_params=pltpu.CompilerParams(vmem_limit_bytes=VMEM_LIMIT),
        name="rwkv_sample_prep",
    )(*args)


STEP_BATCH = 8


def _col_from_row(row, eye):
    return jnp.sum(jnp.where(eye, row, 0.0), axis=1, keepdims=True)


def _row_from_col(col, eye):
    return jnp.sum(jnp.where(eye, col, 0.0), axis=0, keepdims=True)


def _rwkv_step_kernel(r_ref, w_ref, k_ref, v_ref, a_ref, b_ref, g_ref, s_ref,
                      rk_ref, lnw_ref, lnb_ref, o_ref, so_ref):
    eye = _iota2((R_HEAD, R_HEAD), 0) == _iota2((R_HEAD, R_HEAD), 1)
    inv = 1.0 / R_HEAD
    for bi in range(STEP_BATCH):
        rows = []
        for h in range(R_HEADS):
            i = bi * R_HEADS + h
            row = lambda ref: ref[i:i + 1, :]
            r, w, k, v, a, b, g = (row(x) for x in (r_ref, w_ref, k_ref, v_ref, a_ref, b_ref, g_ref))
            s = s_ref[i]
            sa = jnp.sum(s * a, axis=1, keepdims=True)
            s2 = s * w + sa * b + _col_from_row(v, eye) * k
            so_ref[i] = s2
            y = _row_from_col(jnp.sum(s2 * r, axis=1, keepdims=True), eye)
            mu = jnp.sum(y, axis=1, keepdims=True) * inv
            d = y - mu
            var = jnp.sum(d * d, axis=1, keepdims=True) * inv
            yn = d * lax.rsqrt(var + LNX_EPS) * lnw_ref[h:h + 1, :] + lnb_ref[h:h + 1, :]
            bonus = jnp.sum(r * k * rk_ref[h:h + 1, :], axis=1, keepdims=True)
            rows.append((yn + bonus * v) * g)
        o_ref[bi * R_HEADS:(bi + 1) * R_HEADS, :] = jnp.concatenate(rows, axis=0)


def _rwkv_sample_step(vecs, state, r_k, lnx_w, lnx_b):
    nb = state.shape[0]
    rows = STEP_BATCH * R_HEADS
    vec_spec = pl.BlockSpec((rows, R_HEAD), lambda i: (i, 0))
    st_spec = pl.BlockSpec((rows, R_HEAD, R_HEAD), lambda i: (i, 0, 0))
    w_spec = _const_spec((R_HEADS, R_HEAD))
    return pl.pallas_call(
        _rwkv_step_kernel,
        grid=(nb // STEP_BATCH,),
        in_specs=[vec_spec] * 7 + [st_spec] + [w_spec] * 3,
        out_specs=[vec_spec, st_spec],
        out_shape=[jax.ShapeDtypeStruct((nb * R_HEADS, R_HEAD), F32),
                   jax.ShapeDtypeStruct((nb * R_HEADS, R_HEAD, R_HEAD), F32)],
        compiler_params=pltpu.CompilerParams(dimension_semantics=("arbitrary",),
                                             vmem_limit_bytes=VMEM_LIMIT),
        name="rwkv_sample_step",
    )(*[x.reshape(nb * R_HEADS, R_HEAD) for x in vecs],
      state.reshape(nb * R_HEADS, R_HEAD, R_HEAD), r_k, lnx_w, lnx_b)


def _shift_rows(x, prev8, s):
    rolled = pltpu.roll(x, s, axis=0)
    top = jnp.where(_iota2((SUBLANES, 1), 0) < s, pltpu.roll(prev8, s, axis=0), rolled[:SUBLANES])
    return jnp.concatenate([top, rolled[SUBLANES:]], axis=0)


def _head_expand():
    head_of_lane = _iota2((DT_PAD, M_WIDTH), 1) >> int(math.log2(M_HEADDIM))
    return jnp.where(_iota2((DT_PAD, M_WIDTH), 0) == head_of_lane, 1.0, 0.0).astype(BF16)


def _gated_rmsnorm(y, z, gnorm_w):
    yz = y * _silu(z)
    gw = M_WIDTH // M_GROUPS
    outs = []
    for gi in range(M_GROUPS):
        yg = yz[:, gi * gw:(gi + 1) * gw]
        ms = jnp.mean(yg * yg, axis=-1, keepdims=True)
        outs.append(yg * lax.rsqrt(ms + RMS_EPS))
    return jnp.concatenate(outs, axis=1) * gnorm_w


def _mamba_chunk_kernel(z_ref, xbc_ref, xprev_ref, dt_ref, convw_ref, convb_ref, dtb_ref, alog_ref,
                        dskip_ref, gnw_ref, o_ref, hfin_ref, h_scr):
    q = SSD_CHUNK
    ci = pl.program_id(1)

    @pl.when(ci == 0)
    def _():
        h_scr[...] = jnp.zeros_like(h_scr)

    xb = xbc_ref[...]
    prev8 = jnp.where(ci == 0, 0.0, xprev_ref[...])
    conv = convb_ref[...] + xb * convw_ref[CONV_K - 1:CONV_K, :]
    for s in range(1, CONV_K):
        conv = conv + _shift_rows(xb, prev8, s) * convw_ref[CONV_K - 1 - s:CONV_K - s, :]
    xact = _silu(conv)
    xs = xact[:, :M_WIDTH]
    gs = M_GROUPS * D_STATE
    bm = xact[:, M_WIDTH:M_WIDTH + gs]
    cm = xact[:, M_WIDTH + gs:]

    dt = _softplus(dt_ref[...] + dtb_ref[...])
    adt = dt * (-jnp.exp(alog_ref[...]))
    x_dt = xs * _mm_sel_r(dt, _head_expand(), 3)

    ii = _iota2((q, q), 0)
    jj = _iota2((q, q), 1)
    incl = ii >= jj
    cs = _mm_sel_l(jnp.where(incl, 1.0, 0.0).astype(BF16), adt, 3)
    cs_t = cs.T
    lane = _iota2((1, 2 * M_HEADDIM), 1)
    first = lane < M_HEADDIM
    hpg = M_HEADS // M_GROUPS

    y_pairs = []
    for gi in range(M_GROUPS):
        bm_g = bm[:, gi * D_STATE:(gi + 1) * D_STATE]
        cm_g = cm[:, gi * D_STATE:(gi + 1) * D_STATE]
        cb = _mm_nt(cm_g, bm_g)
        bm_t = bm_g.T
        for pp in range(hpg // 2):
            p = gi * (hpg // 2) + pp
            sl = slice(2 * M_HEADDIM * p, 2 * M_HEADDIM * (p + 1))
            x_p = x_dt[:, sl]
            y_diag = jnp.zeros((q, 2 * M_HEADDIM), F32)
            contrib = jnp.zeros((D_STATE, 2 * M_HEADDIM), F32)
            scale_in = []
            scale_end = []
            for e in range(2):
                head = 2 * p + e
                m = first if e == 0 else jnp.logical_not(first)
                a_col = cs[:, head:head + 1]
                a_row = cs_t[head:head + 1, :]
                a_end = cs_t[head:head + 1, q - 1:q]
                lmat = jnp.exp(jnp.where(incl, a_col - a_row, -jnp.inf))
                x_m = jnp.where(m, x_p, 0.0)
                y_diag = y_diag + _mm(cb * lmat, x_m)
                contrib = contrib + _mm(bm_t * jnp.exp(a_end - a_row), x_m)
                scale_in.append(jnp.exp(a_col))
                scale_end.append(jnp.exp(a_end))
            h_in = h_scr[p]
            y_off = _mm(cm_g, h_in) * jnp.where(first, scale_in[0], scale_in[1])
            h_scr[p] = h_in * jnp.where(first, scale_end[0], scale_end[1]) + contrib
            y_pairs.append(y_diag + y_off)

    y = jnp.concatenate(y_pairs, axis=1) + dskip_ref[...] * xs
    o_ref[...] = _gated_rmsnorm(y, z_ref[...], gnw_ref[...])

    @pl.when(ci == pl.num_programs(1) - 1)
    def _():
        for p in range(M_HEADS // 2):
            hfin_ref[0, 2 * M_HEADDIM * p:2 * M_HEADDIM * (p + 1), :] = h_scr[p].T


def _mamba_prompt(z2d, xbc2d, dt2d, batch, seq, wts):
    q = SSD_CHUNK
    nchunk = seq // q
    rows8 = seq // SUBLANES

    def prev_map(b, i):
        return (b * rows8 + jnp.maximum(i * (q // SUBLANES) - 1, 0), 0)

    tok = lambda w: pl.BlockSpec((q, w), lambda b, i: (b * nchunk + i, 0))
    return pl.pallas_call(
        _mamba_chunk_kernel,
        grid=(batch, nchunk),
        in_specs=[tok(M_WIDTH), tok(CONV_DIM), pl.BlockSpec((SUBLANES, CONV_DIM), prev_map), tok(DT_PAD)]
                 + [_const_spec(w.shape) for w in wts],
        out_specs=[tok(M_WIDTH),
                   pl.BlockSpec((1, M_HEADS * M_HEADDIM, D_STATE), lambda b, i: (b, 0, 0))],
        out_shape=[jax.ShapeDtypeStruct((batch * seq, M_WIDTH), F32),
                   jax.ShapeDtypeStruct((batch, M_HEADS * M_HEADDIM, D_STATE), F32)],
        scratch_shapes=[pltpu.VMEM((M_HEADS // 2, D_STATE, 2 * M_HEADDIM), F32)],
        compiler_params=pltpu.CompilerParams(dimension_semantics=("arbitrary", "arbitrary"),
                                             vmem_limit_bytes=VMEM_LIMIT),
        name="mamba_prompt",
    )(z2d, xbc2d, xbc2d, dt2d, *wts)


def _mamba_prep_kernel(xbc_ref, cprev_ref, dt_ref, convw_ref, convb_ref, dtb_ref, alog_ref,
                       xs_ref, xdt_ref, dec_ref, bm_ref, cm_ref):
    conv = convb_ref[...] + xbc_ref[...] * convw_ref[CONV_K - 1:CONV_K, :]
    for j in range(CONV_K - 1):
        conv = conv + cprev_ref[:, j * CONV_DIM:(j + 1) * CONV_DIM] * convw_ref[j:j + 1, :]
    xact = _silu(conv)
    xs = xact[:, :M_WIDTH]
    gs = M_GROUPS * D_STATE
    dt = _softplus(dt_ref[...] + dtb_ref[...])
    adt = dt * (-jnp.exp(alog_ref[...]))
    expand = _head_expand()
    xs_ref[...] = xs
    xdt_ref[...] = xs * _mm_sel_r(dt, expand, 3)
    dec_ref[...] = jnp.exp(_mm_sel_r(adt, expand, 3))
    bm_ref[...] = xact[:, M_WIDTH:M_WIDTH + gs]
    cm_ref[...] = xact[:, M_WIDTH + gs:]


def _mamba_sample_prep(xbc, cprev, dt, wts):
    n = xbc.shape[0]
    args = (xbc, cprev, dt) + tuple(wts)
    widths = (M_WIDTH, M_WIDTH, M_WIDTH, M_GROUPS * D_STATE, M_GROUPS * D_STATE)
    return pl.pallas_call(
        _mamba_prep_kernel,
        grid=(1,),
        in_specs=[_const_spec(a.shape) for a in args],
        out_specs=[_const_spec((n, w)) for w in widths],
        out_shape=[jax.ShapeDtypeStruct((n, w), F32) for w in widths],
        compiler_params=pltpu.CompilerParams(vmem_limit_bytes=VMEM_LIMIT),
        name="mamba_sample_prep",
    )(*args)


def _mamba_step_kernel(xs_ref, xdt_ref, dec_ref, bm_ref, cm_ref, h_ref, dskip_ref, y_ref, ho_ref):
    eye = _iota2((M_HEADDIM, M_HEADDIM), 0) == _iota2((M_HEADDIM, M_HEADDIM), 1)
    hpg = M_HEADS // M_GROUPS
    for bi in range(STEP_BATCH):
        rows = []
        for e in range(M_HEADS):
            i = bi * M_HEADS + e
            gi = e // hpg
            b_row = bm_ref[bi:bi + 1, gi * D_STATE:(gi + 1) * D_STATE]
            c_row = cm_ref[bi:bi + 1, gi * D_STATE:(gi + 1) * D_STATE]
            x_col = _col_from_row(xdt_ref[i:i + 1, :], eye)
            h2 = h_ref[i] * dec_ref[i:i + 1, 0:1] + x_col * b_row
            ho_ref[i] = h2
            y = _row_from_col(jnp.sum(h2 * c_row, axis=1, keepdims=True), eye)
            rows.append(y + dskip_ref[e:e + 1, :] * xs_ref[i:i + 1, :])
        y_ref[bi * M_HEADS:(bi + 1) * M_HEADS, :] = jnp.concatenate(rows, axis=0)


def _mamba_sample_step(xs, xdt, dec, bm, cm, state, dskip_rows):
    nb = state.shape[0]
    rows = STEP_BATCH * M_HEADS
    vec_spec = pl.BlockSpec((rows, M_HEADDIM), lambda i: (i, 0))
    bc_spec = pl.BlockSpec((STEP_BATCH, M_GROUPS * D_STATE), lambda i: (i, 0))
    st_spec = pl.BlockSpec((rows, M_HEADDIM, D_STATE), lambda i: (i, 0, 0))
    flat = lambda x: x.reshape(nb * M_HEADS, M_HEADDIM)
    return pl.pallas_call(
        _mamba_step_kernel,
        grid=(nb // STEP_BATCH,),
        in_specs=[vec_spec] * 3 + [bc_spec] * 2 + [st_spec, _const_spec((M_HEADS, M_HEADDIM))],
        out_specs=[vec_spec, st_spec],
        out_shape=[jax.ShapeDtypeStruct((nb * M_HEADS, M_HEADDIM), F32),
                   jax.ShapeDtypeStruct((nb * M_HEADS, M_HEADDIM, D_STATE), F32)],
        compiler_params=pltpu.CompilerParams(dimension_semantics=("arbitrary",),
                                             vmem_limit_bytes=VMEM_LIMIT),
        name="mamba_sample_step",
    )(flat(xs), flat(xdt), flat(dec), bm, cm,
      state.reshape(nb * M_HEADS, M_HEADDIM, D_STATE), dskip_rows)


FF_SLAB = 1024


def _ffn_kernel(has_gate, *refs):
    if has_gate:
        (x_ref, or_ref, ym_ref, z_ref, gnw_ref, wout_ref, g1_ref, b1_ref, w1_ref, w2_ref,
         g2_ref, b2_ref, y_ref) = refs
        o_m = _gated_rmsnorm(ym_ref[...], z_ref[...], gnw_ref[...])
    else:
        (x_ref, or_ref, om_ref, wout_ref, g1_ref, b1_ref, w1_ref, w2_ref,
         g2_ref, b2_ref, y_ref) = refs
        o_m = om_ref[...]
    mix = _mm(or_ref[...], wout_ref[:R_WIDTH, :]) + _mm(o_m, wout_ref[R_WIDTH:, :])
    h = _layer_norm(ALPHA * x_ref[...] + mix, g1_ref[...], b1_ref[...], LN_EPS)
    hb = h.astype(BF16)
    ff = jnp.zeros_like(h)
    for j in range(D_FF // FF_SLAB):
        t = jnp.dot(hb, w1_ref[:, j * FF_SLAB:(j + 1) * FF_SLAB], preferred_element_type=F32)
        t = jnp.square(jnp.maximum(t, 0.0))
        ff = ff + jnp.dot(t.astype(BF16), w2_ref[j * FF_SLAB:(j + 1) * FF_SLAB, :],
                          preferred_element_type=F32)
    y_ref[...] = _layer_norm(ALPHA * h + ff, g2_ref[...], b2_ref[...], LN_EPS)


def _ffn(x2d, o_r, o_m, gate, wout, g1, b1, w1, w2, g2, b2, tm):
    n = x2d.shape[0]
    tok = lambda w: pl.BlockSpec((tm, w), lambda i: (i, 0))
    once = lambda a: pl.BlockSpec(a.shape, lambda i: (0,) * a.ndim, pipeline_mode=pl.Buffered(1))
    acts = [x2d, o_r, o_m]
    act_specs = [tok(D_MODEL), tok(R_WIDTH), tok(M_WIDTH)]
    if gate is not None:
        acts += [gate[0], gate[1]]
        act_specs += [tok(M_WIDTH), once(gate[1])]
    wts = [wout, g1, b1, w1, w2, g2, b2]
    return pl.pallas_call(
        functools.partial(_ffn_kernel, gate is not None),
        grid=(n // tm,),
        in_specs=act_specs + [once(w) for w in wts],
        out_specs=tok(D_MODEL),
        out_shape=jax.ShapeDtypeStruct((n, D_MODEL), F32),
        compiler_params=pltpu.CompilerParams(dimension_semantics=("arbitrary",),
                                             vmem_limit_bytes=VMEM_LIMIT),
        name="ffn",
    )(*acts, *wts)


def _block_diag_ones():
    idx = np.arange(R_WIDTH) // R_HEAD
    return jnp.asarray(idx[:, None] == idx[None, :], dtype=BF16)


def kernel(x_prompt, x_sample, state_shift, state_wkv, state_conv, state_ssm, w_in, mu_shift, w0, w_up, a0, a_up, g_up, k_k, k_a, r_k, lnx_w, lnx_b, conv_w, conv_b, dt_bias, a_log, d_skip, gnorm_w, w_out, ln1_g, ln1_b, w_ff1, w_ff2, ln2_g, ln2_b):
    assert w_in.shape[0] == DEPTH and x_sample.shape[1] == 1
    batch, seq, _ = x_prompt.shape
    nb = x_sample.shape[0]
    assert seq % SSD_CHUNK == 0 and nb % STEP_BATCH == 0
    row = lambda a: a.reshape(1, -1)

    w_in_p = jnp.pad(w_in[0], ((0, 0), (0, PROJ_PAD - w_in.shape[2]))).astype(BF16)
    zl = jnp.zeros((D_DECAY_LORA, R_WIDTH), F32)
    wup_p = jnp.concatenate([w_up[0], zl], axis=0).astype(BF16)
    aup_p = jnp.concatenate([zl, a_up[0]], axis=0).astype(BF16)
    bd = _block_diag_ones()
    prep_w = (row(mu_shift[0]), row(w0[0]), wup_p, row(a0[0]), aup_p, g_up[0].astype(BF16),
              row(k_k[0]), row(k_a[0]), bd)
    post_w = (row(r_k[0]), row(lnx_w[0]), row(lnx_b[0]))
    pad_dt = lambda a: jnp.pad(a[0], (0, DT_PAD - M_HEADS)).reshape(1, DT_PAD)
    dskip_lanes = jnp.repeat(d_skip[0], M_HEADDIM).reshape(1, M_WIDTH)
    dskip_rows = jnp.broadcast_to(d_skip[0][:, None], (M_HEADS, M_HEADDIM))
    conv_wts = (conv_w[0], row(conv_b[0]), pad_dt(dt_bias), pad_dt(a_log))
    ffn_w = (w_out[0].astype(BF16), row(ln1_g[0]), row(ln1_b[0]), w_ff1[0].astype(BF16),
             w_ff2[0].astype(BF16), row(ln2_g[0]), row(ln2_b[0]))

    xp = x_prompt.reshape(batch * seq, D_MODEL)
    pr_p, z_p, xbc_p, dt_p = _proj(xp, w_in_p, 256)
    o_r_p, wkv_p = _rwkv_prompt(pr_p, batch, seq, prep_w[:-1] + post_w + (bd,))
    o_m_p, ssm_p = _mamba_prompt(z_p, xbc_p, dt_p, batch, seq,
                                 conv_wts + (dskip_lanes, row(gnorm_w[0])))
    y_p = _ffn(xp, o_r_p, o_m_p, None, *ffn_w, tm=256)
    shift_p = pr_p.reshape(batch, seq, RWKV_PROJ)[:, -1]
    conv_p = xbc_p.reshape(batch, seq, CONV_DIM)[:, seq - (CONV_K - 1):]

    xs2d = x_sample.reshape(nb, D_MODEL)
    pr_s, z_s, xbc_s, dt_s = _proj(xs2d, w_in_p, nb)
    vecs = _rwkv_sample_prep(pr_s, state_shift[0], prep_w)
    o_r_s, wkv_s = _rwkv_sample_step(vecs, state_wkv[0], r_k[0], lnx_w[0].reshape(R_HEADS, R_HEAD),
                                     lnx_b[0].reshape(R_HEADS, R_HEAD))
    cprev = state_conv[0].reshape(nb, (CONV_K - 1) * CONV_DIM)
    xs_s, xdt_s, dec_s, bm_s, cm_s = _mamba_sample_prep(xbc_s, cprev, dt_s, conv_wts)
    y_m_s, ssm_s = _mamba_sample_step(xs_s, xdt_s, dec_s, bm_s, cm_s, state_ssm[0], dskip_rows)
    y_s = _ffn(xs2d, o_r_s.reshape(nb, R_WIDTH), y_m_s.reshape(nb, M_WIDTH),
               (z_s, row(gnorm_w[0])), *ffn_w, tm=nb)
    conv_s = jnp.concatenate([state_conv[0][:, 1:], xbc_s[:, None, :]], axis=1)

    return (y_p.reshape(batch, seq, D_MODEL), y_s.reshape(nb, 1, D_MODEL),
            shift_p[None], wkv_p[None], conv_p[None],
            ssm_p.reshape(1, batch, M_HEADS, M_HEADDIM, D_STATE),
            pr_s[None], wkv_s.reshape(1, nb, R_HEADS, R_HEAD, R_HEAD), conv_s[None],
            ssm_s.reshape(1, nb, M_HEADS, M_HEADDIM, D_STATE))
```

```python
import functools
import math

import numpy as np
import jax
import jax.numpy as jnp
from jax import lax
from jax.experimental import pallas as pl
from jax.experimental.pallas import tpu as pltpu

F32 = jnp.float32
BF16 = jnp.bfloat16

D_MODEL = 1024
R_WIDTH = 512
R_HEAD = 64
R_HEADS = 8
D_DECAY_LORA = 64
D_AAA_LORA = 64
D_GATE_LORA = 128
M_WIDTH = 512
M_HEADDIM = 64
M_HEADS = 8
M_GROUPS = 2
D_STATE = 128
CONV_K = 4
CONV_DIM = M_WIDTH + 2 * M_GROUPS * D_STATE
D_FF = 4 * D_MODEL
RWKV_PROJ = 3 * R_WIDTH + D_DECAY_LORA + D_AAA_LORA + D_GATE_LORA
DT_PAD = 128
PROJ_PAD = RWKV_PROJ + M_WIDTH + CONV_DIM + DT_PAD
DEPTH = 1
ALPHA = (2 * DEPTH) ** 0.25
LN_EPS = 1e-5
LNX_EPS = 64e-5
RMS_EPS = 1e-5

RWKV_CHUNK = 64
RWKV_TILE = 256
SSD_CHUNK = 256
SUBLANES = 8
VMEM_LIMIT = 48 * 1024 * 1024

_NT = (((1,), (1,)), ((), ()))


def _mm(a, b):
    return jnp.dot(a.astype(BF16), b.astype(BF16), preferred_element_type=F32)


def _mm_nt(a, b):
    return lax.dot_general(a.astype(BF16), b.astype(BF16), _NT, preferred_element_type=F32)


def _split(x, passes):
    parts = []
    for _ in range(passes):
        p = x.astype(BF16)
        parts.append(p)
        x = x - p.astype(F32)
    return parts


def _mm_sel_r(x, sel, passes):
    return sum(jnp.dot(p, sel, preferred_element_type=F32) for p in _split(x, passes))


def _mm_sel_l(sel, x, passes):
    return sum(jnp.dot(sel, p, preferred_element_type=F32) for p in _split(x, passes))


def _softplus(x):
    return jnp.maximum(x, 0.0) + jnp.log(1.0 + jnp.exp(-jnp.abs(x)))


def _sigmoid(x):
    return 1.0 / (1.0 + jnp.exp(-x))


def _silu(x):
    return x * _sigmoid(x)


def _layer_norm(x, g, b, eps):
    mu = jnp.mean(x, axis=-1, keepdims=True)
    d = x - mu
    var = jnp.mean(d * d, axis=-1, keepdims=True)
    return d * lax.rsqrt(var + eps) * g + b


def _iota2(shape, axis):
    return lax.broadcasted_iota(jnp.int32, shape, axis)


def _proj_kernel(x_ref, w_ref, pr_ref, z_ref, xbc_ref, dt_ref):
    res = jnp.dot(x_ref[...].astype(BF16), w_ref[...], preferred_element_type=F32)
    c0 = RWKV_PROJ
    c1 = c0 + M_WIDTH
    c2 = c1 + CONV_DIM
    pr_ref[...] = res[:, :c0]
    z_ref[...] = res[:, c0:c1]
    xbc_ref[...] = res[:, c1:c2]
    dt_ref[...] = res[:, c2:]


def _proj(x2d, w_bf, tm):
    n = x2d.shape[0]
    widths = (RWKV_PROJ, M_WIDTH, CONV_DIM, DT_PAD)
    return pl.pallas_call(
        _proj_kernel,
        grid=(n // tm,),
        in_specs=[pl.BlockSpec((tm, D_MODEL), lambda i: (i, 0)),
                  pl.BlockSpec((D_MODEL, PROJ_PAD), lambda i: (0, 0))],
        out_specs=[pl.BlockSpec((tm, w), lambda i: (i, 0)) for w in widths],
        out_shape=[jax.ShapeDtypeStruct((n, w), F32) for w in widths],
        compiler_params=pltpu.CompilerParams(dimension_semantics=("arbitrary",),
                                             vmem_limit_bytes=VMEM_LIMIT),
        name="proj",
    )(x2d, w_bf)


def _rwkv_prep(pr, prev, mu, w0, wup, a0, aup, gup, k_k, k_a, bd):
    u = pr + (prev - pr) * mu
    r = u[:, 0:R_WIDTH]
    k = u[:, R_WIDTH:2 * R_WIDTH]
    v = u[:, 2 * R_WIDTH:3 * R_WIDTH]
    lora = u[:, 3 * R_WIDTH:3 * R_WIDTH + D_DECAY_LORA + D_AAA_LORA]
    ug = u[:, 3 * R_WIDTH + D_DECAY_LORA + D_AAA_LORA:]
    w_log = -_softplus(-(w0 + _mm(jnp.tanh(lora), wup))) - 0.5
    logw = -jnp.exp(w_log)
    a = _sigmoid(a0 + _mm(lora, aup))
    g = _mm(_sigmoid(ug), gup)
    kk = k * k_k
    ss = _mm_sel_r(kk * kk, bd, 1)
    kkn = kk * lax.rsqrt(jnp.maximum(ss, 1e-24))
    k2 = k * (1.0 + (a - 1.0) * k_a)
    return r, logw, k2, v, -kkn, kkn * a, g


def _rwkv_post(y, r, k2, v, g, r_k, lnx_w, lnx_b, bd):
    inv = 1.0 / R_HEAD
    mu = _mm_sel_r(y, bd, 1) * inv
    d = y - mu
    var = _mm_sel_r(d * d, bd, 1) * inv
    yn = d * lax.rsqrt(var + LNX_EPS) * lnx_w + lnx_b
    bonus = _mm_sel_r(r * k2 * r_k, bd, 1)
    return (yn + bonus * v) * g


def _unit_lower_inverses(n_list):
    c = RWKV_CHUNK
    ii = _iota2((c, c), 0)
    jj = _iota2((c, c), 1)
    sh = 4
    same = (ii >> sh) == (jj >> sh)
    eye = jnp.where(ii == jj, 1.0, 0.0)
    npow = [jnp.where(same, n, 0.0) for n in n_list]
    t = [eye + n for n in npow]
    for _ in range(sh - 1):
        npow = [_mm(n, n) for n in npow]
        t = [ti + _mm(ti, n) for ti, n in zip(t, npow)]
    while (1 << sh) < c:
        off = ((ii >> (sh + 1)) == (jj >> (sh + 1))) & (((ii >> sh) & 1) > ((jj >> sh) & 1))
        tn = [_mm(ti, jnp.where(off, n, 0.0)) for ti, n in zip(t, n_list)]
        t = [ti + _mm(x, ti) for ti, x in zip(t, tn)]
        sh += 1
    return t


def _rwkv_tile_kernel(pr_ref, prevblk_ref, mu_ref, w0_ref, wup_ref, a0_ref, aup_ref, gup_ref,
                      kk_ref, ka_ref, rk_ref, lnw_ref, lnb_ref, bd_ref,
                      o_ref, sfin_ref, g_scr):
    c = RWKV_CHUNK
    tile = RWKV_TILE
    nck = tile // c
    hw = 2 * R_HEAD
    npair = R_HEADS // 2
    ti = pl.program_id(1)

    @pl.when(ti == 0)
    def _():
        g_scr[...] = jnp.zeros_like(g_scr)

    pr = pr_ref[...]
    prev_row = jnp.where(ti == 0, 0.0, prevblk_ref[SUBLANES - 1:SUBLANES, :])
    prev = jnp.where(_iota2((tile, 1), 0) == 0, prev_row, pltpu.roll(pr, 1, axis=0))
    bd = bd_ref[...]
    r, logw, k2, v, av, bv, g = _rwkv_prep(pr, prev, mu_ref[...], w0_ref[...], wup_ref[...],
                                           a0_ref[...], aup_ref[...], gup_ref[...],
                                           kk_ref[...], ka_ref[...], bd)

    sh_c = int(math.log2(c))
    ti_i = _iota2((tile, tile), 0)
    ti_j = _iota2((tile, tile), 1)
    chunk_tril = jnp.where((ti_i >= ti_j) & ((ti_i >> sh_c) == (ti_j >> sh_c)), 1.0, 0.0).astype(BF16)
    cum = _mm_sel_l(chunk_tril, logw, 3)
    cl = jnp.concatenate(
        [jnp.broadcast_to(cum[(q + 1) * c - 1:(q + 1) * c, :], (c, R_WIDTH)) for q in range(nck)], axis=0)
    p_out = jnp.exp(-cum)
    p_end = jnp.exp(cl - cum)
    rt = r * jnp.exp(cum)
    at = av * jnp.exp(cum - logw)
    bt = bv * p_out
    kt = k2 * p_out
    bh = bv * p_end
    kh = k2 * p_end
    pc = jnp.exp(cl)

    ii = _iota2((c, c), 0)
    jj = _iota2((c, c), 1)
    strict = ii > jj
    incl = ii >= jj
    lane = _iota2((1, hw), 1)
    masks = (lane < R_HEAD, lane >= R_HEAD)
    e_r = _iota2((hw, hw), 0)
    e_c = _iota2((hw, hw), 1)
    eye_hw = e_r == e_c
    blockdiag = (e_r >= R_HEAD) == (e_c >= R_HEAD)

    chains = [(q, p, e) for q in range(nck) for p in range(npair) for e in range(2)]
    blk = lambda x, q, p: x[q * c:(q + 1) * c, hw * p:hw * (p + 1)]
    at_m = [jnp.where(masks[e], blk(at, q, p), 0.0) for q, p, e in chains]
    rt_m = [jnp.where(masks[e], blk(rt, q, p), 0.0) for q, p, e in chains]
    lhs = [jnp.concatenate([a, b], axis=0) for a, b in zip(at_m, rt_m)]
    sc_b = [_mm_nt(x, blk(bt, q, p)) for x, (q, p, e) in zip(lhs, chains)]
    sc_k = [_mm_nt(x, blk(kt, q, p)) for x, (q, p, e) in zip(lhs, chains)]
    a_ab = [jnp.where(strict, s[:c], 0.0) for s in sc_b]
    a_rb = [jnp.where(incl, s[c:], 0.0) for s in sc_b]
    a_ak = [jnp.where(strict, s[:c], 0.0) for s in sc_k]
    a_rk = [jnp.where(incl, s[c:], 0.0) for s in sc_k]
    akv = [_mm(a, blk(v, q, p)) for a, (q, p, e) in zip(a_ak, chains)]
    t_inv = _unit_lower_inverses(a_ab)
    tx = [_mm(t, jnp.concatenate([a, b], axis=1)) for t, a, b in zip(t_inv, at_m, akv)]
    rbx = [_mm(a, x) for a, x in zip(a_rb, tx)]
    rkv = [_mm(a, blk(v, q, p)) for a, (q, p, e) in zip(a_rk, chains)]

    at2_pair, u0_pair, rp_pair, y0_pair = {}, {}, {}, {}
    for idx in range(0, len(chains), 2):
        q, p, _ = chains[idx]
        i0, i1 = idx, idx + 1
        at2_pair[q, p] = tx[i0][:, :hw] + tx[i1][:, :hw]
        u0_pair[q, p] = jnp.where(masks[0], tx[i0][:, hw:], tx[i1][:, hw:])
        rp_pair[q, p] = rt_m[i0] + rt_m[i1] + rbx[i0][:, :hw] + rbx[i1][:, :hw]
        y0_pair[q, p] = jnp.where(masks[0], rbx[i0][:, hw:] + rkv[i0], rbx[i1][:, hw:] + rkv[i1])

    qp = [(q, p) for q in range(nck) for p in range(npair)]
    wt = {k: jnp.concatenate([blk(bh, *k), blk(kh, *k)], axis=0).T for k in qp}
    m_ba = {k: jnp.where(blockdiag, _mm(wt[k][:, :c], at2_pair[k]), 0.0) for k in qp}
    h0c = {k: jnp.where(blockdiag,
                        _mm(wt[k], jnp.concatenate([u0_pair[k], blk(v, *k)], axis=0)), 0.0) for k in qp}
    pc_col = {k: jnp.sum(jnp.where(eye_hw, blk(pc, *k)[:1], 0.0), axis=1, keepdims=True) for k in qp}

    state = [g_scr[p] for p in range(npair)]
    y_rows = []
    for q in range(nck):
        y_rows.append(jnp.concatenate(
            [_mm(rp_pair[q, p], state[p]) + y0_pair[q, p] for p in range(npair)], axis=1))
        state = [pc_col[q, p] * state[p] + _mm(m_ba[q, p], state[p]) + h0c[q, p] for p in range(npair)]
    for p in range(npair):
        g_scr[p] = state[p]

    y = jnp.concatenate(y_rows, axis=0)
    o_ref[...] = _rwkv_post(y, r, k2, v, g, rk_ref[...], lnw_ref[...], lnb_ref[...], bd)

    @pl.when(ti == pl.num_programs(1) - 1)
    def _():
        for p in range(npair):
            gt = g_scr[p].T
            sfin_ref[0, 2 * p] = gt[:R_HEAD, :R_HEAD]
            sfin_ref[0, 2 * p + 1] = pltpu.roll(gt, R_HEAD, axis=1)[R_HEAD:, :R_HEAD]


def _const_spec(shape):
    return pl.BlockSpec(shape, lambda *_: (0,) * len(shape))


def _rwkv_prompt(pr2d, batch, seq, wts):
    tile = RWKV_TILE
    ntile = seq // tile
    rows8 = seq // SUBLANES

    def prev_map(b, i):
        return (b * rows8 + jnp.maximum(i * (tile // SUBLANES) - 1, 0), 0)

    w_specs = [_const_spec(w.shape) for w in wts]
    return pl.pallas_call(
        _rwkv_tile_kernel,
        grid=(batch, ntile),
        in_specs=[pl.BlockSpec((tile, RWKV_PROJ), lambda b, i: (b * ntile + i, 0)),
                  pl.BlockSpec((SUBLANES, RWKV_PROJ), prev_map)] + w_specs,
        out_specs=[pl.BlockSpec((tile, R_WIDTH), lambda b, i: (b * ntile + i, 0)),
                   pl.BlockSpec((1, R_HEADS, R_HEAD, R_HEAD), lambda b, i: (b, 0, 0, 0))],
        out_shape=[jax.ShapeDtypeStruct((batch * seq, R_WIDTH), F32),
                   jax.ShapeDtypeStruct((batch, R_HEADS, R_HEAD, R_HEAD), F32)],
        scratch_shapes=[pltpu.VMEM((R_HEADS // 2, 2 * R_HEAD, 2 * R_HEAD), F32)],
        compiler_params=pltpu.CompilerParams(dimension_semantics=("arbitrary", "arbitrary"),
                                             vmem_limit_bytes=VMEM_LIMIT),
        name="rwkv_prompt",
    )(pr2d, pr2d, *wts)


def _rwkv_prep_kernel(pr_ref, prev_ref, mu_ref, w0_ref, wup_ref, a0_ref, aup_ref, gup_ref,
                      kk_ref, ka_ref, bd_ref,
                      r_ref, w_ref, k_ref, v_ref, a_ref, b_ref, g_ref):
    r, logw, k2, v, av, bv, g = _rwkv_prep(pr_ref[...], prev_ref[...], mu_ref[...], w0_ref[...],
                                           wup_ref[...], a0_ref[...], aup_ref[...], gup_ref[...],
                                           kk_ref[...], ka_ref[...], bd_ref[...])
    r_ref[...] = r
    w_ref[...] = jnp.exp(logw)
    k_ref[...] = k2
    v_ref[...] = v
    a_ref[...] = av
    b_ref[...] = bv
    g_ref[...] = g


def _rwkv_sample_prep(pr, prev, wts):
    n = pr.shape[0]
    args = (pr, prev) + tuple(wts)
    return pl.pallas_call(
        _rwkv_prep_kernel,
        grid=(1,),
        in_specs=[_const_spec(a.shape) for a in args],
        out_specs=[_const_spec((n, R_WIDTH))] * 7,
        out_shape=[jax.ShapeDtypeStruct((n, R_WIDTH), F32)] * 7,
        compiler_params=pltpu.CompilerParams(vmem_limit_bytes=VMEM_LIMIT),
        name="rwkv_sample_prep",
    )(*args)


STEP_BATCH = 8


def _col_from_row(row, eye):
    return jnp.sum(jnp.where(eye, row, 0.0), axis=1, keepdims=True)


def _row_from_col(col, eye):
    return jnp.sum(jnp.where(eye, col, 0.0), axis=0, keepdims=True)


def _rwkv_step_kernel(r_ref, w_ref, k_ref, v_ref, a_ref, b_ref, g_ref, s_ref,
                      rk_ref, lnw_ref, lnb_ref, o_ref, so_ref):
    eye = _iota2((R_HEAD, R_HEAD), 0) == _iota2((R_HEAD, R_HEAD), 1)
    inv = 1.0 / R_HEAD
    for bi in range(STEP_BATCH):
        rows = []
        for h in range(R_HEADS):
            i = bi * R_HEADS + h
            row = lambda ref: ref[i:i + 1, :]
            r, w, k, v, a, b, g = (row(x) for x in (r_ref, w_ref, k_ref, v_ref, a_ref, b_ref, g_ref))
            s = s_ref[i]
            sa = jnp.sum(s * a, axis=1, keepdims=True)
            s2 = s * w + sa * b + _col_from_row(v, eye) * k
            so_ref[i] = s2
            y = _row_from_col(jnp.sum(s2 * r, axis=1, keepdims=True), eye)
            mu = jnp.sum(y, axis=1, keepdims=True) * inv
            d = y - mu
            var = jnp.sum(d * d, axis=1, keepdims=True) * inv
            yn = d * lax.rsqrt(var + LNX_EPS) * lnw_ref[h:h + 1, :] + lnb_ref[h:h + 1, :]
            bonus = jnp.sum(r * k * rk_ref[h:h + 1, :], axis=1, keepdims=True)
            rows.append((yn + bonus * v) * g)
        o_ref[bi * R_HEADS:(bi + 1) * R_HEADS, :] = jnp.concatenate(rows, axis=0)


def _rwkv_sample_step(vecs, state, r_k, lnx_w, lnx_b):
    nb = state.shape[0]
    rows = STEP_BATCH * R_HEADS
    vec_spec = pl.BlockSpec((rows, R_HEAD), lambda i: (i, 0))
    st_spec = pl.BlockSpec((rows, R_HEAD, R_HEAD), lambda i: (i, 0, 0))
    w_spec = _const_spec((R_HEADS, R_HEAD))
    return pl.pallas_call(
        _rwkv_step_kernel,
        grid=(nb // STEP_BATCH,),
        in_specs=[vec_spec] * 7 + [st_spec] + [w_spec] * 3,
        out_specs=[vec_spec, st_spec],
        out_shape=[jax.ShapeDtypeStruct((nb * R_HEADS, R_HEAD), F32),
                   jax.ShapeDtypeStruct((nb * R_HEADS, R_HEAD, R_HEAD), F32)],
        compiler_params=pltpu.CompilerParams(dimension_semantics=("arbitrary",),
                                             vmem_limit_bytes=VMEM_LIMIT),
        name="rwkv_sample_step",
    )(*[x.reshape(nb * R_HEADS, R_HEAD) for x in vecs],
      state.reshape(nb * R_HEADS, R_HEAD, R_HEAD), r_k, lnx_w, lnx_b)


def _shift_rows(x, prev8, s):
    rolled = pltpu.roll(x, s, axis=0)
    top = jnp.where(_iota2((SUBLANES, 1), 0) < s, pltpu.roll(prev8, s, axis=0), rolled[:SUBLANES])
    return jnp.concatenate([top, rolled[SUBLANES:]], axis=0)


def _head_expand():
    head_of_lane = _iota2((DT_PAD, M_WIDTH), 1) >> int(math.log2(M_HEADDIM))
    return jnp.where(_iota2((DT_PAD, M_WIDTH), 0) == head_of_lane, 1.0, 0.0).astype(BF16)


def _gated_rmsnorm(y, z, gnorm_w):
    yz = y * _silu(z)
    gw = M_WIDTH // M_GROUPS
    outs = []
    for gi in range(M_GROUPS):
        yg = yz[:, gi * gw:(gi + 1) * gw]
        ms = jnp.mean(yg * yg, axis=-1, keepdims=True)
        outs.append(yg * lax.rsqrt(ms + RMS_EPS))
    return jnp.concatenate(outs, axis=1) * gnorm_w


def _mamba_chunk_kernel(z_ref, xbc_ref, xprev_ref, dt_ref, convw_ref, convb_ref, dtb_ref, alog_ref,
                        dskip_ref, gnw_ref, o_ref, hfin_ref, h_scr):
    q = SSD_CHUNK
    ci = pl.program_id(1)

    @pl.when(ci == 0)
    def _():
        h_scr[...] = jnp.zeros_like(h_scr)

    xb = xbc_ref[...]
    prev8 = jnp.where(ci == 0, 0.0, xprev_ref[...])
    conv = convb_ref[...] + xb * convw_ref[CONV_K - 1:CONV_K, :]
    for s in range(1, CONV_K):
        conv = conv + _shift_rows(xb, prev8, s) * convw_ref[CONV_K - 1 - s:CONV_K - s, :]
    xact = _silu(conv)
    xs = xact[:, :M_WIDTH]
    gs = M_GROUPS * D_STATE
    bm = xact[:, M_WIDTH:M_WIDTH + gs]
    cm = xact[:, M_WIDTH + gs:]

    dt = _softplus(dt_ref[...] + dtb_ref[...])
    adt = dt * (-jnp.exp(alog_ref[...]))
    x_dt = xs * _mm_sel_r(dt, _head_expand(), 3)

    ii = _iota2((q, q), 0)
    jj = _iota2((q, q), 1)
    incl = ii >= jj
    cs = _mm_sel_l(jnp.where(incl, 1.0, 0.0).astype(BF16), adt, 3)
    cs_t = cs.T
    lane = _iota2((1, 2 * M_HEADDIM), 1)
    first = lane < M_HEADDIM
    hpg = M_HEADS // M_GROUPS

    y_pairs = []
    for gi in range(M_GROUPS):
        bm_g = bm[:, gi * D_STATE:(gi + 1) * D_STATE]
        cm_g = cm[:, gi * D_STATE:(gi + 1) * D_STATE]
        cb = _mm_nt(cm_g, bm_g)
        bm_t = bm_g.T
        for pp in range(hpg // 2):
            p = gi * (hpg // 2) + pp
            sl = slice(2 * M_HEADDIM * p, 2 * M_HEADDIM * (p + 1))
            x_p = x_dt[:, sl]
            y_diag = jnp.zeros((q, 2 * M_HEADDIM), F32)
            contrib = jnp.zeros((D_STATE, 2 * M_HEADDIM), F32)
            scale_in = []
            scale_end = []
            for e in range(2):
                head = 2 * p + e
                m = first if e == 0 else jnp.logical_not(first)
                a_col = cs[:, head:head + 1]
                a_row = cs_t[head:head + 1, :]
                a_end = cs_t[head:head + 1, q - 1:q]
                lmat = jnp.exp(jnp.where(incl, a_col - a_row, -jnp.inf))
                x_m = jnp.where(m, x_p, 0.0)
                y_diag = y_diag + _mm(cb * lmat, x_m)
                contrib = contrib + _mm(bm_t * jnp.exp(a_end - a_row), x_m)
                scale_in.append(jnp.exp(a_col))
                scale_end.append(jnp.exp(a_end))
            h_in = h_scr[p]
            y_off = _mm(cm_g, h_in) * jnp.where(first, scale_in[0], scale_in[1])
            h_scr[p] = h_in * jnp.where(first, scale_end[0], scale_end[1]) + contrib
            y_pairs.append(y_diag + y_off)

    y = jnp.concatenate(y_pairs, axis=1) + dskip_ref[...] * xs
    o_ref[...] = _gated_rmsnorm(y, z_ref[...], gnw_ref[...])

    @pl.when(ci == pl.num_programs(1) - 1)
    def _():
        for p in range(M_HEADS // 2):
            hfin_ref[0, 2 * M_HEADDIM * p:2 * M_HEADDIM * (p + 1), :] = h_scr[p].T


def _mamba_prompt(z2d, xbc2d, dt2d, batch, seq, wts):
    q = SSD_CHUNK
    nchunk = seq // q
    rows8 = seq // SUBLANES

    def prev_map(b, i):
        return (b * rows8 + jnp.maximum(i * (q // SUBLANES) - 1, 0), 0)

    tok = lambda w: pl.BlockSpec((q, w), lambda b, i: (b * nchunk + i, 0))
    return pl.pallas_call(
        _mamba_chunk_kernel,
        grid=(batch, nchunk),
        in_specs=[tok(M_WIDTH), tok(CONV_DIM), pl.BlockSpec((SUBLANES, CONV_DIM), prev_map), tok(DT_PAD)]
                 + [_const_spec(w.shape) for w in wts],
        out_specs=[tok(M_WIDTH),
                   pl.BlockSpec((1, M_HEADS * M_HEADDIM, D_STATE), lambda b, i: (b, 0, 0))],
        out_shape=[jax.ShapeDtypeStruct((batch * seq, M_WIDTH), F32),
                   jax.ShapeDtypeStruct((batch, M_HEADS * M_HEADDIM, D_STATE), F32)],
        scratch_shapes=[pltpu.VMEM((M_HEADS // 2, D_STATE, 2 * M_HEADDIM), F32)],
        compiler_params=pltpu.CompilerParams(dimension_semantics=("arbitrary", "arbitrary"),
                                             vmem_limit_bytes=VMEM_LIMIT),
        name="mamba_prompt",
    )(z2d, xbc2d, xbc2d, dt2d, *wts)


def _mamba_prep_kernel(xbc_ref, cprev_ref, dt_ref, convw_ref, convb_ref, dtb_ref, alog_ref,
                       xs_ref, xdt_ref, dec_ref, bm_ref, cm_ref):
    conv = convb_ref[...] + xbc_ref[...] * convw_ref[CONV_K - 1:CONV_K, :]
    for j in range(CONV_K - 1):
        conv = conv + cprev_ref[:, j * CONV_DIM:(j + 1) * CONV_DIM] * convw_ref[j:j + 1, :]
    xact = _silu(conv)
    xs = xact[:, :M_WIDTH]
    gs = M_GROUPS * D_STATE
    dt = _softplus(dt_ref[...] + dtb_ref[...])
    adt = dt * (-jnp.exp(alog_ref[...]))
    expand = _head_expand()
    xs_ref[...] = xs
    xdt_ref[...] = xs * _mm_sel_r(dt, expand, 3)
    dec_ref[...] = jnp.exp(_mm_sel_r(adt, expand, 3))
    bm_ref[...] = xact[:, M_WIDTH:M_WIDTH + gs]
    cm_ref[...] = xact[:, M_WIDTH + gs:]


def _mamba_sample_prep(xbc, cprev, dt, wts):
    n = xbc.shape[0]
    args = (xbc, cprev, dt) + tuple(wts)
    widths = (M_WIDTH, M_WIDTH, M_WIDTH, M_GROUPS * D_STATE, M_GROUPS * D_STATE)
    return pl.pallas_call(
        _mamba_prep_kernel,
        grid=(1,),
        in_specs=[_const_spec(a.shape) for a in args],
        out_specs=[_const_spec((n, w)) for w in widths],
        out_shape=[jax.ShapeDtypeStruct((n, w), F32) for w in widths],
        compiler_params=pltpu.CompilerParams(vmem_limit_bytes=VMEM_LIMIT),
        name="mamba_sample_prep",
    )(*args)


def _mamba_step_kernel(xs_ref, xdt_ref, dec_ref, bm_ref, cm_ref, h_ref, dskip_ref, y_ref, ho_ref):
    eye = _iota2((M_HEADDIM, M_HEADDIM), 0) == _iota2((M_HEADDIM, M_HEADDIM), 1)
    hpg = M_HEADS // M_GROUPS
    for bi in range(STEP_BATCH):
        rows = []
        for e in range(M_HEADS):
            i = bi * M_HEADS + e
            gi = e // hpg
            b_row = bm_ref[bi:bi + 1, gi * D_STATE:(gi + 1) * D_STATE]
            c_row = cm_ref[bi:bi + 1, gi * D_STATE:(gi + 1) * D_STATE]
            x_col = _col_from_row(xdt_ref[i:i + 1, :], eye)
            h2 = h_ref[i] * dec_ref[i:i + 1, 0:1] + x_col * b_row
            ho_ref[i] = h2
            y = _row_from_col(jnp.sum(h2 * c_row, axis=1, keepdims=True), eye)
            rows.append(y + dskip_ref[e:e + 1, :] * xs_ref[i:i + 1, :])
        y_ref[bi * M_HEADS:(bi + 1) * M_HEADS, :] = jnp.concatenate(rows, axis=0)


def _mamba_sample_step(xs, xdt, dec, bm, cm, state, dskip_rows):
    nb = state.shape[0]
    rows = STEP_BATCH * M_HEADS
    vec_spec = pl.BlockSpec((rows, M_HEADDIM), lambda i: (i, 0))
    bc_spec = pl.BlockSpec((STEP_BATCH, M_GROUPS * D_STATE), lambda i: (i, 0))
    st_spec = pl.BlockSpec((rows, M_HEADDIM, D_STATE), lambda i: (i, 0, 0))
    flat = lambda x: x.reshape(nb * M_HEADS, M_HEADDIM)
    return pl.pallas_call(
        _mamba_step_kernel,
        grid=(nb // STEP_BATCH,),
        in_specs=[vec_spec] * 3 + [bc_spec] * 2 + [st_spec, _const_spec((M_HEADS, M_HEADDIM))],
        out_specs=[vec_spec, st_spec],
        out_shape=[jax.ShapeDtypeStruct((nb * M_HEADS, M_HEADDIM), F32),
                   jax.ShapeDtypeStruct((nb * M_HEADS, M_HEADDIM, D_STATE), F32)],
        compiler_params=pltpu.CompilerParams(dimension_semantics=("arbitrary",),
                                             vmem_limit_bytes=VMEM_LIMIT),
        name="mamba_sample_step",
    )(flat(xs), flat(xdt), flat(dec), bm, cm,
      state.reshape(nb * M_HEADS, M_HEADDIM, D_STATE), dskip_rows)


FF_SLAB = 1024


def _ffn_kernel(has_gate, *refs):
    if has_gate:
        (x_ref, or_ref, ym_ref, z_ref, gnw_ref, wout_ref, g1_ref, b1_ref, w1_ref, w2_ref,
         g2_ref, b2_ref, y_ref) = refs
        o_m = _gated_rmsnorm(ym_ref[...], z_ref[...], gnw_ref[...])
    else:
        (x_ref, or_ref, om_ref, wout_ref, g1_ref, b1_ref, w1_ref, w2_ref,
         g2_ref, b2_ref, y_ref) = refs
        o_m = om_ref[...]
    mix = _mm(or_ref[...], wout_ref[:R_WIDTH, :]) + _mm(o_m, wout_ref[R_WIDTH:, :])
    h = _layer_norm(ALPHA * x_ref[...] + mix, g1_ref[...], b1_ref[...], LN_EPS)
    hb = h.astype(BF16)
    ff = jnp.zeros_like(h)
    for j in range(D_FF // FF_SLAB):
        t = jnp.dot(hb, w1_ref[:, j * FF_SLAB:(j + 1) * FF_SLAB], preferred_element_type=F32)
        t = jnp.square(jnp.maximum(t, 0.0))
        ff = ff + jnp.dot(t.astype(BF16), w2_ref[j * FF_SLAB:(j + 1) * FF_SLAB, :],
                          preferred_element_type=F32)
    y_ref[...] = _layer_norm(ALPHA * h + ff, g2_ref[...], b2_ref[...], LN_EPS)


def _ffn(x2d, o_r, o_m, gate, wout, g1, b1, w1, w2, g2, b2, tm):
    n = x2d.shape[0]
    tok = lambda w: pl.BlockSpec((tm, w), lambda i: (i, 0))
    once = lambda a: pl.BlockSpec(a.shape, lambda i: (0,) * a.ndim, pipeline_mode=pl.Buffered(1))
    acts = [x2d, o_r, o_m]
    act_specs = [tok(D_MODEL), tok(R_WIDTH), tok(M_WIDTH)]
    if gate is not None:
        acts += [gate[0], gate[1]]
        act_specs += [tok(M_WIDTH), once(gate[1])]
    wts = [wout, g1, b1, w1, w2, g2, b2]
    return pl.pallas_call(
        functools.partial(_ffn_kernel, gate is not None),
        grid=(n // tm,),
        in_specs=act_specs + [once(w) for w in wts],
        out_specs=tok(D_MODEL),
        out_shape=jax.ShapeDtypeStruct((n, D_MODEL), F32),
        compiler_params=pltpu.CompilerParams(dimension_semantics=("arbitrary",),
                                             vmem_limit_bytes=VMEM_LIMIT),
        name="ffn",
    )(*acts, *wts)


def _block_diag_ones():
    idx = np.arange(R_WIDTH) // R_HEAD
    return jnp.asarray(idx[:, None] == idx[None, :], dtype=BF16)


def kernel(x_prompt, x_sample, state_shift, state_wkv, state_conv, state_ssm, w_in, mu_shift, w0, w_up, a0, a_up, g_up, k_k, k_a, r_k, lnx_w, lnx_b, conv_w, conv_b, dt_bias, a_log, d_skip, gnorm_w, w_out, ln1_g, ln1_b, w_ff1, w_ff2, ln2_g, ln2_b):
    assert w_in.shape[0] == DEPTH and x_sample.shape[1] == 1
    batch, seq, _ = x_prompt.shape
    nb = x_sample.shape[0]
    assert seq % SSD_CHUNK == 0 and seq % RWKV_TILE == 0 and nb % STEP_BATCH == 0
    row = lambda a: a.reshape(1, -1)

    w_in_p = jnp.pad(w_in[0], ((0, 0), (0, PROJ_PAD - w_in.shape[2]))).astype(BF16)
    zl = jnp.zeros((D_DECAY_LORA, R_WIDTH), F32)
    wup_p = jnp.concatenate([w_up[0], zl], axis=0).astype(BF16)
    aup_p = jnp.concatenate([zl, a_up[0]], axis=0).astype(BF16)
    bd = _block_diag_ones()
    prep_w = (row(mu_shift[0]), row(w0[0]), wup_p, row(a0[0]), aup_p, g_up[0].astype(BF16),
              row(k_k[0]), row(k_a[0]), bd)
    post_w = (row(r_k[0]), row(lnx_w[0]), row(lnx_b[0]))
    pad_dt = lambda a: jnp.pad(a[0], (0, DT_PAD - M_HEADS)).reshape(1, DT_PAD)
    dskip_lanes = jnp.repeat(d_skip[0], M_HEADDIM).reshape(1, M_WIDTH)
    dskip_rows = jnp.broadcast_to(d_skip[0][:, None], (M_HEADS, M_HEADDIM))
    conv_wts = (conv_w[0], row(conv_b[0]), pad_dt(dt_bias), pad_dt(a_log))
    ffn_w = (w_out[0].astype(BF16), row(ln1_g[0]), row(ln1_b[0]), w_ff1[0].astype(BF16),
             w_ff2[0].astype(BF16), row(ln2_g[0]), row(ln2_b[0]))

    xp = x_prompt.reshape(batch * seq, D_MODEL)
    pr_p, z_p, xbc_p, dt_p = _proj(xp, w_in_p, 256)
    o_r_p, wkv_p = _rwkv_prompt(pr_p, batch, seq, prep_w[:-1] + post_w + (bd,))
    o_m_p, ssm_p = _mamba_prompt(z_p, xbc_p, dt_p, batch, seq,
                                 conv_wts + (dskip_lanes, row(gnorm_w[0])))
    y_p = _ffn(xp, o_r_p, o_m_p, None, *ffn_w, tm=256)
    shift_p = pr_p.reshape(batch, seq, RWKV_PROJ)[:, -1]
    conv_p = xbc_p.reshape(batch, seq, CONV_DIM)[:, seq - (CONV_K - 1):]

    xs2d = x_sample.reshape(nb, D_MODEL)
    pr_s, z_s, xbc_s, dt_s = _proj(xs2d, w_in_p, nb)
    vecs = _rwkv_sample_prep(pr_s, state_shift[0], prep_w)
    o_r_s, wkv_s = _rwkv_sample_step(vecs, state_wkv[0], r_k[0], lnx_w[0].reshape(R_HEADS, R_HEAD),
                                     lnx_b[0].reshape(R_HEADS, R_HEAD))
    cprev = state_conv[0].reshape(nb, (CONV_K - 1) * CONV_DIM)
    xs_s, xdt_s, dec_s, bm_s, cm_s = _mamba_sample_prep(xbc_s, cprev, dt_s, conv_wts)
    y_m_s, ssm_s = _mamba_sample_step(xs_s, xdt_s, dec_s, bm_s, cm_s, state_ssm[0], dskip_rows)
    y_s = _ffn(xs2d, o_r_s.reshape(nb, R_WIDTH), y_m_s.reshape(nb, M_WIDTH),
               (z_s, row(gnorm_w[0])), *ffn_w, tm=nb)
    conv_s = jnp.concatenate([state_conv[0][:, 1:], xbc_s[:, None, :]], axis=1)

    return (y_p.reshape(batch, seq, D_MODEL), y_s.reshape(nb, 1, D_MODEL),
            shift_p[None], wkv_p[None], conv_p[None],
            ssm_p.reshape(1, batch, M_HEADS, M_HEADDIM, D_STATE),
            pr_s[None], wkv_s.reshape(1, nb, R_HEADS, R_HEAD, R_HEAD), conv_s[None],
            ssm_s.reshape(1, nb, M_HEADS, M_HEADDIM, D_STATE))
```

```python
import functools
import math

import numpy as np
import jax
import jax.numpy as jnp
from jax import lax
from jax.experimental import pallas as pl
from jax.experimental.pallas import tpu as pltpu

F32 = jnp.float32
BF16 = jnp.bfloat16

D_MODEL = 1024
R_WIDTH = 512
R_HEAD = 64
R_HEADS = 8
D_DECAY_LORA = 64
D_AAA_LORA = 64
D_GATE_LORA = 128
M_WIDTH = 512
M_HEADDIM = 64
M_HEADS = 8
M_GROUPS = 2
D_STATE = 128
CONV_K = 4
CONV_DIM = M_WIDTH + 2 * M_GROUPS * D_STATE
D_FF = 4 * D_MODEL
RWKV_PROJ = 3 * R_WIDTH + D_DECAY_LORA + D_AAA_LORA + D_GATE_LORA
DT_PAD = 128
PROJ_PAD = RWKV_PROJ + M_WIDTH + CONV_DIM + DT_PAD
DEPTH = 1
ALPHA = (2 * DEPTH) ** 0.25
LN_EPS = 1e-5
LNX_EPS = 64e-5
RMS_EPS = 1e-5

RWKV_CHUNK = 64
RWKV_TILE = 256
SSD_CHUNK = 256
SUBLANES = 8
VMEM_LIMIT = 48 * 1024 * 1024

_NT = (((1,), (1,)), ((), ()))


def _mm(a, b):
    return jnp.dot(a.astype(BF16), b.astype(BF16), preferred_element_type=F32)


def _mm_nt(a, b):
    return lax.dot_general(a.astype(BF16), b.astype(BF16), _NT, preferred_element_type=F32)


def _split(x, passes):
    parts = []
    for _ in range(passes):
        p = x.astype(BF16)
        parts.append(p)
        x = x - p.astype(F32)
    return parts


def _mm_sel_r(x, sel, passes):
    return sum(jnp.dot(p, sel, preferred_element_type=F32) for p in _split(x, passes))


def _mm_sel_l(sel, x, passes):
    return sum(jnp.dot(sel, p, preferred_element_type=F32) for p in _split(x, passes))


def _softplus(x):
    return jnp.maximum(x, 0.0) + jnp.log(1.0 + jnp.exp(-jnp.abs(x)))


def _sigmoid(x):
    return 1.0 / (1.0 + jnp.exp(-x))


def _silu(x):
    return x * _sigmoid(x)


def _layer_norm(x, g, b, eps):
    mu = jnp.mean(x, axis=-1, keepdims=True)
    d = x - mu
    var = jnp.mean(d * d, axis=-1, keepdims=True)
    return d * lax.rsqrt(var + eps) * g + b


def _iota2(shape, axis):
    return lax.broadcasted_iota(jnp.int32, shape, axis)


def _proj_kernel(x_ref, w_ref, pr_ref, z_ref, xbc_ref, dt_ref):
    res = jnp.dot(x_ref[...].astype(BF16), w_ref[...], preferred_element_type=F32)
    c0 = RWKV_PROJ
    c1 = c0 + M_WIDTH
    c2 = c1 + CONV_DIM
    pr_ref[...] = res[:, :c0]
    z_ref[...] = res[:, c0:c1]
    xbc_ref[...] = res[:, c1:c2]
    dt_ref[...] = res[:, c2:]


def _proj(x2d, w_bf, tm):
    n = x2d.shape[0]
    widths = (RWKV_PROJ, M_WIDTH, CONV_DIM, DT_PAD)
    return pl.pallas_call(
        _proj_kernel,
        grid=(n // tm,),
        in_specs=[pl.BlockSpec((tm, D_MODEL), lambda i: (i, 0)),
                  pl.BlockSpec((D_MODEL, PROJ_PAD), lambda i: (0, 0))],
        out_specs=[pl.BlockSpec((tm, w), lambda i: (i, 0)) for w in widths],
        out_shape=[jax.ShapeDtypeStruct((n, w), F32) for w in widths],
        compiler_params=pltpu.CompilerParams(dimension_semantics=("arbitrary",),
                                             vmem_limit_bytes=VMEM_LIMIT),
        name="proj",
    )(x2d, w_bf)


def _rwkv_prep(pr, prev, mu, w0, wup, a0, aup, gup, k_k, k_a, bd):
    u = pr + (prev - pr) * mu
    r = u[:, 0:R_WIDTH]
    k = u[:, R_WIDTH:2 * R_WIDTH]
    v = u[:, 2 * R_WIDTH:3 * R_WIDTH]
    lora = u[:, 3 * R_WIDTH:3 * R_WIDTH + D_DECAY_LORA + D_AAA_LORA]
    ug = u[:, 3 * R_WIDTH + D_DECAY_LORA + D_AAA_LORA:]
    w_log = -_softplus(-(w0 + _mm(jnp.tanh(lora), wup))) - 0.5
    logw = -jnp.exp(w_log)
    a = _sigmoid(a0 + _mm(lora, aup))
    g = _mm(_sigmoid(ug), gup)
    kk = k * k_k
    ss = _mm_sel_r(kk * kk, bd, 1)
    kkn = kk * lax.rsqrt(jnp.maximum(ss, 1e-24))
    k2 = k * (1.0 + (a - 1.0) * k_a)
    return r, logw, k2, v, -kkn, kkn * a, g


def _rwkv_post(y, r, k2, v, g, r_k, lnx_w, lnx_b, bd):
    inv = 1.0 / R_HEAD
    mu = _mm_sel_r(y, bd, 1) * inv
    d = y - mu
    var = _mm_sel_r(d * d, bd, 1) * inv
    yn = d * lax.rsqrt(var + LNX_EPS) * lnx_w + lnx_b
    bonus = _mm_sel_r(r * k2 * r_k, bd, 1)
    return (yn + bonus * v) * g


def _unit_lower_inverses(n_list):
    c = RWKV_CHUNK
    ii = _iota2((c, c), 0)
    jj = _iota2((c, c), 1)
    sh = 4
    same = (ii >> sh) == (jj >> sh)
    eye = jnp.where(ii == jj, 1.0, 0.0)
    npow = [jnp.where(same, n, 0.0) for n in n_list]
    t = [eye + n for n in npow]
    for _ in range(sh - 1):
        npow = [_mm(n, n) for n in npow]
        t = [ti + _mm(ti, n) for ti, n in zip(t, npow)]
    while (1 << sh) < c:
        off = ((ii >> (sh + 1)) == (jj >> (sh + 1))) & (((ii >> sh) & 1) > ((jj >> sh) & 1))
        tn = [_mm(ti, jnp.where(off, n, 0.0)) for ti, n in zip(t, n_list)]
        t = [ti + _mm(x, ti) for ti, x in zip(t, tn)]
        sh += 1
    return t


def _rwkv_tile_kernel(pr_ref, prevblk_ref, mu_ref, w0_ref, wup_ref, a0_ref, aup_ref, gup_ref,
                      kk_ref, ka_ref, rk_ref, lnw_ref, lnb_ref, bd_ref,
                      o_ref, sfin_ref, g_scr):
    c = RWKV_CHUNK
    tile = RWKV_TILE
    nck = tile // c
    hw = 2 * R_HEAD
    npair = R_HEADS // 2
    ti = pl.program_id(1)

    @pl.when(ti == 0)
    def _():
        g_scr[...] = jnp.zeros_like(g_scr)

    pr = pr_ref[...]
    prev_row = jnp.where(ti == 0, 0.0, prevblk_ref[SUBLANES - 1:SUBLANES, :])
    prev = jnp.where(_iota2((tile, 1), 0) == 0, prev_row, pltpu.roll(pr, 1, axis=0))
    bd = bd_ref[...]
    r, logw, k2, v, av, bv, g = _rwkv_prep(pr, prev, mu_ref[...], w0_ref[...], wup_ref[...],
                                           a0_ref[...], aup_ref[...], gup_ref[...],
                                           kk_ref[...], ka_ref[...], bd)

    sh_c = int(math.log2(c))
    ti_i = _iota2((tile, tile), 0)
    ti_j = _iota2((tile, tile), 1)
    chunk_tril = jnp.where((ti_i >= ti_j) & ((ti_i >> sh_c) == (ti_j >> sh_c)), 1.0, 0.0).astype(BF16)
    cum = _mm_sel_l(chunk_tril, logw, 3)
    cl = jnp.concatenate(
        [jnp.broadcast_to(cum[(q + 1) * c - 1:(q + 1) * c, :], (c, R_WIDTH)) for q in range(nck)], axis=0)
    p_out = jnp.exp(-cum)
    p_end = jnp.exp(cl - cum)
    rt = r * jnp.exp(cum)
    at = av * jnp.exp(cum - logw)
    bt = bv * p_out
    kt = k2 * p_out
    bh = bv * p_end
    kh = k2 * p_end
    pc = jnp.exp(cl)

    ii = _iota2((c, c), 0)
    jj = _iota2((c, c), 1)
    strict = ii > jj
    incl = ii >= jj
    lane = _iota2((1, hw), 1)
    masks = (lane < R_HEAD, lane >= R_HEAD)
    e_r = _iota2((hw, hw), 0)
    e_c = _iota2((hw, hw), 1)
    eye_hw = e_r == e_c
    blockdiag = (e_r >= R_HEAD) == (e_c >= R_HEAD)

    chains = [(q, p, e) for q in range(nck) for p in range(npair) for e in range(2)]
    blk = lambda x, q, p: x[q * c:(q + 1) * c, hw * p:hw * (p + 1)]
    at_m = [jnp.where(masks[e], blk(at, q, p), 0.0) for q, p, e in chains]
    rt_m = [jnp.where(masks[e], blk(rt, q, p), 0.0) for q, p, e in chains]
    lhs = [jnp.concatenate([a, b], axis=0) for a, b in zip(at_m, rt_m)]
    sc_b = [_mm_nt(x, blk(bt, q, p)) for x, (q, p, e) in zip(lhs, chains)]
    sc_k = [_mm_nt(x, blk(kt, q, p)) for x, (q, p, e) in zip(lhs, chains)]
    a_ab = [jnp.where(strict, s[:c], 0.0) for s in sc_b]
    a_rb = [jnp.where(incl, s[c:], 0.0) for s in sc_b]
    a_ak = [jnp.where(strict, s[:c], 0.0) for s in sc_k]
    a_rk = [jnp.where(incl, s[c:], 0.0) for s in sc_k]
    akv = [_mm(a, blk(v, q, p)) for a, (q, p, e) in zip(a_ak, chains)]
    t_inv = _unit_lower_inverses(a_ab)
    tx = [_mm(t, jnp.concatenate([a, b], axis=1)) for t, a, b in zip(t_inv, at_m, akv)]
    rbx = [_mm(a, x) for a, x in zip(a_rb, tx)]
    rkv = [_mm(a, blk(v, q, p)) for a, (q, p, e) in zip(a_rk, chains)]

    at2_pair, u0_pair, rp_pair, y0_pair = {}, {}, {}, {}
    for idx in range(0, len(chains), 2):
        q, p, _ = chains[idx]
        i0, i1 = idx, idx + 1
        at2_pair[q, p] = tx[i0][:, :hw] + tx[i1][:, :hw]
        u0_pair[q, p] = jnp.where(masks[0], tx[i0][:, hw:], tx[i1][:, hw:])
        rp_pair[q, p] = rt_m[i0] + rt_m[i1] + rbx[i0][:, :hw] + rbx[i1][:, :hw]
        y0_pair[q, p] = jnp.where(masks[0], rbx[i0][:, hw:] + rkv[i0], rbx[i1][:, hw:] + rkv[i1])

    qp = [(q, p) for q in range(nck) for p in range(npair)]
    wt = {k: jnp.concatenate([blk(bh, *k), blk(kh, *k)], axis=0).T for k in qp}
    m_ba = {k: jnp.where(blockdiag, _mm(wt[k][:, :c], at2_pair[k]), 0.0) for k in qp}
    h0c = {k: jnp.where(blockdiag,
                        _mm(wt[k], jnp.concatenate([u0_pair[k], blk(v, *k)], axis=0)), 0.0) for k in qp}
    pc_col = {k: jnp.sum(jnp.where(eye_hw, blk(pc, *k)[:1], 0.0), axis=1, keepdims=True) for k in qp}

    state = [g_scr[p] for p in range(npair)]
    y_rows = []
    for q in range(nck):
        y_rows.append(jnp.concatenate(
            [_mm(rp_pair[q, p], state[p]) + y0_pair[q, p] for p in range(npair)], axis=1))
        state = [pc_col[q, p] * state[p] + _mm(m_ba[q, p], state[p]) + h0c[q, p] for p in range(npair)]
    for p in range(npair):
        g_scr[p] = state[p]

    y = jnp.concatenate(y_rows, axis=0)
    o_ref[...] = _rwkv_post(y, r, k2, v, g, rk_ref[...], lnw_ref[...], lnb_ref[...], bd)

    @pl.when(ti == pl.num_programs(1) - 1)
    def _():
        for p in range(npair):
            gt = g_scr[p].T
            sfin_ref[0, 2 * p] = gt[:R_HEAD, :R_HEAD]
            sfin_ref[0, 2 * p + 1] = pltpu.roll(gt, R_HEAD, axis=1)[R_HEAD:, :R_HEAD]


def _const_spec(shape):
    return pl.BlockSpec(shape, lambda *_: (0,) * len(shape))


def _rwkv_prompt(pr2d, batch, seq, wts):
    tile = RWKV_TILE
    ntile = seq // tile
    rows8 = seq // SUBLANES

    def prev_map(b, i):
        return (b * rows8 + jnp.maximum(i * (tile // SUBLANES) - 1, 0), 0)

    w_specs = [_const_spec(w.shape) for w in wts]
    return pl.pallas_call(
        _rwkv_tile_kernel,
        grid=(batch, ntile),
        in_specs=[pl.BlockSpec((tile, RWKV_PROJ), lambda b, i: (b * ntile + i, 0)),
                  pl.BlockSpec((SUBLANES, RWKV_PROJ), prev_map)] + w_specs,
        out_specs=[pl.BlockSpec((tile, R_WIDTH), lambda b, i: (b * ntile + i, 0)),
                   pl.BlockSpec((1, R_HEADS, R_HEAD, R_HEAD), lambda b, i: (b, 0, 0, 0))],
        out_shape=[jax.ShapeDtypeStruct((batch * seq, R_WIDTH), F32),
                   jax.ShapeDtypeStruct((batch, R_HEADS, R_HEAD, R_HEAD), F32)],
        scratch_shapes=[pltpu.VMEM((R_HEADS // 2, 2 * R_HEAD, 2 * R_HEAD), F32)],
        compiler_params=pltpu.CompilerParams(dimension_semantics=("arbitrary", "arbitrary"),
                                             vmem_limit_bytes=VMEM_LIMIT),
        name="rwkv_prompt",
    )(pr2d, pr2d, *wts)


def _rwkv_prep_kernel(pr_ref, prev_ref, mu_ref, w0_ref, wup_ref, a0_ref, aup_ref, gup_ref,
                      kk_ref, ka_ref, bd_ref,
                      r_ref, w_ref, k_ref, v_ref, a_ref, b_ref, g_ref):
    r, logw, k2, v, av, bv, g = _rwkv_prep(pr_ref[...], prev_ref[...], mu_ref[...], w0_ref[...],
                                           wup_ref[...], a0_ref[...], aup_ref[...], gup_ref[...],
                                           kk_ref[...], ka_ref[...], bd_ref[...])
    r_ref[...] = r
    w_ref[...] = jnp.exp(logw)
    k_ref[...] = k2
    v_ref[...] = v
    a_ref[...] = av
    b_ref[...] = bv
    g_ref[...] = g


def _rwkv_sample_prep(pr, prev, wts):
    n = pr.shape[0]
    args = (pr, prev) + tuple(wts)
    return pl.pallas_call(
        _rwkv_prep_kernel,
        grid=(1,),
        in_specs=[_const_spec(a.shape) for a in args],
        out_specs=[_const_spec((n, R_WIDTH))] * 7,
        out_shape=[jax.ShapeDtypeStruct((n, R_WIDTH), F32)] * 7,
        compiler_params=pltpu.CompilerParams(vmem_limit_bytes=VMEM_LIMIT),
        name="rwkv_sample_prep",
    )(*args)


STEP_BATCH = 8


def _per_head_rows(x, n):
    return jnp.concatenate([jnp.broadcast_to(x[h:h + 1, :], (n, x.shape[1])) for h in range(x.shape[0])],
                           axis=0)


def _rows_from_columns(col_b, diag, n):
    picked = jnp.where(diag, col_b, 0.0)
    return jnp.concatenate([jnp.sum(picked[h * n:(h + 1) * n], axis=0, keepdims=True)
                            for h in range(col_b.shape[0] // n)], axis=0)


def _rwkv_step_kernel(r_ref, w_ref, k_ref, v_ref, a_ref, b_ref, g_ref, s_ref,
                      rk_ref, lnw_ref, lnb_ref, o_ref, so_ref):
    inv = 1.0 / R_HEAD
    n = R_HEADS * R_HEAD
    ones = jnp.ones((R_HEAD, R_HEAD), BF16)
    diag = (_iota2((n, R_HEAD), 0) & (R_HEAD - 1)) == _iota2((n, R_HEAD), 1)

    def batch_row(bi, carry):
        rows = pl.ds(pl.multiple_of(bi * R_HEADS, R_HEADS), R_HEADS)
        srows = pl.ds(pl.multiple_of(bi * n, n), n)
        r, w, k, v, a, b, g = (x[rows, :] for x in (r_ref, w_ref, k_ref, v_ref, a_ref, b_ref, g_ref))
        s = s_ref[srows, :]
        sa = _mm(s * _per_head_rows(a, R_HEAD), ones)
        v_col = _mm(jnp.where(diag, _per_head_rows(v, R_HEAD), 0.0), ones)
        s2 = s * _per_head_rows(w, R_HEAD) + sa * _per_head_rows(b, R_HEAD) + v_col * _per_head_rows(k, R_HEAD)
        so_ref[srows, :] = s2
        y = _rows_from_columns(_mm(s2 * _per_head_rows(r, R_HEAD), ones), diag, R_HEAD)
        mu = jnp.sum(y, axis=1, keepdims=True) * inv
        d = y - mu
        var = jnp.sum(d * d, axis=1, keepdims=True) * inv
        yn = d * lax.rsqrt(var + LNX_EPS) * lnw_ref[...] + lnb_ref[...]
        bonus = jnp.sum(r * k * rk_ref[...], axis=1, keepdims=True)
        o_ref[rows, :] = (yn + bonus * v) * g
        return carry

    lax.fori_loop(0, STEP_BATCH, batch_row, 0, unroll=True)


def _rwkv_sample_step(vecs, state, r_k, lnx_w, lnx_b):
    nb = state.shape[0]
    rows = STEP_BATCH * R_HEADS
    vec_spec = pl.BlockSpec((rows, R_HEAD), lambda i: (i, 0))
    st_spec = pl.BlockSpec((rows * R_HEAD, R_HEAD), lambda i: (i, 0))
    w_spec = _const_spec((R_HEADS, R_HEAD))
    return pl.pallas_call(
        _rwkv_step_kernel,
        grid=(nb // STEP_BATCH,),
        in_specs=[vec_spec] * 7 + [st_spec] + [w_spec] * 3,
        out_specs=[vec_spec, st_spec],
        out_shape=[jax.ShapeDtypeStruct((nb * R_HEADS, R_HEAD), F32),
                   jax.ShapeDtypeStruct((nb * R_HEADS * R_HEAD, R_HEAD), F32)],
        compiler_params=pltpu.CompilerParams(dimension_semantics=("arbitrary",),
                                             vmem_limit_bytes=VMEM_LIMIT),
        name="rwkv_sample_step",
    )(*[x.reshape(nb * R_HEADS, R_HEAD) for x in vecs],
      state.reshape(nb * R_HEADS * R_HEAD, R_HEAD), r_k, lnx_w, lnx_b)


def _shift_rows(x, prev8, s):
    rolled = pltpu.roll(x, s, axis=0)
    top = jnp.where(_iota2((SUBLANES, 1), 0) < s, pltpu.roll(prev8, s, axis=0), rolled[:SUBLANES])
    return jnp.concatenate([top, rolled[SUBLANES:]], axis=0)


def _head_expand(lanes_per_head=M_HEADDIM):
    shape = (DT_PAD, M_HEADS * lanes_per_head)
    head_of_lane = _iota2(shape, 1) >> int(math.log2(lanes_per_head))
    return jnp.where(_iota2(shape, 0) == head_of_lane, 1.0, 0.0).astype(BF16)


def _gated_rmsnorm(y, z, gnorm_w):
    yz = y * _silu(z)
    gw = M_WIDTH // M_GROUPS
    outs = []
    for gi in range(M_GROUPS):
        yg = yz[:, gi * gw:(gi + 1) * gw]
        ms = jnp.mean(yg * yg, axis=-1, keepdims=True)
        outs.append(yg * lax.rsqrt(ms + RMS_EPS))
    return jnp.concatenate(outs, axis=1) * gnorm_w


def _mamba_chunk_kernel(z_ref, xbc_ref, xprev_ref, dt_ref, convw_ref, convb_ref, dtb_ref, alog_ref,
                        dskip_ref, gnw_ref, o_ref, hfin_ref, h_scr):
    q = SSD_CHUNK
    ci = pl.program_id(1)

    @pl.when(ci == 0)
    def _():
        h_scr[...] = jnp.zeros_like(h_scr)

    xb = xbc_ref[...]
    prev8 = jnp.where(ci == 0, 0.0, xprev_ref[...])
    conv = convb_ref[...] + xb * convw_ref[CONV_K - 1:CONV_K, :]
    for s in range(1, CONV_K):
        conv = conv + _shift_rows(xb, prev8, s) * convw_ref[CONV_K - 1 - s:CONV_K - s, :]
    xact = _silu(conv)
    xs = xact[:, :M_WIDTH]
    gs = M_GROUPS * D_STATE
    bm = xact[:, M_WIDTH:M_WIDTH + gs]
    cm = xact[:, M_WIDTH + gs:]

    dt = _softplus(dt_ref[...] + dtb_ref[...])
    adt = dt * (-jnp.exp(alog_ref[...]))
    x_dt = xs * _mm_sel_r(dt, _head_expand(), 3)

    ii = _iota2((q, q), 0)
    jj = _iota2((q, q), 1)
    incl = ii >= jj
    cs = _mm_sel_l(jnp.where(incl, 1.0, 0.0).astype(BF16), adt, 3)
    cs_t = cs.T
    lane = _iota2((1, 2 * M_HEADDIM), 1)
    first = lane < M_HEADDIM
    hpg = M_HEADS // M_GROUPS

    y_pairs = []
    for gi in range(M_GROUPS):
        bm_g = bm[:, gi * D_STATE:(gi + 1) * D_STATE]
        cm_g = cm[:, gi * D_STATE:(gi + 1) * D_STATE]
        cb = _mm_nt(cm_g, bm_g)
        bm_t = bm_g.T
        for pp in range(hpg // 2):
            p = gi * (hpg // 2) + pp
            sl = slice(2 * M_HEADDIM * p, 2 * M_HEADDIM * (p + 1))
            x_p = x_dt[:, sl]
            y_diag = jnp.zeros((q, 2 * M_HEADDIM), F32)
            contrib = jnp.zeros((D_STATE, 2 * M_HEADDIM), F32)
            scale_in = []
            scale_end = []
            for e in range(2):
                head = 2 * p + e
                m = first if e == 0 else jnp.logical_not(first)
                a_col = cs[:, head:head + 1]
                a_row = cs_t[head:head + 1, :]
                a_end = cs_t[head:head + 1, q - 1:q]
                lmat = jnp.exp(jnp.where(incl, a_col - a_row, -jnp.inf))
                x_m = jnp.where(m, x_p, 0.0)
                y_diag = y_diag + _mm(cb * lmat, x_m)
                contrib = contrib + _mm(bm_t * jnp.exp(a_end - a_row), x_m)
                scale_in.append(jnp.exp(a_col))
                scale_end.append(jnp.exp(a_end))
            h_in = h_scr[p]
            y_off = _mm(cm_g, h_in) * jnp.where(first, scale_in[0], scale_in[1])
            h_scr[p] = h_in * jnp.where(first, scale_end[0], scale_end[1]) + contrib
            y_pairs.append(y_diag + y_off)

    y = jnp.concatenate(y_pairs, axis=1) + dskip_ref[...] * xs
    o_ref[...] = _gated_rmsnorm(y, z_ref[...], gnw_ref[...])

    @pl.when(ci == pl.num_programs(1) - 1)
    def _():
        for p in range(M_HEADS // 2):
            hfin_ref[0, 2 * M_HEADDIM * p:2 * M_HEADDIM * (p + 1), :] = h_scr[p].T


def _mamba_prompt(z2d, xbc2d, dt2d, batch, seq, wts):
    q = SSD_CHUNK
    nchunk = seq // q
    rows8 = seq // SUBLANES

    def prev_map(b, i):
        return (b * rows8 + jnp.maximum(i * (q // SUBLANES) - 1, 0), 0)

    tok = lambda w: pl.BlockSpec((q, w), lambda b, i: (b * nchunk + i, 0))
    return pl.pallas_call(
        _mamba_chunk_kernel,
        grid=(batch, nchunk),
        in_specs=[tok(M_WIDTH), tok(CONV_DIM), pl.BlockSpec((SUBLANES, CONV_DIM), prev_map), tok(DT_PAD)]
                 + [_const_spec(w.shape) for w in wts],
        out_specs=[tok(M_WIDTH),
                   pl.BlockSpec((1, M_HEADS * M_HEADDIM, D_STATE), lambda b, i: (b, 0, 0))],
        out_shape=[jax.ShapeDtypeStruct((batch * seq, M_WIDTH), F32),
                   jax.ShapeDtypeStruct((batch, M_HEADS * M_HEADDIM, D_STATE), F32)],
        scratch_shapes=[pltpu.VMEM((M_HEADS // 2, D_STATE, 2 * M_HEADDIM), F32)],
        compiler_params=pltpu.CompilerParams(dimension_semantics=("arbitrary", "arbitrary"),
                                             vmem_limit_bytes=VMEM_LIMIT),
        name="mamba_prompt",
    )(z2d, xbc2d, xbc2d, dt2d, *wts)


def _mamba_prep_kernel(xbc_ref, cprev_ref, dt_ref, convw_ref, convb_ref, dtb_ref, alog_ref,
                       xs_ref, xdt_ref, dec_ref, bm_ref, cm_ref):
    conv = convb_ref[...] + xbc_ref[...] * convw_ref[CONV_K - 1:CONV_K, :]
    for j in range(CONV_K - 1):
        conv = conv + cprev_ref[:, j * CONV_DIM:(j + 1) * CONV_DIM] * convw_ref[j:j + 1, :]
    xact = _silu(conv)
    xs = xact[:, :M_WIDTH]
    gs = M_GROUPS * D_STATE
    dt = _softplus(dt_ref[...] + dtb_ref[...])
    adt = dt * (-jnp.exp(alog_ref[...]))
    expand = _head_expand()
    xs_ref[...] = xs
    xdt_ref[...] = xs * _mm_sel_r(dt, expand, 3)
    dec_ref[...] = jnp.exp(_mm_sel_r(adt, _head_expand(D_STATE), 3))
    hpg = M_HEADS // M_GROUPS
    per_head = lambda m: jnp.concatenate(
        [m[:, (e // hpg) * D_STATE:(e // hpg + 1) * D_STATE] for e in range(M_HEADS)], axis=1)
    bm_ref[...] = per_head(xact[:, M_WIDTH:M_WIDTH + gs])
    cm_ref[...] = per_head(xact[:, M_WIDTH + gs:])


def _mamba_sample_prep(xbc, cprev, dt, wts):
    n = xbc.shape[0]
    args = (xbc, cprev, dt) + tuple(wts)
    widths = (M_WIDTH, M_WIDTH, M_HEADS * D_STATE, M_HEADS * D_STATE, M_HEADS * D_STATE)
    return pl.pallas_call(
        _mamba_prep_kernel,
        grid=(1,),
        in_specs=[_const_spec(a.shape) for a in args],
        out_specs=[_const_spec((n, w)) for w in widths],
        out_shape=[jax.ShapeDtypeStruct((n, w), F32) for w in widths],
        compiler_params=pltpu.CompilerParams(vmem_limit_bytes=VMEM_LIMIT),
        name="mamba_sample_prep",
    )(*args)


def _mamba_step_kernel(xs_ref, xdt_ref, dec_ref, bm_ref, cm_ref, h_ref, dskip_ref, y_ref, ho_ref):
    n = M_HEADS * M_HEADDIM
    ones_in = jnp.ones((M_HEADDIM, D_STATE), BF16)
    ones_out = jnp.ones((D_STATE, M_HEADDIM), BF16)
    diag = (_iota2((n, M_HEADDIM), 0) & (M_HEADDIM - 1)) == _iota2((n, M_HEADDIM), 1)

    def batch_row(bi, carry):
        rows = pl.ds(pl.multiple_of(bi * M_HEADS, M_HEADS), M_HEADS)
        hrows = pl.ds(pl.multiple_of(bi * n, n), n)
        x_col = _mm(jnp.where(diag, _per_head_rows(xdt_ref[rows, :], M_HEADDIM), 0.0), ones_in)
        h2 = (h_ref[hrows, :] * _per_head_rows(dec_ref[rows, :], M_HEADDIM)
              + x_col * _per_head_rows(bm_ref[rows, :], M_HEADDIM))
        ho_ref[hrows, :] = h2
        y = _rows_from_columns(_mm(h2 * _per_head_rows(cm_ref[rows, :], M_HEADDIM), ones_out),
                               diag, M_HEADDIM)
        y_ref[rows, :] = y + dskip_ref[...] * xs_ref[rows, :]
        return carry

    lax.fori_loop(0, STEP_BATCH, batch_row, 0, unroll=True)


def _mamba_sample_step(xs, xdt, dec, bm, cm, state, dskip_rows):
    nb = state.shape[0]
    rows = STEP_BATCH * M_HEADS
    vec_spec = pl.BlockSpec((rows, M_HEADDIM), lambda i: (i, 0))
    bc_spec = pl.BlockSpec((rows, D_STATE), lambda i: (i, 0))
    st_spec = pl.BlockSpec((rows * M_HEADDIM, D_STATE), lambda i: (i, 0))
    flat = lambda x, w: x.reshape(nb * M_HEADS, w)
    return pl.pallas_call(
        _mamba_step_kernel,
        grid=(nb // STEP_BATCH,),
        in_specs=[vec_spec] * 2 + [bc_spec] * 3 + [st_spec, _const_spec((M_HEADS, M_HEADDIM))],
        out_specs=[vec_spec, st_spec],
        out_shape=[jax.ShapeDtypeStruct((nb * M_HEADS, M_HEADDIM), F32),
                   jax.ShapeDtypeStruct((nb * M_HEADS * M_HEADDIM, D_STATE), F32)],
        compiler_params=pltpu.CompilerParams(dimension_semantics=("arbitrary",),
                                             vmem_limit_bytes=VMEM_LIMIT),
        name="mamba_sample_step",
    )(flat(xs, M_HEADDIM), flat(xdt, M_HEADDIM), flat(dec, D_STATE), flat(bm, D_STATE),
      flat(cm, D_STATE), state.reshape(nb * M_HEADS * M_HEADDIM, D_STATE), dskip_rows)


FF_SLAB = 1024


def _ffn_kernel(has_gate, *refs):
    if has_gate:
        (x_ref, or_ref, ym_ref, z_ref, gnw_ref, wout_ref, g1_ref, b1_ref, w1_ref, w2_ref,
         g2_ref, b2_ref, y_ref) = refs
        o_m = _gated_rmsnorm(ym_ref[...], z_ref[...], gnw_ref[...])
    else:
        (x_ref, or_ref, om_ref, wout_ref, g1_ref, b1_ref, w1_ref, w2_ref,
         g2_ref, b2_ref, y_ref) = refs
        o_m = om_ref[...]
    mix = _mm(or_ref[...], wout_ref[:R_WIDTH, :]) + _mm(o_m, wout_ref[R_WIDTH:, :])
    h = _layer_norm(ALPHA * x_ref[...] + mix, g1_ref[...], b1_ref[...], LN_EPS)
    hb = h.astype(BF16)
    ff = jnp.zeros_like(h)
    for j in range(D_FF // FF_SLAB):
        t = jnp.dot(hb, w1_ref[:, j * FF_SLAB:(j + 1) * FF_SLAB], preferred_element_type=F32)
        t = jnp.square(jnp.maximum(t, 0.0))
        ff = ff + jnp.dot(t.astype(BF16), w2_ref[j * FF_SLAB:(j + 1) * FF_SLAB, :],
                          preferred_element_type=F32)
    y_ref[...] = _layer_norm(ALPHA * h + ff, g2_ref[...], b2_ref[...], LN_EPS)


def _ffn(x2d, o_r, o_m, gate, wout, g1, b1, w1, w2, g2, b2, tm):
    n = x2d.shape[0]
    tok = lambda w: pl.BlockSpec((tm, w), lambda i: (i, 0))
    once = lambda a: pl.BlockSpec(a.shape, lambda i: (0,) * a.ndim, pipeline_mode=pl.Buffered(1))
    acts = [x2d, o_r, o_m]
    act_specs = [tok(D_MODEL), tok(R_WIDTH), tok(M_WIDTH)]
    if gate is not None:
        acts += [gate[0], gate[1]]
        act_specs += [tok(M_WIDTH), once(gate[1])]
    wts = [wout, g1, b1, w1, w2, g2, b2]
    return pl.pallas_call(
        functools.partial(_ffn_kernel, gate is not None),
        grid=(n // tm,),
        in_specs=act_specs + [once(w) for w in wts],
        out_specs=tok(D_MODEL),
        out_shape=jax.ShapeDtypeStruct((n, D_MODEL), F32),
        compiler_params=pltpu.CompilerParams(dimension_semantics=("arbitrary",),
                                             vmem_limit_bytes=VMEM_LIMIT),
        name="ffn",
    )(*acts, *wts)


def _block_diag_ones():
    idx = np.arange(R_WIDTH) // R_HEAD
    return jnp.asarray(idx[:, None] == idx[None, :], dtype=BF16)


def kernel(x_prompt, x_sample, state_shift, state_wkv, state_conv, state_ssm, w_in, mu_shift, w0, w_up, a0, a_up, g_up, k_k, k_a, r_k, lnx_w, lnx_b, conv_w, conv_b, dt_bias, a_log, d_skip, gnorm_w, w_out, ln1_g, ln1_b, w_ff1, w_ff2, ln2_g, ln2_b):
    assert w_in.shape[0] == DEPTH and x_sample.shape[1] == 1
    batch, seq, _ = x_prompt.shape
    nb = x_sample.shape[0]
    assert seq % SSD_CHUNK == 0 and seq % RWKV_TILE == 0 and nb % STEP_BATCH == 0
    row = lambda a: a.reshape(1, -1)

    w_in_p = jnp.pad(w_in[0], ((0, 0), (0, PROJ_PAD - w_in.shape[2]))).astype(BF16)
    zl = jnp.zeros((D_DECAY_LORA, R_WIDTH), F32)
    wup_p = jnp.concatenate([w_up[0], zl], axis=0).astype(BF16)
    aup_p = jnp.concatenate([zl, a_up[0]], axis=0).astype(BF16)
    bd = _block_diag_ones()
    prep_w = (row(mu_shift[0]), row(w0[0]), wup_p, row(a0[0]), aup_p, g_up[0].astype(BF16),
              row(k_k[0]), row(k_a[0]), bd)
    post_w = (row(r_k[0]), row(lnx_w[0]), row(lnx_b[0]))
    pad_dt = lambda a: jnp.pad(a[0], (0, DT_PAD - M_HEADS)).reshape(1, DT_PAD)
    dskip_lanes = jnp.repeat(d_skip[0], M_HEADDIM).reshape(1, M_WIDTH)
    dskip_rows = jnp.broadcast_to(d_skip[0][:, None], (M_HEADS, M_HEADDIM))
    conv_wts = (conv_w[0], row(conv_b[0]), pad_dt(dt_bias), pad_dt(a_log))
    ffn_w = (w_out[0].astype(BF16), row(ln1_g[0]), row(ln1_b[0]), w_ff1[0].astype(BF16),
             w_ff2[0].astype(BF16), row(ln2_g[0]), row(ln2_b[0]))

    xp = x_prompt.reshape(batch * seq, D_MODEL)
    pr_p, z_p, xbc_p, dt_p = _proj(xp, w_in_p, 256)
    o_r_p, wkv_p = _rwkv_prompt(pr_p, batch, seq, prep_w[:-1] + post_w + (bd,))
    o_m_p, ssm_p = _mamba_prompt(z_p, xbc_p, dt_p, batch, seq,
                                 conv_wts + (dskip_lanes, row(gnorm_w[0])))
    y_p = _ffn(xp, o_r_p, o_m_p, None, *ffn_w, tm=256)
    shift_p = pr_p.reshape(batch, seq, RWKV_PROJ)[:, -1]
    conv_p = xbc_p.reshape(batch, seq, CONV_DIM)[:, seq - (CONV_K - 1):]

    xs2d = x_sample.reshape(nb, D_MODEL)
    pr_s, z_s, xbc_s, dt_s = _proj(xs2d, w_in_p, nb)
    vecs = _rwkv_sample_prep(pr_s, state_shift[0], prep_w)
    o_r_s, wkv_s = _rwkv_sample_step(vecs, state_wkv[0], r_k[0], lnx_w[0].reshape(R_HEADS, R_HEAD),
                                     lnx_b[0].reshape(R_HEADS, R_HEAD))
    cprev = state_conv[0].reshape(nb, (CONV_K - 1) * CONV_DIM)
    xs_s, xdt_s, dec_s, bm_s, cm_s = _mamba_sample_prep(xbc_s, cprev, dt_s, conv_wts)
    y_m_s, ssm_s = _mamba_sample_step(xs_s, xdt_s, dec_s, bm_s, cm_s, state_ssm[0], dskip_rows)
    y_s = _ffn(xs2d, o_r_s.reshape(nb, R_WIDTH), y_m_s.reshape(nb, M_WIDTH),
               (z_s, row(gnorm_w[0])), *ffn_w, tm=nb)
    conv_s = jnp.concatenate([state_conv[0][:, 1:], xbc_s[:, None, :]], axis=1)

    return (y_p.reshape(batch, seq, D_MODEL), y_s.reshape(nb, 1, D_MODEL),
            shift_p[None], wkv_p[None], conv_p[None],
            ssm_p.reshape(1, batch, M_HEADS, M_HEADDIM, D_STATE),
            pr_s[None], wkv_s.reshape(1, nb, R_HEADS, R_HEAD, R_HEAD), conv_s[None],
            ssm_s.reshape(1, nb, M_HEADS, M_HEADDIM, D_STATE))
```

```python
import functools
import math

import numpy as np
import jax
import jax.numpy as jnp
from jax import lax
from jax.experimental import pallas as pl
from jax.experimental.pallas import tpu as pltpu

F32 = jnp.float32
BF16 = jnp.bfloat16

D_MODEL = 1024
R_WIDTH = 512
R_HEAD = 64
R_HEADS = 8
D_DECAY_LORA = 64
D_AAA_LORA = 64
D_GATE_LORA = 128
M_WIDTH = 512
M_HEADDIM = 64
M_HEADS = 8
M_GROUPS = 2
D_STATE = 128
CONV_K = 4
CONV_DIM = M_WIDTH + 2 * M_GROUPS * D_STATE
D_FF = 4 * D_MODEL
RWKV_PROJ = 3 * R_WIDTH + D_DECAY_LORA + D_AAA_LORA + D_GATE_LORA
DT_PAD = 128
PROJ_PAD = RWKV_PROJ + M_WIDTH + CONV_DIM + DT_PAD
DEPTH = 1
ALPHA = (2 * DEPTH) ** 0.25
LN_EPS = 1e-5
LNX_EPS = 64e-5
RMS_EPS = 1e-5

RWKV_CHUNK = 64
RWKV_TILE = 256
SSD_CHUNK = 256
SUBLANES = 8
VMEM_LIMIT = 48 * 1024 * 1024

_NT = (((1,), (1,)), ((), ()))


def _mm(a, b):
    return jnp.dot(a.astype(BF16), b.astype(BF16), preferred_element_type=F32)


def _mm_nt(a, b):
    return lax.dot_general(a.astype(BF16), b.astype(BF16), _NT, preferred_element_type=F32)


def _split(x, passes):
    parts = []
    for _ in range(passes):
        p = x.astype(BF16)
        parts.append(p)
        x = x - p.astype(F32)
    return parts


def _mm_sel_r(x, sel, passes):
    return sum(jnp.dot(p, sel, preferred_element_type=F32) for p in _split(x, passes))


def _mm_sel_l(sel, x, passes):
    return sum(jnp.dot(sel, p, preferred_element_type=F32) for p in _split(x, passes))


def _softplus(x):
    return jnp.maximum(x, 0.0) + jnp.log(1.0 + jnp.exp(-jnp.abs(x)))


def _sigmoid(x):
    return 1.0 / (1.0 + jnp.exp(-x))


def _silu(x):
    return x * _sigmoid(x)


def _layer_norm(x, g, b, eps):
    mu = jnp.mean(x, axis=-1, keepdims=True)
    d = x - mu
    var = jnp.mean(d * d, axis=-1, keepdims=True)
    return d * lax.rsqrt(var + eps) * g + b


def _iota2(shape, axis):
    return lax.broadcasted_iota(jnp.int32, shape, axis)


def _proj_kernel(x_ref, w_ref, pr_ref, z_ref, xbc_ref, dt_ref):
    res = jnp.dot(x_ref[...].astype(BF16), w_ref[...], preferred_element_type=F32)
    c0 = RWKV_PROJ
    c1 = c0 + M_WIDTH
    c2 = c1 + CONV_DIM
    pr_ref[...] = res[:, :c0]
    z_ref[...] = res[:, c0:c1]
    xbc_ref[...] = res[:, c1:c2]
    dt_ref[...] = res[:, c2:]


def _proj(x2d, w_bf, tm):
    n = x2d.shape[0]
    widths = (RWKV_PROJ, M_WIDTH, CONV_DIM, DT_PAD)
    return pl.pallas_call(
        _proj_kernel,
        grid=(n // tm,),
        in_specs=[pl.BlockSpec((tm, D_MODEL), lambda i: (i, 0)),
                  pl.BlockSpec((D_MODEL, PROJ_PAD), lambda i: (0, 0))],
        out_specs=[pl.BlockSpec((tm, w), lambda i: (i, 0)) for w in widths],
        out_shape=[jax.ShapeDtypeStruct((n, w), F32) for w in widths],
        compiler_params=pltpu.CompilerParams(dimension_semantics=("arbitrary",),
                                             vmem_limit_bytes=VMEM_LIMIT),
        name="proj",
    )(x2d, w_bf)


def _rwkv_prep(pr, prev, mu, w0, wup, a0, aup, gup, k_k, k_a, bd):
    u = pr + (prev - pr) * mu
    r = u[:, 0:R_WIDTH]
    k = u[:, R_WIDTH:2 * R_WIDTH]
    v = u[:, 2 * R_WIDTH:3 * R_WIDTH]
    lora = u[:, 3 * R_WIDTH:3 * R_WIDTH + D_DECAY_LORA + D_AAA_LORA]
    ug = u[:, 3 * R_WIDTH + D_DECAY_LORA + D_AAA_LORA:]
    w_log = -_softplus(-(w0 + _mm(jnp.tanh(lora), wup))) - 0.5
    logw = -jnp.exp(w_log)
    a = _sigmoid(a0 + _mm(lora, aup))
    g = _mm(_sigmoid(ug), gup)
    kk = k * k_k
    ss = _mm_sel_r(kk * kk, bd, 1)
    kkn = kk * lax.rsqrt(jnp.maximum(ss, 1e-24))
    k2 = k * (1.0 + (a - 1.0) * k_a)
    return r, logw, k2, v, -kkn, kkn * a, g


def _rwkv_post(y, r, k2, v, g, r_k, lnx_w, lnx_b, bd):
    inv = 1.0 / R_HEAD
    mu = _mm_sel_r(y, bd, 1) * inv
    d = y - mu
    var = _mm_sel_r(d * d, bd, 1) * inv
    yn = d * lax.rsqrt(var + LNX_EPS) * lnx_w + lnx_b
    bonus = _mm_sel_r(r * k2 * r_k, bd, 1)
    return (yn + bonus * v) * g


def _unit_lower_inverses(n_list):
    c = RWKV_CHUNK
    ii = _iota2((c, c), 0)
    jj = _iota2((c, c), 1)
    sh = 4
    same = (ii >> sh) == (jj >> sh)
    eye = jnp.where(ii == jj, 1.0, 0.0)
    npow = [jnp.where(same, n, 0.0) for n in n_list]
    t = [eye + n for n in npow]
    for _ in range(sh - 1):
        npow = [_mm(n, n) for n in npow]
        t = [ti + _mm(ti, n) for ti, n in zip(t, npow)]
    while (1 << sh) < c:
        off = ((ii >> (sh + 1)) == (jj >> (sh + 1))) & (((ii >> sh) & 1) > ((jj >> sh) & 1))
        tn = [_mm(ti, jnp.where(off, n, 0.0)) for ti, n in zip(t, n_list)]
        t = [ti + _mm(x, ti) for ti, x in zip(t, tn)]
        sh += 1
    return t


def _rwkv_tile_kernel(pr_ref, prevblk_ref, mu_ref, w0_ref, wup_ref, a0_ref, aup_ref, gup_ref,
                      kk_ref, ka_ref, rk_ref, lnw_ref, lnb_ref, bd_ref,
                      o_ref, sfin_ref, g_scr):
    c = RWKV_CHUNK
    tile = RWKV_TILE
    nck = tile // c
    hw = 2 * R_HEAD
    npair = R_HEADS // 2
    ti = pl.program_id(1)

    @pl.when(ti == 0)
    def _():
        g_scr[...] = jnp.zeros_like(g_scr)

    pr = pr_ref[...]
    prev_row = jnp.where(ti == 0, 0.0, prevblk_ref[SUBLANES - 1:SUBLANES, :])
    prev = jnp.where(_iota2((tile, 1), 0) == 0, prev_row, pltpu.roll(pr, 1, axis=0))
    bd = bd_ref[...]
    r, logw, k2, v, av, bv, g = _rwkv_prep(pr, prev, mu_ref[...], w0_ref[...], wup_ref[...],
                                           a0_ref[...], aup_ref[...], gup_ref[...],
                                           kk_ref[...], ka_ref[...], bd)

    sh_c = int(math.log2(c))
    ti_i = _iota2((tile, tile), 0)
    ti_j = _iota2((tile, tile), 1)
    chunk_tril = jnp.where((ti_i >= ti_j) & ((ti_i >> sh_c) == (ti_j >> sh_c)), 1.0, 0.0).astype(BF16)
    cum = _mm_sel_l(chunk_tril, logw, 3)
    cl = jnp.concatenate(
        [jnp.broadcast_to(cum[(q + 1) * c - 1:(q + 1) * c, :], (c, R_WIDTH)) for q in range(nck)], axis=0)
    p_out = jnp.exp(-cum)
    p_end = jnp.exp(cl - cum)
    rt = r * jnp.exp(cum)
    at = av * jnp.exp(cum - logw)
    bt = bv * p_out
    kt = k2 * p_out
    bh = bv * p_end
    kh = k2 * p_end
    pc = jnp.exp(cl)

    ii = _iota2((c, c), 0)
    jj = _iota2((c, c), 1)
    strict = ii > jj
    incl = ii >= jj
    lane = _iota2((1, hw), 1)
    masks = (lane < R_HEAD, lane >= R_HEAD)
    e_r = _iota2((hw, hw), 0)
    e_c = _iota2((hw, hw), 1)
    eye_hw = e_r == e_c
    blockdiag = (e_r >= R_HEAD) == (e_c >= R_HEAD)

    chains = [(q, p, e) for q in range(nck) for p in range(npair) for e in range(2)]
    blk = lambda x, q, p: x[q * c:(q + 1) * c, hw * p:hw * (p + 1)]
    at_m = [jnp.where(masks[e], blk(at, q, p), 0.0) for q, p, e in chains]
    rt_m = [jnp.where(masks[e], blk(rt, q, p), 0.0) for q, p, e in chains]
    lhs = [jnp.concatenate([a, b], axis=0) for a, b in zip(at_m, rt_m)]
    sc_b = [_mm_nt(x, blk(bt, q, p)) for x, (q, p, e) in zip(lhs, chains)]
    sc_k = [_mm_nt(x, blk(kt, q, p)) for x, (q, p, e) in zip(lhs, chains)]
    a_ab = [jnp.where(strict, s[:c], 0.0) for s in sc_b]
    a_rb = [jnp.where(incl, s[c:], 0.0) for s in sc_b]
    a_ak = [jnp.where(strict, s[:c], 0.0) for s in sc_k]
    a_rk = [jnp.where(incl, s[c:], 0.0) for s in sc_k]
    akv = [_mm(a, blk(v, q, p)) for a, (q, p, e) in zip(a_ak, chains)]
    t_inv = _unit_lower_inverses(a_ab)
    tx = [_mm(t, jnp.concatenate([a, b], axis=1)) for t, a, b in zip(t_inv, at_m, akv)]
    rbx = [_mm(a, x) for a, x in zip(a_rb, tx)]
    rkv = [_mm(a, blk(v, q, p)) for a, (q, p, e) in zip(a_rk, chains)]

    at2_pair, u0_pair, rp_pair, y0_pair = {}, {}, {}, {}
    for idx in range(0, len(chains), 2):
        q, p, _ = chains[idx]
        i0, i1 = idx, idx + 1
        at2_pair[q, p] = tx[i0][:, :hw] + tx[i1][:, :hw]
        u0_pair[q, p] = jnp.where(masks[0], tx[i0][:, hw:], tx[i1][:, hw:])
        rp_pair[q, p] = rt_m[i0] + rt_m[i1] + rbx[i0][:, :hw] + rbx[i1][:, :hw]
        y0_pair[q, p] = jnp.where(masks[0], rbx[i0][:, hw:] + rkv[i0], rbx[i1][:, hw:] + rkv[i1])

    qp = [(q, p) for q in range(nck) for p in range(npair)]
    wt = {k: jnp.concatenate([blk(bh, *k), blk(kh, *k)], axis=0).T for k in qp}
    m_ba = {k: jnp.where(blockdiag, _mm(wt[k][:, :c], at2_pair[k]), 0.0) for k in qp}
    h0c = {k: jnp.where(blockdiag,
                        _mm(wt[k], jnp.concatenate([u0_pair[k], blk(v, *k)], axis=0)), 0.0) for k in qp}
    pc_col = {k: jnp.sum(jnp.where(eye_hw, blk(pc, *k)[:1], 0.0), axis=1, keepdims=True) for k in qp}

    state = [g_scr[p] for p in range(npair)]
    y_rows = []
    for q in range(nck):
        y_rows.append(jnp.concatenate(
            [_mm(rp_pair[q, p], state[p]) + y0_pair[q, p] for p in range(npair)], axis=1))
        state = [pc_col[q, p] * state[p] + _mm(m_ba[q, p], state[p]) + h0c[q, p] for p in range(npair)]
    for p in range(npair):
        g_scr[p] = state[p]

    y = jnp.concatenate(y_rows, axis=0)
    o_ref[...] = _rwkv_post(y, r, k2, v, g, rk_ref[...], lnw_ref[...], lnb_ref[...], bd)

    @pl.when(ti == pl.num_programs(1) - 1)
    def _():
        for p in range(npair):
            gt = g_scr[p].T
            sfin_ref[0, 2 * p] = gt[:R_HEAD, :R_HEAD]
            sfin_ref[0, 2 * p + 1] = pltpu.roll(gt, R_HEAD, axis=1)[R_HEAD:, :R_HEAD]


def _const_spec(shape):
    return pl.BlockSpec(shape, lambda *_: (0,) * len(shape))


def _rwkv_prompt(pr2d, batch, seq, wts):
    tile = RWKV_TILE
    ntile = seq // tile
    rows8 = seq // SUBLANES

    def prev_map(b, i):
        return (b * rows8 + jnp.maximum(i * (tile // SUBLANES) - 1, 0), 0)

    w_specs = [_const_spec(w.shape) for w in wts]
    return pl.pallas_call(
        _rwkv_tile_kernel,
        grid=(batch, ntile),
        in_specs=[pl.BlockSpec((tile, RWKV_PROJ), lambda b, i: (b * ntile + i, 0)),
                  pl.BlockSpec((SUBLANES, RWKV_PROJ), prev_map)] + w_specs,
        out_specs=[pl.BlockSpec((tile, R_WIDTH), lambda b, i: (b * ntile + i, 0)),
                   pl.BlockSpec((1, R_HEADS, R_HEAD, R_HEAD), lambda b, i: (b, 0, 0, 0))],
        out_shape=[jax.ShapeDtypeStruct((batch * seq, R_WIDTH), F32),
                   jax.ShapeDtypeStruct((batch, R_HEADS, R_HEAD, R_HEAD), F32)],
        scratch_shapes=[pltpu.VMEM((R_HEADS // 2, 2 * R_HEAD, 2 * R_HEAD), F32)],
        compiler_params=pltpu.CompilerParams(dimension_semantics=("arbitrary", "arbitrary"),
                                             vmem_limit_bytes=VMEM_LIMIT),
        name="rwkv_prompt",
    )(pr2d, pr2d, *wts)


def _rwkv_sample_kernel(pr_ref, prev_ref, mu_ref, w0_ref, wup_ref, a0_ref, aup_ref, gup_ref,
                        kk_ref, ka_ref, bd_ref, rk_ref, lnw_ref, lnb_ref, s_ref,
                        o_ref, so_ref, vec_scr):
    h = pl.program_id(0)

    @pl.when(h == 0)
    def _():
        r, logw, k2, v, av, bv, g = _rwkv_prep(pr_ref[...], prev_ref[...], mu_ref[...], w0_ref[...],
                                               wup_ref[...], a0_ref[...], aup_ref[...], gup_ref[...],
                                               kk_ref[...], ka_ref[...], bd_ref[...])
        for i, x in enumerate((r, jnp.exp(logw), k2, v, av, bv, g)):
            vec_scr[i] = x.T

    rows = pl.ds(pl.multiple_of(h * R_HEAD, R_HEAD), R_HEAD)
    r, w, k, v, a, b, g = (vec_scr[i, rows, :] for i in range(7))
    ys = []
    for vi in range(R_HEAD):
        s = s_ref[0, vi]
        sa = jnp.sum(s * a, axis=0, keepdims=True)
        s2 = s * w + sa * b + v[vi:vi + 1, :] * k
        so_ref[0, vi] = s2
        ys.append(jnp.sum(s2 * r, axis=0, keepdims=True))
    y = jnp.concatenate(ys, axis=0)
    mu = jnp.mean(y, axis=0, keepdims=True)
    d = y - mu
    var = jnp.mean(d * d, axis=0, keepdims=True)
    yn = d * lax.rsqrt(var + LNX_EPS) * lnw_ref[...] + lnb_ref[...]
    bonus = jnp.sum(r * k * rk_ref[...], axis=0, keepdims=True)
    o_ref[...] = (yn + bonus * v) * g


def _rwkv_sample(pr, prev, state_t, prep_w, post_cols):
    nb = pr.shape[0]
    args = (pr, prev) + tuple(prep_w)
    head_rows = pl.BlockSpec((R_HEAD, nb), lambda h: (h, 0))
    st_spec = pl.BlockSpec((1, R_HEAD, R_HEAD, nb), lambda h: (h, 0, 0, 0))
    return pl.pallas_call(
        _rwkv_sample_kernel,
        grid=(R_HEADS,),
        in_specs=[_const_spec(a.shape) for a in args] + [head_rows] * 3 + [st_spec],
        out_specs=[head_rows, st_spec],
        out_shape=[jax.ShapeDtypeStruct((R_WIDTH, nb), F32),
                   jax.ShapeDtypeStruct(state_t.shape, F32)],
        scratch_shapes=[pltpu.VMEM((7, R_WIDTH, nb), F32)],
        compiler_params=pltpu.CompilerParams(dimension_semantics=("arbitrary",),
                                             vmem_limit_bytes=VMEM_LIMIT),
        name="rwkv_sample",
    )(*args, *post_cols, state_t)


STEP_BATCH = 8


def _per_head_rows(x, n):
    return jnp.concatenate([jnp.broadcast_to(x[h:h + 1, :], (n, x.shape[1])) for h in range(x.shape[0])],
                           axis=0)


def _rows_from_columns(col_b, diag, n):
    picked = jnp.where(diag, col_b, 0.0)
    return jnp.concatenate([jnp.sum(picked[h * n:(h + 1) * n], axis=0, keepdims=True)
                            for h in range(col_b.shape[0] // n)], axis=0)


def _shift_rows(x, prev8, s):
    rolled = pltpu.roll(x, s, axis=0)
    top = jnp.where(_iota2((SUBLANES, 1), 0) < s, pltpu.roll(prev8, s, axis=0), rolled[:SUBLANES])
    return jnp.concatenate([top, rolled[SUBLANES:]], axis=0)


def _head_expand(lanes_per_head=M_HEADDIM):
    shape = (DT_PAD, M_HEADS * lanes_per_head)
    head_of_lane = _iota2(shape, 1) >> int(math.log2(lanes_per_head))
    return jnp.where(_iota2(shape, 0) == head_of_lane, 1.0, 0.0).astype(BF16)


def _gated_rmsnorm(y, z, gnorm_w):
    yz = y * _silu(z)
    gw = M_WIDTH // M_GROUPS
    outs = []
    for gi in range(M_GROUPS):
        yg = yz[:, gi * gw:(gi + 1) * gw]
        ms = jnp.mean(yg * yg, axis=-1, keepdims=True)
        outs.append(yg * lax.rsqrt(ms + RMS_EPS))
    return jnp.concatenate(outs, axis=1) * gnorm_w


def _mamba_chunk_kernel(z_ref, xbc_ref, xprev_ref, dt_ref, convw_ref, convb_ref, dtb_ref, alog_ref,
                        dskip_ref, gnw_ref, o_ref, hfin_ref, h_scr):
    q = SSD_CHUNK
    ci = pl.program_id(1)

    @pl.when(ci == 0)
    def _():
        h_scr[...] = jnp.zeros_like(h_scr)

    xb = xbc_ref[...]
    prev8 = jnp.where(ci == 0, 0.0, xprev_ref[...])
    conv = convb_ref[...] + xb * convw_ref[CONV_K - 1:CONV_K, :]
    for s in range(1, CONV_K):
        conv = conv + _shift_rows(xb, prev8, s) * convw_ref[CONV_K - 1 - s:CONV_K - s, :]
    xact = _silu(conv)
    xs = xact[:, :M_WIDTH]
    gs = M_GROUPS * D_STATE
    bm = xact[:, M_WIDTH:M_WIDTH + gs]
    cm = xact[:, M_WIDTH + gs:]

    dt = _softplus(dt_ref[...] + dtb_ref[...])
    adt = dt * (-jnp.exp(alog_ref[...]))
    x_dt = xs * _mm_sel_r(dt, _head_expand(), 3)

    ii = _iota2((q, q), 0)
    jj = _iota2((q, q), 1)
    incl = ii >= jj
    cs = _mm_sel_l(jnp.where(incl, 1.0, 0.0).astype(BF16), adt, 3)
    cs_t = cs.T
    lane = _iota2((1, 2 * M_HEADDIM), 1)
    first = lane < M_HEADDIM
    hpg = M_HEADS // M_GROUPS

    y_pairs = []
    for gi in range(M_GROUPS):
        bm_g = bm[:, gi * D_STATE:(gi + 1) * D_STATE]
        cm_g = cm[:, gi * D_STATE:(gi + 1) * D_STATE]
        cb = _mm_nt(cm_g, bm_g)
        bm_t = bm_g.T
        for pp in range(hpg // 2):
            p = gi * (hpg // 2) + pp
            sl = slice(2 * M_HEADDIM * p, 2 * M_HEADDIM * (p + 1))
            x_p = x_dt[:, sl]
            y_diag = jnp.zeros((q, 2 * M_HEADDIM), F32)
            contrib = jnp.zeros((D_STATE, 2 * M_HEADDIM), F32)
            scale_in = []
            scale_end = []
            for e in range(2):
                head = 2 * p + e
                m = first if e == 0 else jnp.logical_not(first)
                a_col = cs[:, head:head + 1]
                a_row = cs_t[head:head + 1, :]
                a_end = cs_t[head:head + 1, q - 1:q]
                lmat = jnp.exp(jnp.where(incl, a_col - a_row, -jnp.inf))
                x_m = jnp.where(m, x_p, 0.0)
                y_diag = y_diag + _mm(cb * lmat, x_m)
                contrib = contrib + _mm(bm_t * jnp.exp(a_end - a_row), x_m)
                scale_in.append(jnp.exp(a_col))
                scale_end.append(jnp.exp(a_end))
            h_in = h_scr[p]
            y_off = _mm(cm_g, h_in) * jnp.where(first, scale_in[0], scale_in[1])
            h_scr[p] = h_in * jnp.where(first, scale_end[0], scale_end[1]) + contrib
            y_pairs.append(y_diag + y_off)

    y = jnp.concatenate(y_pairs, axis=1) + dskip_ref[...] * xs
    o_ref[...] = _gated_rmsnorm(y, z_ref[...], gnw_ref[...])

    @pl.when(ci == pl.num_programs(1) - 1)
    def _():
        for p in range(M_HEADS // 2):
            hfin_ref[0, 2 * M_HEADDIM * p:2 * M_HEADDIM * (p + 1), :] = h_scr[p].T


def _mamba_prompt(z2d, xbc2d, dt2d, batch, seq, wts):
    q = SSD_CHUNK
    nchunk = seq // q
    rows8 = seq // SUBLANES

    def prev_map(b, i):
        return (b * rows8 + jnp.maximum(i * (q // SUBLANES) - 1, 0), 0)

    tok = lambda w: pl.BlockSpec((q, w), lambda b, i: (b * nchunk + i, 0))
    return pl.pallas_call(
        _mamba_chunk_kernel,
        grid=(batch, nchunk),
        in_specs=[tok(M_WIDTH), tok(CONV_DIM), pl.BlockSpec((SUBLANES, CONV_DIM), prev_map), tok(DT_PAD)]
                 + [_const_spec(w.shape) for w in wts],
        out_specs=[tok(M_WIDTH),
                   pl.BlockSpec((1, M_HEADS * M_HEADDIM, D_STATE), lambda b, i: (b, 0, 0))],
        out_shape=[jax.ShapeDtypeStruct((batch * seq, M_WIDTH), F32),
                   jax.ShapeDtypeStruct((batch, M_HEADS * M_HEADDIM, D_STATE), F32)],
        scratch_shapes=[pltpu.VMEM((M_HEADS // 2, D_STATE, 2 * M_HEADDIM), F32)],
        compiler_params=pltpu.CompilerParams(dimension_semantics=("arbitrary", "arbitrary"),
                                             vmem_limit_bytes=VMEM_LIMIT),
        name="mamba_prompt",
    )(z2d, xbc2d, xbc2d, dt2d, *wts)


def _mamba_prep_kernel(xbc_ref, cprev_ref, dt_ref, convw_ref, convb_ref, dtb_ref, alog_ref,
                       xs_ref, xdt_ref, dec_ref, bm_ref, cm_ref):
    conv = convb_ref[...] + xbc_ref[...] * convw_ref[CONV_K - 1:CONV_K, :]
    for j in range(CONV_K - 1):
        conv = conv + cprev_ref[:, j * CONV_DIM:(j + 1) * CONV_DIM] * convw_ref[j:j + 1, :]
    xact = _silu(conv)
    xs = xact[:, :M_WIDTH]
    gs = M_GROUPS * D_STATE
    dt = _softplus(dt_ref[...] + dtb_ref[...])
    adt = dt * (-jnp.exp(alog_ref[...]))
    expand = _head_expand()
    xs_ref[...] = xs
    xdt_ref[...] = xs * _mm_sel_r(dt, expand, 3)
    dec_ref[...] = jnp.exp(_mm_sel_r(adt, _head_expand(D_STATE), 3))
    hpg = M_HEADS // M_GROUPS
    per_head = lambda m: jnp.concatenate(
        [m[:, (e // hpg) * D_STATE:(e // hpg + 1) * D_STATE] for e in range(M_HEADS)], axis=1)
    bm_ref[...] = per_head(xact[:, M_WIDTH:M_WIDTH + gs])
    cm_ref[...] = per_head(xact[:, M_WIDTH + gs:])


def _mamba_sample_prep(xbc, cprev, dt, wts):
    n = xbc.shape[0]
    args = (xbc, cprev, dt) + tuple(wts)
    widths = (M_WIDTH, M_WIDTH, M_HEADS * D_STATE, M_HEADS * D_STATE, M_HEADS * D_STATE)
    return pl.pallas_call(
        _mamba_prep_kernel,
        grid=(1,),
        in_specs=[_const_spec(a.shape) for a in args],
        out_specs=[_const_spec((n, w)) for w in widths],
        out_shape=[jax.ShapeDtypeStruct((n, w), F32) for w in widths],
        compiler_params=pltpu.CompilerParams(vmem_limit_bytes=VMEM_LIMIT),
        name="mamba_sample_prep",
    )(*args)


def _mamba_step_kernel(xs_ref, xdt_ref, dec_ref, bm_ref, cm_ref, h_ref, dskip_ref, y_ref, ho_ref):
    n = M_HEADS * M_HEADDIM
    ones_in = jnp.ones((M_HEADDIM, D_STATE), BF16)
    ones_out = jnp.ones((D_STATE, M_HEADDIM), BF16)
    diag = (_iota2((n, M_HEADDIM), 0) & (M_HEADDIM - 1)) == _iota2((n, M_HEADDIM), 1)

    def batch_row(bi, carry):
        rows = pl.ds(pl.multiple_of(bi * M_HEADS, M_HEADS), M_HEADS)
        hrows = pl.ds(pl.multiple_of(bi * n, n), n)
        x_col = _mm(jnp.where(diag, _per_head_rows(xdt_ref[rows, :], M_HEADDIM), 0.0), ones_in)
        h2 = (h_ref[hrows, :] * _per_head_rows(dec_ref[rows, :], M_HEADDIM)
              + x_col * _per_head_rows(bm_ref[rows, :], M_HEADDIM))
        ho_ref[hrows, :] = h2
        y = _rows_from_columns(_mm(h2 * _per_head_rows(cm_ref[rows, :], M_HEADDIM), ones_out),
                               diag, M_HEADDIM)
        y_ref[rows, :] = y + dskip_ref[...] * xs_ref[rows, :]
        return carry

    lax.fori_loop(0, STEP_BATCH, batch_row, 0, unroll=True)


def _mamba_sample_step(xs, xdt, dec, bm, cm, state, dskip_rows):
    nb = state.shape[0]
    rows = STEP_BATCH * M_HEADS
    vec_spec = pl.BlockSpec((rows, M_HEADDIM), lambda i: (i, 0))
    bc_spec = pl.BlockSpec((rows, D_STATE), lambda i: (i, 0))
    st_spec = pl.BlockSpec((rows * M_HEADDIM, D_STATE), lambda i: (i, 0))
    flat = lambda x, w: x.reshape(nb * M_HEADS, w)
    return pl.pallas_call(
        _mamba_step_kernel,
        grid=(nb // STEP_BATCH,),
        in_specs=[vec_spec] * 2 + [bc_spec] * 3 + [st_spec, _const_spec((M_HEADS, M_HEADDIM))],
        out_specs=[vec_spec, st_spec],
        out_shape=[jax.ShapeDtypeStruct((nb * M_HEADS, M_HEADDIM), F32),
                   jax.ShapeDtypeStruct((nb * M_HEADS * M_HEADDIM, D_STATE), F32)],
        compiler_params=pltpu.CompilerParams(dimension_semantics=("arbitrary",),
                                             vmem_limit_bytes=VMEM_LIMIT),
        name="mamba_sample_step",
    )(flat(xs, M_HEADDIM), flat(xdt, M_HEADDIM), flat(dec, D_STATE), flat(bm, D_STATE),
      flat(cm, D_STATE), state.reshape(nb * M_HEADS * M_HEADDIM, D_STATE), dskip_rows)


FF_SLAB = 1024


def _ffn_kernel(has_gate, *refs):
    if has_gate:
        (x_ref, or_ref, ym_ref, z_ref, gnw_ref, wout_ref, g1_ref, b1_ref, w1_ref, w2_ref,
         g2_ref, b2_ref, y_ref) = refs
        o_m = _gated_rmsnorm(ym_ref[...], z_ref[...], gnw_ref[...])
    else:
        (x_ref, or_ref, om_ref, wout_ref, g1_ref, b1_ref, w1_ref, w2_ref,
         g2_ref, b2_ref, y_ref) = refs
        o_m = om_ref[...]
    mix = _mm(or_ref[...], wout_ref[:R_WIDTH, :]) + _mm(o_m, wout_ref[R_WIDTH:, :])
    h = _layer_norm(ALPHA * x_ref[...] + mix, g1_ref[...], b1_ref[...], LN_EPS)
    hb = h.astype(BF16)
    ff = jnp.zeros_like(h)
    for j in range(D_FF // FF_SLAB):
        t = jnp.dot(hb, w1_ref[:, j * FF_SLAB:(j + 1) * FF_SLAB], preferred_element_type=F32)
        t = jnp.square(jnp.maximum(t, 0.0))
        ff = ff + jnp.dot(t.astype(BF16), w2_ref[j * FF_SLAB:(j + 1) * FF_SLAB, :],
                          preferred_element_type=F32)
    y_ref[...] = _layer_norm(ALPHA * h + ff, g2_ref[...], b2_ref[...], LN_EPS)


def _ffn(x2d, o_r, o_m, gate, wout, g1, b1, w1, w2, g2, b2, tm):
    n = x2d.shape[0]
    tok = lambda w: pl.BlockSpec((tm, w), lambda i: (i, 0))
    once = lambda a: pl.BlockSpec(a.shape, lambda i: (0,) * a.ndim, pipeline_mode=pl.Buffered(1))
    acts = [x2d, o_r, o_m]
    act_specs = [tok(D_MODEL), tok(R_WIDTH), tok(M_WIDTH)]
    if gate is not None:
        acts += [gate[0], gate[1]]
        act_specs += [tok(M_WIDTH), once(gate[1])]
    wts = [wout, g1, b1, w1, w2, g2, b2]
    return pl.pallas_call(
        functools.partial(_ffn_kernel, gate is not None),
        grid=(n // tm,),
        in_specs=act_specs + [once(w) for w in wts],
        out_specs=tok(D_MODEL),
        out_shape=jax.ShapeDtypeStruct((n, D_MODEL), F32),
        compiler_params=pltpu.CompilerParams(dimension_semantics=("arbitrary",),
                                             vmem_limit_bytes=VMEM_LIMIT),
        name="ffn",
    )(*acts, *wts)


def _block_diag_ones():
    idx = np.arange(R_WIDTH) // R_HEAD
    return jnp.asarray(idx[:, None] == idx[None, :], dtype=BF16)


def kernel(x_prompt, x_sample, state_shift, state_wkv, state_conv, state_ssm, w_in, mu_shift, w0, w_up, a0, a_up, g_up, k_k, k_a, r_k, lnx_w, lnx_b, conv_w, conv_b, dt_bias, a_log, d_skip, gnorm_w, w_out, ln1_g, ln1_b, w_ff1, w_ff2, ln2_g, ln2_b):
    assert w_in.shape[0] == DEPTH and x_sample.shape[1] == 1
    batch, seq, _ = x_prompt.shape
    nb = x_sample.shape[0]
    assert seq % SSD_CHUNK == 0 and seq % RWKV_TILE == 0 and nb % STEP_BATCH == 0
    row = lambda a: a.reshape(1, -1)

    w_in_p = jnp.pad(w_in[0], ((0, 0), (0, PROJ_PAD - w_in.shape[2]))).astype(BF16)
    zl = jnp.zeros((D_DECAY_LORA, R_WIDTH), F32)
    wup_p = jnp.concatenate([w_up[0], zl], axis=0).astype(BF16)
    aup_p = jnp.concatenate([zl, a_up[0]], axis=0).astype(BF16)
    bd = _block_diag_ones()
    prep_w = (row(mu_shift[0]), row(w0[0]), wup_p, row(a0[0]), aup_p, g_up[0].astype(BF16),
              row(k_k[0]), row(k_a[0]), bd)
    post_w = (row(r_k[0]), row(lnx_w[0]), row(lnx_b[0]))
    pad_dt = lambda a: jnp.pad(a[0], (0, DT_PAD - M_HEADS)).reshape(1, DT_PAD)
    dskip_lanes = jnp.repeat(d_skip[0], M_HEADDIM).reshape(1, M_WIDTH)
    dskip_rows = jnp.broadcast_to(d_skip[0][:, None], (M_HEADS, M_HEADDIM))
    conv_wts = (conv_w[0], row(conv_b[0]), pad_dt(dt_bias), pad_dt(a_log))
    ffn_w = (w_out[0].astype(BF16), row(ln1_g[0]), row(ln1_b[0]), w_ff1[0].astype(BF16),
             w_ff2[0].astype(BF16), row(ln2_g[0]), row(ln2_b[0]))

    xp = x_prompt.reshape(batch * seq, D_MODEL)
    pr_p, z_p, xbc_p, dt_p = _proj(xp, w_in_p, 256)
    o_r_p, wkv_p = _rwkv_prompt(pr_p, batch, seq, prep_w[:-1] + post_w + (bd,))
    o_m_p, ssm_p = _mamba_prompt(z_p, xbc_p, dt_p, batch, seq,
                                 conv_wts + (dskip_lanes, row(gnorm_w[0])))
    y_p = _ffn(xp, o_r_p, o_m_p, None, *ffn_w, tm=256)
    shift_p = pr_p.reshape(batch, seq, RWKV_PROJ)[:, -1]
    conv_p = xbc_p.reshape(batch, seq, CONV_DIM)[:, seq - (CONV_K - 1):]

    xs2d = x_sample.reshape(nb, D_MODEL)
    pr_s, z_s, xbc_s, dt_s = _proj(xs2d, w_in_p, nb)
    col = lambda a: jnp.broadcast_to(a.reshape(R_WIDTH, 1), (R_WIDTH, nb))
    o_r_t, wkv_t = _rwkv_sample(pr_s, state_shift[0], jnp.transpose(state_wkv[0], (1, 2, 3, 0)),
                                prep_w, (col(r_k[0]), col(lnx_w[0]), col(lnx_b[0])))
    o_r_s = o_r_t.T
    wkv_s = jnp.transpose(wkv_t, (3, 0, 1, 2))
    cprev = state_conv[0].reshape(nb, (CONV_K - 1) * CONV_DIM)
    xs_s, xdt_s, dec_s, bm_s, cm_s = _mamba_sample_prep(xbc_s, cprev, dt_s, conv_wts)
    y_m_s, ssm_s = _mamba_sample_step(xs_s, xdt_s, dec_s, bm_s, cm_s, state_ssm[0], dskip_rows)
    y_s = _ffn(xs2d, o_r_s.reshape(nb, R_WIDTH), y_m_s.reshape(nb, M_WIDTH),
               (z_s, row(gnorm_w[0])), *ffn_w, tm=nb)
    conv_s = jnp.concatenate([state_conv[0][:, 1:], xbc_s[:, None, :]], axis=1)

    return (y_p.reshape(batch, seq, D_MODEL), y_s.reshape(nb, 1, D_MODEL),
            shift_p[None], wkv_p[None], conv_p[None],
            ssm_p.reshape(1, batch, M_HEADS, M_HEADDIM, D_STATE),
            pr_s[None], wkv_s.reshape(1, nb, R_HEADS, R_HEAD, R_HEAD), conv_s[None],
            ssm_s.reshape(1, nb, M_HEADS, M_HEADDIM, D_STATE))
```

```python
import functools
import math

import numpy as np
import jax
import jax.numpy as jnp
from jax import lax
from jax.experimental import pallas as pl
from jax.experimental.pallas import tpu as pltpu

F32 = jnp.float32
BF16 = jnp.bfloat16

D_MODEL = 1024
R_WIDTH = 512
R_HEAD = 64
R_HEADS = 8
D_DECAY_LORA = 64
D_AAA_LORA = 64
D_GATE_LORA = 128
M_WIDTH = 512
M_HEADDIM = 64
M_HEADS = 8
M_GROUPS = 2
D_STATE = 128
CONV_K = 4
CONV_DIM = M_WIDTH + 2 * M_GROUPS * D_STATE
D_FF = 4 * D_MODEL
RWKV_PROJ = 3 * R_WIDTH + D_DECAY_LORA + D_AAA_LORA + D_GATE_LORA
DT_PAD = 128
PROJ_PAD = RWKV_PROJ + M_WIDTH + CONV_DIM + DT_PAD
DEPTH = 1
ALPHA = (2 * DEPTH) ** 0.25
LN_EPS = 1e-5
LNX_EPS = 64e-5
RMS_EPS = 1e-5

RWKV_CHUNK = 64
RWKV_TILE = 256
SSD_CHUNK = 256
SUBLANES = 8
DENSE_ROWS = 512
VMEM_LIMIT = 48 * 1024 * 1024

_NT = (((1,), (1,)), ((), ()))


def _mm(a, b):
    return jnp.dot(a.astype(BF16), b.astype(BF16), preferred_element_type=F32)


def _mm_nt(a, b):
    return lax.dot_general(a.astype(BF16), b.astype(BF16), _NT, preferred_element_type=F32)


def _split(x, passes):
    parts = []
    for _ in range(passes):
        p = x.astype(BF16)
        parts.append(p)
        x = x - p.astype(F32)
    return parts


def _mm_sel_r(x, sel, passes):
    return sum(jnp.dot(p, sel, preferred_element_type=F32) for p in _split(x, passes))


def _mm_sel_l(sel, x, passes):
    return sum(jnp.dot(sel, p, preferred_element_type=F32) for p in _split(x, passes))


def _softplus(x):
    return jnp.maximum(x, 0.0) + jnp.log(1.0 + jnp.exp(-jnp.abs(x)))


def _sigmoid(x):
    return 1.0 / (1.0 + jnp.exp(-x))


def _silu(x):
    return x * _sigmoid(x)


def _layer_norm(x, g, b, eps):
    mu = jnp.mean(x, axis=-1, keepdims=True)
    d = x - mu
    var = jnp.mean(d * d, axis=-1, keepdims=True)
    return d * lax.rsqrt(var + eps) * g + b


def _iota2(shape, axis):
    return lax.broadcasted_iota(jnp.int32, shape, axis)


def _proj_kernel(x_ref, w_ref, pr_ref, z_ref, xbc_ref, dt_ref):
    res = jnp.dot(x_ref[...].astype(BF16), w_ref[...], preferred_element_type=F32)
    c0 = RWKV_PROJ
    c1 = c0 + M_WIDTH
    c2 = c1 + CONV_DIM
    pr_ref[...] = res[:, :c0]
    z_ref[...] = res[:, c0:c1]
    xbc_ref[...] = res[:, c1:c2]
    dt_ref[...] = res[:, c2:]


def _proj(x2d, w_bf, tm):
    n = x2d.shape[0]
    widths = (RWKV_PROJ, M_WIDTH, CONV_DIM, DT_PAD)
    return pl.pallas_call(
        _proj_kernel,
        grid=(n // tm,),
        in_specs=[pl.BlockSpec((tm, D_MODEL), lambda i: (i, 0)),
                  pl.BlockSpec((D_MODEL, PROJ_PAD), lambda i: (0, 0), pipeline_mode=pl.Buffered(1))],
        out_specs=[pl.BlockSpec((tm, w), lambda i: (i, 0)) for w in widths],
        out_shape=[jax.ShapeDtypeStruct((n, w), F32) for w in widths],
        compiler_params=pltpu.CompilerParams(dimension_semantics=("arbitrary",),
                                             vmem_limit_bytes=VMEM_LIMIT),
        name="proj",
    )(x2d, w_bf)


def _rwkv_prep(pr, prev, mu, w0, wup, a0, aup, gup, k_k, k_a, bd):
    u = pr + (prev - pr) * mu
    r = u[:, 0:R_WIDTH]
    k = u[:, R_WIDTH:2 * R_WIDTH]
    v = u[:, 2 * R_WIDTH:3 * R_WIDTH]
    lora = u[:, 3 * R_WIDTH:3 * R_WIDTH + D_DECAY_LORA + D_AAA_LORA]
    ug = u[:, 3 * R_WIDTH + D_DECAY_LORA + D_AAA_LORA:]
    w_log = -_softplus(-(w0 + _mm(jnp.tanh(lora), wup))) - 0.5
    logw = -jnp.exp(w_log)
    a = _sigmoid(a0 + _mm(lora, aup))
    g = _mm(_sigmoid(ug), gup)
    kk = k * k_k
    ss = _mm_sel_r(kk * kk, bd, 1)
    kkn = kk * lax.rsqrt(jnp.maximum(ss, 1e-24))
    k2 = k * (1.0 + (a - 1.0) * k_a)
    return r, logw, k2, v, -kkn, kkn * a, g


def _rwkv_post(y, r, k2, v, g, r_k, lnx_w, lnx_b, bd):
    inv = 1.0 / R_HEAD
    mu = _mm_sel_r(y, bd, 1) * inv
    d = y - mu
    var = _mm_sel_r(d * d, bd, 1) * inv
    yn = d * lax.rsqrt(var + LNX_EPS) * lnx_w + lnx_b
    bonus = _mm_sel_r(r * k2 * r_k, bd, 1)
    return (yn + bonus * v) * g


def _rwkv_tile_kernel(pr_ref, prevblk_ref, mu_ref, w0_ref, wup_ref, a0_ref, aup_ref, gup_ref,
                      kk_ref, ka_ref, rk_ref, lnw_ref, lnb_ref, bd_ref,
                      o_ref, sfin_ref, g_scr):
    c = RWKV_CHUNK
    tile = RWKV_TILE
    nck = tile // c
    hw = 2 * R_HEAD
    npair = R_HEADS // 2
    ti = pl.program_id(1)

    @pl.when(ti == 0)
    def _():
        g_scr[...] = jnp.zeros_like(g_scr)

    pr = pr_ref[...]
    prev_row = jnp.where(ti == 0, 0.0, prevblk_ref[SUBLANES - 1:SUBLANES, :])
    prev = jnp.where(_iota2((tile, 1), 0) == 0, prev_row, pltpu.roll(pr, 1, axis=0))
    head_ones = bd_ref[...]
    r, logw, k2, v, av, bv, g = _rwkv_prep(pr, prev, mu_ref[...], w0_ref[...], wup_ref[...],
                                           a0_ref[...], aup_ref[...], gup_ref[...],
                                           kk_ref[...], ka_ref[...], head_ones)

    sh_c = int(math.log2(c))
    ti_i = _iota2((tile, tile), 0)
    ti_j = _iota2((tile, tile), 1)
    chunk_tril = jnp.where((ti_i >= ti_j) & ((ti_i >> sh_c) == (ti_j >> sh_c)), 1.0, 0.0).astype(BF16)
    cum = _mm_sel_l(chunk_tril, logw, 3)
    cl = jnp.concatenate(
        [jnp.broadcast_to(cum[(q + 1) * c - 1:(q + 1) * c, :], (c, R_WIDTH)) for q in range(nck)], axis=0)
    p_out = jnp.exp(-cum)
    p_end = jnp.exp(cl - cum)
    rt = r * jnp.exp(cum)
    at = av * jnp.exp(cum - logw)
    bt = bv * p_out
    kt = k2 * p_out
    bh = bv * p_end
    kh = k2 * p_end
    pc = jnp.exp(cl)

    ii = _iota2((c, hw), 0)
    jj = _iota2((c, hw), 1) & (c - 1)
    strict = ii > jj
    incl = ii >= jj
    lane = _iota2((1, hw), 1)
    keep = (jnp.where(lane < R_HEAD, 1.0, 0.0).astype(BF16), jnp.where(lane >= R_HEAD, 1.0, 0.0).astype(BF16))
    e_r = _iota2((hw, hw), 0)
    e_c = _iota2((hw, hw), 1)
    eye_hw = e_r == e_c
    blockdiag = (e_r >= R_HEAD) == (e_c >= R_HEAD)

    def bd(x):
        xb = x.astype(BF16)
        return jnp.concatenate([xb * keep[0], xb * keep[1]], axis=0)

    def pdot(a, b_bd):
        return jnp.dot(a.astype(BF16), b_bd, preferred_element_type=F32)

    qp = [(q, p) for q in range(nck) for p in range(npair)]
    blk = lambda x, q, p: x[q * c:(q + 1) * c, hw * p:hw * (p + 1)]
    lhs = [jnp.concatenate([blk(at, *k), blk(rt, *k)], axis=0).astype(BF16) for k in qp]
    sc_b = [lax.dot_general(x, bd(blk(bt, *k)), _NT, preferred_element_type=F32) for x, k in zip(lhs, qp)]
    sc_k = [lax.dot_general(x, bd(blk(kt, *k)), _NT, preferred_element_type=F32) for x, k in zip(lhs, qp)]
    a_ab = [jnp.where(strict, s[:c], 0.0) for s in sc_b]
    a_rb = [jnp.where(incl, s[c:], 0.0) for s in sc_b]
    a_akrk = [jnp.concatenate([jnp.where(strict, s[:c], 0.0), jnp.where(incl, s[c:], 0.0)], axis=0)
              for s in sc_k]
    v_bd = [bd(blk(v, *k)) for k in qp]
    akrk_v = [pdot(a, x) for a, x in zip(a_akrk, v_bd)]

    sh = 4
    eye = jnp.where(ii == jj, 1.0, 0.0)
    npow = [jnp.where((ii >> sh) == (jj >> sh), n, 0.0) for n in a_ab]
    t_inv = [eye + n for n in npow]
    n_bd = [bd(n) for n in npow]
    for _ in range(sh - 1):
        npow = [pdot(n, x) for n, x in zip(npow, n_bd)]
        n_bd = [bd(n) for n in npow]
        t_inv = [t + pdot(t, x) for t, x in zip(t_inv, n_bd)]
    while (1 << sh) < c:
        off = ((ii >> (sh + 1)) == (jj >> (sh + 1))) & (((ii >> sh) & 1) > ((jj >> sh) & 1))
        tn = [pdot(t, bd(jnp.where(off, n, 0.0))) for t, n in zip(t_inv, a_ab)]
        t_inv = [t + pdot(x, bd(t)) for t, x in zip(t_inv, tn)]
        sh += 1

    tx = [pdot(t, jnp.concatenate([bd(blk(at, *k)), bd(x[:c])], axis=1))
          for t, x, k in zip(t_inv, akrk_v, qp)]
    rbx = [pdot(a, jnp.concatenate([bd(x[:, :hw]), bd(x[:, hw:])], axis=1)) for a, x in zip(a_rb, tx)]
    at2_pair = {k: x[:, :hw] for k, x in zip(qp, tx)}
    u0_pair = {k: x[:, hw:] for k, x in zip(qp, tx)}
    rp_pair = {k: blk(rt, *k) + x[:, :hw] for k, x in zip(qp, rbx)}
    y0_pair = {k: x[:, hw:] + y[c:] for k, x, y in zip(qp, rbx, akrk_v)}

    wt = {k: jnp.concatenate([blk(bh, *k), blk(kh, *k)], axis=0).T for k in qp}
    m_ba = {k: jnp.where(blockdiag, _mm(wt[k][:, :c], at2_pair[k]), 0.0) for k in qp}
    h0c = {k: jnp.where(blockdiag,
                        _mm(wt[k], jnp.concatenate([u0_pair[k], blk(v, *k)], axis=0)), 0.0) for k in qp}
    pc_col = {k: jnp.sum(jnp.where(eye_hw, blk(pc, *k)[:1], 0.0), axis=1, keepdims=True) for k in qp}

    state = [g_scr[p] for p in range(npair)]
    y_rows = []
    for q in range(nck):
        y_rows.append(jnp.concatenate(
            [_mm(rp_pair[q, p], state[p]) + y0_pair[q, p] for p in range(npair)], axis=1))
        state = [pc_col[q, p] * state[p] + _mm(m_ba[q, p], state[p]) + h0c[q, p] for p in range(npair)]
    for p in range(npair):
        g_scr[p] = state[p]

    y = jnp.concatenate(y_rows, axis=0)
    o_ref[...] = _rwkv_post(y, r, k2, v, g, rk_ref[...], lnw_ref[...], lnb_ref[...], head_ones)

    @pl.when(ti == pl.num_programs(1) - 1)
    def _():
        for p in range(npair):
            gt = g_scr[p].T
            sfin_ref[0, 2 * p] = gt[:R_HEAD, :R_HEAD]
            sfin_ref[0, 2 * p + 1] = pltpu.roll(gt, R_HEAD, axis=1)[R_HEAD:, :R_HEAD]


def _const_spec(shape):
    return pl.BlockSpec(shape, lambda *_: (0,) * len(shape))


def _rwkv_prompt(pr2d, batch, seq, wts):
    tile = RWKV_TILE
    ntile = seq // tile
    rows8 = seq // SUBLANES

    def prev_map(b, i):
        return (b * rows8 + jnp.maximum(i * (tile // SUBLANES) - 1, 0), 0)

    w_specs = [_const_spec(w.shape) for w in wts]
    return pl.pallas_call(
        _rwkv_tile_kernel,
        grid=(batch, ntile),
        in_specs=[pl.BlockSpec((tile, RWKV_PROJ), lambda b, i: (b * ntile + i, 0)),
                  pl.BlockSpec((SUBLANES, RWKV_PROJ), prev_map)] + w_specs,
        out_specs=[pl.BlockSpec((tile, R_WIDTH), lambda b, i: (b * ntile + i, 0)),
                   pl.BlockSpec((1, R_HEADS, R_HEAD, R_HEAD), lambda b, i: (b, 0, 0, 0))],
        out_shape=[jax.ShapeDtypeStruct((batch * seq, R_WIDTH), F32),
                   jax.ShapeDtypeStruct((batch, R_HEADS, R_HEAD, R_HEAD), F32)],
        scratch_shapes=[pltpu.VMEM((R_HEADS // 2, 2 * R_HEAD, 2 * R_HEAD), F32)],
        compiler_params=pltpu.CompilerParams(dimension_semantics=("arbitrary", "arbitrary"),
                                             vmem_limit_bytes=VMEM_LIMIT),
        name="rwkv_prompt",
    )(pr2d, pr2d, *wts)


def _rwkv_sample_kernel(pr_ref, prev_ref, mu_ref, w0_ref, wup_ref, a0_ref, aup_ref, gup_ref,
                        kk_ref, ka_ref, bd_ref, rk_ref, lnw_ref, lnb_ref, s_ref,
                        o_ref, so_ref, vec_scr):
    h = pl.program_id(0)

    @pl.when(h == 0)
    def _():
        r, logw, k2, v, av, bv, g = _rwkv_prep(pr_ref[...], prev_ref[...], mu_ref[...], w0_ref[...],
                                               wup_ref[...], a0_ref[...], aup_ref[...], gup_ref[...],
                                               kk_ref[...], ka_ref[...], bd_ref[...])
        for i, x in enumerate((r, jnp.exp(logw), k2, v, av, bv, g)):
            vec_scr[i] = x.T

    rows = pl.ds(pl.multiple_of(h * R_HEAD, R_HEAD), R_HEAD)
    r, w, k, v, a, b, g = (vec_scr[i, rows, :] for i in range(7))
    ys = []
    for vi in range(R_HEAD):
        s = s_ref[0, vi]
        sa = jnp.sum(s * a, axis=0, keepdims=True)
        s2 = s * w + sa * b + v[vi:vi + 1, :] * k
        so_ref[0, vi] = s2
        ys.append(jnp.sum(s2 * r, axis=0, keepdims=True))
    y = jnp.concatenate(ys, axis=0)
    mu = jnp.mean(y, axis=0, keepdims=True)
    d = y - mu
    var = jnp.mean(d * d, axis=0, keepdims=True)
    yn = d * lax.rsqrt(var + LNX_EPS) * lnw_ref[...] + lnb_ref[...]
    bonus = jnp.sum(r * k * rk_ref[...], axis=0, keepdims=True)
    o_ref[...] = (yn + bonus * v) * g


def _rwkv_sample(pr, prev, state_t, prep_w, post_cols):
    nb = pr.shape[0]
    args = (pr, prev) + tuple(prep_w)
    head_rows = pl.BlockSpec((R_HEAD, nb), lambda h: (h, 0))
    st_spec = pl.BlockSpec((1, R_HEAD, R_HEAD, nb), lambda h: (h, 0, 0, 0))
    return pl.pallas_call(
        _rwkv_sample_kernel,
        grid=(R_HEADS,),
        in_specs=[_const_spec(a.shape) for a in args] + [head_rows] * 3 + [st_spec],
        out_specs=[head_rows, st_spec],
        out_shape=[jax.ShapeDtypeStruct((R_WIDTH, nb), F32),
                   jax.ShapeDtypeStruct(state_t.shape, F32)],
        scratch_shapes=[pltpu.VMEM((7, R_WIDTH, nb), F32)],
        compiler_params=pltpu.CompilerParams(dimension_semantics=("arbitrary",),
                                             vmem_limit_bytes=VMEM_LIMIT),
        name="rwkv_sample",
    )(*args, *post_cols, state_t)


STEP_BATCH = 8


def _per_head_rows(x, n):
    return jnp.concatenate([jnp.broadcast_to(x[h:h + 1, :], (n, x.shape[1])) for h in range(x.shape[0])],
                           axis=0)


def _rows_from_columns(col_b, diag, n):
    picked = jnp.where(diag, col_b, 0.0)
    return jnp.concatenate([jnp.sum(picked[h * n:(h + 1) * n], axis=0, keepdims=True)
                            for h in range(col_b.shape[0] // n)], axis=0)


def _shift_rows(x, prev8, s):
    rolled = pltpu.roll(x, s, axis=0)
    top = jnp.where(_iota2((SUBLANES, 1), 0) < s, pltpu.roll(prev8, s, axis=0), rolled[:SUBLANES])
    return jnp.concatenate([top, rolled[SUBLANES:]], axis=0)


def _head_expand(lanes_per_head=M_HEADDIM):
    shape = (DT_PAD, M_HEADS * lanes_per_head)
    head_of_lane = _iota2(shape, 1) >> int(math.log2(lanes_per_head))
    return jnp.where(_iota2(shape, 0) == head_of_lane, 1.0, 0.0).astype(BF16)


def _gated_rmsnorm(y, z, gnorm_w):
    yz = y * _silu(z)
    gw = M_WIDTH // M_GROUPS
    outs = []
    for gi in range(M_GROUPS):
        yg = yz[:, gi * gw:(gi + 1) * gw]
        ms = jnp.mean(yg * yg, axis=-1, keepdims=True)
        outs.append(yg * lax.rsqrt(ms + RMS_EPS))
    return jnp.concatenate(outs, axis=1) * gnorm_w


def _mamba_chunk_kernel(z_ref, xbc_ref, xprev_ref, dt_ref, convw_ref, convb_ref, dtb_ref, alog_ref,
                        dskip_ref, gnw_ref, o_ref, hfin_ref, h_scr):
    q = SSD_CHUNK
    ci = pl.program_id(1)

    @pl.when(ci == 0)
    def _():
        h_scr[...] = jnp.zeros_like(h_scr)

    xb = xbc_ref[...]
    prev8 = jnp.where(ci == 0, 0.0, xprev_ref[...])
    conv = convb_ref[...] + xb * convw_ref[CONV_K - 1:CONV_K, :]
    for s in range(1, CONV_K):
        conv = conv + _shift_rows(xb, prev8, s) * convw_ref[CONV_K - 1 - s:CONV_K - s, :]
    xact = _silu(conv)
    xs = xact[:, :M_WIDTH]
    gs = M_GROUPS * D_STATE
    bm = xact[:, M_WIDTH:M_WIDTH + gs]
    cm = xact[:, M_WIDTH + gs:]

    dt = _softplus(dt_ref[...] + dtb_ref[...])
    adt = dt * (-jnp.exp(alog_ref[...]))
    x_dt = xs * _mm_sel_r(dt, _head_expand(), 3)

    ii = _iota2((q, q), 0)
    jj = _iota2((q, q), 1)
    incl = ii >= jj
    cs = _mm_sel_l(jnp.where(incl, 1.0, 0.0).astype(BF16), adt, 3)
    cs_t = cs.T
    lane = _iota2((1, 2 * M_HEADDIM), 1)
    first = lane < M_HEADDIM
    hpg = M_HEADS // M_GROUPS

    y_pairs = []
    for gi in range(M_GROUPS):
        bm_g = bm[:, gi * D_STATE:(gi + 1) * D_STATE]
        cm_g = cm[:, gi * D_STATE:(gi + 1) * D_STATE]
        cb = _mm_nt(cm_g, bm_g)
        bm_t = bm_g.T
        for pp in range(hpg // 2):
            p = gi * (hpg // 2) + pp
            sl = slice(2 * M_HEADDIM * p, 2 * M_HEADDIM * (p + 1))
            x_p = x_dt[:, sl]
            y_diag = jnp.zeros((q, 2 * M_HEADDIM), F32)
            contrib = jnp.zeros((D_STATE, 2 * M_HEADDIM), F32)
            scale_in = []
            scale_end = []
            for e in range(2):
                head = 2 * p + e
                m = first if e == 0 else jnp.logical_not(first)
                a_col = cs[:, head:head + 1]
                a_row = cs_t[head:head + 1, :]
                a_end = cs_t[head:head + 1, q - 1:q]
                lmat = jnp.exp(jnp.where(incl, a_col - a_row, -jnp.inf))
                x_m = jnp.where(m, x_p, 0.0)
                y_diag = y_diag + _mm(cb * lmat, x_m)
                contrib = contrib + _mm(bm_t * jnp.exp(a_end - a_row), x_m)
                scale_in.append(jnp.exp(a_col))
                scale_end.append(jnp.exp(a_end))
            h_in = h_scr[p]
            y_off = _mm(cm_g, h_in) * jnp.where(first, scale_in[0], scale_in[1])
            h_scr[p] = h_in * jnp.where(first, scale_end[0], scale_end[1]) + contrib
            y_pairs.append(y_diag + y_off)

    y = jnp.concatenate(y_pairs, axis=1) + dskip_ref[...] * xs
    o_ref[...] = _gated_rmsnorm(y, z_ref[...], gnw_ref[...])

    @pl.when(ci == pl.num_programs(1) - 1)
    def _():
        for p in range(M_HEADS // 2):
            hfin_ref[0, 2 * M_HEADDIM * p:2 * M_HEADDIM * (p + 1), :] = h_scr[p].T


def _mamba_prompt(z2d, xbc2d, dt2d, batch, seq, wts):
    q = SSD_CHUNK
    nchunk = seq // q
    rows8 = seq // SUBLANES

    def prev_map(b, i):
        return (b * rows8 + jnp.maximum(i * (q // SUBLANES) - 1, 0), 0)

    tok = lambda w: pl.BlockSpec((q, w), lambda b, i: (b * nchunk + i, 0))
    return pl.pallas_call(
        _mamba_chunk_kernel,
        grid=(batch, nchunk),
        in_specs=[tok(M_WIDTH), tok(CONV_DIM), pl.BlockSpec((SUBLANES, CONV_DIM), prev_map), tok(DT_PAD)]
                 + [_const_spec(w.shape) for w in wts],
        out_specs=[tok(M_WIDTH),
                   pl.BlockSpec((1, M_HEADS * M_HEADDIM, D_STATE), lambda b, i: (b, 0, 0))],
        out_shape=[jax.ShapeDtypeStruct((batch * seq, M_WIDTH), F32),
                   jax.ShapeDtypeStruct((batch, M_HEADS * M_HEADDIM, D_STATE), F32)],
        scratch_shapes=[pltpu.VMEM((M_HEADS // 2, D_STATE, 2 * M_HEADDIM), F32)],
        compiler_params=pltpu.CompilerParams(dimension_semantics=("arbitrary", "arbitrary"),
                                             vmem_limit_bytes=VMEM_LIMIT),
        name="mamba_prompt",
    )(z2d, xbc2d, xbc2d, dt2d, *wts)


def _mamba_prep_kernel(xbc_ref, cprev_ref, dt_ref, convw_ref, convb_ref, dtb_ref, alog_ref,
                       xs_ref, xdt_ref, dec_ref, bm_ref, cm_ref):
    conv = convb_ref[...] + xbc_ref[...] * convw_ref[CONV_K - 1:CONV_K, :]
    for j in range(CONV_K - 1):
        conv = conv + cprev_ref[:, j * CONV_DIM:(j + 1) * CONV_DIM] * convw_ref[j:j + 1, :]
    xact = _silu(conv)
    xs = xact[:, :M_WIDTH]
    gs = M_GROUPS * D_STATE
    dt = _softplus(dt_ref[...] + dtb_ref[...])
    adt = dt * (-jnp.exp(alog_ref[...]))
    expand = _head_expand()
    xs_ref[...] = xs
    xdt_ref[...] = xs * _mm_sel_r(dt, expand, 3)
    dec_ref[...] = jnp.exp(_mm_sel_r(adt, _head_expand(D_STATE), 3))
    hpg = M_HEADS // M_GROUPS
    per_head = lambda m: jnp.concatenate(
        [m[:, (e // hpg) * D_STATE:(e // hpg + 1) * D_STATE] for e in range(M_HEADS)], axis=1)
    bm_ref[...] = per_head(xact[:, M_WIDTH:M_WIDTH + gs])
    cm_ref[...] = per_head(xact[:, M_WIDTH + gs:])


def _mamba_sample_prep(xbc, cprev, dt, wts):
    n = xbc.shape[0]
    args = (xbc, cprev, dt) + tuple(wts)
    widths = (M_WIDTH, M_WIDTH, M_HEADS * D_STATE, M_HEADS * D_STATE, M_HEADS * D_STATE)
    return pl.pallas_call(
        _mamba_prep_kernel,
        grid=(1,),
        in_specs=[_const_spec(a.shape) for a in args],
        out_specs=[_const_spec((n, w)) for w in widths],
        out_shape=[jax.ShapeDtypeStruct((n, w), F32) for w in widths],
        compiler_params=pltpu.CompilerParams(vmem_limit_bytes=VMEM_LIMIT),
        name="mamba_sample_prep",
    )(*args)


def _mamba_step_kernel(xs_ref, xdt_ref, dec_ref, bm_ref, cm_ref, h_ref, dskip_ref, y_ref, ho_ref):
    n = M_HEADS * M_HEADDIM
    ones_in = jnp.ones((M_HEADDIM, D_STATE), BF16)
    ones_out = jnp.ones((D_STATE, M_HEADDIM), BF16)
    diag = (_iota2((n, M_HEADDIM), 0) & (M_HEADDIM - 1)) == _iota2((n, M_HEADDIM), 1)

    def batch_row(bi, carry):
        rows = pl.ds(pl.multiple_of(bi * M_HEADS, M_HEADS), M_HEADS)
        hrows = pl.ds(pl.multiple_of(bi * n, n), n)
        x_col = _mm(jnp.where(diag, _per_head_rows(xdt_ref[rows, :], M_HEADDIM), 0.0), ones_in)
        h2 = (h_ref[hrows, :] * _per_head_rows(dec_ref[rows, :], M_HEADDIM)
              + x_col * _per_head_rows(bm_ref[rows, :], M_HEADDIM))
        ho_ref[hrows, :] = h2
        y = _rows_from_columns(_mm(h2 * _per_head_rows(cm_ref[rows, :], M_HEADDIM), ones_out),
                               diag, M_HEADDIM)
        y_ref[rows, :] = y + dskip_ref[...] * xs_ref[rows, :]
        return carry

    lax.fori_loop(0, STEP_BATCH, batch_row, 0, unroll=True)


def _mamba_sample_step(xs, xdt, dec, bm, cm, state, dskip_rows):
    nb = state.shape[0]
    rows = STEP_BATCH * M_HEADS
    vec_spec = pl.BlockSpec((rows, M_HEADDIM), lambda i: (i, 0))
    bc_spec = pl.BlockSpec((rows, D_STATE), lambda i: (i, 0))
    st_spec = pl.BlockSpec((rows * M_HEADDIM, D_STATE), lambda i: (i, 0))
    flat = lambda x, w: x.reshape(nb * M_HEADS, w)
    return pl.pallas_call(
        _mamba_step_kernel,
        grid=(nb // STEP_BATCH,),
        in_specs=[vec_spec] * 2 + [bc_spec] * 3 + [st_spec, _const_spec((M_HEADS, M_HEADDIM))],
        out_specs=[vec_spec, st_spec],
        out_shape=[jax.ShapeDtypeStruct((nb * M_HEADS, M_HEADDIM), F32),
                   jax.ShapeDtypeStruct((nb * M_HEADS * M_HEADDIM, D_STATE), F32)],
        compiler_params=pltpu.CompilerParams(dimension_semantics=("arbitrary",),
                                             vmem_limit_bytes=VMEM_LIMIT),
        name="mamba_sample_step",
    )(flat(xs, M_HEADDIM), flat(xdt, M_HEADDIM), flat(dec, D_STATE), flat(bm, D_STATE),
      flat(cm, D_STATE), state.reshape(nb * M_HEADS * M_HEADDIM, D_STATE), dskip_rows)


FF_SLAB = 1024


def _ffn_kernel(has_gate, *refs):
    if has_gate:
        (x_ref, or_ref, ym_ref, z_ref, gnw_ref, wout_ref, g1_ref, b1_ref, w1_ref, w2_ref,
         g2_ref, b2_ref, y_ref) = refs
        o_m = _gated_rmsnorm(ym_ref[...], z_ref[...], gnw_ref[...])
    else:
        (x_ref, or_ref, om_ref, wout_ref, g1_ref, b1_ref, w1_ref, w2_ref,
         g2_ref, b2_ref, y_ref) = refs
        o_m = om_ref[...]
    mix = _mm(or_ref[...], wout_ref[:R_WIDTH, :]) + _mm(o_m, wout_ref[R_WIDTH:, :])
    h = _layer_norm(ALPHA * x_ref[...] + mix, g1_ref[...], b1_ref[...], LN_EPS)
    hb = h.astype(BF16)
    ff = jnp.zeros_like(h)
    for j in range(D_FF // FF_SLAB):
        t = jnp.dot(hb, w1_ref[:, j * FF_SLAB:(j + 1) * FF_SLAB], preferred_element_type=F32)
        t = jnp.square(jnp.maximum(t, 0.0))
        ff = ff + jnp.dot(t.astype(BF16), w2_ref[j * FF_SLAB:(j + 1) * FF_SLAB, :],
                          preferred_element_type=F32)
    y_ref[...] = _layer_norm(ALPHA * h + ff, g2_ref[...], b2_ref[...], LN_EPS)


def _ffn(x2d, o_r, o_m, gate, wout, g1, b1, w1, w2, g2, b2, tm):
    n = x2d.shape[0]
    tok = lambda w: pl.BlockSpec((tm, w), lambda i: (i, 0))
    once = lambda a: pl.BlockSpec(a.shape, lambda i: (0,) * a.ndim, pipeline_mode=pl.Buffered(1))
    acts = [x2d, o_r, o_m]
    act_specs = [tok(D_MODEL), tok(R_WIDTH), tok(M_WIDTH)]
    if gate is not None:
        acts += [gate[0], gate[1]]
        act_specs += [tok(M_WIDTH), once(gate[1])]
    wts = [wout, g1, b1, w1, w2, g2, b2]
    return pl.pallas_call(
        functools.partial(_ffn_kernel, gate is not None),
        grid=(n // tm,),
        in_specs=act_specs + [once(w) for w in wts],
        out_specs=tok(D_MODEL),
        out_shape=jax.ShapeDtypeStruct((n, D_MODEL), F32),
        compiler_params=pltpu.CompilerParams(dimension_semantics=("arbitrary",),
                                             vmem_limit_bytes=VMEM_LIMIT),
        name="ffn",
    )(*acts, *wts)


def _block_diag_ones():
    idx = np.arange(R_WIDTH) // R_HEAD
    return jnp.asarray(idx[:, None] == idx[None, :], dtype=BF16)


def kernel(x_prompt, x_sample, state_shift, state_wkv, state_conv, state_ssm, w_in, mu_shift, w0, w_up, a0, a_up, g_up, k_k, k_a, r_k, lnx_w, lnx_b, conv_w, conv_b, dt_bias, a_log, d_skip, gnorm_w, w_out, ln1_g, ln1_b, w_ff1, w_ff2, ln2_g, ln2_b):
    assert w_in.shape[0] == DEPTH and x_sample.shape[1] == 1
    batch, seq, _ = x_prompt.shape
    nb = x_sample.shape[0]
    assert seq % SSD_CHUNK == 0 and seq % RWKV_TILE == 0 and nb % STEP_BATCH == 0
    row = lambda a: a.reshape(1, -1)

    w_in_p = jnp.pad(w_in[0], ((0, 0), (0, PROJ_PAD - w_in.shape[2]))).astype(BF16)
    zl = jnp.zeros((D_DECAY_LORA, R_WIDTH), F32)
    wup_p = jnp.concatenate([w_up[0], zl], axis=0).astype(BF16)
    aup_p = jnp.concatenate([zl, a_up[0]], axis=0).astype(BF16)
    bd = _block_diag_ones()
    prep_w = (row(mu_shift[0]), row(w0[0]), wup_p, row(a0[0]), aup_p, g_up[0].astype(BF16),
              row(k_k[0]), row(k_a[0]), bd)
    post_w = (row(r_k[0]), row(lnx_w[0]), row(lnx_b[0]))
    pad_dt = lambda a: jnp.pad(a[0], (0, DT_PAD - M_HEADS)).reshape(1, DT_PAD)
    dskip_lanes = jnp.repeat(d_skip[0], M_HEADDIM).reshape(1, M_WIDTH)
    dskip_rows = jnp.broadcast_to(d_skip[0][:, None], (M_HEADS, M_HEADDIM))
    conv_wts = (conv_w[0], row(conv_b[0]), pad_dt(dt_bias), pad_dt(a_log))
    ffn_w = (w_out[0].astype(BF16), row(ln1_g[0]), row(ln1_b[0]), w_ff1[0].astype(BF16),
             w_ff2[0].astype(BF16), row(ln2_g[0]), row(ln2_b[0]))

    xp = x_prompt.reshape(batch * seq, D_MODEL)
    pr_p, z_p, xbc_p, dt_p = _proj(xp, w_in_p, DENSE_ROWS)
    o_r_p, wkv_p = _rwkv_prompt(pr_p, batch, seq, prep_w[:-1] + post_w + (bd,))
    o_m_p, ssm_p = _mamba_prompt(z_p, xbc_p, dt_p, batch, seq,
                                 conv_wts + (dskip_lanes, row(gnorm_w[0])))
    y_p = _ffn(xp, o_r_p, o_m_p, None, *ffn_w, tm=DENSE_ROWS)
    shift_p = pr_p.reshape(batch, seq, RWKV_PROJ)[:, -1]
    conv_p = xbc_p.reshape(batch, seq, CONV_DIM)[:, seq - (CONV_K - 1):]

    xs2d = x_sample.reshape(nb, D_MODEL)
    pr_s, z_s, xbc_s, dt_s = _proj(xs2d, w_in_p, nb)
    col = lambda a: jnp.broadcast_to(a.reshape(R_WIDTH, 1), (R_WIDTH, nb))
    o_r_t, wkv_t = _rwkv_sample(pr_s, state_shift[0], jnp.transpose(state_wkv[0], (1, 2, 3, 0)),
                                prep_w, (col(r_k[0]), col(lnx_w[0]), col(lnx_b[0])))
    o_r_s = o_r_t.T
    wkv_s = jnp.transpose(wkv_t, (3, 0, 1, 2))
    cprev = state_conv[0].reshape(nb, (CONV_K - 1) * CONV_DIM)
    xs_s, xdt_s, dec_s, bm_s, cm_s = _mamba_sample_prep(xbc_s, cprev, dt_s, conv_wts)
    y_m_s, ssm_s = _mamba_sample_step(xs_s, xdt_s, dec_s, bm_s, cm_s, state_ssm[0], dskip_rows)
    y_s = _ffn(xs2d, o_r_s.reshape(nb, R_WIDTH), y_m_s.reshape(nb, M_WIDTH),
               (z_s, row(gnorm_w[0])), *ffn_w, tm=nb)
    conv_s = jnp.concatenate([state_conv[0][:, 1:], xbc_s[:, None, :]], axis=1)

    return (y_p.reshape(batch, seq, D_MODEL), y_s.reshape(nb, 1, D_MODEL),
            shift_p[None], wkv_p[None], conv_p[None],
            ssm_p.reshape(1, batch, M_HEADS, M_HEADDIM, D_STATE),
            pr_s[None], wkv_s.reshape(1, nb, R_HEADS, R_HEAD, R_HEAD), conv_s[None],
            ssm_s.reshape(1, nb, M_HEADS, M_HEADDIM, D_STATE))
```

```python
import functools
import math

import numpy as np
import jax
import jax.numpy as jnp
from jax import lax
from jax.experimental import pallas as pl
from jax.experimental.pallas import tpu as pltpu

F32 = jnp.float32
BF16 = jnp.bfloat16

D_MODEL = 1024
R_WIDTH = 512
R_HEAD = 64
R_HEADS = 8
D_DECAY_LORA = 64
D_AAA_LORA = 64
D_GATE_LORA = 128
M_WIDTH = 512
M_HEADDIM = 64
M_HEADS = 8
M_GROUPS = 2
D_STATE = 128
CONV_K = 4
CONV_DIM = M_WIDTH + 2 * M_GROUPS * D_STATE
D_FF = 4 * D_MODEL
RWKV_PROJ = 3 * R_WIDTH + D_DECAY_LORA + D_AAA_LORA + D_GATE_LORA
DT_PAD = 128
PROJ_MAIN = RWKV_PROJ + M_WIDTH + CONV_DIM
DEPTH = 1
ALPHA = (2 * DEPTH) ** 0.25
LN_EPS = 1e-5
LNX_EPS = 64e-5
RMS_EPS = 1e-5

RWKV_CHUNK = 64
RWKV_TILE = 256
SSD_CHUNK = 128
MAMBA_TILE = 256
SUBLANES = 8
DENSE_ROWS = 512
VMEM_LIMIT = 48 * 1024 * 1024

_NT = (((1,), (1,)), ((), ()))


def _mm(a, b):
    return jnp.dot(a.astype(BF16), b.astype(BF16), preferred_element_type=F32)


def _mm_nt(a, b):
    return lax.dot_general(a.astype(BF16), b.astype(BF16), _NT, preferred_element_type=F32)


def _split(x, passes):
    parts = []
    for _ in range(passes):
        p = x.astype(BF16)
        parts.append(p)
        x = x - p.astype(F32)
    return parts


def _mm_sel_r(x, sel, passes):
    return sum(jnp.dot(p, sel, preferred_element_type=F32) for p in _split(x, passes))


def _mm_sel_l(sel, x, passes):
    return sum(jnp.dot(sel, p, preferred_element_type=F32) for p in _split(x, passes))


def _softplus(x):
    return jnp.maximum(x, 0.0) + jnp.log(1.0 + jnp.exp(-jnp.abs(x)))


def _sigmoid(x):
    return 0.5 + 0.5 * jnp.tanh(0.5 * x)


def _silu(x):
    return x * _sigmoid(x)


def _layer_norm(x, g, b, eps):
    mu = jnp.mean(x, axis=-1, keepdims=True)
    d = x - mu
    var = jnp.mean(d * d, axis=-1, keepdims=True)
    return d * lax.rsqrt(var + eps) * g + b


def _iota2(shape, axis):
    return lax.broadcasted_iota(jnp.int32, shape, axis)


def _proj_kernel(x_ref, w_ref, wdt_ref, pr_ref, z_ref, xbc_ref, dt_ref):
    xb = x_ref[...].astype(BF16)
    res = jnp.dot(xb, w_ref[...], preferred_element_type=F32)
    c0 = RWKV_PROJ
    c1 = c0 + M_WIDTH
    pr_ref[...] = res[:, :c0]
    z_ref[...] = res[:, c0:c1]
    xbc_ref[...] = res[:, c1:]
    dt_ref[...] = jnp.dot(xb, wdt_ref[...], preferred_element_type=F32)


def _proj(x2d, w_bf, wdt_bf, tm):
    n = x2d.shape[0]
    widths = (RWKV_PROJ, M_WIDTH, CONV_DIM, DT_PAD)
    once = lambda a: pl.BlockSpec(a.shape, lambda i: (0, 0), pipeline_mode=pl.Buffered(1))
    return pl.pallas_call(
        _proj_kernel,
        grid=(n // tm,),
        in_specs=[pl.BlockSpec((tm, D_MODEL), lambda i: (i, 0)), once(w_bf), once(wdt_bf)],
        out_specs=[pl.BlockSpec((tm, w), lambda i: (i, 0)) for w in widths],
        out_shape=[jax.ShapeDtypeStruct((n, w), F32) for w in widths],
        compiler_params=pltpu.CompilerParams(dimension_semantics=("arbitrary",),
                                             vmem_limit_bytes=VMEM_LIMIT),
        name="proj",
    )(x2d, w_bf, wdt_bf)


def _rwkv_prep(pr, prev, mu, w0, wup, a0, aup, gup, k_k, k_a, bd):
    u = pr + (prev - pr) * mu
    r = u[:, 0:R_WIDTH]
    k = u[:, R_WIDTH:2 * R_WIDTH]
    v = u[:, 2 * R_WIDTH:3 * R_WIDTH]
    lora = u[:, 3 * R_WIDTH:3 * R_WIDTH + D_DECAY_LORA + D_AAA_LORA]
    ug = u[:, 3 * R_WIDTH + D_DECAY_LORA + D_AAA_LORA:]
    w_log = -_softplus(-(w0 + _mm(jnp.tanh(lora), wup))) - 0.5
    logw = -jnp.exp(w_log)
    a = _sigmoid(a0 + _mm(lora, aup))
    g = _mm(_sigmoid(ug), gup)
    kk = k * k_k
    ss = _mm_sel_r(kk * kk, bd, 1)
    kkn = kk * lax.rsqrt(jnp.maximum(ss, 1e-24))
    k2 = k * (1.0 + (a - 1.0) * k_a)
    return r, logw, k2, v, -kkn, kkn * a, g


def _rwkv_post(y, r, k2, v, g, r_k, lnx_w, lnx_b, bd):
    inv = 1.0 / R_HEAD
    mu = _mm_sel_r(y, bd, 1) * inv
    d = y - mu
    var = _mm_sel_r(d * d, bd, 1) * inv
    yn = d * lax.rsqrt(var + LNX_EPS) * lnx_w + lnx_b
    bonus = _mm_sel_r(r * k2 * r_k, bd, 1)
    return (yn + bonus * v) * g


def _rwkv_tile_kernel(pr_ref, prevblk_ref, mu_ref, w0_ref, wup_ref, a0_ref, aup_ref, gup_ref,
                      kk_ref, ka_ref, rk_ref, lnw_ref, lnb_ref, bd_ref,
                      o_ref, sfin_ref, g_scr):
    c = RWKV_CHUNK
    tile = RWKV_TILE
    nck = tile // c
    hw = 2 * R_HEAD
    npair = R_HEADS // 2
    ti = pl.program_id(1)

    @pl.when(ti == 0)
    def _():
        g_scr[...] = jnp.zeros_like(g_scr)

    pr = pr_ref[...]
    prev_row = jnp.where(ti == 0, 0.0, prevblk_ref[SUBLANES - 1:SUBLANES, :])
    prev = jnp.where(_iota2((tile, 1), 0) == 0, prev_row, pltpu.roll(pr, 1, axis=0))
    head_ones = bd_ref[...]
    r, logw, k2, v, av, bv, g = _rwkv_prep(pr, prev, mu_ref[...], w0_ref[...], wup_ref[...],
                                           a0_ref[...], aup_ref[...], gup_ref[...],
                                           kk_ref[...], ka_ref[...], head_ones)

    sh_c = int(math.log2(c))
    ti_i = _iota2((tile, tile), 0)
    ti_j = _iota2((tile, tile), 1)
    chunk_tril = jnp.where((ti_i >= ti_j) & ((ti_i >> sh_c) == (ti_j >> sh_c)), 1.0, 0.0).astype(BF16)
    cum = _mm_sel_l(chunk_tril, logw, 3)
    cl = jnp.concatenate(
        [jnp.broadcast_to(cum[(q + 1) * c - 1:(q + 1) * c, :], (c, R_WIDTH)) for q in range(nck)], axis=0)
    p_out = jnp.exp(-cum)
    p_end = jnp.exp(cl - cum)
    rt = r * jnp.exp(cum)
    at = av * jnp.exp(cum - logw)
    bt = bv * p_out
    kt = k2 * p_out
    bh = bv * p_end
    kh = k2 * p_end
    pc = jnp.exp(cl)

    ii = _iota2((c, hw), 0)
    jj = _iota2((c, hw), 1) & (c - 1)
    strict = ii > jj
    incl = ii >= jj
    lane = _iota2((1, hw), 1)
    keep = (jnp.where(lane < R_HEAD, 1.0, 0.0).astype(BF16), jnp.where(lane >= R_HEAD, 1.0, 0.0).astype(BF16))
    e_r = _iota2((hw, hw), 0)
    e_c = _iota2((hw, hw), 1)
    eye_hw = e_r == e_c
    blockdiag = (e_r >= R_HEAD) == (e_c >= R_HEAD)

    def bd(x):
        xb = x.astype(BF16)
        return jnp.concatenate([xb * keep[0], xb * keep[1]], axis=0)

    def pdot(a, b_bd):
        return jnp.dot(a.astype(BF16), b_bd, preferred_element_type=F32)

    qp = [(q, p) for q in range(nck) for p in range(npair)]
    blk = lambda x, q, p: x[q * c:(q + 1) * c, hw * p:hw * (p + 1)]
    lhs = [jnp.concatenate([blk(at, *k), blk(rt, *k)], axis=0).astype(BF16) for k in qp]
    sc_b = [lax.dot_general(x, bd(blk(bt, *k)), _NT, preferred_element_type=F32) for x, k in zip(lhs, qp)]
    sc_k = [lax.dot_general(x, bd(blk(kt, *k)), _NT, preferred_element_type=F32) for x, k in zip(lhs, qp)]
    a_ab = [jnp.where(strict, s[:c], 0.0) for s in sc_b]
    a_rb = [jnp.where(incl, s[c:], 0.0) for s in sc_b]
    a_akrk = [jnp.concatenate([jnp.where(strict, s[:c], 0.0), jnp.where(incl, s[c:], 0.0)], axis=0)
              for s in sc_k]
    v_bd = [bd(blk(v, *k)) for k in qp]
    akrk_v = [pdot(a, x) for a, x in zip(a_akrk, v_bd)]

    sh = 4
    eye = jnp.where(ii == jj, 1.0, 0.0)
    npow = [jnp.where((ii >> sh) == (jj >> sh), n, 0.0) for n in a_ab]
    t_inv = [eye + n for n in npow]
    n_bd = [bd(n) for n in npow]
    for _ in range(sh - 1):
        npow = [pdot(n, x) for n, x in zip(npow, n_bd)]
        n_bd = [bd(n) for n in npow]
        t_inv = [t + pdot(t, x) for t, x in zip(t_inv, n_bd)]
    while (1 << sh) < c:
        off = ((ii >> (sh + 1)) == (jj >> (sh + 1))) & (((ii >> sh) & 1) > ((jj >> sh) & 1))
        tn = [pdot(t, bd(jnp.where(off, n, 0.0))) for t, n in zip(t_inv, a_ab)]
        t_inv = [t + pdot(x, bd(t)) for t, x in zip(t_inv, tn)]
        sh += 1

    tx = [pdot(t, jnp.concatenate([bd(blk(at, *k)), bd(x[:c])], axis=1))
          for t, x, k in zip(t_inv, akrk_v, qp)]
    rbx = [pdot(a, jnp.concatenate([bd(x[:, :hw]), bd(x[:, hw:])], axis=1)) for a, x in zip(a_rb, tx)]
    at2_pair = {k: x[:, :hw] for k, x in zip(qp, tx)}
    u0_pair = {k: x[:, hw:] for k, x in zip(qp, tx)}
    rp_pair = {k: blk(rt, *k) + x[:, :hw] for k, x in zip(qp, rbx)}
    y0_pair = {k: x[:, hw:] + y[c:] for k, x, y in zip(qp, rbx, akrk_v)}

    wt = {k: jnp.concatenate([blk(bh, *k), blk(kh, *k)], axis=0).T for k in qp}
    m_ba = {k: jnp.where(blockdiag, _mm(wt[k][:, :c], at2_pair[k]), 0.0) for k in qp}
    h0c = {k: jnp.where(blockdiag,
                        _mm(wt[k], jnp.concatenate([u0_pair[k], blk(v, *k)], axis=0)), 0.0) for k in qp}
    pc_col = {k: jnp.sum(jnp.where(eye_hw, blk(pc, *k)[:1], 0.0), axis=1, keepdims=True) for k in qp}

    state = [g_scr[p] for p in range(npair)]
    y_rows = []
    for q in range(nck):
        y_rows.append(jnp.concatenate(
            [_mm(rp_pair[q, p], state[p]) + y0_pair[q, p] for p in range(npair)], axis=1))
        state = [pc_col[q, p] * state[p] + _mm(m_ba[q, p], state[p]) + h0c[q, p] for p in range(npair)]
    for p in range(npair):
        g_scr[p] = state[p]

    y = jnp.concatenate(y_rows, axis=0)
    o_ref[...] = _rwkv_post(y, r, k2, v, g, rk_ref[...], lnw_ref[...], lnb_ref[...], head_ones)

    @pl.when(ti == pl.num_programs(1) - 1)
    def _():
        for p in range(npair):
            gt = g_scr[p].T
            sfin_ref[0, 2 * p] = gt[:R_HEAD, :R_HEAD]
            sfin_ref[0, 2 * p + 1] = pltpu.roll(gt, R_HEAD, axis=1)[R_HEAD:, :R_HEAD]


def _const_spec(shape):
    return pl.BlockSpec(shape, lambda *_: (0,) * len(shape))


def _rwkv_prompt(pr2d, batch, seq, wts):
    tile = RWKV_TILE
    ntile = seq // tile
    rows8 = seq // SUBLANES

    def prev_map(b, i):
        return (b * rows8 + jnp.maximum(i * (tile // SUBLANES) - 1, 0), 0)

    w_specs = [_const_spec(w.shape) for w in wts]
    return pl.pallas_call(
        _rwkv_tile_kernel,
        grid=(batch, ntile),
        in_specs=[pl.BlockSpec((tile, RWKV_PROJ), lambda b, i: (b * ntile + i, 0)),
                  pl.BlockSpec((SUBLANES, RWKV_PROJ), prev_map)] + w_specs,
        out_specs=[pl.BlockSpec((tile, R_WIDTH), lambda b, i: (b * ntile + i, 0)),
                   pl.BlockSpec((1, R_HEADS, R_HEAD, R_HEAD), lambda b, i: (b, 0, 0, 0))],
        out_shape=[jax.ShapeDtypeStruct((batch * seq, R_WIDTH), F32),
                   jax.ShapeDtypeStruct((batch, R_HEADS, R_HEAD, R_HEAD), F32)],
        scratch_shapes=[pltpu.VMEM((R_HEADS // 2, 2 * R_HEAD, 2 * R_HEAD), F32)],
        compiler_params=pltpu.CompilerParams(dimension_semantics=("arbitrary", "arbitrary"),
                                             vmem_limit_bytes=VMEM_LIMIT),
        name="rwkv_prompt",
    )(pr2d, pr2d, *wts)


def _rwkv_sample_kernel(pr_ref, prev_ref, mu_ref, w0_ref, wup_ref, a0_ref, aup_ref, gup_ref,
                        kk_ref, ka_ref, bd_ref, rk_ref, lnw_ref, lnb_ref, s_ref,
                        o_ref, so_ref, vec_scr):
    h = pl.program_id(0)

    @pl.when(h == 0)
    def _():
        r, logw, k2, v, av, bv, g = _rwkv_prep(pr_ref[...], prev_ref[...], mu_ref[...], w0_ref[...],
                                               wup_ref[...], a0_ref[...], aup_ref[...], gup_ref[...],
                                               kk_ref[...], ka_ref[...], bd_ref[...])
        for i, x in enumerate((r, jnp.exp(logw), k2, v, av, bv, g)):
            vec_scr[i] = x.T

    rows = pl.ds(pl.multiple_of(h * R_HEAD, R_HEAD), R_HEAD)
    r, w, k, v, a, b, g = (vec_scr[i, rows, :] for i in range(7))
    ys = []
    for vi in range(R_HEAD):
        s = s_ref[0, vi]
        sa = jnp.sum(s * a, axis=0, keepdims=True)
        s2 = s * w + sa * b + v[vi:vi + 1, :] * k
        so_ref[0, vi] = s2
        ys.append(jnp.sum(s2 * r, axis=0, keepdims=True))
    y = jnp.concatenate(ys, axis=0)
    mu = jnp.mean(y, axis=0, keepdims=True)
    d = y - mu
    var = jnp.mean(d * d, axis=0, keepdims=True)
    yn = d * lax.rsqrt(var + LNX_EPS) * lnw_ref[...] + lnb_ref[...]
    bonus = jnp.sum(r * k * rk_ref[...], axis=0, keepdims=True)
    o_ref[...] = (yn + bonus * v) * g


def _rwkv_sample(pr, prev, state_t, prep_w, post_cols):
    nb = pr.shape[0]
    args = (pr, prev) + tuple(prep_w)
    head_rows = pl.BlockSpec((R_HEAD, nb), lambda h: (h, 0))
    st_spec = pl.BlockSpec((1, R_HEAD, R_HEAD, nb), lambda h: (h, 0, 0, 0))
    return pl.pallas_call(
        _rwkv_sample_kernel,
        grid=(R_HEADS,),
        in_specs=[_const_spec(a.shape) for a in args] + [head_rows] * 3 + [st_spec],
        out_specs=[head_rows, st_spec],
        out_shape=[jax.ShapeDtypeStruct((R_WIDTH, nb), F32),
                   jax.ShapeDtypeStruct(state_t.shape, F32)],
        scratch_shapes=[pltpu.VMEM((7, R_WIDTH, nb), F32)],
        compiler_params=pltpu.CompilerParams(dimension_semantics=("arbitrary",),
                                             vmem_limit_bytes=VMEM_LIMIT),
        name="rwkv_sample",
    )(*args, *post_cols, state_t)


STEP_BATCH = 8


def _per_head_rows(x, n):
    return jnp.concatenate([jnp.broadcast_to(x[h:h + 1, :], (n, x.shape[1])) for h in range(x.shape[0])],
                           axis=0)


def _rows_from_columns(col_b, diag, n):
    picked = jnp.where(diag, col_b, 0.0)
    return jnp.concatenate([jnp.sum(picked[h * n:(h + 1) * n], axis=0, keepdims=True)
                            for h in range(col_b.shape[0] // n)], axis=0)


def _shift_rows(x, prev8, s):
    rolled = pltpu.roll(x, s, axis=0)
    top = jnp.where(_iota2((SUBLANES, 1), 0) < s, pltpu.roll(prev8, s, axis=0), rolled[:SUBLANES])
    return jnp.concatenate([top, rolled[SUBLANES:]], axis=0)


def _head_expand(lanes_per_head=M_HEADDIM):
    shape = (DT_PAD, M_HEADS * lanes_per_head)
    head_of_lane = _iota2(shape, 1) >> int(math.log2(lanes_per_head))
    return jnp.where(_iota2(shape, 0) == head_of_lane, 1.0, 0.0).astype(BF16)


def _gated_rmsnorm(y, z, gnorm_w):
    yz = y * _silu(z)
    gw = M_WIDTH // M_GROUPS
    outs = []
    for gi in range(M_GROUPS):
        yg = yz[:, gi * gw:(gi + 1) * gw]
        ms = jnp.mean(yg * yg, axis=-1, keepdims=True)
        outs.append(yg * lax.rsqrt(ms + RMS_EPS))
    return jnp.concatenate(outs, axis=1) * gnorm_w


def _mamba_tile_kernel(z_ref, xbc_ref, xprev_ref, dt_ref, convw_ref, convb_ref, dtb_ref, alog_ref,
                       dskip_ref, gnw_ref, o_ref, hfin_ref, h_scr):
    tile = MAMBA_TILE
    q = SSD_CHUNK
    ti = pl.program_id(1)

    @pl.when(ti == 0)
    def _():
        h_scr[...] = jnp.zeros_like(h_scr)

    xb = xbc_ref[...]
    prev8 = jnp.where(ti == 0, 0.0, xprev_ref[...])
    conv = convb_ref[...] + xb * convw_ref[CONV_K - 1:CONV_K, :]
    for s in range(1, CONV_K):
        conv = conv + _shift_rows(xb, prev8, s) * convw_ref[CONV_K - 1 - s:CONV_K - s, :]
    xact = _silu(conv)
    xs = xact[:, :M_WIDTH]
    gs = M_GROUPS * D_STATE
    bm = xact[:, M_WIDTH:M_WIDTH + gs]
    cm = xact[:, M_WIDTH + gs:]

    dt = _softplus(dt_ref[...] + dtb_ref[...])
    adt = dt * (-jnp.exp(alog_ref[...]))
    x_dt = xs * _mm_sel_r(dt, _head_expand(), 3)

    sh_q = int(math.log2(q))
    ti_i = _iota2((tile, tile), 0)
    ti_j = _iota2((tile, tile), 1)
    chunk_tril = jnp.where((ti_i >= ti_j) & ((ti_i >> sh_q) == (ti_j >> sh_q)), 1.0, 0.0).astype(BF16)
    cs = _mm_sel_l(chunk_tril, adt, 3)
    cs_t = cs.T
    incl = _iota2((q, q), 0) >= _iota2((q, q), 1)
    lane = _iota2((1, 2 * M_HEADDIM), 1)
    first = lane < M_HEADDIM
    hpg = M_HEADS // M_GROUPS

    state = [h_scr[p] for p in range(M_HEADS // 2)]
    y_rows = []
    for j in range(tile // q):
        rs = slice(j * q, (j + 1) * q)
        y_pairs = []
        for gi in range(M_GROUPS):
            bm_g = bm[rs, gi * D_STATE:(gi + 1) * D_STATE]
            cm_g = cm[rs, gi * D_STATE:(gi + 1) * D_STATE]
            cb = _mm_nt(cm_g, bm_g)
            bm_t = bm_g.T
            for pp in range(hpg // 2):
                p = gi * (hpg // 2) + pp
                x_p = x_dt[rs, 2 * M_HEADDIM * p:2 * M_HEADDIM * (p + 1)]
                y_diag = jnp.zeros((q, 2 * M_HEADDIM), F32)
                contrib = jnp.zeros((D_STATE, 2 * M_HEADDIM), F32)
                scale_in = []
                scale_end = []
                for e in range(2):
                    head = 2 * p + e
                    m = first if e == 0 else jnp.logical_not(first)
                    a_col = cs[rs, head:head + 1]
                    a_row = cs_t[head:head + 1, rs]
                    a_end = cs_t[head:head + 1, (j + 1) * q - 1:(j + 1) * q]
                    lmat = jnp.exp(jnp.where(incl, a_col - a_row, -jnp.inf))
                    x_m = jnp.where(m, x_p, 0.0)
                    y_diag = y_diag + _mm(cb * lmat, x_m)
                    contrib = contrib + _mm(bm_t * jnp.exp(a_end - a_row), x_m)
                    scale_in.append(jnp.exp(a_col))
                    scale_end.append(jnp.exp(a_end))
                y_off = _mm(cm_g, state[p]) * jnp.where(first, scale_in[0], scale_in[1])
                state[p] = state[p] * jnp.where(first, scale_end[0], scale_end[1]) + contrib
                y_pairs.append(y_diag + y_off)
        y_rows.append(jnp.concatenate(y_pairs, axis=1))
    for p in range(M_HEADS // 2):
        h_scr[p] = state[p]

    y = jnp.concatenate(y_rows, axis=0) + dskip_ref[...] * xs
    o_ref[...] = _gated_rmsnorm(y, z_ref[...], gnw_ref[...])

    @pl.when(ti == pl.num_programs(1) - 1)
    def _():
        for p in range(M_HEADS // 2):
            hfin_ref[0, 2 * M_HEADDIM * p:2 * M_HEADDIM * (p + 1), :] = h_scr[p].T


def _mamba_prompt(z2d, xbc2d, dt2d, batch, seq, wts):
    tile = MAMBA_TILE
    ntile = seq // tile
    rows8 = seq // SUBLANES

    def prev_map(b, i):
        return (b * rows8 + jnp.maximum(i * (tile // SUBLANES) - 1, 0), 0)

    tok = lambda w: pl.BlockSpec((tile, w), lambda b, i: (b * ntile + i, 0))
    return pl.pallas_call(
        _mamba_tile_kernel,
        grid=(batch, ntile),
        in_specs=[tok(M_WIDTH), tok(CONV_DIM), pl.BlockSpec((SUBLANES, CONV_DIM), prev_map), tok(DT_PAD)]
                 + [_const_spec(w.shape) for w in wts],
        out_specs=[tok(M_WIDTH),
                   pl.BlockSpec((1, M_HEADS * M_HEADDIM, D_STATE), lambda b, i: (b, 0, 0))],
        out_shape=[jax.ShapeDtypeStruct((batch * seq, M_WIDTH), F32),
                   jax.ShapeDtypeStruct((batch, M_HEADS * M_HEADDIM, D_STATE), F32)],
        scratch_shapes=[pltpu.VMEM((M_HEADS // 2, D_STATE, 2 * M_HEADDIM), F32)],
        compiler_params=pltpu.CompilerParams(dimension_semantics=("arbitrary", "arbitrary"),
                                             vmem_limit_bytes=VMEM_LIMIT),
        name="mamba_prompt",
    )(z2d, xbc2d, xbc2d, dt2d, *wts)


def _mamba_prep_kernel(xbc_ref, cprev_ref, dt_ref, convw_ref, convb_ref, dtb_ref, alog_ref,
                       xs_ref, xdt_ref, dec_ref, bm_ref, cm_ref):
    conv = convb_ref[...] + xbc_ref[...] * convw_ref[CONV_K - 1:CONV_K, :]
    for j in range(CONV_K - 1):
        conv = conv + cprev_ref[:, j * CONV_DIM:(j + 1) * CONV_DIM] * convw_ref[j:j + 1, :]
    xact = _silu(conv)
    xs = xact[:, :M_WIDTH]
    gs = M_GROUPS * D_STATE
    dt = _softplus(dt_ref[...] + dtb_ref[...])
    adt = dt * (-jnp.exp(alog_ref[...]))
    expand = _head_expand()
    xs_ref[...] = xs
    xdt_ref[...] = xs * _mm_sel_r(dt, expand, 3)
    dec_ref[...] = jnp.exp(_mm_sel_r(adt, _head_expand(D_STATE), 3))
    hpg = M_HEADS // M_GROUPS
    per_head = lambda m: jnp.concatenate(
        [m[:, (e // hpg) * D_STATE:(e // hpg + 1) * D_STATE] for e in range(M_HEADS)], axis=1)
    bm_ref[...] = per_head(xact[:, M_WIDTH:M_WIDTH + gs])
    cm_ref[...] = per_head(xact[:, M_WIDTH + gs:])


def _mamba_sample_prep(xbc, cprev, dt, wts):
    n = xbc.shape[0]
    args = (xbc, cprev, dt) + tuple(wts)
    widths = (M_WIDTH, M_WIDTH, M_HEADS * D_STATE, M_HEADS * D_STATE, M_HEADS * D_STATE)
    return pl.pallas_call(
        _mamba_prep_kernel,
        grid=(1,),
        in_specs=[_const_spec(a.shape) for a in args],
        out_specs=[_const_spec((n, w)) for w in widths],
        out_shape=[jax.ShapeDtypeStruct((n, w), F32) for w in widths],
        compiler_params=pltpu.CompilerParams(vmem_limit_bytes=VMEM_LIMIT),
        name="mamba_sample_prep",
    )(*args)


def _mamba_step_kernel(xs_ref, xdt_ref, dec_ref, bm_ref, cm_ref, h_ref, dskip_ref, y_ref, ho_ref):
    n = M_HEADS * M_HEADDIM
    ones_in = jnp.ones((M_HEADDIM, D_STATE), BF16)
    ones_out = jnp.ones((D_STATE, M_HEADDIM), BF16)
    diag = (_iota2((n, M_HEADDIM), 0) & (M_HEADDIM - 1)) == _iota2((n, M_HEADDIM), 1)

    def batch_row(bi, carry):
        rows = pl.ds(pl.multiple_of(bi * M_HEADS, M_HEADS), M_HEADS)
        hrows = pl.ds(pl.multiple_of(bi * n, n), n)
        x_col = _mm(jnp.where(diag, _per_head_rows(xdt_ref[rows, :], M_HEADDIM), 0.0), ones_in)
        h2 = (h_ref[hrows, :] * _per_head_rows(dec_ref[rows, :], M_HEADDIM)
              + x_col * _per_head_rows(bm_ref[rows, :], M_HEADDIM))
        ho_ref[hrows, :] = h2
        y = _rows_from_columns(_mm(h2 * _per_head_rows(cm_ref[rows, :], M_HEADDIM), ones_out),
                               diag, M_HEADDIM)
        y_ref[rows, :] = y + dskip_ref[...] * xs_ref[rows, :]
        return carry

    lax.fori_loop(0, STEP_BATCH, batch_row, 0, unroll=True)


def _mamba_sample_step(xs, xdt, dec, bm, cm, state, dskip_rows):
    nb = state.shape[0]
    rows = STEP_BATCH * M_HEADS
    vec_spec = pl.BlockSpec((rows, M_HEADDIM), lambda i: (i, 0))
    bc_spec = pl.BlockSpec((rows, D_STATE), lambda i: (i, 0))
    st_spec = pl.BlockSpec((rows * M_HEADDIM, D_STATE), lambda i: (i, 0))
    flat = lambda x, w: x.reshape(nb * M_HEADS, w)
    return pl.pallas_call(
        _mamba_step_kernel,
        grid=(nb // STEP_BATCH,),
        in_specs=[vec_spec] * 2 + [bc_spec] * 3 + [st_spec, _const_spec((M_HEADS, M_HEADDIM))],
        out_specs=[vec_spec, st_spec],
        out_shape=[jax.ShapeDtypeStruct((nb * M_HEADS, M_HEADDIM), F32),
                   jax.ShapeDtypeStruct((nb * M_HEADS * M_HEADDIM, D_STATE), F32)],
        compiler_params=pltpu.CompilerParams(dimension_semantics=("arbitrary",),
                                             vmem_limit_bytes=VMEM_LIMIT),
        name="mamba_sample_step",
    )(flat(xs, M_HEADDIM), flat(xdt, M_HEADDIM), flat(dec, D_STATE), flat(bm, D_STATE),
      flat(cm, D_STATE), state.reshape(nb * M_HEADS * M_HEADDIM, D_STATE), dskip_rows)


FF_SLAB = 1024


def _ffn_kernel(has_gate, *refs):
    if has_gate:
        (x_ref, or_ref, ym_ref, z_ref, gnw_ref, wout_ref, g1_ref, b1_ref, w1_ref, w2_ref,
         g2_ref, b2_ref, y_ref) = refs
        o_m = _gated_rmsnorm(ym_ref[...], z_ref[...], gnw_ref[...])
    else:
        (x_ref, or_ref, om_ref, wout_ref, g1_ref, b1_ref, w1_ref, w2_ref,
         g2_ref, b2_ref, y_ref) = refs
        o_m = om_ref[...]
    mix = _mm(or_ref[...], wout_ref[:R_WIDTH, :]) + _mm(o_m, wout_ref[R_WIDTH:, :])
    h = _layer_norm(ALPHA * x_ref[...] + mix, g1_ref[...], b1_ref[...], LN_EPS)
    hb = h.astype(BF16)
    ff = jnp.zeros_like(h)
    for j in range(D_FF // FF_SLAB):
        t = jnp.dot(hb, w1_ref[:, j * FF_SLAB:(j + 1) * FF_SLAB], preferred_element_type=F32)
        t = jnp.square(jnp.maximum(t, 0.0))
        ff = ff + jnp.dot(t.astype(BF16), w2_ref[j * FF_SLAB:(j + 1) * FF_SLAB, :],
                          preferred_element_type=F32)
    y_ref[...] = _layer_norm(ALPHA * h + ff, g2_ref[...], b2_ref[...], LN_EPS)


def _ffn(x2d, o_r, o_m, gate, wout, g1, b1, w1, w2, g2, b2, tm):
    n = x2d.shape[0]
    tok = lambda w: pl.BlockSpec((tm, w), lambda i: (i, 0))
    once = lambda a: pl.BlockSpec(a.shape, lambda i: (0,) * a.ndim, pipeline_mode=pl.Buffered(1))
    acts = [x2d, o_r, o_m]
    act_specs = [tok(D_MODEL), tok(R_WIDTH), tok(M_WIDTH)]
    if gate is not None:
        acts += [gate[0], gate[1]]
        act_specs += [tok(M_WIDTH), once(gate[1])]
    wts = [wout, g1, b1, w1, w2, g2, b2]
    return pl.pallas_call(
        functools.partial(_ffn_kernel, gate is not None),
        grid=(n // tm,),
        in_specs=act_specs + [once(w) for w in wts],
        out_specs=tok(D_MODEL),
        out_shape=jax.ShapeDtypeStruct((n, D_MODEL), F32),
        compiler_params=pltpu.CompilerParams(dimension_semantics=("arbitrary",),
                                             vmem_limit_bytes=VMEM_LIMIT),
        name="ffn",
    )(*acts, *wts)


def _block_diag_ones():
    idx = np.arange(R_WIDTH) // R_HEAD
    return jnp.asarray(idx[:, None] == idx[None, :], dtype=BF16)


def kernel(x_prompt, x_sample, state_shift, state_wkv, state_conv, state_ssm, w_in, mu_shift, w0, w_up, a0, a_up, g_up, k_k, k_a, r_k, lnx_w, lnx_b, conv_w, conv_b, dt_bias, a_log, d_skip, gnorm_w, w_out, ln1_g, ln1_b, w_ff1, w_ff2, ln2_g, ln2_b):
    assert w_in.shape[0] == DEPTH and x_sample.shape[1] == 1
    batch, seq, _ = x_prompt.shape
    nb = x_sample.shape[0]
    assert seq % MAMBA_TILE == 0 and seq % RWKV_TILE == 0 and seq % DENSE_ROWS == 0
    assert nb % STEP_BATCH == 0
    row = lambda a: a.reshape(1, -1)

    w_main = w_in[0][:, :PROJ_MAIN].astype(BF16)
    w_dt = jnp.pad(w_in[0][:, PROJ_MAIN:], ((0, 0), (0, DT_PAD - M_HEADS))).astype(BF16)
    zl = jnp.zeros((D_DECAY_LORA, R_WIDTH), F32)
    wup_p = jnp.concatenate([w_up[0], zl], axis=0).astype(BF16)
    aup_p = jnp.concatenate([zl, a_up[0]], axis=0).astype(BF16)
    bd = _block_diag_ones()
    prep_w = (row(mu_shift[0]), row(w0[0]), wup_p, row(a0[0]), aup_p, g_up[0].astype(BF16),
              row(k_k[0]), row(k_a[0]), bd)
    post_w = (row(r_k[0]), row(lnx_w[0]), row(lnx_b[0]))
    pad_dt = lambda a: jnp.pad(a[0], (0, DT_PAD - M_HEADS)).reshape(1, DT_PAD)
    dskip_lanes = jnp.repeat(d_skip[0], M_HEADDIM).reshape(1, M_WIDTH)
    dskip_rows = jnp.broadcast_to(d_skip[0][:, None], (M_HEADS, M_HEADDIM))
    conv_wts = (conv_w[0], row(conv_b[0]), pad_dt(dt_bias), pad_dt(a_log))
    ffn_w = (w_out[0].astype(BF16), row(ln1_g[0]), row(ln1_b[0]), w_ff1[0].astype(BF16),
             w_ff2[0].astype(BF16), row(ln2_g[0]), row(ln2_b[0]))

    xp = x_prompt.reshape(batch * seq, D_MODEL)
    pr_p, z_p, xbc_p, dt_p = _proj(xp, w_main, w_dt, DENSE_ROWS)
    o_r_p, wkv_p = _rwkv_prompt(pr_p, batch, seq, prep_w[:-1] + post_w + (bd,))
    o_m_p, ssm_p = _mamba_prompt(z_p, xbc_p, dt_p, batch, seq,
                                 conv_wts + (dskip_lanes, row(gnorm_w[0])))
    y_p = _ffn(xp, o_r_p, o_m_p, None, *ffn_w, tm=DENSE_ROWS)
    shift_p = pr_p.reshape(batch, seq, RWKV_PROJ)[:, -1]
    conv_p = xbc_p.reshape(batch, seq, CONV_DIM)[:, seq - (CONV_K - 1):]

    xs2d = x_sample.reshape(nb, D_MODEL)
    pr_s, z_s, xbc_s, dt_s = _proj(xs2d, w_main, w_dt, nb)
    col = lambda a: jnp.broadcast_to(a.reshape(R_WIDTH, 1), (R_WIDTH, nb))
    o_r_t, wkv_t = _rwkv_sample(pr_s, state_shift[0], jnp.transpose(state_wkv[0], (1, 2, 3, 0)),
                                prep_w, (col(r_k[0]), col(lnx_w[0]), col(lnx_b[0])))
    o_r_s = o_r_t.T
    wkv_s = jnp.transpose(wkv_t, (3, 0, 1, 2))
    cprev = state_conv[0].reshape(nb, (CONV_K - 1) * CONV_DIM)
    xs_s, xdt_s, dec_s, bm_s, cm_s = _mamba_sample_prep(xbc_s, cprev, dt_s, conv_wts)
    y_m_s, ssm_s = _mamba_sample_step(xs_s, xdt_s, dec_s, bm_s, cm_s, state_ssm[0], dskip_rows)
    y_s = _ffn(xs2d, o_r_s.reshape(nb, R_WIDTH), y_m_s.reshape(nb, M_WIDTH),
               (z_s, row(gnorm_w[0])), *ffn_w, tm=nb)
    conv_s = jnp.concatenate([state_conv[0][:, 1:], xbc_s[:, None, :]], axis=1)

    return (y_p.reshape(batch, seq, D_MODEL), y_s.reshape(nb, 1, D_MODEL),
            shift_p[None], wkv_p[None], conv_p[None],
            ssm_p.reshape(1, batch, M_HEADS, M_HEADDIM, D_STATE),
            pr_s[None], wkv_s.reshape(1, nb, R_HEADS, R_HEAD, R_HEAD), conv_s[None],
            ssm_s.reshape(1, nb, M_HEADS, M_HEADDIM, D_STATE))
```

```python
import functools
import math

import numpy as np
import jax
import jax.numpy as jnp
from jax import lax
from jax.experimental import pallas as pl
from jax.experimental.pallas import tpu as pltpu

F32 = jnp.float32
BF16 = jnp.bfloat16

D_MODEL = 1024
R_WIDTH = 512
R_HEAD = 64
R_HEADS = 8
D_DECAY_LORA = 64
D_AAA_LORA = 64
D_GATE_LORA = 128
M_WIDTH = 512
M_HEADDIM = 64
M_HEADS = 8
M_GROUPS = 2
D_STATE = 128
CONV_K = 4
CONV_DIM = M_WIDTH + 2 * M_GROUPS * D_STATE
D_FF = 4 * D_MODEL
RWKV_PROJ = 3 * R_WIDTH + D_DECAY_LORA + D_AAA_LORA + D_GATE_LORA
DT_PAD = 128
PROJ_MAIN = RWKV_PROJ + M_WIDTH + CONV_DIM
DEPTH = 1
ALPHA = (2 * DEPTH) ** 0.25
LN_EPS = 1e-5
LNX_EPS = 64e-5
RMS_EPS = 1e-5

RWKV_CHUNK = 64
RWKV_TILE = 256
RWKV_ROWS = 2
SSD_CHUNK = 128
MAMBA_TILE = 256
SUBLANES = 8
DENSE_ROWS = 512
VMEM_LIMIT = 48 * 1024 * 1024

_NT = (((1,), (1,)), ((), ()))


def _mm(a, b):
    return jnp.dot(a.astype(BF16), b.astype(BF16), preferred_element_type=F32)


def _mm_nt(a, b):
    return lax.dot_general(a.astype(BF16), b.astype(BF16), _NT, preferred_element_type=F32)


def _split(x, passes):
    parts = []
    for _ in range(passes):
        p = x.astype(BF16)
        parts.append(p)
        x = x - p.astype(F32)
    return parts


def _mm_sel_r(x, sel, passes):
    return sum(jnp.dot(p, sel, preferred_element_type=F32) for p in _split(x, passes))


def _mm_sel_l(sel, x, passes):
    return sum(jnp.dot(sel, p, preferred_element_type=F32) for p in _split(x, passes))


def _softplus(x):
    return jnp.maximum(x, 0.0) + jnp.log(1.0 + jnp.exp(-jnp.abs(x)))


def _sigmoid(x):
    return 0.5 + 0.5 * jnp.tanh(0.5 * x)


def _silu(x):
    return x * _sigmoid(x)


def _layer_norm(x, g, b, eps):
    mu = jnp.mean(x, axis=-1, keepdims=True)
    d = x - mu
    var = jnp.mean(d * d, axis=-1, keepdims=True)
    return d * lax.rsqrt(var + eps) * g + b


def _iota2(shape, axis):
    return lax.broadcasted_iota(jnp.int32, shape, axis)


def _proj_kernel(x_ref, w_ref, wdt_ref, pr_ref, z_ref, xbc_ref, dt_ref):
    xb = x_ref[...].astype(BF16)
    res = jnp.dot(xb, w_ref[...], preferred_element_type=F32)
    c0 = RWKV_PROJ
    c1 = c0 + M_WIDTH
    pr_ref[...] = res[:, :c0]
    z_ref[...] = res[:, c0:c1]
    xbc_ref[...] = res[:, c1:]
    dt_ref[...] = jnp.dot(xb, wdt_ref[...], preferred_element_type=F32)


def _proj(x2d, w_bf, wdt_bf, tm):
    n = x2d.shape[0]
    widths = (RWKV_PROJ, M_WIDTH, CONV_DIM, DT_PAD)
    once = lambda a: pl.BlockSpec(a.shape, lambda i: (0, 0), pipeline_mode=pl.Buffered(1))
    return pl.pallas_call(
        _proj_kernel,
        grid=(n // tm,),
        in_specs=[pl.BlockSpec((tm, D_MODEL), lambda i: (i, 0)), once(w_bf), once(wdt_bf)],
        out_specs=[pl.BlockSpec((tm, w), lambda i: (i, 0)) for w in widths],
        out_shape=[jax.ShapeDtypeStruct((n, w), F32) for w in widths],
        compiler_params=pltpu.CompilerParams(dimension_semantics=("arbitrary",),
                                             vmem_limit_bytes=VMEM_LIMIT),
        name="proj",
    )(x2d, w_bf, wdt_bf)


def _rwkv_prep(pr, prev, mu, w0, wup, a0, aup, gup, k_k, k_a, bd):
    u = pr + (prev - pr) * mu
    r = u[:, 0:R_WIDTH]
    k = u[:, R_WIDTH:2 * R_WIDTH]
    v = u[:, 2 * R_WIDTH:3 * R_WIDTH]
    lora = u[:, 3 * R_WIDTH:3 * R_WIDTH + D_DECAY_LORA + D_AAA_LORA]
    ug = u[:, 3 * R_WIDTH + D_DECAY_LORA + D_AAA_LORA:]
    w_log = -_softplus(-(w0 + _mm(jnp.tanh(lora), wup))) - 0.5
    logw = -jnp.exp(w_log)
    a = _sigmoid(a0 + _mm(lora, aup))
    g = _mm(_sigmoid(ug), gup)
    kk = k * k_k
    ss = _mm_sel_r(kk * kk, bd, 1)
    kkn = kk * lax.rsqrt(jnp.maximum(ss, 1e-24))
    k2 = k * (1.0 + (a - 1.0) * k_a)
    return r, logw, k2, v, -kkn, kkn * a, g


def _rwkv_post(y, r, k2, v, g, r_k, lnx_w, lnx_b, bd):
    inv = 1.0 / R_HEAD
    mu = _mm_sel_r(y, bd, 1) * inv
    d = y - mu
    var = _mm_sel_r(d * d, bd, 1) * inv
    yn = d * lax.rsqrt(var + LNX_EPS) * lnx_w + lnx_b
    bonus = _mm_sel_r(r * k2 * r_k, bd, 1)
    return (yn + bonus * v) * g


def _rwkv_tile_kernel(pr_ref, prevblk_ref, mu_ref, w0_ref, wup_ref, a0_ref, aup_ref, gup_ref,
                      kk_ref, ka_ref, rk_ref, lnw_ref, lnb_ref, bd_ref,
                      o_ref, sfin_ref, g_scr):
    c = RWKV_CHUNK
    tile = RWKV_TILE
    nseq = RWKV_ROWS
    nck = tile // c
    hw = 2 * R_HEAD
    npair = R_HEADS // 2
    ti = pl.program_id(1)

    @pl.when(ti == 0)
    def _():
        g_scr[...] = jnp.zeros_like(g_scr)

    pr = jnp.concatenate([pr_ref[s] for s in range(nseq)], axis=0)
    prev = pltpu.roll(pr, 1, axis=0)
    row_id = _iota2((nseq * tile, 1), 0)
    for s in range(nseq):
        prev_row = jnp.where(ti == 0, 0.0, prevblk_ref[s, SUBLANES - 1:SUBLANES, :])
        prev = jnp.where(row_id == s * tile, prev_row, prev)
    head_ones = bd_ref[...]
    r, logw, k2, v, av, bv, g = _rwkv_prep(pr, prev, mu_ref[...], w0_ref[...], wup_ref[...],
                                           a0_ref[...], aup_ref[...], gup_ref[...],
                                           kk_ref[...], ka_ref[...], head_ones)

    sh_c = int(math.log2(c))
    ti_i = _iota2((tile, tile), 0)
    ti_j = _iota2((tile, tile), 1)
    chunk_tril = jnp.where((ti_i >= ti_j) & ((ti_i >> sh_c) == (ti_j >> sh_c)), 1.0, 0.0).astype(BF16)
    cum = jnp.concatenate([_mm_sel_l(chunk_tril, logw[s * tile:(s + 1) * tile], 3) for s in range(nseq)],
                          axis=0)
    cl = jnp.concatenate(
        [jnp.broadcast_to(cum[(q + 1) * c - 1:(q + 1) * c, :], (c, R_WIDTH)) for q in range(nseq * nck)],
        axis=0)
    p_out = jnp.exp(-cum)
    p_end = jnp.exp(cl - cum)
    rt = r * jnp.exp(cum)
    at = av * jnp.exp(cum - logw)
    bt = bv * p_out
    kt = k2 * p_out
    bh = bv * p_end
    kh = k2 * p_end
    pc = jnp.exp(cl)

    ii = _iota2((c, hw), 0)
    jj = _iota2((c, hw), 1) & (c - 1)
    strict = ii > jj
    incl = ii >= jj
    lane = _iota2((1, hw), 1)
    keep = (jnp.where(lane < R_HEAD, 1.0, 0.0).astype(BF16), jnp.where(lane >= R_HEAD, 1.0, 0.0).astype(BF16))
    e_r = _iota2((hw, hw), 0)
    e_c = _iota2((hw, hw), 1)
    eye_hw = e_r == e_c
    blockdiag = (e_r >= R_HEAD) == (e_c >= R_HEAD)

    def bd(x):
        xb = x.astype(BF16)
        return jnp.concatenate([xb * keep[0], xb * keep[1]], axis=0)

    def pdot(a, b_bd):
        return jnp.dot(a.astype(BF16), b_bd, preferred_element_type=F32)

    qp = [(q, p) for q in range(nseq * nck) for p in range(npair)]
    blk = lambda x, q, p: x[q * c:(q + 1) * c, hw * p:hw * (p + 1)]
    lhs = [jnp.concatenate([blk(at, *k), blk(rt, *k)], axis=0).astype(BF16) for k in qp]
    sc = [lax.dot_general(x, jnp.concatenate([bd(blk(bt, *k)), bd(blk(kt, *k))], axis=0), _NT,
                          preferred_element_type=F32) for x, k in zip(lhs, qp)]
    a_ab = [jnp.where(strict, s[:c, :hw], 0.0) for s in sc]
    a_rb = [jnp.where(incl, s[c:, :hw], 0.0) for s in sc]
    a_akrk = [jnp.concatenate([jnp.where(strict, s[:c, hw:], 0.0), jnp.where(incl, s[c:, hw:], 0.0)], axis=0)
              for s in sc]
    v_bd = [bd(blk(v, *k)) for k in qp]
    akrk_v = [pdot(a, x) for a, x in zip(a_akrk, v_bd)]

    sh = 4
    eye = jnp.where(ii == jj, 1.0, 0.0)
    npow = [jnp.where((ii >> sh) == (jj >> sh), n, 0.0) for n in a_ab]
    t_inv = [eye + n for n in npow]
    npow = [pdot(n, bd(n)) for n in npow]
    for _ in range(sh - 2):
        both = [pdot(jnp.concatenate([t, n], axis=0), bd(n)) for t, n in zip(t_inv, npow)]
        t_inv = [t + x[:c] for t, x in zip(t_inv, both)]
        npow = [x[c:] for x in both]
    t_inv = [t + pdot(t, bd(n)) for t, n in zip(t_inv, npow)]
    while (1 << sh) < c:
        off = ((ii >> (sh + 1)) == (jj >> (sh + 1))) & (((ii >> sh) & 1) > ((jj >> sh) & 1))
        tn = [pdot(t, bd(jnp.where(off, n, 0.0))) for t, n in zip(t_inv, a_ab)]
        t_inv = [t + pdot(x, bd(t)) for t, x in zip(t_inv, tn)]
        sh += 1

    tx = [pdot(t, jnp.concatenate([bd(blk(at, *k)), bd(x[:c])], axis=1))
          for t, x, k in zip(t_inv, akrk_v, qp)]
    rbx = [pdot(a, jnp.concatenate([bd(x[:, :hw]), bd(x[:, hw:])], axis=1)) for a, x in zip(a_rb, tx)]
    rp_pair = {k: blk(rt, *k) + x[:, :hw] for k, x in zip(qp, rbx)}
    y0_pair = {k: x[:, hw:] + y[c:] for k, x, y in zip(qp, rbx, akrk_v)}

    zeros_c = jnp.zeros((c, hw), F32)
    wt = {k: jnp.concatenate([blk(bh, *k), blk(kh, *k)], axis=0).T for k in qp}
    trans = {k: _mm(wt[k], jnp.concatenate([x, jnp.concatenate([zeros_c, blk(v, *k)], axis=1)], axis=0))
             for k, x in zip(qp, tx)}
    m_ba = {k: jnp.where(blockdiag, trans[k][:, :hw], 0.0) for k in qp}
    h0c = {k: jnp.where(blockdiag, trans[k][:, hw:], 0.0) for k in qp}
    pc_col = {k: jnp.sum(jnp.where(eye_hw, blk(pc, *k)[:1], 0.0), axis=1, keepdims=True) for k in qp}

    state = [g_scr[i] for i in range(nseq * npair)]
    y_rows = [None] * (nseq * nck)
    for q in range(nck):
        for s in range(nseq):
            qq = s * nck + q
            st = state[s * npair:(s + 1) * npair]
            on_st = [_mm(jnp.concatenate([rp_pair[qq, p], m_ba[qq, p]], axis=0), st[p]) for p in range(npair)]
            y_rows[qq] = jnp.concatenate([on_st[p][:c] + y0_pair[qq, p] for p in range(npair)], axis=1)
            state[s * npair:(s + 1) * npair] = [
                pc_col[qq, p] * st[p] + on_st[p][c:] + h0c[qq, p] for p in range(npair)]
    for i in range(nseq * npair):
        g_scr[i] = state[i]

    y = jnp.concatenate(y_rows, axis=0)
    out = _rwkv_post(y, r, k2, v, g, rk_ref[...], lnw_ref[...], lnb_ref[...], head_ones)
    for s in range(nseq):
        o_ref[s] = out[s * tile:(s + 1) * tile]

    @pl.when(ti == pl.num_programs(1) - 1)
    def _():
        for s in range(nseq):
            for p in range(npair):
                gt = g_scr[s * npair + p].T
                sfin_ref[s, 2 * p] = gt[:R_HEAD, :R_HEAD]
                sfin_ref[s, 2 * p + 1] = pltpu.roll(gt, R_HEAD, axis=1)[R_HEAD:, :R_HEAD]


def _const_spec(shape):
    return pl.BlockSpec(shape, lambda *_: (0,) * len(shape))


def _rwkv_prompt(pr3d, wts):
    batch, seq, _ = pr3d.shape
    tile = RWKV_TILE
    nseq = RWKV_ROWS

    def prev_map(b, i):
        return (b, jnp.maximum(i * (tile // SUBLANES) - 1, 0), 0)

    w_specs = [_const_spec(w.shape) for w in wts]
    return pl.pallas_call(
        _rwkv_tile_kernel,
        grid=(batch // nseq, seq // tile),
        in_specs=[pl.BlockSpec((nseq, tile, RWKV_PROJ), lambda b, i: (b, i, 0)),
                  pl.BlockSpec((nseq, SUBLANES, RWKV_PROJ), prev_map)] + w_specs,
        out_specs=[pl.BlockSpec((nseq, tile, R_WIDTH), lambda b, i: (b, i, 0)),
                   pl.BlockSpec((nseq, R_HEADS, R_HEAD, R_HEAD), lambda b, i: (b, 0, 0, 0))],
        out_shape=[jax.ShapeDtypeStruct((batch, seq, R_WIDTH), F32),
                   jax.ShapeDtypeStruct((batch, R_HEADS, R_HEAD, R_HEAD), F32)],
        scratch_shapes=[pltpu.VMEM((nseq * (R_HEADS // 2), 2 * R_HEAD, 2 * R_HEAD), F32)],
        compiler_params=pltpu.CompilerParams(dimension_semantics=("arbitrary", "arbitrary"),
                                             vmem_limit_bytes=VMEM_LIMIT),
        name="rwkv_prompt",
    )(pr3d, pr3d, *wts)


def _rwkv_sample_kernel(pr_ref, prev_ref, mu_ref, w0_ref, wup_ref, a0_ref, aup_ref, gup_ref,
                        kk_ref, ka_ref, bd_ref, rk_ref, lnw_ref, lnb_ref, s_ref,
                        o_ref, so_ref, vec_scr):
    h = pl.program_id(0)

    @pl.when(h == 0)
    def _():
        r, logw, k2, v, av, bv, g = _rwkv_prep(pr_ref[...], prev_ref[...], mu_ref[...], w0_ref[...],
                                               wup_ref[...], a0_ref[...], aup_ref[...], gup_ref[...],
                                               kk_ref[...], ka_ref[...], bd_ref[...])
        for i, x in enumerate((r, jnp.exp(logw), k2, v, av, bv, g)):
            vec_scr[i] = x.T

    rows = pl.ds(pl.multiple_of(h * R_HEAD, R_HEAD), R_HEAD)
    r, w, k, v, a, b, g = (vec_scr[i, rows, :] for i in range(7))
    ys = []
    for vi in range(R_HEAD):
        s = s_ref[0, vi]
        sa = jnp.sum(s * a, axis=0, keepdims=True)
        s2 = s * w + sa * b + v[vi:vi + 1, :] * k
        so_ref[0, vi] = s2
        ys.append(jnp.sum(s2 * r, axis=0, keepdims=True))
    y = jnp.concatenate(ys, axis=0)
    mu = jnp.mean(y, axis=0, keepdims=True)
    d = y - mu
    var = jnp.mean(d * d, axis=0, keepdims=True)
    yn = d * lax.rsqrt(var + LNX_EPS) * lnw_ref[...] + lnb_ref[...]
    bonus = jnp.sum(r * k * rk_ref[...], axis=0, keepdims=True)
    o_ref[...] = (yn + bonus * v) * g


def _rwkv_sample(pr, prev, state_t, prep_w, post_cols):
    nb = pr.shape[0]
    args = (pr, prev) + tuple(prep_w)
    head_rows = pl.BlockSpec((R_HEAD, nb), lambda h: (h, 0))
    st_spec = pl.BlockSpec((1, R_HEAD, R_HEAD, nb), lambda h: (h, 0, 0, 0))
    return pl.pallas_call(
        _rwkv_sample_kernel,
        grid=(R_HEADS,),
        in_specs=[_const_spec(a.shape) for a in args] + [head_rows] * 3 + [st_spec],
        out_specs=[head_rows, st_spec],
        out_shape=[jax.ShapeDtypeStruct((R_WIDTH, nb), F32),
                   jax.ShapeDtypeStruct(state_t.shape, F32)],
        scratch_shapes=[pltpu.VMEM((7, R_WIDTH, nb), F32)],
        compiler_params=pltpu.CompilerParams(dimension_semantics=("arbitrary",),
                                             vmem_limit_bytes=VMEM_LIMIT),
        name="rwkv_sample",
    )(*args, *post_cols, state_t)


STEP_BATCH = 8


def _per_head_rows(x, n):
    return jnp.concatenate([jnp.broadcast_to(x[h:h + 1, :], (n, x.shape[1])) for h in range(x.shape[0])],
                           axis=0)


def _rows_from_columns(col_b, diag, n):
    picked = jnp.where(diag, col_b, 0.0)
    return jnp.concatenate([jnp.sum(picked[h * n:(h + 1) * n], axis=0, keepdims=True)
                            for h in range(col_b.shape[0] // n)], axis=0)


def _shift_rows(x, prev8, s):
    rolled = pltpu.roll(x, s, axis=0)
    top = jnp.where(_iota2((SUBLANES, 1), 0) < s, pltpu.roll(prev8, s, axis=0), rolled[:SUBLANES])
    return jnp.concatenate([top, rolled[SUBLANES:]], axis=0)


def _head_expand(lanes_per_head=M_HEADDIM):
    shape = (DT_PAD, M_HEADS * lanes_per_head)
    head_of_lane = _iota2(shape, 1) >> int(math.log2(lanes_per_head))
    return jnp.where(_iota2(shape, 0) == head_of_lane, 1.0, 0.0).astype(BF16)


def _gated_rmsnorm(y, z, gnorm_w):
    yz = y * _silu(z)
    gw = M_WIDTH // M_GROUPS
    outs = []
    for gi in range(M_GROUPS):
        yg = yz[:, gi * gw:(gi + 1) * gw]
        ms = jnp.mean(yg * yg, axis=-1, keepdims=True)
        outs.append(yg * lax.rsqrt(ms + RMS_EPS))
    return jnp.concatenate(outs, axis=1) * gnorm_w


def _mamba_tile_kernel(z_ref, xbc_ref, xprev_ref, dt_ref, convw_ref, convb_ref, dtb_ref, alog_ref,
                       dskip_ref, gnw_ref, o_ref, hfin_ref, h_scr):
    tile = MAMBA_TILE
    q = SSD_CHUNK
    ti = pl.program_id(1)

    @pl.when(ti == 0)
    def _():
        h_scr[...] = jnp.zeros_like(h_scr)

    xb = xbc_ref[...]
    prev8 = jnp.where(ti == 0, 0.0, xprev_ref[...])
    conv = convb_ref[...] + xb * convw_ref[CONV_K - 1:CONV_K, :]
    for s in range(1, CONV_K):
        conv = conv + _shift_rows(xb, prev8, s) * convw_ref[CONV_K - 1 - s:CONV_K - s, :]
    xact = _silu(conv)
    xs = xact[:, :M_WIDTH]
    gs = M_GROUPS * D_STATE
    bm = xact[:, M_WIDTH:M_WIDTH + gs]
    cm = xact[:, M_WIDTH + gs:]

    dt = _softplus(dt_ref[...] + dtb_ref[...])
    adt = dt * (-jnp.exp(alog_ref[...]))
    x_dt = xs * _mm_sel_r(dt, _head_expand(), 3)

    sh_q = int(math.log2(q))
    ti_i = _iota2((tile, tile), 0)
    ti_j = _iota2((tile, tile), 1)
    chunk_tril = jnp.where((ti_i >= ti_j) & ((ti_i >> sh_q) == (ti_j >> sh_q)), 1.0, 0.0).astype(BF16)
    cs = _mm_sel_l(chunk_tril, adt, 3)
    cs_t = cs.T
    incl = _iota2((q, q), 0) >= _iota2((q, q), 1)
    lane = _iota2((1, 2 * M_HEADDIM), 1)
    first = lane < M_HEADDIM
    hpg = M_HEADS // M_GROUPS

    state = [h_scr[p] for p in range(M_HEADS // 2)]
    y_rows = []
    for j in range(tile // q):
        rs = slice(j * q, (j + 1) * q)
        y_pairs = []
        for gi in range(M_GROUPS):
            bm_g = bm[rs, gi * D_STATE:(gi + 1) * D_STATE]
            cm_g = cm[rs, gi * D_STATE:(gi + 1) * D_STATE]
            cb = _mm_nt(cm_g, bm_g)
            bm_t = bm_g.T
            for pp in range(hpg // 2):
                p = gi * (hpg // 2) + pp
                x_p = x_dt[rs, 2 * M_HEADDIM * p:2 * M_HEADDIM * (p + 1)]
                y_diag = jnp.zeros((q, 2 * M_HEADDIM), F32)
                contrib = jnp.zeros((D_STATE, 2 * M_HEADDIM), F32)
                scale_in = []
                scale_end = []
                for e in range(2):
                    head = 2 * p + e
                    m = first if e == 0 else jnp.logical_not(first)
                    a_col = cs[rs, head:head + 1]
                    a_row = cs_t[head:head + 1, rs]
                    a_end = cs_t[head:head + 1, (j + 1) * q - 1:(j + 1) * q]
                    lmat = jnp.exp(jnp.where(incl, a_col - a_row, -jnp.inf))
                    x_m = jnp.where(m, x_p, 0.0)
                    y_diag = y_diag + _mm(cb * lmat, x_m)
                    contrib = contrib + _mm(bm_t * jnp.exp(a_end - a_row), x_m)
                    scale_in.append(jnp.exp(a_col))
                    scale_end.append(jnp.exp(a_end))
                y_off = _mm(cm_g, state[p]) * jnp.where(first, scale_in[0], scale_in[1])
                state[p] = state[p] * jnp.where(first, scale_end[0], scale_end[1]) + contrib
                y_pairs.append(y_diag + y_off)
        y_rows.append(jnp.concatenate(y_pairs, axis=1))
    for p in range(M_HEADS // 2):
        h_scr[p] = state[p]

    y = jnp.concatenate(y_rows, axis=0) + dskip_ref[...] * xs
    o_ref[...] = _gated_rmsnorm(y, z_ref[...], gnw_ref[...])

    @pl.when(ti == pl.num_programs(1) - 1)
    def _():
        for p in range(M_HEADS // 2):
            hfin_ref[0, 2 * M_HEADDIM * p:2 * M_HEADDIM * (p + 1), :] = h_scr[p].T


def _mamba_prompt(z2d, xbc2d, dt2d, batch, seq, wts):
    tile = MAMBA_TILE
    ntile = seq // tile
    rows8 = seq // SUBLANES

    def prev_map(b, i):
        return (b * rows8 + jnp.maximum(i * (tile // SUBLANES) - 1, 0), 0)

    tok = lambda w: pl.BlockSpec((tile, w), lambda b, i: (b * ntile + i, 0))
    return pl.pallas_call(
        _mamba_tile_kernel,
        grid=(batch, ntile),
        in_specs=[tok(M_WIDTH), tok(CONV_DIM), pl.BlockSpec((SUBLANES, CONV_DIM), prev_map), tok(DT_PAD)]
                 + [_const_spec(w.shape) for w in wts],
        out_specs=[tok(M_WIDTH),
                   pl.BlockSpec((1, M_HEADS * M_HEADDIM, D_STATE), lambda b, i: (b, 0, 0))],
        out_shape=[jax.ShapeDtypeStruct((batch * seq, M_WIDTH), F32),
                   jax.ShapeDtypeStruct((batch, M_HEADS * M_HEADDIM, D_STATE), F32)],
        scratch_shapes=[pltpu.VMEM((M_HEADS // 2, D_STATE, 2 * M_HEADDIM), F32)],
        compiler_params=pltpu.CompilerParams(dimension_semantics=("arbitrary", "arbitrary"),
                                             vmem_limit_bytes=VMEM_LIMIT),
        name="mamba_prompt",
    )(z2d, xbc2d, xbc2d, dt2d, *wts)


def _mamba_prep_kernel(xbc_ref, cprev_ref, dt_ref, convw_ref, convb_ref, dtb_ref, alog_ref,
                       xs_ref, xdt_ref, dec_ref, bm_ref, cm_ref):
    conv = convb_ref[...] + xbc_ref[...] * convw_ref[CONV_K - 1:CONV_K, :]
    for j in range(CONV_K - 1):
        conv = conv + cprev_ref[:, j * CONV_DIM:(j + 1) * CONV_DIM] * convw_ref[j:j + 1, :]
    xact = _silu(conv)
    xs = xact[:, :M_WIDTH]
    gs = M_GROUPS * D_STATE
    dt = _softplus(dt_ref[...] + dtb_ref[...])
    adt = dt * (-jnp.exp(alog_ref[...]))
    expand = _head_expand()
    xs_ref[...] = xs
    xdt_ref[...] = xs * _mm_sel_r(dt, expand, 3)
    dec_ref[...] = jnp.exp(_mm_sel_r(adt, _head_expand(D_STATE), 3))
    hpg = M_HEADS // M_GROUPS
    per_head = lambda m: jnp.concatenate(
        [m[:, (e // hpg) * D_STATE:(e // hpg + 1) * D_STATE] for e in range(M_HEADS)], axis=1)
    bm_ref[...] = per_head(xact[:, M_WIDTH:M_WIDTH + gs])
    cm_ref[...] = per_head(xact[:, M_WIDTH + gs:])


def _mamba_sample_prep(xbc, cprev, dt, wts):
    n = xbc.shape[0]
    args = (xbc, cprev, dt) + tuple(wts)
    widths = (M_WIDTH, M_WIDTH, M_HEADS * D_STATE, M_HEADS * D_STATE, M_HEADS * D_STATE)
    return pl.pallas_call(
        _mamba_prep_kernel,
        grid=(1,),
        in_specs=[_const_spec(a.shape) for a in args],
        out_specs=[_const_spec((n, w)) for w in widths],
        out_shape=[jax.ShapeDtypeStruct((n, w), F32) for w in widths],
        compiler_params=pltpu.CompilerParams(vmem_limit_bytes=VMEM_LIMIT),
        name="mamba_sample_prep",
    )(*args)


def _mamba_step_kernel(xs_ref, xdt_ref, dec_ref, bm_ref, cm_ref, h_ref, dskip_ref, y_ref, ho_ref):
    n = M_HEADS * M_HEADDIM
    ones_in = jnp.ones((M_HEADDIM, D_STATE), BF16)
    ones_out = jnp.ones((D_STATE, M_HEADDIM), BF16)
    diag = (_iota2((n, M_HEADDIM), 0) & (M_HEADDIM - 1)) == _iota2((n, M_HEADDIM), 1)

    def batch_row(bi, carry):
        rows = pl.ds(pl.multiple_of(bi * M_HEADS, M_HEADS), M_HEADS)
        hrows = pl.ds(pl.multiple_of(bi * n, n), n)
        x_col = _mm(jnp.where(diag, _per_head_rows(xdt_ref[rows, :], M_HEADDIM), 0.0), ones_in)
        h2 = (h_ref[hrows, :] * _per_head_rows(dec_ref[rows, :], M_HEADDIM)
              + x_col * _per_head_rows(bm_ref[rows, :], M_HEADDIM))
        ho_ref[hrows, :] = h2
        y = _rows_from_columns(_mm(h2 * _per_head_rows(cm_ref[rows, :], M_HEADDIM), ones_out),
                               diag, M_HEADDIM)
        y_ref[rows, :] = y + dskip_ref[...] * xs_ref[rows, :]
        return carry

    lax.fori_loop(0, STEP_BATCH, batch_row, 0, unroll=True)


def _mamba_sample_step(xs, xdt, dec, bm, cm, state, dskip_rows):
    nb = state.shape[0]
    rows = STEP_BATCH * M_HEADS
    vec_spec = pl.BlockSpec((rows, M_HEADDIM), lambda i: (i, 0))
    bc_spec = pl.BlockSpec((rows, D_STATE), lambda i: (i, 0))
    st_spec = pl.BlockSpec((rows * M_HEADDIM, D_STATE), lambda i: (i, 0))
    flat = lambda x, w: x.reshape(nb * M_HEADS, w)
    return pl.pallas_call(
        _mamba_step_kernel,
        grid=(nb // STEP_BATCH,),
        in_specs=[vec_spec] * 2 + [bc_spec] * 3 + [st_spec, _const_spec((M_HEADS, M_HEADDIM))],
        out_specs=[vec_spec, st_spec],
        out_shape=[jax.ShapeDtypeStruct((nb * M_HEADS, M_HEADDIM), F32),
                   jax.ShapeDtypeStruct((nb * M_HEADS * M_HEADDIM, D_STATE), F32)],
        compiler_params=pltpu.CompilerParams(dimension_semantics=("arbitrary",),
                                             vmem_limit_bytes=VMEM_LIMIT),
        name="mamba_sample_step",
    )(flat(xs, M_HEADDIM), flat(xdt, M_HEADDIM), flat(dec, D_STATE), flat(bm, D_STATE),
      flat(cm, D_STATE), state.reshape(nb * M_HEADS * M_HEADDIM, D_STATE), dskip_rows)


FF_SLAB = 1024


def _ffn_kernel(has_gate, *refs):
    if has_gate:
        (x_ref, or_ref, ym_ref, z_ref, gnw_ref, wout_ref, g1_ref, b1_ref, w1_ref, w2_ref,
         g2_ref, b2_ref, y_ref) = refs
        o_m = _gated_rmsnorm(ym_ref[...], z_ref[...], gnw_ref[...])
    else:
        (x_ref, or_ref, om_ref, wout_ref, g1_ref, b1_ref, w1_ref, w2_ref,
         g2_ref, b2_ref, y_ref) = refs
        o_m = om_ref[...]
    mix = _mm(or_ref[...], wout_ref[:R_WIDTH, :]) + _mm(o_m, wout_ref[R_WIDTH:, :])
    h = _layer_norm(ALPHA * x_ref[...] + mix, g1_ref[...], b1_ref[...], LN_EPS)
    hb = h.astype(BF16)
    ff = jnp.zeros_like(h)
    for j in range(D_FF // FF_SLAB):
        t = jnp.dot(hb, w1_ref[:, j * FF_SLAB:(j + 1) * FF_SLAB], preferred_element_type=F32)
        t = jnp.square(jnp.maximum(t, 0.0))
        ff = ff + jnp.dot(t.astype(BF16), w2_ref[j * FF_SLAB:(j + 1) * FF_SLAB, :],
                          preferred_element_type=F32)
    y_ref[...] = _layer_norm(ALPHA * h + ff, g2_ref[...], b2_ref[...], LN_EPS)


def _ffn(x2d, o_r, o_m, gate, wout, g1, b1, w1, w2, g2, b2, tm):
    n = x2d.shape[0]
    tok = lambda w: pl.BlockSpec((tm, w), lambda i: (i, 0))
    once = lambda a: pl.BlockSpec(a.shape, lambda i: (0,) * a.ndim, pipeline_mode=pl.Buffered(1))
    acts = [x2d, o_r, o_m]
    act_specs = [tok(D_MODEL), tok(R_WIDTH), tok(M_WIDTH)]
    if gate is not None:
        acts += [gate[0], gate[1]]
        act_specs += [tok(M_WIDTH), once(gate[1])]
    wts = [wout, g1, b1, w1, w2, g2, b2]
    return pl.pallas_call(
        functools.partial(_ffn_kernel, gate is not None),
        grid=(n // tm,),
        in_specs=act_specs + [once(w) for w in wts],
        out_specs=tok(D_MODEL),
        out_shape=jax.ShapeDtypeStruct((n, D_MODEL), F32),
        compiler_params=pltpu.CompilerParams(dimension_semantics=("arbitrary",),
                                             vmem_limit_bytes=VMEM_LIMIT),
        name="ffn",
    )(*acts, *wts)


def _block_diag_ones():
    idx = np.arange(R_WIDTH) // R_HEAD
    return jnp.asarray(idx[:, None] == idx[None, :], dtype=BF16)


def kernel(x_prompt, x_sample, state_shift, state_wkv, state_conv, state_ssm, w_in, mu_shift, w0, w_up, a0, a_up, g_up, k_k, k_a, r_k, lnx_w, lnx_b, conv_w, conv_b, dt_bias, a_log, d_skip, gnorm_w, w_out, ln1_g, ln1_b, w_ff1, w_ff2, ln2_g, ln2_b):
    assert w_in.shape[0] == DEPTH and x_sample.shape[1] == 1
    batch, seq, _ = x_prompt.shape
    nb = x_sample.shape[0]
    assert seq % MAMBA_TILE == 0 and seq % RWKV_TILE == 0 and seq % DENSE_ROWS == 0
    assert nb % STEP_BATCH == 0 and batch % RWKV_ROWS == 0
    row = lambda a: a.reshape(1, -1)

    w_main = w_in[0][:, :PROJ_MAIN].astype(BF16)
    w_dt = jnp.pad(w_in[0][:, PROJ_MAIN:], ((0, 0), (0, DT_PAD - M_HEADS))).astype(BF16)
    zl = jnp.zeros((D_DECAY_LORA, R_WIDTH), F32)
    wup_p = jnp.concatenate([w_up[0], zl], axis=0).astype(BF16)
    aup_p = jnp.concatenate([zl, a_up[0]], axis=0).astype(BF16)
    bd = _block_diag_ones()
    prep_w = (row(mu_shift[0]), row(w0[0]), wup_p, row(a0[0]), aup_p, g_up[0].astype(BF16),
              row(k_k[0]), row(k_a[0]), bd)
    post_w = (row(r_k[0]), row(lnx_w[0]), row(lnx_b[0]))
    pad_dt = lambda a: jnp.pad(a[0], (0, DT_PAD - M_HEADS)).reshape(1, DT_PAD)
    dskip_lanes = jnp.repeat(d_skip[0], M_HEADDIM).reshape(1, M_WIDTH)
    dskip_rows = jnp.broadcast_to(d_skip[0][:, None], (M_HEADS, M_HEADDIM))
    conv_wts = (conv_w[0], row(conv_b[0]), pad_dt(dt_bias), pad_dt(a_log))
    ffn_w = (w_out[0].astype(BF16), row(ln1_g[0]), row(ln1_b[0]), w_ff1[0].astype(BF16),
             w_ff2[0].astype(BF16), row(ln2_g[0]), row(ln2_b[0]))

    xp = x_prompt.reshape(batch * seq, D_MODEL)
    pr_p, z_p, xbc_p, dt_p = _proj(xp, w_main, w_dt, DENSE_ROWS)
    o_r_p, wkv_p = _rwkv_prompt(pr_p.reshape(batch, seq, RWKV_PROJ), prep_w[:-1] + post_w + (bd,))
    o_r_p = o_r_p.reshape(batch * seq, R_WIDTH)
    o_m_p, ssm_p = _mamba_prompt(z_p, xbc_p, dt_p, batch, seq,
                                 conv_wts + (dskip_lanes, row(gnorm_w[0])))
    y_p = _ffn(xp, o_r_p, o_m_p, None, *ffn_w, tm=DENSE_ROWS)
    shift_p = pr_p.reshape(batch, seq, RWKV_PROJ)[:, -1]
    conv_p = xbc_p.reshape(batch, seq, CONV_DIM)[:, seq - (CONV_K - 1):]

    xs2d = x_sample.reshape(nb, D_MODEL)
    pr_s, z_s, xbc_s, dt_s = _proj(xs2d, w_main, w_dt, nb)
    col = lambda a: jnp.broadcast_to(a.reshape(R_WIDTH, 1), (R_WIDTH, nb))
    o_r_t, wkv_t = _rwkv_sample(pr_s, state_shift[0], jnp.transpose(state_wkv[0], (1, 2, 3, 0)),
                                prep_w, (col(r_k[0]), col(lnx_w[0]), col(lnx_b[0])))
    o_r_s = o_r_t.T
    wkv_s = jnp.transpose(wkv_t, (3, 0, 1, 2))
    cprev = state_conv[0].reshape(nb, (CONV_K - 1) * CONV_DIM)
    xs_s, xdt_s, dec_s, bm_s, cm_s = _mamba_sample_prep(xbc_s, cprev, dt_s, conv_wts)
    y_m_s, ssm_s = _mamba_sample_step(xs_s, xdt_s, dec_s, bm_s, cm_s, state_ssm[0], dskip_rows)
    y_s = _ffn(xs2d, o_r_s.reshape(nb, R_WIDTH), y_m_s.reshape(nb, M_WIDTH),
               (z_s, row(gnorm_w[0])), *ffn_w, tm=nb)
    conv_s = jnp.concatenate([state_conv[0][:, 1:], xbc_s[:, None, :]], axis=1)

    return (y_p.reshape(batch, seq, D_MODEL), y_s.reshape(nb, 1, D_MODEL),
            shift_p[None], wkv_p[None], conv_p[None],
            ssm_p.reshape(1, batch, M_HEADS, M_HEADDIM, D_STATE),
            pr_s[None], wkv_s.reshape(1, nb, R_HEADS, R_HEAD, R_HEAD), conv_s[None],
            ssm_s.reshape(1, nb, M_HEADS, M_HEADDIM, D_STATE))
```

```python
import functools
import math

import numpy as np
import jax
import jax.numpy as jnp
from jax import lax
from jax.experimental import pallas as pl
from jax.experimental.pallas import tpu as pltpu

F32 = jnp.float32
BF16 = jnp.bfloat16

D_MODEL = 1024
R_WIDTH = 512
R_HEAD = 64
R_HEADS = 8
D_DECAY_LORA = 64
D_AAA_LORA = 64
D_GATE_LORA = 128
M_WIDTH = 512
M_HEADDIM = 64
M_HEADS = 8
M_GROUPS = 2
D_STATE = 128
CONV_K = 4
CONV_DIM = M_WIDTH + 2 * M_GROUPS * D_STATE
D_FF = 4 * D_MODEL
RWKV_PROJ = 3 * R_WIDTH + D_DECAY_LORA + D_AAA_LORA + D_GATE_LORA
DT_PAD = 128
PROJ_MAIN = RWKV_PROJ + M_WIDTH + CONV_DIM
DEPTH = 1
ALPHA = (2 * DEPTH) ** 0.25
LN_EPS = 1e-5
LNX_EPS = 64e-5
RMS_EPS = 1e-5

RWKV_CHUNK = 64
RWKV_TILE = 256
RWKV_ROWS = 2
SSD_CHUNK = 128
MAMBA_TILE = 256
SUBLANES = 8
DENSE_ROWS = 512
VMEM_LIMIT = 48 * 1024 * 1024

_NT = (((1,), (1,)), ((), ()))


def _mm(a, b):
    return jnp.dot(a.astype(BF16), b.astype(BF16), preferred_element_type=F32)


def _mm_nt(a, b):
    return lax.dot_general(a.astype(BF16), b.astype(BF16), _NT, preferred_element_type=F32)


def _split(x, passes):
    parts = []
    for _ in range(passes):
        p = x.astype(BF16)
        parts.append(p)
        x = x - p.astype(F32)
    return parts


def _mm_sel_r(x, sel, passes):
    return sum(jnp.dot(p, sel, preferred_element_type=F32) for p in _split(x, passes))


def _mm_sel_l(sel, x, passes):
    return sum(jnp.dot(sel, p, preferred_element_type=F32) for p in _split(x, passes))


def _softplus(x):
    return jnp.maximum(x, 0.0) + jnp.log(1.0 + jnp.exp(-jnp.abs(x)))


def _sigmoid(x):
    return 0.5 + 0.5 * jnp.tanh(0.5 * x)


def _silu(x):
    return x * _sigmoid(x)


def _layer_norm(x, g, b, eps):
    mu = jnp.mean(x, axis=-1, keepdims=True)
    d = x - mu
    var = jnp.mean(d * d, axis=-1, keepdims=True)
    return d * lax.rsqrt(var + eps) * g + b


def _iota2(shape, axis):
    return lax.broadcasted_iota(jnp.int32, shape, axis)


def _proj_kernel(x_ref, w_ref, wdt_ref, pr_ref, z_ref, xbc_ref, dt_ref):
    xb = x_ref[...].astype(BF16)
    res = jnp.dot(xb, w_ref[...], preferred_element_type=F32)
    c0 = RWKV_PROJ
    c1 = c0 + M_WIDTH
    pr_ref[...] = res[:, :c0]
    z_ref[...] = res[:, c0:c1]
    xbc_ref[...] = res[:, c1:]
    dt_ref[...] = jnp.dot(xb, wdt_ref[...], preferred_element_type=F32)


def _proj(x2d, w_bf, wdt_bf, tm):
    n = x2d.shape[0]
    widths = (RWKV_PROJ, M_WIDTH, CONV_DIM, DT_PAD)
    once = lambda a: pl.BlockSpec(a.shape, lambda i: (0, 0), pipeline_mode=pl.Buffered(1))
    return pl.pallas_call(
        _proj_kernel,
        grid=(n // tm,),
        in_specs=[pl.BlockSpec((tm, D_MODEL), lambda i: (i, 0)), once(w_bf), once(wdt_bf)],
        out_specs=[pl.BlockSpec((tm, w), lambda i: (i, 0)) for w in widths],
        out_shape=[jax.ShapeDtypeStruct((n, w), F32) for w in widths],
        compiler_params=pltpu.CompilerParams(dimension_semantics=("arbitrary",),
                                             vmem_limit_bytes=VMEM_LIMIT),
        name="proj",
    )(x2d, w_bf, wdt_bf)


def _rwkv_prep(pr, prev, mu, rv, wup, aup, gup, bd):
    w0, a0, k_k, k_a = (rv[i:i + 1, :] for i in range(4))
    u = pr + (prev - pr) * mu
    r = u[:, 0:R_WIDTH]
    k = u[:, R_WIDTH:2 * R_WIDTH]
    v = u[:, 2 * R_WIDTH:3 * R_WIDTH]
    lora = u[:, 3 * R_WIDTH:3 * R_WIDTH + D_DECAY_LORA + D_AAA_LORA]
    ug = u[:, 3 * R_WIDTH + D_DECAY_LORA + D_AAA_LORA:]
    w_log = -_softplus(-(w0 + _mm(jnp.tanh(lora), wup))) - 0.5
    logw = -jnp.exp(w_log)
    a = _sigmoid(a0 + _mm(lora, aup))
    g = _mm(_sigmoid(ug), gup)
    kk = k * k_k
    ss = _mm_sel_r(kk * kk, bd, 1)
    kkn = kk * lax.rsqrt(jnp.maximum(ss, 1e-24))
    k2 = k * (1.0 + (a - 1.0) * k_a)
    return r, logw, k2, v, -kkn, kkn * a, g


def _rwkv_post(y, r, k2, v, g, rv, bd):
    r_k, lnx_w, lnx_b = (rv[i:i + 1, :] for i in range(4, 7))
    inv = 1.0 / R_HEAD
    mu = _mm_sel_r(y, bd, 1) * inv
    d = y - mu
    var = _mm_sel_r(d * d, bd, 1) * inv
    yn = d * lax.rsqrt(var + LNX_EPS) * lnx_w + lnx_b
    bonus = _mm_sel_r(r * k2 * r_k, bd, 1)
    return (yn + bonus * v) * g


def _rwkv_tile_kernel(pr_ref, prevblk_ref, mu_ref, rv_ref, wup_ref, aup_ref, gup_ref, bd_ref,
                      o_ref, sfin_ref, g_scr):
    c = RWKV_CHUNK
    tile = RWKV_TILE
    nseq = RWKV_ROWS
    nck = tile // c
    hw = 2 * R_HEAD
    npair = R_HEADS // 2
    ti = pl.program_id(1)

    @pl.when(ti == 0)
    def _():
        g_scr[...] = jnp.zeros_like(g_scr)

    pr = jnp.concatenate([pr_ref[s] for s in range(nseq)], axis=0)
    prev = pltpu.roll(pr, 1, axis=0)
    row_id = _iota2((nseq * tile, 1), 0)
    for s in range(nseq):
        prev_row = jnp.where(ti == 0, 0.0, prevblk_ref[s, SUBLANES - 1:SUBLANES, :])
        prev = jnp.where(row_id == s * tile, prev_row, prev)
    head_ones = bd_ref[...]
    r, logw, k2, v, av, bv, g = _rwkv_prep(pr, prev, mu_ref[...], rv_ref[...], wup_ref[...],
                                           aup_ref[...], gup_ref[...], head_ones)

    sh_c = int(math.log2(c))
    ti_i = _iota2((tile, tile), 0)
    ti_j = _iota2((tile, tile), 1)
    chunk_tril = jnp.where((ti_i >= ti_j) & ((ti_i >> sh_c) == (ti_j >> sh_c)), 1.0, 0.0).astype(BF16)
    cum = jnp.concatenate([_mm_sel_l(chunk_tril, logw[s * tile:(s + 1) * tile], 3) for s in range(nseq)],
                          axis=0)
    cl = jnp.concatenate(
        [jnp.broadcast_to(cum[(q + 1) * c - 1:(q + 1) * c, :], (c, R_WIDTH)) for q in range(nseq * nck)],
        axis=0)
    p_out = jnp.exp(-cum)
    p_end = jnp.exp(cl - cum)
    rt = r * jnp.exp(cum)
    at = av * jnp.exp(cum - logw)
    bt = bv * p_out
    kt = k2 * p_out
    bh = bv * p_end
    kh = k2 * p_end
    pc = jnp.exp(cl)

    ii = _iota2((c, hw), 0)
    jj = _iota2((c, hw), 1) & (c - 1)
    strict = ii > jj
    incl = ii >= jj
    lane = _iota2((1, hw), 1)
    keep = (jnp.where(lane < R_HEAD, 1.0, 0.0).astype(BF16), jnp.where(lane >= R_HEAD, 1.0, 0.0).astype(BF16))
    e_r = _iota2((hw, hw), 0)
    e_c = _iota2((hw, hw), 1)
    eye_hw = e_r == e_c
    blockdiag = (e_r >= R_HEAD) == (e_c >= R_HEAD)

    def bd(x):
        xb = x.astype(BF16)
        return jnp.concatenate([xb * keep[0], xb * keep[1]], axis=0)

    def pdot(a, b_bd):
        return jnp.dot(a.astype(BF16), b_bd, preferred_element_type=F32)

    qp = [(q, p) for q in range(nseq * nck) for p in range(npair)]
    blk = lambda x, q, p: x[q * c:(q + 1) * c, hw * p:hw * (p + 1)]
    lhs = [jnp.concatenate([blk(at, *k), blk(rt, *k)], axis=0).astype(BF16) for k in qp]
    sc = [lax.dot_general(x, jnp.concatenate([bd(blk(bt, *k)), bd(blk(kt, *k))], axis=0), _NT,
                          preferred_element_type=F32) for x, k in zip(lhs, qp)]
    a_ab = [jnp.where(strict, s[:c, :hw], 0.0) for s in sc]
    a_rb = [jnp.where(incl, s[c:, :hw], 0.0) for s in sc]
    a_akrk = [jnp.concatenate([jnp.where(strict, s[:c, hw:], 0.0), jnp.where(incl, s[c:, hw:], 0.0)], axis=0)
              for s in sc]
    v_bd = [bd(blk(v, *k)) for k in qp]
    akrk_v = [pdot(a, x) for a, x in zip(a_akrk, v_bd)]

    sh = 4
    eye = jnp.where(ii == jj, 1.0, 0.0)
    npow = [jnp.where((ii >> sh) == (jj >> sh), n, 0.0) for n in a_ab]
    t_inv = [eye + n for n in npow]
    npow = [pdot(n, bd(n)) for n in npow]
    for _ in range(sh - 2):
        both = [pdot(jnp.concatenate([t, n], axis=0), bd(n)) for t, n in zip(t_inv, npow)]
        t_inv = [t + x[:c] for t, x in zip(t_inv, both)]
        npow = [x[c:] for x in both]
    t_inv = [t + pdot(t, bd(n)) for t, n in zip(t_inv, npow)]
    while (1 << sh) < c:
        off = ((ii >> (sh + 1)) == (jj >> (sh + 1))) & (((ii >> sh) & 1) > ((jj >> sh) & 1))
        tn = [pdot(t, bd(jnp.where(off, n, 0.0))) for t, n in zip(t_inv, a_ab)]
        t_inv = [t + pdot(x, bd(t)) for t, x in zip(t_inv, tn)]
        sh += 1

    tx = [pdot(t, jnp.concatenate([bd(blk(at, *k)), bd(x[:c])], axis=1))
          for t, x, k in zip(t_inv, akrk_v, qp)]
    rbx = [pdot(a, jnp.concatenate([bd(x[:, :hw]), bd(x[:, hw:])], axis=1)) for a, x in zip(a_rb, tx)]
    rp_pair = {k: blk(rt, *k) + x[:, :hw] for k, x in zip(qp, rbx)}
    y0_pair = {k: x[:, hw:] + y[c:] for k, x, y in zip(qp, rbx, akrk_v)}

    zeros_c = jnp.zeros((c, hw), F32)
    wt = {k: jnp.concatenate([blk(bh, *k), blk(kh, *k)], axis=0).T for k in qp}
    trans = {k: _mm(wt[k], jnp.concatenate([x, jnp.concatenate([zeros_c, blk(v, *k)], axis=1)], axis=0))
             for k, x in zip(qp, tx)}
    m_ba = {k: jnp.where(blockdiag, trans[k][:, :hw], 0.0) for k in qp}
    h0c = {k: jnp.where(blockdiag, trans[k][:, hw:], 0.0) for k in qp}
    pc_col = {k: jnp.sum(jnp.where(eye_hw, blk(pc, *k)[:1], 0.0), axis=1, keepdims=True) for k in qp}

    state = [g_scr[i] for i in range(nseq * npair)]
    y_rows = [None] * (nseq * nck)
    for q in range(nck):
        for s in range(nseq):
            qq = s * nck + q
            st = state[s * npair:(s + 1) * npair]
            on_st = [_mm(jnp.concatenate([rp_pair[qq, p], m_ba[qq, p]], axis=0), st[p]) for p in range(npair)]
            y_rows[qq] = jnp.concatenate([on_st[p][:c] + y0_pair[qq, p] for p in range(npair)], axis=1)
            state[s * npair:(s + 1) * npair] = [
                pc_col[qq, p] * st[p] + on_st[p][c:] + h0c[qq, p] for p in range(npair)]
    for i in range(nseq * npair):
        g_scr[i] = state[i]

    y = jnp.concatenate(y_rows, axis=0)
    out = _rwkv_post(y, r, k2, v, g, rv_ref[...], head_ones)
    for s in range(nseq):
        o_ref[s] = out[s * tile:(s + 1) * tile]

    @pl.when(ti == pl.num_programs(1) - 1)
    def _():
        for s in range(nseq):
            for p in range(npair):
                gt = g_scr[s * npair + p].T
                sfin_ref[s, 2 * p] = gt[:R_HEAD, :R_HEAD]
                sfin_ref[s, 2 * p + 1] = pltpu.roll(gt, R_HEAD, axis=1)[R_HEAD:, :R_HEAD]


def _const_spec(shape):
    return pl.BlockSpec(shape, lambda *_: (0,) * len(shape))


def _rwkv_prompt(pr3d, wts):
    batch, seq, _ = pr3d.shape
    tile = RWKV_TILE
    nseq = RWKV_ROWS

    def prev_map(b, i):
        return (b, jnp.maximum(i * (tile // SUBLANES) - 1, 0), 0)

    w_specs = [_const_spec(w.shape) for w in wts]
    return pl.pallas_call(
        _rwkv_tile_kernel,
        grid=(batch // nseq, seq // tile),
        in_specs=[pl.BlockSpec((nseq, tile, RWKV_PROJ), lambda b, i: (b, i, 0)),
                  pl.BlockSpec((nseq, SUBLANES, RWKV_PROJ), prev_map)] + w_specs,
        out_specs=[pl.BlockSpec((nseq, tile, R_WIDTH), lambda b, i: (b, i, 0)),
                   pl.BlockSpec((nseq, R_HEADS, R_HEAD, R_HEAD), lambda b, i: (b, 0, 0, 0))],
        out_shape=[jax.ShapeDtypeStruct((batch, seq, R_WIDTH), F32),
                   jax.ShapeDtypeStruct((batch, R_HEADS, R_HEAD, R_HEAD), F32)],
        scratch_shapes=[pltpu.VMEM((nseq * (R_HEADS // 2), 2 * R_HEAD, 2 * R_HEAD), F32)],
        compiler_params=pltpu.CompilerParams(dimension_semantics=("arbitrary", "arbitrary"),
                                             vmem_limit_bytes=VMEM_LIMIT),
        name="rwkv_prompt",
    )(pr3d, pr3d, *wts)


def _rwkv_sample_kernel(pr_ref, prev_ref, mu_ref, rv_ref, wup_ref, aup_ref, gup_ref, bd_ref,
                        cols_ref, s_ref, o_ref, so_ref, vec_scr):
    h = pl.program_id(0)

    @pl.when(h == 0)
    def _():
        r, logw, k2, v, av, bv, g = _rwkv_prep(pr_ref[...], prev_ref[...], mu_ref[...], rv_ref[...],
                                               wup_ref[...], aup_ref[...], gup_ref[...], bd_ref[...])
        for i, x in enumerate((r, jnp.exp(logw), k2, v, av, bv, g)):
            vec_scr[i] = x.T

    rows = pl.ds(pl.multiple_of(h * R_HEAD, R_HEAD), R_HEAD)
    r, w, k, v, a, b, g = (vec_scr[i, rows, :] for i in range(7))
    ys = []
    for vi in range(R_HEAD):
        s = s_ref[0, vi]
        sa = jnp.sum(s * a, axis=0, keepdims=True)
        s2 = s * w + sa * b + v[vi:vi + 1, :] * k
        so_ref[0, vi] = s2
        ys.append(jnp.sum(s2 * r, axis=0, keepdims=True))
    y = jnp.concatenate(ys, axis=0)
    mu = jnp.mean(y, axis=0, keepdims=True)
    d = y - mu
    var = jnp.mean(d * d, axis=0, keepdims=True)
    yn = d * lax.rsqrt(var + LNX_EPS) * cols_ref[1] + cols_ref[2]
    bonus = jnp.sum(r * k * cols_ref[0], axis=0, keepdims=True)
    o_ref[...] = (yn + bonus * v) * g


def _rwkv_sample(pr, prev, state_t, prep_w, cols):
    nb = pr.shape[0]
    args = (pr, prev) + tuple(prep_w)
    head_rows = pl.BlockSpec((R_HEAD, nb), lambda h: (h, 0))
    st_spec = pl.BlockSpec((1, R_HEAD, R_HEAD, nb), lambda h: (h, 0, 0, 0))
    return pl.pallas_call(
        _rwkv_sample_kernel,
        grid=(R_HEADS,),
        in_specs=[_const_spec(a.shape) for a in args]
                 + [pl.BlockSpec((3, R_HEAD, nb), lambda h: (0, h, 0)), st_spec],
        out_specs=[head_rows, st_spec],
        out_shape=[jax.ShapeDtypeStruct((R_WIDTH, nb), F32),
                   jax.ShapeDtypeStruct(state_t.shape, F32)],
        scratch_shapes=[pltpu.VMEM((7, R_WIDTH, nb), F32)],
        compiler_params=pltpu.CompilerParams(dimension_semantics=("arbitrary",),
                                             vmem_limit_bytes=VMEM_LIMIT),
        name="rwkv_sample",
    )(*args, cols, state_t)


STEP_BATCH = 8


def _per_head_rows(x, n):
    return jnp.concatenate([jnp.broadcast_to(x[h:h + 1, :], (n, x.shape[1])) for h in range(x.shape[0])],
                           axis=0)


def _rows_from_columns(col_b, diag, n):
    picked = jnp.where(diag, col_b, 0.0)
    return jnp.concatenate([jnp.sum(picked[h * n:(h + 1) * n], axis=0, keepdims=True)
                            for h in range(col_b.shape[0] // n)], axis=0)


def _shift_rows(x, prev8, s):
    rolled = pltpu.roll(x, s, axis=0)
    top = jnp.where(_iota2((SUBLANES, 1), 0) < s, pltpu.roll(prev8, s, axis=0), rolled[:SUBLANES])
    return jnp.concatenate([top, rolled[SUBLANES:]], axis=0)


def _head_expand(lanes_per_head=M_HEADDIM):
    shape = (DT_PAD, M_HEADS * lanes_per_head)
    head_of_lane = _iota2(shape, 1) >> int(math.log2(lanes_per_head))
    return jnp.where(_iota2(shape, 0) == head_of_lane, 1.0, 0.0).astype(BF16)


def _gated_rmsnorm(y, z, gnorm_w):
    yz = y * _silu(z)
    gw = M_WIDTH // M_GROUPS
    outs = []
    for gi in range(M_GROUPS):
        yg = yz[:, gi * gw:(gi + 1) * gw]
        ms = jnp.mean(yg * yg, axis=-1, keepdims=True)
        outs.append(yg * lax.rsqrt(ms + RMS_EPS))
    return jnp.concatenate(outs, axis=1) * gnorm_w


def _mamba_tile_kernel(z_ref, xbc_ref, xprev_ref, dt_ref, cw_ref, dta_ref, mv_ref,
                       o_ref, hfin_ref, h_scr):
    tile = MAMBA_TILE
    q = SSD_CHUNK
    ti = pl.program_id(1)

    @pl.when(ti == 0)
    def _():
        h_scr[...] = jnp.zeros_like(h_scr)

    xb = xbc_ref[...]
    prev8 = jnp.where(ti == 0, 0.0, xprev_ref[...])
    conv = cw_ref[CONV_K:CONV_K + 1, :] + xb * cw_ref[CONV_K - 1:CONV_K, :]
    for s in range(1, CONV_K):
        conv = conv + _shift_rows(xb, prev8, s) * cw_ref[CONV_K - 1 - s:CONV_K - s, :]
    xact = _silu(conv)
    xs = xact[:, :M_WIDTH]
    gs = M_GROUPS * D_STATE
    bm = xact[:, M_WIDTH:M_WIDTH + gs]
    cm = xact[:, M_WIDTH + gs:]

    dt = _softplus(dt_ref[...] + dta_ref[0:1, :])
    adt = dt * (-jnp.exp(dta_ref[1:2, :]))
    x_dt = xs * _mm_sel_r(dt, _head_expand(), 3)

    sh_q = int(math.log2(q))
    ti_i = _iota2((tile, tile), 0)
    ti_j = _iota2((tile, tile), 1)
    chunk_tril = jnp.where((ti_i >= ti_j) & ((ti_i >> sh_q) == (ti_j >> sh_q)), 1.0, 0.0).astype(BF16)
    cs = _mm_sel_l(chunk_tril, adt, 3)
    cs_t = cs.T
    incl = _iota2((q, q), 0) >= _iota2((q, q), 1)
    lane = _iota2((1, 2 * M_HEADDIM), 1)
    first = lane < M_HEADDIM
    hpg = M_HEADS // M_GROUPS

    state = [h_scr[p] for p in range(M_HEADS // 2)]
    y_rows = []
    for j in range(tile // q):
        rs = slice(j * q, (j + 1) * q)
        y_pairs = []
        for gi in range(M_GROUPS):
            bm_g = bm[rs, gi * D_STATE:(gi + 1) * D_STATE]
            cm_g = cm[rs, gi * D_STATE:(gi + 1) * D_STATE]
            cb = _mm_nt(cm_g, bm_g)
            bm_t = bm_g.T
            for pp in range(hpg // 2):
                p = gi * (hpg // 2) + pp
                x_p = x_dt[rs, 2 * M_HEADDIM * p:2 * M_HEADDIM * (p + 1)]
                y_diag = jnp.zeros((q, 2 * M_HEADDIM), F32)
                contrib = jnp.zeros((D_STATE, 2 * M_HEADDIM), F32)
                scale_in = []
                scale_end = []
                for e in range(2):
                    head = 2 * p + e
                    m = first if e == 0 else jnp.logical_not(first)
                    a_col = cs[rs, head:head + 1]
                    a_row = cs_t[head:head + 1, rs]
                    a_end = cs_t[head:head + 1, (j + 1) * q - 1:(j + 1) * q]
                    lmat = jnp.exp(jnp.where(incl, a_col - a_row, -jnp.inf))
                    x_m = jnp.where(m, x_p, 0.0)
                    y_diag = y_diag + _mm(cb * lmat, x_m)
                    contrib = contrib + _mm(bm_t * jnp.exp(a_end - a_row), x_m)
                    scale_in.append(jnp.exp(a_col))
                    scale_end.append(jnp.exp(a_end))
                y_off = _mm(cm_g, state[p]) * jnp.where(first, scale_in[0], scale_in[1])
                state[p] = state[p] * jnp.where(first, scale_end[0], scale_end[1]) + contrib
                y_pairs.append(y_diag + y_off)
        y_rows.append(jnp.concatenate(y_pairs, axis=1))
    for p in range(M_HEADS // 2):
        h_scr[p] = state[p]

    y = jnp.concatenate(y_rows, axis=0) + mv_ref[0:1, :] * xs
    o_ref[...] = _gated_rmsnorm(y, z_ref[...], mv_ref[1:2, :])

    @pl.when(ti == pl.num_programs(1) - 1)
    def _():
        for p in range(M_HEADS // 2):
            hfin_ref[0, 2 * M_HEADDIM * p:2 * M_HEADDIM * (p + 1), :] = h_scr[p].T


def _mamba_prompt(z2d, xbc2d, dt2d, batch, seq, wts):
    tile = MAMBA_TILE
    ntile = seq // tile
    rows8 = seq // SUBLANES

    def prev_map(b, i):
        return (b * rows8 + jnp.maximum(i * (tile // SUBLANES) - 1, 0), 0)

    tok = lambda w: pl.BlockSpec((tile, w), lambda b, i: (b * ntile + i, 0))
    return pl.pallas_call(
        _mamba_tile_kernel,
        grid=(batch, ntile),
        in_specs=[tok(M_WIDTH), tok(CONV_DIM), pl.BlockSpec((SUBLANES, CONV_DIM), prev_map), tok(DT_PAD)]
                 + [_const_spec(w.shape) for w in wts],
        out_specs=[tok(M_WIDTH),
                   pl.BlockSpec((1, M_HEADS * M_HEADDIM, D_STATE), lambda b, i: (b, 0, 0))],
        out_shape=[jax.ShapeDtypeStruct((batch * seq, M_WIDTH), F32),
                   jax.ShapeDtypeStruct((batch, M_HEADS * M_HEADDIM, D_STATE), F32)],
        scratch_shapes=[pltpu.VMEM((M_HEADS // 2, D_STATE, 2 * M_HEADDIM), F32)],
        compiler_params=pltpu.CompilerParams(dimension_semantics=("arbitrary", "arbitrary"),
                                             vmem_limit_bytes=VMEM_LIMIT),
        name="mamba_prompt",
    )(z2d, xbc2d, xbc2d, dt2d, *wts)


def _mamba_prep_kernel(xbc_ref, cprev_ref, dt_ref, cw_ref, dta_ref,
                       xs_ref, xdt_ref, dec_ref, bm_ref, cm_ref, cnew_ref):
    xbc = xbc_ref[...]
    conv = cw_ref[CONV_K:CONV_K + 1, :] + xbc * cw_ref[CONV_K - 1:CONV_K, :]
    for j in range(CONV_K - 1):
        conv = conv + cprev_ref[j] * cw_ref[j:j + 1, :]
        cnew_ref[j] = cprev_ref[j + 1] if j + 1 < CONV_K - 1 else xbc
    xact = _silu(conv)
    xs = xact[:, :M_WIDTH]
    gs = M_GROUPS * D_STATE
    dt = _softplus(dt_ref[...] + dta_ref[0:1, :])
    adt = dt * (-jnp.exp(dta_ref[1:2, :]))
    expand = _head_expand()
    xs_ref[...] = xs
    xdt_ref[...] = xs * _mm_sel_r(dt, expand, 3)
    dec_ref[...] = jnp.exp(_mm_sel_r(adt, _head_expand(D_STATE), 3))
    hpg = M_HEADS // M_GROUPS
    per_head = lambda m: jnp.concatenate(
        [m[:, (e // hpg) * D_STATE:(e // hpg + 1) * D_STATE] for e in range(M_HEADS)], axis=1)
    bm_ref[...] = per_head(xact[:, M_WIDTH:M_WIDTH + gs])
    cm_ref[...] = per_head(xact[:, M_WIDTH + gs:])


def _mamba_sample_prep(xbc, cprev_t, dt, cw, dta):
    n = xbc.shape[0]
    args = (xbc, cprev_t, dt, cw, dta)
    widths = (M_WIDTH, M_WIDTH, M_HEADS * D_STATE, M_HEADS * D_STATE, M_HEADS * D_STATE)
    out_shapes = [(n, w) for w in widths] + [cprev_t.shape]
    return pl.pallas_call(
        _mamba_prep_kernel,
        grid=(1,),
        in_specs=[_const_spec(a.shape) for a in args],
        out_specs=[_const_spec(s) for s in out_shapes],
        out_shape=[jax.ShapeDtypeStruct(s, F32) for s in out_shapes],
        compiler_params=pltpu.CompilerParams(vmem_limit_bytes=VMEM_LIMIT),
        name="mamba_sample_prep",
    )(*args)


def _mamba_step_kernel(xs_ref, xdt_ref, dec_ref, bm_ref, cm_ref, h_ref, dskip_ref, y_ref, ho_ref):
    n = M_HEADS * M_HEADDIM
    ones_in = jnp.ones((M_HEADDIM, D_STATE), BF16)
    ones_out = jnp.ones((D_STATE, M_HEADDIM), BF16)
    diag = (_iota2((n, M_HEADDIM), 0) & (M_HEADDIM - 1)) == _iota2((n, M_HEADDIM), 1)

    def head_rows(ref, bi):
        return jnp.concatenate(
            [jnp.broadcast_to(ref[bi:bi + 1, e * D_STATE:(e + 1) * D_STATE], (M_HEADDIM, D_STATE))
             for e in range(M_HEADS)], axis=0)

    for bi in range(STEP_BATCH):
        rows = slice(bi * M_HEADS, (bi + 1) * M_HEADS)
        hrows = slice(bi * n, (bi + 1) * n)
        x_col = _mm(jnp.where(diag, _per_head_rows(xdt_ref[rows, :], M_HEADDIM), 0.0), ones_in)
        h2 = h_ref[hrows, :] * head_rows(dec_ref, bi) + x_col * head_rows(bm_ref, bi)
        ho_ref[hrows, :] = h2
        y = _rows_from_columns(_mm(h2 * head_rows(cm_ref, bi), ones_out), diag, M_HEADDIM)
        y_ref[rows, :] = y + dskip_ref[...] * xs_ref[rows, :]


def _mamba_sample_step(xs, xdt, dec, bm, cm, state, dskip_rows):
    nb = state.shape[0]
    rows = STEP_BATCH * M_HEADS
    vec_spec = pl.BlockSpec((rows, M_HEADDIM), lambda i: (i, 0))
    bc_spec = pl.BlockSpec((STEP_BATCH, M_HEADS * D_STATE), lambda i: (i, 0))
    st_spec = pl.BlockSpec((rows * M_HEADDIM, D_STATE), lambda i: (i, 0))
    flat = lambda x, w: x.reshape(nb * M_HEADS, w)
    return pl.pallas_call(
        _mamba_step_kernel,
        grid=(nb // STEP_BATCH,),
        in_specs=[vec_spec] * 2 + [bc_spec] * 3 + [st_spec, _const_spec((M_HEADS, M_HEADDIM))],
        out_specs=[vec_spec, st_spec],
        out_shape=[jax.ShapeDtypeStruct((nb * M_HEADS, M_HEADDIM), F32),
                   jax.ShapeDtypeStruct((nb * M_HEADS * M_HEADDIM, D_STATE), F32)],
        compiler_params=pltpu.CompilerParams(dimension_semantics=("arbitrary",),
                                             vmem_limit_bytes=VMEM_LIMIT),
        name="mamba_sample_step",
    )(flat(xs, M_HEADDIM), flat(xdt, M_HEADDIM), dec, bm, cm,
      state.reshape(nb * M_HEADS * M_HEADDIM, D_STATE), dskip_rows)


FF_SLAB = 1024


def _ffn_kernel(has_gate, *refs):
    if has_gate:
        x_ref, or_ref, ym_ref, z_ref, mv_ref, wout_ref, lnp_ref, w1_ref, w2_ref, y_ref = refs
        o_m = _gated_rmsnorm(ym_ref[...], z_ref[...], mv_ref[1:2, :])
    else:
        x_ref, or_ref, om_ref, wout_ref, lnp_ref, w1_ref, w2_ref, y_ref = refs
        o_m = om_ref[...]
    mix = _mm(or_ref[...], wout_ref[:R_WIDTH, :]) + _mm(o_m, wout_ref[R_WIDTH:, :])
    h = _layer_norm(ALPHA * x_ref[...] + mix, lnp_ref[0:1, :], lnp_ref[1:2, :], LN_EPS)
    hb = h.astype(BF16)
    ff = jnp.zeros_like(h)
    for j in range(D_FF // FF_SLAB):
        t = jnp.dot(hb, w1_ref[:, j * FF_SLAB:(j + 1) * FF_SLAB], preferred_element_type=F32)
        t = jnp.square(jnp.maximum(t, 0.0))
        ff = ff + jnp.dot(t.astype(BF16), w2_ref[j * FF_SLAB:(j + 1) * FF_SLAB, :],
                          preferred_element_type=F32)
    y_ref[...] = _layer_norm(ALPHA * h + ff, lnp_ref[2:3, :], lnp_ref[3:4, :], LN_EPS)


def _ffn(x2d, o_r, o_m, gate, wout, lnp, w1, w2, tm):
    n = x2d.shape[0]
    tok = lambda w: pl.BlockSpec((tm, w), lambda i: (i, 0))
    once = lambda a: pl.BlockSpec(a.shape, lambda i: (0,) * a.ndim, pipeline_mode=pl.Buffered(1))
    acts = [x2d, o_r, o_m]
    act_specs = [tok(D_MODEL), tok(R_WIDTH), tok(M_WIDTH)]
    if gate is not None:
        acts += [gate[0], gate[1]]
        act_specs += [tok(M_WIDTH), once(gate[1])]
    wts = [wout, lnp, w1, w2]
    return pl.pallas_call(
        functools.partial(_ffn_kernel, gate is not None),
        grid=(n // tm,),
        in_specs=act_specs + [once(w) for w in wts],
        out_specs=tok(D_MODEL),
        out_shape=jax.ShapeDtypeStruct((n, D_MODEL), F32),
        compiler_params=pltpu.CompilerParams(dimension_semantics=("arbitrary",),
                                             vmem_limit_bytes=VMEM_LIMIT),
        name="ffn",
    )(*acts, *wts)


def _block_diag_ones():
    idx = np.arange(R_WIDTH) // R_HEAD
    return jnp.asarray(idx[:, None] == idx[None, :], dtype=BF16)


def kernel(x_prompt, x_sample, state_shift, state_wkv, state_conv, state_ssm, w_in, mu_shift, w0, w_up, a0, a_up, g_up, k_k, k_a, r_k, lnx_w, lnx_b, conv_w, conv_b, dt_bias, a_log, d_skip, gnorm_w, w_out, ln1_g, ln1_b, w_ff1, w_ff2, ln2_g, ln2_b):
    assert w_in.shape[0] == DEPTH and x_sample.shape[1] == 1
    batch, seq, _ = x_prompt.shape
    nb = x_sample.shape[0]
    assert seq % MAMBA_TILE == 0 and seq % RWKV_TILE == 0 and seq % DENSE_ROWS == 0
    assert nb % STEP_BATCH == 0 and batch % RWKV_ROWS == 0
    row = lambda a: a.reshape(1, -1)

    w_main = w_in[0][:, :PROJ_MAIN].astype(BF16)
    w_dt = jnp.pad(w_in[0][:, PROJ_MAIN:], ((0, 0), (0, DT_PAD - M_HEADS))).astype(BF16)
    zl = jnp.zeros((D_DECAY_LORA, R_WIDTH), F32)
    wup_p = jnp.concatenate([w_up[0], zl], axis=0).astype(BF16)
    aup_p = jnp.concatenate([zl, a_up[0]], axis=0).astype(BF16)
    bd = _block_diag_ones()

    def rows8(*vecs):
        m = jnp.stack([v.reshape(-1) for v in vecs])
        return jnp.pad(m, ((0, SUBLANES - len(vecs)), (0, 0)))

    rv = rows8(w0[0], a0[0], k_k[0], k_a[0], r_k[0], lnx_w[0], lnx_b[0])
    prep_w = (row(mu_shift[0]), rv, wup_p, aup_p, g_up[0].astype(BF16), bd)
    cw = rows8(*conv_w[0], conv_b[0])
    dta = rows8(jnp.pad(dt_bias[0], (0, DT_PAD - M_HEADS)), jnp.pad(a_log[0], (0, DT_PAD - M_HEADS)))
    mv = rows8(jnp.repeat(d_skip[0], M_HEADDIM), gnorm_w[0])
    dskip_rows = jnp.broadcast_to(d_skip[0][:, None], (M_HEADS, M_HEADDIM))
    lnp = rows8(ln1_g[0], ln1_b[0], ln2_g[0], ln2_b[0])
    ffn_w = (w_out[0].astype(BF16), lnp, w_ff1[0].astype(BF16), w_ff2[0].astype(BF16))

    xp = x_prompt.reshape(batch * seq, D_MODEL)
    pr_p, z_p, xbc_p, dt_p = _proj(xp, w_main, w_dt, DENSE_ROWS)
    o_r_p, wkv_p = _rwkv_prompt(pr_p.reshape(batch, seq, RWKV_PROJ), prep_w)
    o_r_p = o_r_p.reshape(batch * seq, R_WIDTH)
    o_m_p, ssm_p = _mamba_prompt(z_p, xbc_p, dt_p, batch, seq, (cw, dta, mv))
    y_p = _ffn(xp, o_r_p, o_m_p, None, *ffn_w, tm=DENSE_ROWS)
    shift_p = pr_p.reshape(batch, seq, RWKV_PROJ)[:, -1]
    conv_p = xbc_p.reshape(batch, seq, CONV_DIM)[:, seq - (CONV_K - 1):]

    xs2d = x_sample.reshape(nb, D_MODEL)
    pr_s, z_s, xbc_s, dt_s = _proj(xs2d, w_main, w_dt, nb)
    cols = jnp.broadcast_to(jnp.stack([r_k[0].reshape(-1), lnx_w[0], lnx_b[0]])[:, :, None],
                            (3, R_WIDTH, nb))
    o_r_t, wkv_t = _rwkv_sample(pr_s, state_shift[0], jnp.transpose(state_wkv[0], (1, 2, 3, 0)),
                                prep_w, cols)
    o_r_s = o_r_t.T
    wkv_s = jnp.transpose(wkv_t, (3, 0, 1, 2))
    xs_s, xdt_s, dec_s, bm_s, cm_s, conv_t = _mamba_sample_prep(
        xbc_s, jnp.transpose(state_conv[0], (1, 0, 2)), dt_s, cw, dta)
    conv_s = jnp.transpose(conv_t, (1, 0, 2))
    y_m_s, ssm_s = _mamba_sample_step(xs_s, xdt_s, dec_s, bm_s, cm_s, state_ssm[0], dskip_rows)
    y_s = _ffn(xs2d, o_r_s, y_m_s.reshape(nb, M_WIDTH), (z_s, mv), *ffn_w, tm=nb)

    return (y_p.reshape(batch, seq, D_MODEL), y_s.reshape(nb, 1, D_MODEL),
            shift_p[None], wkv_p[None], conv_p[None],
            ssm_p.reshape(1, batch, M_HEADS, M_HEADDIM, D_STATE),
            pr_s[None], wkv_s.reshape(1, nb, R_HEADS, R_HEAD, R_HEAD), conv_s[None],
            ssm_s.reshape(1, nb, M_HEADS, M_HEADDIM, D_STATE))
```

```python
import functools
import math

import numpy as np
import jax
import jax.numpy as jnp
from jax import lax
from jax.experimental import pallas as pl
from jax.experimental.pallas import tpu as pltpu

F32 = jnp.float32
BF16 = jnp.bfloat16

D_MODEL = 1024
R_WIDTH = 512
R_HEAD = 64
R_HEADS = 8
D_DECAY_LORA = 64
D_AAA_LORA = 64
D_GATE_LORA = 128
M_WIDTH = 512
M_HEADDIM = 64
M_HEADS = 8
M_GROUPS = 2
D_STATE = 128
CONV_K = 4
CONV_DIM = M_WIDTH + 2 * M_GROUPS * D_STATE
D_FF = 4 * D_MODEL
RWKV_PROJ = 3 * R_WIDTH + D_DECAY_LORA + D_AAA_LORA + D_GATE_LORA
DT_PAD = 128
PROJ_MAIN = RWKV_PROJ + M_WIDTH + CONV_DIM
DEPTH = 1
ALPHA = (2 * DEPTH) ** 0.25
LN_EPS = 1e-5
LNX_EPS = 64e-5
RMS_EPS = 1e-5

RWKV_CHUNK = 64
RWKV_TILE = 256
RWKV_ROWS = 2
SSD_CHUNK = 128
MAMBA_TILE = 256
SUBLANES = 8
DENSE_ROWS = 512
VMEM_LIMIT = 48 * 1024 * 1024
PROJ_W_VMEM_LIMIT = 56 * 1024 * 1024

_NT = (((1,), (1,)), ((), ()))


def _mm(a, b):
    return jnp.dot(a.astype(BF16), b.astype(BF16), preferred_element_type=F32)


def _mm_nt(a, b):
    return lax.dot_general(a.astype(BF16), b.astype(BF16), _NT, preferred_element_type=F32)


def _split(x, passes):
    parts = []
    for _ in range(passes):
        p = x.astype(BF16)
        parts.append(p)
        x = x - p.astype(F32)
    return parts


def _mm_sel_r(x, sel, passes):
    return sum(jnp.dot(p, sel, preferred_element_type=F32) for p in _split(x, passes))


def _mm_sel_l(sel, x, passes):
    return sum(jnp.dot(sel, p, preferred_element_type=F32) for p in _split(x, passes))


def _softplus(x):
    return jnp.maximum(x, 0.0) + jnp.log(1.0 + jnp.exp(-jnp.abs(x)))


def _sigmoid(x):
    return 0.5 + 0.5 * jnp.tanh(0.5 * x)


def _silu(x):
    return x * _sigmoid(x)


def _layer_norm(x, g, b, eps):
    mu = jnp.mean(x, axis=-1, keepdims=True)
    d = x - mu
    var = jnp.mean(d * d, axis=-1, keepdims=True)
    return d * lax.rsqrt(var + eps) * g + b


def _iota2(shape, axis):
    return lax.broadcasted_iota(jnp.int32, shape, axis)


def _dt_columns(xb, wdt_t):
    pad = jnp.zeros((DT_PAD - M_HEADS, D_MODEL), F32)
    return _mm_nt(xb, jnp.concatenate([wdt_t, pad], axis=0))


def _proj_kernel(x_ref, w_ref, wdt_ref, pr_ref, z_ref, xbc_ref, dt_ref):
    xb = x_ref[...].astype(BF16)
    res = jnp.dot(xb, w_ref[...], preferred_element_type=F32)
    c0 = RWKV_PROJ
    c1 = c0 + M_WIDTH
    pr_ref[...] = res[:, :c0]
    z_ref[...] = res[:, c0:c1]
    xbc_ref[...] = res[:, c1:]
    dt_ref[...] = _dt_columns(xb, wdt_ref[...])


def _proj(x2d, w_bf, wdt_bf, tm):
    n = x2d.shape[0]
    widths = (RWKV_PROJ, M_WIDTH, CONV_DIM, DT_PAD)
    once = lambda a: pl.BlockSpec(a.shape, lambda i: (0, 0), pipeline_mode=pl.Buffered(1))
    return pl.pallas_call(
        _proj_kernel,
        grid=(n // tm,),
        in_specs=[pl.BlockSpec((tm, D_MODEL), lambda i: (i, 0)), once(w_bf), once(wdt_bf)],
        out_specs=[pl.BlockSpec((tm, w), lambda i: (i, 0)) for w in widths],
        out_shape=[jax.ShapeDtypeStruct((n, w), F32) for w in widths],
        compiler_params=pltpu.CompilerParams(dimension_semantics=("arbitrary",),
                                             vmem_limit_bytes=VMEM_LIMIT),
        name="proj",
    )(x2d, w_bf, wdt_bf)


W_SLAB = 256


def _proj_w_kernel(x_ref, wt_ref, wdt_ref, pr_ref, z_ref, xbc_ref, dt_ref, wbf_ref):
    @pl.when(pl.program_id(0) == 0)
    def _():
        for c in range(0, PROJ_MAIN, W_SLAB):
            wbf_ref[:, c:c + W_SLAB] = wt_ref[c:c + W_SLAB, :].T.astype(BF16)

    xb = x_ref[...].astype(BF16)
    res = jnp.dot(xb, wbf_ref[...], preferred_element_type=F32)
    c0 = RWKV_PROJ
    c1 = c0 + M_WIDTH
    pr_ref[...] = res[:, :c0]
    z_ref[...] = res[:, c0:c1]
    xbc_ref[...] = res[:, c1:]
    dt_ref[...] = _dt_columns(xb, wdt_ref[...])


def _proj_w(x2d, w_t, wdt_bf, tm):
    n = x2d.shape[0]
    widths = (RWKV_PROJ, M_WIDTH, CONV_DIM, DT_PAD)
    once = lambda a: pl.BlockSpec(a.shape, lambda i: (0, 0), pipeline_mode=pl.Buffered(1))
    return pl.pallas_call(
        _proj_w_kernel,
        grid=(n // tm,),
        in_specs=[pl.BlockSpec((tm, D_MODEL), lambda i: (i, 0)), once(w_t), once(wdt_bf)],
        out_specs=[pl.BlockSpec((tm, w), lambda i: (i, 0)) for w in widths]
                  + [pl.BlockSpec((D_MODEL, PROJ_MAIN), lambda i: (0, 0))],
        out_shape=[jax.ShapeDtypeStruct((n, w), F32) for w in widths]
                  + [jax.ShapeDtypeStruct((D_MODEL, PROJ_MAIN), BF16)],
        compiler_params=pltpu.CompilerParams(dimension_semantics=("arbitrary",),
                                             vmem_limit_bytes=PROJ_W_VMEM_LIMIT),
        name="proj_w",
    )(x2d, w_t, wdt_bf)


def _rwkv_prep(pr, prev, mu, rv, wup, aup, gup, bd):
    w0, a0, k_k, k_a = (rv[i:i + 1, :] for i in range(4))
    u = pr + (prev - pr) * mu
    r = u[:, 0:R_WIDTH]
    k = u[:, R_WIDTH:2 * R_WIDTH]
    v = u[:, 2 * R_WIDTH:3 * R_WIDTH]
    lora = u[:, 3 * R_WIDTH:3 * R_WIDTH + D_DECAY_LORA + D_AAA_LORA]
    ug = u[:, 3 * R_WIDTH + D_DECAY_LORA + D_AAA_LORA:]
    w_log = -_softplus(-(w0 + _mm(jnp.tanh(lora), wup))) - 0.5
    logw = -jnp.exp(w_log)
    a = _sigmoid(a0 + _mm(lora, aup))
    g = _mm(_sigmoid(ug), gup)
    kk = k * k_k
    ss = _mm_sel_r(kk * kk, bd, 1)
    kkn = kk * lax.rsqrt(jnp.maximum(ss, 1e-24))
    k2 = k * (1.0 + (a - 1.0) * k_a)
    return r, logw, k2, v, -kkn, kkn * a, g


def _rwkv_post(y, r, k2, v, g, rv, bd):
    r_k, lnx_w, lnx_b = (rv[i:i + 1, :] for i in range(4, 7))
    inv = 1.0 / R_HEAD
    mu = _mm_sel_r(y, bd, 1) * inv
    d = y - mu
    var = _mm_sel_r(d * d, bd, 1) * inv
    yn = d * lax.rsqrt(var + LNX_EPS) * lnx_w + lnx_b
    bonus = _mm_sel_r(r * k2 * r_k, bd, 1)
    return (yn + bonus * v) * g


def _rwkv_tile_kernel(pr_ref, prevblk_ref, mu_ref, rv_ref, wup_ref, aup_ref, gup_ref, bd_ref,
                      o_ref, sfin_ref, g_scr):
    c = RWKV_CHUNK
    tile = RWKV_TILE
    nseq = RWKV_ROWS
    nck = tile // c
    hw = 2 * R_HEAD
    npair = R_HEADS // 2
    ti = pl.program_id(1)

    @pl.when(ti == 0)
    def _():
        g_scr[...] = jnp.zeros_like(g_scr)

    pr = jnp.concatenate([pr_ref[s] for s in range(nseq)], axis=0)
    prev = pltpu.roll(pr, 1, axis=0)
    row_id = _iota2((nseq * tile, 1), 0)
    for s in range(nseq):
        prev_row = jnp.where(ti == 0, 0.0, prevblk_ref[s, SUBLANES - 1:SUBLANES, :])
        prev = jnp.where(row_id == s * tile, prev_row, prev)
    head_ones = bd_ref[...]
    r, logw, k2, v, av, bv, g = _rwkv_prep(pr, prev, mu_ref[...], rv_ref[...], wup_ref[...],
                                           aup_ref[...], gup_ref[...], head_ones)

    sh_c = int(math.log2(c))
    ti_i = _iota2((tile, tile), 0)
    ti_j = _iota2((tile, tile), 1)
    chunk_tril = jnp.where((ti_i >= ti_j) & ((ti_i >> sh_c) == (ti_j >> sh_c)), 1.0, 0.0).astype(BF16)
    cum = jnp.concatenate([_mm_sel_l(chunk_tril, logw[s * tile:(s + 1) * tile], 3) for s in range(nseq)],
                          axis=0)
    cl = jnp.concatenate(
        [jnp.broadcast_to(cum[(q + 1) * c - 1:(q + 1) * c, :], (c, R_WIDTH)) for q in range(nseq * nck)],
        axis=0)
    p_out = jnp.exp(-cum)
    p_end = jnp.exp(cl - cum)
    rt = r * jnp.exp(cum)
    at = av * jnp.exp(cum - logw)
    bt = bv * p_out
    kt = k2 * p_out
    bh = bv * p_end
    kh = k2 * p_end
    pc = jnp.exp(cl)

    ii = _iota2((c, hw), 0)
    jj = _iota2((c, hw), 1) & (c - 1)
    strict = ii > jj
    incl = ii >= jj
    lane = _iota2((1, hw), 1)
    keep = (jnp.where(lane < R_HEAD, 1.0, 0.0).astype(BF16), jnp.where(lane >= R_HEAD, 1.0, 0.0).astype(BF16))
    e_r = _iota2((hw, hw), 0)
    e_c = _iota2((hw, hw), 1)
    eye_hw = e_r == e_c
    blockdiag = (e_r >= R_HEAD) == (e_c >= R_HEAD)

    def bd(x):
        xb = x.astype(BF16)
        return jnp.concatenate([xb * keep[0], xb * keep[1]], axis=0)

    def pdot(a, b_bd):
        return jnp.dot(a.astype(BF16), b_bd, preferred_element_type=F32)

    qp = [(q, p) for q in range(nseq * nck) for p in range(npair)]
    blk = lambda x, q, p: x[q * c:(q + 1) * c, hw * p:hw * (p + 1)]
    lhs = [jnp.concatenate([blk(at, *k), blk(rt, *k)], axis=0).astype(BF16) for k in qp]
    sc = [lax.dot_general(x, jnp.concatenate([bd(blk(bt, *k)), bd(blk(kt, *k))], axis=0), _NT,
                          preferred_element_type=F32) for x, k in zip(lhs, qp)]
    a_ab = [jnp.where(strict, s[:c, :hw], 0.0) for s in sc]
    a_rb = [jnp.where(incl, s[c:, :hw], 0.0) for s in sc]
    a_akrk = [jnp.concatenate([jnp.where(strict, s[:c, hw:], 0.0), jnp.where(incl, s[c:, hw:], 0.0)], axis=0)
              for s in sc]
    v_bd = [bd(blk(v, *k)) for k in qp]
    akrk_v = [pdot(a, x) for a, x in zip(a_akrk, v_bd)]

    sh = 4
    eye = jnp.where(ii == jj, 1.0, 0.0)
    npow = [jnp.where((ii >> sh) == (jj >> sh), n, 0.0) for n in a_ab]
    t_inv = [eye + n for n in npow]
    npow = [pdot(n, bd(n)) for n in npow]
    for _ in range(sh - 2):
        both = [pdot(jnp.concatenate([t, n], axis=0), bd(n)) for t, n in zip(t_inv, npow)]
        t_inv = [t + x[:c] for t, x in zip(t_inv, both)]
        npow = [x[c:] for x in both]
    t_inv = [t + pdot(t, bd(n)) for t, n in zip(t_inv, npow)]
    while (1 << sh) < c:
        off = ((ii >> (sh + 1)) == (jj >> (sh + 1))) & (((ii >> sh) & 1) > ((jj >> sh) & 1))
        tn = [pdot(t, bd(jnp.where(off, n, 0.0))) for t, n in zip(t_inv, a_ab)]
        t_inv = [t + pdot(x, bd(t)) for t, x in zip(t_inv, tn)]
        sh += 1

    tx = [pdot(t, jnp.concatenate([bd(blk(at, *k)), bd(x[:c])], axis=1))
          for t, x, k in zip(t_inv, akrk_v, qp)]
    rbx = [pdot(a, jnp.concatenate([bd(x[:, :hw]), bd(x[:, hw:])], axis=1)) for a, x in zip(a_rb, tx)]
    rp_pair = {k: blk(rt, *k) + x[:, :hw] for k, x in zip(qp, rbx)}
    y0_pair = {k: x[:, hw:] + y[c:] for k, x, y in zip(qp, rbx, akrk_v)}

    zeros_c = jnp.zeros((c, hw), F32)
    wt = {k: jnp.concatenate([blk(bh, *k), blk(kh, *k)], axis=0).T for k in qp}
    trans = {k: _mm(wt[k], jnp.concatenate([x, jnp.concatenate([zeros_c, blk(v, *k)], axis=1)], axis=0))
             for k, x in zip(qp, tx)}
    m_ba = {k: jnp.where(blockdiag, trans[k][:, :hw], 0.0) for k in qp}
    h0c = {k: jnp.where(blockdiag, trans[k][:, hw:], 0.0) for k in qp}
    pc_col = {k: jnp.sum(jnp.where(eye_hw, blk(pc, *k)[:1], 0.0), axis=1, keepdims=True) for k in qp}

    state = [g_scr[i] for i in range(nseq * npair)]
    y_rows = [None] * (nseq * nck)
    for q in range(nck):
        for s in range(nseq):
            qq = s * nck + q
            st = state[s * npair:(s + 1) * npair]
            on_st = [_mm(jnp.concatenate([rp_pair[qq, p], m_ba[qq, p]], axis=0), st[p]) for p in range(npair)]
            y_rows[qq] = jnp.concatenate([on_st[p][:c] + y0_pair[qq, p] for p in range(npair)], axis=1)
            state[s * npair:(s + 1) * npair] = [
                pc_col[qq, p] * st[p] + on_st[p][c:] + h0c[qq, p] for p in range(npair)]
    for i in range(nseq * npair):
        g_scr[i] = state[i]

    y = jnp.concatenate(y_rows, axis=0)
    out = _rwkv_post(y, r, k2, v, g, rv_ref[...], head_ones)
    for s in range(nseq):
        o_ref[s] = out[s * tile:(s + 1) * tile]

    @pl.when(ti == pl.num_programs(1) - 1)
    def _():
        for s in range(nseq):
            for p in range(npair):
                gt = g_scr[s * npair + p].T
                sfin_ref[s, 2 * p] = gt[:R_HEAD, :R_HEAD]
                sfin_ref[s, 2 * p + 1] = pltpu.roll(gt, R_HEAD, axis=1)[R_HEAD:, :R_HEAD]


def _const_spec(shape):
    return pl.BlockSpec(shape, lambda *_: (0,) * len(shape))


def _rwkv_prompt(pr3d, wts):
    batch, seq, _ = pr3d.shape
    tile = RWKV_TILE
    nseq = RWKV_ROWS

    def prev_map(b, i):
        return (b, jnp.maximum(i * (tile // SUBLANES) - 1, 0), 0)

    w_specs = [_const_spec(w.shape) for w in wts]
    return pl.pallas_call(
        _rwkv_tile_kernel,
        grid=(batch // nseq, seq // tile),
        in_specs=[pl.BlockSpec((nseq, tile, RWKV_PROJ), lambda b, i: (b, i, 0)),
                  pl.BlockSpec((nseq, SUBLANES, RWKV_PROJ), prev_map)] + w_specs,
        out_specs=[pl.BlockSpec((nseq, tile, R_WIDTH), lambda b, i: (b, i, 0)),
                   pl.BlockSpec((nseq, R_HEADS, R_HEAD, R_HEAD), lambda b, i: (b, 0, 0, 0))],
        out_shape=[jax.ShapeDtypeStruct((batch, seq, R_WIDTH), F32),
                   jax.ShapeDtypeStruct((batch, R_HEADS, R_HEAD, R_HEAD), F32)],
        scratch_shapes=[pltpu.VMEM((nseq * (R_HEADS // 2), 2 * R_HEAD, 2 * R_HEAD), F32)],
        compiler_params=pltpu.CompilerParams(dimension_semantics=("arbitrary", "arbitrary"),
                                             vmem_limit_bytes=VMEM_LIMIT),
        name="rwkv_prompt",
    )(pr3d, pr3d, *wts)


def _rwkv_sample_kernel(pr_ref, prev_ref, mu_ref, rv_ref, wup_ref, aup_ref, gup_ref, bd_ref,
                        cols_ref, s_ref, o_ref, so_ref, vec_scr):
    h = pl.program_id(0)

    @pl.when(h == 0)
    def _():
        r, logw, k2, v, av, bv, g = _rwkv_prep(pr_ref[...], prev_ref[...], mu_ref[...], rv_ref[...],
                                               wup_ref[...], aup_ref[...], gup_ref[...], bd_ref[...])
        for i, x in enumerate((r, jnp.exp(logw), k2, v, av, bv, g)):
            vec_scr[i] = x.T

    rows = pl.ds(pl.multiple_of(h * R_HEAD, R_HEAD), R_HEAD)
    r, w, k, v, a, b, g = (vec_scr[i, rows, :] for i in range(7))
    ys = []
    for vi in range(R_HEAD):
        s = s_ref[0, vi]
        sa = jnp.sum(s * a, axis=0, keepdims=True)
        s2 = s * w + sa * b + v[vi:vi + 1, :] * k
        so_ref[0, vi] = s2
        ys.append(jnp.sum(s2 * r, axis=0, keepdims=True))
    y = jnp.concatenate(ys, axis=0)
    mu = jnp.mean(y, axis=0, keepdims=True)
    d = y - mu
    var = jnp.mean(d * d, axis=0, keepdims=True)
    yn = d * lax.rsqrt(var + LNX_EPS) * cols_ref[1] + cols_ref[2]
    bonus = jnp.sum(r * k * cols_ref[0], axis=0, keepdims=True)
    o_ref[...] = (yn + bonus * v) * g


def _rwkv_sample(pr, prev, state_t, prep_w, cols):
    nb = pr.shape[0]
    args = (pr, prev) + tuple(prep_w)
    head_rows = pl.BlockSpec((R_HEAD, nb), lambda h: (h, 0))
    st_spec = pl.BlockSpec((1, R_HEAD, R_HEAD, nb), lambda h: (h, 0, 0, 0))
    return pl.pallas_call(
        _rwkv_sample_kernel,
        grid=(R_HEADS,),
        in_specs=[_const_spec(a.shape) for a in args]
                 + [pl.BlockSpec((3, R_HEAD, nb), lambda h: (0, h, 0)), st_spec],
        out_specs=[head_rows, st_spec],
        out_shape=[jax.ShapeDtypeStruct((R_WIDTH, nb), F32),
                   jax.ShapeDtypeStruct(state_t.shape, F32)],
        scratch_shapes=[pltpu.VMEM((7, R_WIDTH, nb), F32)],
        compiler_params=pltpu.CompilerParams(dimension_semantics=("arbitrary",),
                                             vmem_limit_bytes=VMEM_LIMIT),
        name="rwkv_sample",
    )(*args, cols, state_t)


STEP_BATCH = 8


def _per_head_rows(x, n):
    return jnp.concatenate([jnp.broadcast_to(x[h:h + 1, :], (n, x.shape[1])) for h in range(x.shape[0])],
                           axis=0)


def _rows_from_columns(col_b, diag, n):
    picked = jnp.where(diag, col_b, 0.0)
    return jnp.concatenate([jnp.sum(picked[h * n:(h + 1) * n], axis=0, keepdims=True)
                            for h in range(col_b.shape[0] // n)], axis=0)


def _shift_rows(x, prev8, s):
    rolled = pltpu.roll(x, s, axis=0)
    top = jnp.where(_iota2((SUBLANES, 1), 0) < s, pltpu.roll(prev8, s, axis=0), rolled[:SUBLANES])
    return jnp.concatenate([top, rolled[SUBLANES:]], axis=0)


def _head_expand(lanes_per_head=M_HEADDIM):
    shape = (DT_PAD, M_HEADS * lanes_per_head)
    head_of_lane = _iota2(shape, 1) >> int(math.log2(lanes_per_head))
    return jnp.where(_iota2(shape, 0) == head_of_lane, 1.0, 0.0).astype(BF16)


def _gated_rmsnorm(y, z, gnorm_w):
    yz = y * _silu(z)
    gw = M_WIDTH // M_GROUPS
    outs = []
    for gi in range(M_GROUPS):
        yg = yz[:, gi * gw:(gi + 1) * gw]
        ms = jnp.mean(yg * yg, axis=-1, keepdims=True)
        outs.append(yg * lax.rsqrt(ms + RMS_EPS))
    return jnp.concatenate(outs, axis=1) * gnorm_w


def _mamba_tile_kernel(z_ref, xbc_ref, xprev_ref, dt_ref, cw_ref, dta_ref, mv_ref,
                       o_ref, hfin_ref, h_scr):
    tile = MAMBA_TILE
    q = SSD_CHUNK
    ti = pl.program_id(1)

    @pl.when(ti == 0)
    def _():
        h_scr[...] = jnp.zeros_like(h_scr)

    xb = xbc_ref[...]
    prev8 = jnp.where(ti == 0, 0.0, xprev_ref[...])
    conv = cw_ref[CONV_K:CONV_K + 1, :] + xb * cw_ref[CONV_K - 1:CONV_K, :]
    for s in range(1, CONV_K):
        conv = conv + _shift_rows(xb, prev8, s) * cw_ref[CONV_K - 1 - s:CONV_K - s, :]
    xact = _silu(conv)
    xs = xact[:, :M_WIDTH]
    gs = M_GROUPS * D_STATE
    bm = xact[:, M_WIDTH:M_WIDTH + gs]
    cm = xact[:, M_WIDTH + gs:]

    dt = _softplus(dt_ref[...] + dta_ref[0:1, :])
    adt = dt * (-jnp.exp(dta_ref[1:2, :]))
    x_dt = xs * _mm_sel_r(dt, _head_expand(), 3)

    sh_q = int(math.log2(q))
    ti_i = _iota2((tile, tile), 0)
    ti_j = _iota2((tile, tile), 1)
    chunk_tril = jnp.where((ti_i >= ti_j) & ((ti_i >> sh_q) == (ti_j >> sh_q)), 1.0, 0.0).astype(BF16)
    cs = _mm_sel_l(chunk_tril, adt, 3)
    cs_t = cs.T
    incl = _iota2((q, q), 0) >= _iota2((q, q), 1)
    lane = _iota2((1, 2 * M_HEADDIM), 1)
    first = lane < M_HEADDIM
    hpg = M_HEADS // M_GROUPS

    state = [h_scr[p] for p in range(M_HEADS // 2)]
    y_rows = []
    for j in range(tile // q):
        rs = slice(j * q, (j + 1) * q)
        y_pairs = []
        for gi in range(M_GROUPS):
            bm_g = bm[rs, gi * D_STATE:(gi + 1) * D_STATE]
            cm_g = cm[rs, gi * D_STATE:(gi + 1) * D_STATE]
            cb = _mm_nt(cm_g, bm_g)
            bm_t = bm_g.T
            for pp in range(hpg // 2):
                p = gi * (hpg // 2) + pp
                x_p = x_dt[rs, 2 * M_HEADDIM * p:2 * M_HEADDIM * (p + 1)]
                y_diag = jnp.zeros((q, 2 * M_HEADDIM), F32)
                contrib = jnp.zeros((D_STATE, 2 * M_HEADDIM), F32)
                scale_in = []
                scale_end = []
                for e in range(2):
                    head = 2 * p + e
                    m = first if e == 0 else jnp.logical_not(first)
                    a_col = cs[rs, head:head + 1]
                    a_row = cs_t[head:head + 1, rs]
                    a_end = cs_t[head:head + 1, (j + 1) * q - 1:(j + 1) * q]
                    lmat = jnp.exp(jnp.where(incl, a_col - a_row, -jnp.inf))
                    x_m = jnp.where(m, x_p, 0.0)
                    y_diag = y_diag + _mm(cb * lmat, x_m)
                    contrib = contrib + _mm(bm_t * jnp.exp(a_end - a_row), x_m)
                    scale_in.append(jnp.exp(a_col))
                    scale_end.append(jnp.exp(a_end))
                y_off = _mm(cm_g, state[p]) * jnp.where(first, scale_in[0], scale_in[1])
                state[p] = state[p] * jnp.where(first, scale_end[0], scale_end[1]) + contrib
                y_pairs.append(y_diag + y_off)
        y_rows.append(jnp.concatenate(y_pairs, axis=1))
    for p in range(M_HEADS // 2):
        h_scr[p] = state[p]

    y = jnp.concatenate(y_rows, axis=0) + mv_ref[0:1, :] * xs
    o_ref[...] = _gated_rmsnorm(y, z_ref[...], mv_ref[1:2, :])

    @pl.when(ti == pl.num_programs(1) - 1)
    def _():
        for p in range(M_HEADS // 2):
            hfin_ref[0, 2 * M_HEADDIM * p:2 * M_HEADDIM * (p + 1), :] = h_scr[p].T


def _mamba_prompt(z2d, xbc2d, dt2d, batch, seq, wts):
    tile = MAMBA_TILE
    ntile = seq // tile
    rows8 = seq // SUBLANES

    def prev_map(b, i):
        return (b * rows8 + jnp.maximum(i * (tile // SUBLANES) - 1, 0), 0)

    tok = lambda w: pl.BlockSpec((tile, w), lambda b, i: (b * ntile + i, 0))
    return pl.pallas_call(
        _mamba_tile_kernel,
        grid=(batch, ntile),
        in_specs=[tok(M_WIDTH), tok(CONV_DIM), pl.BlockSpec((SUBLANES, CONV_DIM), prev_map), tok(DT_PAD)]
                 + [_const_spec(w.shape) for w in wts],
        out_specs=[tok(M_WIDTH),
                   pl.BlockSpec((1, M_HEADS * M_HEADDIM, D_STATE), lambda b, i: (b, 0, 0))],
        out_shape=[jax.ShapeDtypeStruct((batch * seq, M_WIDTH), F32),
                   jax.ShapeDtypeStruct((batch, M_HEADS * M_HEADDIM, D_STATE), F32)],
        scratch_shapes=[pltpu.VMEM((M_HEADS // 2, D_STATE, 2 * M_HEADDIM), F32)],
        compiler_params=pltpu.CompilerParams(dimension_semantics=("arbitrary", "arbitrary"),
                                             vmem_limit_bytes=VMEM_LIMIT),
        name="mamba_prompt",
    )(z2d, xbc2d, xbc2d, dt2d, *wts)


def _mamba_prep_kernel(xbc_ref, cprev_ref, dt_ref, cw_ref, dta_ref,
                       xs_ref, xdt_ref, dec_ref, bm_ref, cm_ref, cnew_ref):
    xbc = xbc_ref[...]
    conv = cw_ref[CONV_K:CONV_K + 1, :] + xbc * cw_ref[CONV_K - 1:CONV_K, :]
    for j in range(CONV_K - 1):
        conv = conv + cprev_ref[j] * cw_ref[j:j + 1, :]
        cnew_ref[j] = cprev_ref[j + 1] if j + 1 < CONV_K - 1 else xbc
    xact = _silu(conv)
    xs = xact[:, :M_WIDTH]
    gs = M_GROUPS * D_STATE
    dt = _softplus(dt_ref[...] + dta_ref[0:1, :])
    adt = dt * (-jnp.exp(dta_ref[1:2, :]))
    expand = _head_expand()
    xs_ref[...] = xs
    xdt_ref[...] = xs * _mm_sel_r(dt, expand, 3)
    dec_ref[...] = jnp.exp(_mm_sel_r(adt, _head_expand(D_STATE), 3))
    hpg = M_HEADS // M_GROUPS
    per_head = lambda m: jnp.concatenate(
        [m[:, (e // hpg) * D_STATE:(e // hpg + 1) * D_STATE] for e in range(M_HEADS)], axis=1)
    bm_ref[...] = per_head(xact[:, M_WIDTH:M_WIDTH + gs])
    cm_ref[...] = per_head(xact[:, M_WIDTH + gs:])


def _mamba_sample_prep(xbc, cprev_t, dt, cw, dta):
    n = xbc.shape[0]
    args = (xbc, cprev_t, dt, cw, dta)
    widths = (M_WIDTH, M_WIDTH, M_HEADS * D_STATE, M_HEADS * D_STATE, M_HEADS * D_STATE)
    out_shapes = [(n, w) for w in widths] + [cprev_t.shape]
    return pl.pallas_call(
        _mamba_prep_kernel,
        grid=(1,),
        in_specs=[_const_spec(a.shape) for a in args],
        out_specs=[_const_spec(s) for s in out_shapes],
        out_shape=[jax.ShapeDtypeStruct(s, F32) for s in out_shapes],
        compiler_params=pltpu.CompilerParams(vmem_limit_bytes=VMEM_LIMIT),
        name="mamba_sample_prep",
    )(*args)


def _mamba_step_kernel(xs_ref, xdt_ref, dec_ref, bm_ref, cm_ref, h_ref, dskip_ref, y_ref, ho_ref):
    n = M_HEADS * M_HEADDIM
    ones_in = jnp.ones((M_HEADDIM, D_STATE), BF16)
    ones_out = jnp.ones((D_STATE, M_HEADDIM), BF16)
    diag = (_iota2((n, M_HEADDIM), 0) & (M_HEADDIM - 1)) == _iota2((n, M_HEADDIM), 1)

    def head_rows(ref, bi):
        return jnp.concatenate(
            [jnp.broadcast_to(ref[bi:bi + 1, e * D_STATE:(e + 1) * D_STATE], (M_HEADDIM, D_STATE))
             for e in range(M_HEADS)], axis=0)

    for bi in range(STEP_BATCH):
        rows = slice(bi * M_HEADS, (bi + 1) * M_HEADS)
        hrows = slice(bi * n, (bi + 1) * n)
        x_col = _mm(jnp.where(diag, _per_head_rows(xdt_ref[rows, :], M_HEADDIM), 0.0), ones_in)
        h2 = h_ref[hrows, :] * head_rows(dec_ref, bi) + x_col * head_rows(bm_ref, bi)
        ho_ref[hrows, :] = h2
        y = _rows_from_columns(_mm(h2 * head_rows(cm_ref, bi), ones_out), diag, M_HEADDIM)
        y_ref[rows, :] = y + dskip_ref[...] * xs_ref[rows, :]


def _mamba_sample_step(xs, xdt, dec, bm, cm, state, dskip_rows):
    nb = state.shape[0]
    rows = STEP_BATCH * M_HEADS
    vec_spec = pl.BlockSpec((rows, M_HEADDIM), lambda i: (i, 0))
    bc_spec = pl.BlockSpec((STEP_BATCH, M_HEADS * D_STATE), lambda i: (i, 0))
    st_spec = pl.BlockSpec((rows * M_HEADDIM, D_STATE), lambda i: (i, 0))
    flat = lambda x, w: x.reshape(nb * M_HEADS, w)
    return pl.pallas_call(
        _mamba_step_kernel,
        grid=(nb // STEP_BATCH,),
        in_specs=[vec_spec] * 2 + [bc_spec] * 3 + [st_spec, _const_spec((M_HEADS, M_HEADDIM))],
        out_specs=[vec_spec, st_spec],
        out_shape=[jax.ShapeDtypeStruct((nb * M_HEADS, M_HEADDIM), F32),
                   jax.ShapeDtypeStruct((nb * M_HEADS * M_HEADDIM, D_STATE), F32)],
        compiler_params=pltpu.CompilerParams(dimension_semantics=("arbitrary",),
                                             vmem_limit_bytes=VMEM_LIMIT),
        name="mamba_sample_step",
    )(flat(xs, M_HEADDIM), flat(xdt, M_HEADDIM), dec, bm, cm,
      state.reshape(nb * M_HEADS * M_HEADDIM, D_STATE), dskip_rows)


FF_SLAB = 1024


def _ffn_kernel(has_gate, *refs):
    if has_gate:
        x_ref, or_ref, ym_ref, z_ref, mv_ref, wout_ref, lnp_ref, w1_ref, w2_ref, y_ref = refs
        o_m = _gated_rmsnorm(ym_ref[...], z_ref[...], mv_ref[1:2, :])
    else:
        x_ref, or_ref, om_ref, wout_ref, lnp_ref, w1_ref, w2_ref, y_ref = refs
        o_m = om_ref[...]
    mix = _mm(or_ref[...], wout_ref[:R_WIDTH, :]) + _mm(o_m, wout_ref[R_WIDTH:, :])
    h = _layer_norm(ALPHA * x_ref[...] + mix, lnp_ref[0:1, :], lnp_ref[1:2, :], LN_EPS)
    hb = h.astype(BF16)
    ff = jnp.zeros_like(h)
    for j in range(D_FF // FF_SLAB):
        t = jnp.dot(hb, w1_ref[:, j * FF_SLAB:(j + 1) * FF_SLAB], preferred_element_type=F32)
        t = jnp.square(jnp.maximum(t, 0.0))
        ff = ff + jnp.dot(t.astype(BF16), w2_ref[j * FF_SLAB:(j + 1) * FF_SLAB, :],
                          preferred_element_type=F32)
    y_ref[...] = _layer_norm(ALPHA * h + ff, lnp_ref[2:3, :], lnp_ref[3:4, :], LN_EPS)


def _ffn(x2d, o_r, o_m, gate, wout, lnp, w1, w2, tm):
    n = x2d.shape[0]
    tok = lambda w: pl.BlockSpec((tm, w), lambda i: (i, 0))
    once = lambda a: pl.BlockSpec(a.shape, lambda i: (0,) * a.ndim, pipeline_mode=pl.Buffered(1))
    acts = [x2d, o_r, o_m]
    act_specs = [tok(D_MODEL), tok(R_WIDTH), tok(M_WIDTH)]
    if gate is not None:
        acts += [gate[0], gate[1]]
        act_specs += [tok(M_WIDTH), once(gate[1])]
    wts = [wout, lnp, w1, w2]
    return pl.pallas_call(
        functools.partial(_ffn_kernel, gate is not None),
        grid=(n // tm,),
        in_specs=act_specs + [once(w) for w in wts],
        out_specs=tok(D_MODEL),
        out_shape=jax.ShapeDtypeStruct((n, D_MODEL), F32),
        compiler_params=pltpu.CompilerParams(dimension_semantics=("arbitrary",),
                                             vmem_limit_bytes=VMEM_LIMIT),
        name="ffn",
    )(*acts, *wts)


def _block_diag_ones():
    idx = np.arange(R_WIDTH) // R_HEAD
    return jnp.asarray(idx[:, None] == idx[None, :], dtype=BF16)


def kernel(x_prompt, x_sample, state_shift, state_wkv, state_conv, state_ssm, w_in, mu_shift, w0, w_up, a0, a_up, g_up, k_k, k_a, r_k, lnx_w, lnx_b, conv_w, conv_b, dt_bias, a_log, d_skip, gnorm_w, w_out, ln1_g, ln1_b, w_ff1, w_ff2, ln2_g, ln2_b):
    assert w_in.shape[0] == DEPTH and x_sample.shape[1] == 1
    batch, seq, _ = x_prompt.shape
    nb = x_sample.shape[0]
    assert seq % MAMBA_TILE == 0 and seq % RWKV_TILE == 0 and seq % DENSE_ROWS == 0
    assert nb % STEP_BATCH == 0 and batch % RWKV_ROWS == 0
    row = lambda a: a.reshape(1, -1)

    w_t = jnp.transpose(w_in[0])
    w_dt = w_t[PROJ_MAIN:]
    zl = jnp.zeros((D_DECAY_LORA, R_WIDTH), F32)
    wup_p = jnp.concatenate([w_up[0], zl], axis=0).astype(BF16)
    aup_p = jnp.concatenate([zl, a_up[0]], axis=0).astype(BF16)
    bd = _block_diag_ones()

    def rows8(*vecs):
        m = jnp.stack([v.reshape(-1) for v in vecs])
        return jnp.pad(m, ((0, SUBLANES - len(vecs)), (0, 0)))

    rv = rows8(w0[0], a0[0], k_k[0], k_a[0], r_k[0], lnx_w[0], lnx_b[0])
    prep_w = (row(mu_shift[0]), rv, wup_p, aup_p, g_up[0].astype(BF16), bd)
    cw = rows8(*conv_w[0], conv_b[0])
    dta = rows8(jnp.pad(dt_bias[0], (0, DT_PAD - M_HEADS)), jnp.pad(a_log[0], (0, DT_PAD - M_HEADS)))
    mv = rows8(jnp.repeat(d_skip[0], M_HEADDIM), gnorm_w[0])
    dskip_rows = jnp.broadcast_to(d_skip[0][:, None], (M_HEADS, M_HEADDIM))
    lnp = rows8(ln1_g[0], ln1_b[0], ln2_g[0], ln2_b[0])
    ffn_w = (w_out[0].astype(BF16), lnp, w_ff1[0].astype(BF16), w_ff2[0].astype(BF16))

    xp = x_prompt.reshape(batch * seq, D_MODEL)
    pr_p, z_p, xbc_p, dt_p, w_main = _proj_w(xp, w_t, w_dt, DENSE_ROWS)
    o_r_p, wkv_p = _rwkv_prompt(pr_p.reshape(batch, seq, RWKV_PROJ), prep_w)
    o_r_p = o_r_p.reshape(batch * seq, R_WIDTH)
    o_m_p, ssm_p = _mamba_prompt(z_p, xbc_p, dt_p, batch, seq, (cw, dta, mv))
    y_p = _ffn(xp, o_r_p, o_m_p, None, *ffn_w, tm=DENSE_ROWS)
    shift_p = pr_p.reshape(batch, seq, RWKV_PROJ)[:, -1]
    conv_p = xbc_p.reshape(batch, seq, CONV_DIM)[:, seq - (CONV_K - 1):]

    xs2d = x_sample.reshape(nb, D_MODEL)
    pr_s, z_s, xbc_s, dt_s = _proj(xs2d, w_main, w_dt, nb)
    cols = jnp.broadcast_to(jnp.stack([r_k[0].reshape(-1), lnx_w[0], lnx_b[0]])[:, :, None],
                            (3, R_WIDTH, nb))
    o_r_t, wkv_t = _rwkv_sample(pr_s, state_shift[0], jnp.transpose(state_wkv[0], (1, 2, 3, 0)),
                                prep_w, cols)
    o_r_s = o_r_t.T
    wkv_s = jnp.transpose(wkv_t, (3, 0, 1, 2))
    xs_s, xdt_s, dec_s, bm_s, cm_s, conv_t = _mamba_sample_prep(
        xbc_s, jnp.transpose(state_conv[0], (1, 0, 2)), dt_s, cw, dta)
    conv_s = jnp.transpose(conv_t, (1, 0, 2))
    y_m_s, ssm_s = _mamba_sample_step(xs_s, xdt_s, dec_s, bm_s, cm_s, state_ssm[0], dskip_rows)
    y_s = _ffn(xs2d, o_r_s, y_m_s.reshape(nb, M_WIDTH), (z_s, mv), *ffn_w, tm=nb)

    return (y_p.reshape(batch, seq, D_MODEL), y_s.reshape(nb, 1, D_MODEL),
            shift_p[None], wkv_p[None], conv_p[None],
            ssm_p.reshape(1, batch, M_HEADS, M_HEADDIM, D_STATE),
            pr_s[None], wkv_s.reshape(1, nb, R_HEADS, R_HEAD, R_HEAD), conv_s[None],
            ssm_s.reshape(1, nb, M_HEADS, M_HEADDIM, D_STATE))
```

```python
import math

import numpy as np
import jax
import jax.numpy as jnp
from jax import lax
from jax.experimental import pallas as pl
from jax.experimental.pallas import tpu as pltpu

F32 = jnp.float32
BF16 = jnp.bfloat16

D_MODEL = 1024
R_WIDTH = 512
R_HEAD = 64
R_HEADS = 8
D_DECAY_LORA = 64
D_AAA_LORA = 64
D_GATE_LORA = 128
M_WIDTH = 512
M_HEADDIM = 64
M_HEADS = 8
M_GROUPS = 2
D_STATE = 128
CONV_K = 4
CONV_DIM = M_WIDTH + 2 * M_GROUPS * D_STATE
D_FF = 4 * D_MODEL
RWKV_PROJ = 3 * R_WIDTH + D_DECAY_LORA + D_AAA_LORA + D_GATE_LORA
DT_PAD = 128
PROJ_MAIN = RWKV_PROJ + M_WIDTH + CONV_DIM
DEPTH = 1
ALPHA = (2 * DEPTH) ** 0.25
LN_EPS = 1e-5
LNX_EPS = 64e-5
RMS_EPS = 1e-5

RWKV_CHUNK = 64
RWKV_TILE = 256
RWKV_ROWS = 2
SSD_CHUNK = 128
MAMBA_TILE = 256
SUBLANES = 8
DENSE_ROWS = 512
VMEM_LIMIT = 48 * 1024 * 1024
PROJ_W_VMEM_LIMIT = 56 * 1024 * 1024

_NT = (((1,), (1,)), ((), ()))


def _mm(a, b):
    return jnp.dot(a.astype(BF16), b.astype(BF16), preferred_element_type=F32)


def _mm_nt(a, b):
    return lax.dot_general(a.astype(BF16), b.astype(BF16), _NT, preferred_element_type=F32)


def _split(x, passes):
    parts = []
    for _ in range(passes):
        p = x.astype(BF16)
        parts.append(p)
        x = x - p.astype(F32)
    return parts


def _mm_sel_r(x, sel, passes):
    return sum(jnp.dot(p, sel, preferred_element_type=F32) for p in _split(x, passes))


def _mm_sel_l(sel, x, passes):
    return sum(jnp.dot(sel, p, preferred_element_type=F32) for p in _split(x, passes))


def _softplus(x):
    return jnp.maximum(x, 0.0) + jnp.log(1.0 + jnp.exp(-jnp.abs(x)))


def _sigmoid(x):
    return 0.5 + 0.5 * jnp.tanh(0.5 * x)


def _silu(x):
    return x * _sigmoid(x)


def _layer_norm(x, g, b, eps):
    mu = jnp.mean(x, axis=-1, keepdims=True)
    d = x - mu
    var = jnp.mean(d * d, axis=-1, keepdims=True)
    return d * lax.rsqrt(var + eps) * g + b


def _iota2(shape, axis):
    return lax.broadcasted_iota(jnp.int32, shape, axis)


def _dt_columns(xb, wdt_t):
    pad = jnp.zeros((DT_PAD - M_HEADS, D_MODEL), F32)
    return _mm_nt(xb, jnp.concatenate([wdt_t, pad], axis=0))


def _proj_kernel(x_ref, w_ref, wdt_ref, pr_ref, z_ref, xbc_ref, dt_ref):
    xb = x_ref[...].astype(BF16)
    res = jnp.dot(xb, w_ref[...], preferred_element_type=F32)
    c0 = RWKV_PROJ
    c1 = c0 + M_WIDTH
    pr_ref[...] = res[:, :c0]
    z_ref[...] = res[:, c0:c1]
    xbc_ref[...] = res[:, c1:]
    dt_ref[...] = _dt_columns(xb, wdt_ref[...])


def _proj(x2d, w_bf, wdt_bf, tm):
    n = x2d.shape[0]
    widths = (RWKV_PROJ, M_WIDTH, CONV_DIM, DT_PAD)
    once = lambda a: pl.BlockSpec(a.shape, lambda i: (0, 0), pipeline_mode=pl.Buffered(1))
    return pl.pallas_call(
        _proj_kernel,
        grid=(n // tm,),
        in_specs=[pl.BlockSpec((tm, D_MODEL), lambda i: (i, 0)), once(w_bf), once(wdt_bf)],
        out_specs=[pl.BlockSpec((tm, w), lambda i: (i, 0)) for w in widths],
        out_shape=[jax.ShapeDtypeStruct((n, w), F32) for w in widths],
        compiler_params=pltpu.CompilerParams(dimension_semantics=("arbitrary",),
                                             vmem_limit_bytes=VMEM_LIMIT),
        name="proj",
    )(x2d, w_bf, wdt_bf)


W_SLAB = 256


def _proj_w_kernel(x_ref, wt_ref, wdt_ref, pr_ref, z_ref, xbc_ref, dt_ref, wbf_ref):
    @pl.when(pl.program_id(0) == 0)
    def _():
        for c in range(0, PROJ_MAIN, W_SLAB):
            wbf_ref[:, c:c + W_SLAB] = wt_ref[c:c + W_SLAB, :].T.astype(BF16)

    xb = x_ref[...].astype(BF16)
    res = jnp.dot(xb, wbf_ref[...], preferred_element_type=F32)
    c0 = RWKV_PROJ
    c1 = c0 + M_WIDTH
    pr_ref[...] = res[:, :c0]
    z_ref[...] = res[:, c0:c1]
    xbc_ref[...] = res[:, c1:]
    dt_ref[...] = _dt_columns(xb, wdt_ref[...])


def _proj_w(x2d, w_t, wdt_bf, tm):
    n = x2d.shape[0]
    widths = (RWKV_PROJ, M_WIDTH, CONV_DIM, DT_PAD)
    once = lambda a: pl.BlockSpec(a.shape, lambda i: (0, 0), pipeline_mode=pl.Buffered(1))
    return pl.pallas_call(
        _proj_w_kernel,
        grid=(n // tm,),
        in_specs=[pl.BlockSpec((tm, D_MODEL), lambda i: (i, 0)), once(w_t), once(wdt_bf)],
        out_specs=[pl.BlockSpec((tm, w), lambda i: (i, 0)) for w in widths]
                  + [pl.BlockSpec((D_MODEL, PROJ_MAIN), lambda i: (0, 0))],
        out_shape=[jax.ShapeDtypeStruct((n, w), F32) for w in widths]
                  + [jax.ShapeDtypeStruct((D_MODEL, PROJ_MAIN), BF16)],
        compiler_params=pltpu.CompilerParams(dimension_semantics=("arbitrary",),
                                             vmem_limit_bytes=PROJ_W_VMEM_LIMIT),
        name="proj_w",
    )(x2d, w_t, wdt_bf)


def _rwkv_prep(pr, prev, mu, rv, wup, aup, gup, bd):
    w0, a0, k_k, k_a = (rv[i:i + 1, :] for i in range(4))
    u = pr + (prev - pr) * mu
    r = u[:, 0:R_WIDTH]
    k = u[:, R_WIDTH:2 * R_WIDTH]
    v = u[:, 2 * R_WIDTH:3 * R_WIDTH]
    lora = u[:, 3 * R_WIDTH:3 * R_WIDTH + D_DECAY_LORA + D_AAA_LORA]
    ug = u[:, 3 * R_WIDTH + D_DECAY_LORA + D_AAA_LORA:]
    w_log = -_softplus(-(w0 + _mm(jnp.tanh(lora), wup))) - 0.5
    logw = -jnp.exp(w_log)
    a = _sigmoid(a0 + _mm(lora, aup))
    g = _mm(_sigmoid(ug), gup)
    kk = k * k_k
    ss = _mm_sel_r(kk * kk, bd, 1)
    kkn = kk * lax.rsqrt(jnp.maximum(ss, 1e-24))
    k2 = k * (1.0 + (a - 1.0) * k_a)
    return r, logw, k2, v, -kkn, kkn * a, g


def _rwkv_post(y, r, k2, v, g, rv, bd):
    r_k, lnx_w, lnx_b = (rv[i:i + 1, :] for i in range(4, 7))
    inv = 1.0 / R_HEAD
    mu = _mm_sel_r(y, bd, 1) * inv
    d = y - mu
    var = _mm_sel_r(d * d, bd, 1) * inv
    yn = d * lax.rsqrt(var + LNX_EPS) * lnx_w + lnx_b
    bonus = _mm_sel_r(r * k2 * r_k, bd, 1)
    return (yn + bonus * v) * g


def _rwkv_tile_kernel(pr_ref, prevblk_ref, mu_ref, rv_ref, wup_ref, aup_ref, gup_ref, bd_ref,
                      o_ref, sfin_ref, g_scr):
    c = RWKV_CHUNK
    tile = RWKV_TILE
    nseq = RWKV_ROWS
    nck = tile // c
    hw = 2 * R_HEAD
    npair = R_HEADS // 2
    ti = pl.program_id(1)

    @pl.when(ti == 0)
    def _():
        g_scr[...] = jnp.zeros_like(g_scr)

    pr = jnp.concatenate([pr_ref[s] for s in range(nseq)], axis=0)
    prev = pltpu.roll(pr, 1, axis=0)
    row_id = _iota2((nseq * tile, 1), 0)
    for s in range(nseq):
        prev_row = jnp.where(ti == 0, 0.0, prevblk_ref[s, SUBLANES - 1:SUBLANES, :])
        prev = jnp.where(row_id == s * tile, prev_row, prev)
    head_ones = bd_ref[...]
    r, logw, k2, v, av, bv, g = _rwkv_prep(pr, prev, mu_ref[...], rv_ref[...], wup_ref[...],
                                           aup_ref[...], gup_ref[...], head_ones)

    sh_c = int(math.log2(c))
    ti_i = _iota2((tile, tile), 0)
    ti_j = _iota2((tile, tile), 1)
    chunk_tril = jnp.where((ti_i >= ti_j) & ((ti_i >> sh_c) == (ti_j >> sh_c)), 1.0, 0.0).astype(BF16)
    cum = jnp.concatenate([_mm_sel_l(chunk_tril, logw[s * tile:(s + 1) * tile], 3) for s in range(nseq)],
                          axis=0)
    cl = jnp.concatenate(
        [jnp.broadcast_to(cum[(q + 1) * c - 1:(q + 1) * c, :], (c, R_WIDTH)) for q in range(nseq * nck)],
        axis=0)
    p_out = jnp.exp(-cum)
    p_end = jnp.exp(cl - cum)
    rt = r * jnp.exp(cum)
    at = av * jnp.exp(cum - logw)
    bt = bv * p_out
    kt = k2 * p_out
    bh = bv * p_end
    kh = k2 * p_end
    pc = jnp.exp(cl)

    ii = _iota2((c, hw), 0)
    jj = _iota2((c, hw), 1) & (c - 1)
    strict = ii > jj
    incl = ii >= jj
    lane = _iota2((1, hw), 1)
    keep = (jnp.where(lane < R_HEAD, 1.0, 0.0).astype(BF16), jnp.where(lane >= R_HEAD, 1.0, 0.0).astype(BF16))
    e_r = _iota2((hw, hw), 0)
    e_c = _iota2((hw, hw), 1)
    eye_hw = e_r == e_c
    blockdiag = (e_r >= R_HEAD) == (e_c >= R_HEAD)

    def bd(x):
        xb = x.astype(BF16)
        return jnp.concatenate([xb * keep[0], xb * keep[1]], axis=0)

    def pdot(a, b_bd):
        return jnp.dot(a.astype(BF16), b_bd, preferred_element_type=F32)

    qp = [(q, p) for q in range(nseq * nck) for p in range(npair)]
    blk = lambda x, q, p: x[q * c:(q + 1) * c, hw * p:hw * (p + 1)]
    lhs = [jnp.concatenate([blk(at, *k), blk(rt, *k)], axis=0).astype(BF16) for k in qp]
    sc = [lax.dot_general(x, jnp.concatenate([bd(blk(bt, *k)), bd(blk(kt, *k))], axis=0), _NT,
                          preferred_element_type=F32) for x, k in zip(lhs, qp)]
    a_ab = [jnp.where(strict, s[:c, :hw], 0.0) for s in sc]
    a_rb = [jnp.where(incl, s[c:, :hw], 0.0) for s in sc]
    a_akrk = [jnp.concatenate([jnp.where(strict, s[:c, hw:], 0.0), jnp.where(incl, s[c:, hw:], 0.0)], axis=0)
              for s in sc]
    v_bd = [bd(blk(v, *k)) for k in qp]
    akrk_v = [pdot(a, x) for a, x in zip(a_akrk, v_bd)]

    sh = 4
    eye = jnp.where(ii == jj, 1.0, 0.0)
    npow = [jnp.where((ii >> sh) == (jj >> sh), n, 0.0) for n in a_ab]
    t_inv = [eye + n for n in npow]
    npow = [pdot(n, bd(n)) for n in npow]
    for _ in range(sh - 2):
        both = [pdot(jnp.concatenate([t, n], axis=0), bd(n)) for t, n in zip(t_inv, npow)]
        t_inv = [t + x[:c] for t, x in zip(t_inv, both)]
        npow = [x[c:] for x in both]
    t_inv = [t + pdot(t, bd(n)) for t, n in zip(t_inv, npow)]
    while (1 << sh) < c:
        off = ((ii >> (sh + 1)) == (jj >> (sh + 1))) & (((ii >> sh) & 1) > ((jj >> sh) & 1))
        tn = [pdot(t, bd(jnp.where(off, n, 0.0))) for t, n in zip(t_inv, a_ab)]
        t_inv = [t + pdot(x, bd(t)) for t, x in zip(t_inv, tn)]
        sh += 1

    tx = [pdot(t, jnp.concatenate([bd(blk(at, *k)), bd(x[:c])], axis=1))
          for t, x, k in zip(t_inv, akrk_v, qp)]
    rbx = [pdot(a, jnp.concatenate([bd(x[:, :hw]), bd(x[:, hw:])], axis=1)) for a, x in zip(a_rb, tx)]
    rp_pair = {k: blk(rt, *k) + x[:, :hw] for k, x in zip(qp, rbx)}
    y0_pair = {k: x[:, hw:] + y[c:] for k, x, y in zip(qp, rbx, akrk_v)}

    zeros_c = jnp.zeros((c, hw), F32)
    wt = {k: jnp.concatenate([blk(bh, *k), blk(kh, *k)], axis=0).T for k in qp}
    trans = {k: _mm(wt[k], jnp.concatenate([x, jnp.concatenate([zeros_c, blk(v, *k)], axis=1)], axis=0))
             for k, x in zip(qp, tx)}
    m_ba = {k: jnp.where(blockdiag, trans[k][:, :hw], 0.0) for k in qp}
    h0c = {k: jnp.where(blockdiag, trans[k][:, hw:], 0.0) for k in qp}
    pc_col = {k: jnp.sum(jnp.where(eye_hw, blk(pc, *k)[:1], 0.0), axis=1, keepdims=True) for k in qp}

    state = [g_scr[i] for i in range(nseq * npair)]
    y_rows = [None] * (nseq * nck)
    for q in range(nck):
        for s in range(nseq):
            qq = s * nck + q
            st = state[s * npair:(s + 1) * npair]
            on_st = [_mm(jnp.concatenate([rp_pair[qq, p], m_ba[qq, p]], axis=0), st[p]) for p in range(npair)]
            y_rows[qq] = jnp.concatenate([on_st[p][:c] + y0_pair[qq, p] for p in range(npair)], axis=1)
            state[s * npair:(s + 1) * npair] = [
                pc_col[qq, p] * st[p] + on_st[p][c:] + h0c[qq, p] for p in range(npair)]
    for i in range(nseq * npair):
        g_scr[i] = state[i]

    y = jnp.concatenate(y_rows, axis=0)
    out = _rwkv_post(y, r, k2, v, g, rv_ref[...], head_ones)
    for s in range(nseq):
        o_ref[s] = out[s * tile:(s + 1) * tile]

    @pl.when(ti == pl.num_programs(1) - 1)
    def _():
        for s in range(nseq):
            for p in range(npair):
                gt = g_scr[s * npair + p].T
                sfin_ref[s, 2 * p] = gt[:R_HEAD, :R_HEAD]
                sfin_ref[s, 2 * p + 1] = pltpu.roll(gt, R_HEAD, axis=1)[R_HEAD:, :R_HEAD]


def _const_spec(shape):
    return pl.BlockSpec(shape, lambda *_: (0,) * len(shape))


def _rwkv_prompt(pr3d, wts):
    batch, seq, _ = pr3d.shape
    tile = RWKV_TILE
    nseq = RWKV_ROWS

    def prev_map(b, i):
        return (b, jnp.maximum(i * (tile // SUBLANES) - 1, 0), 0)

    w_specs = [_const_spec(w.shape) for w in wts]
    return pl.pallas_call(
        _rwkv_tile_kernel,
        grid=(batch // nseq, seq // tile),
        in_specs=[pl.BlockSpec((nseq, tile, RWKV_PROJ), lambda b, i: (b, i, 0)),
                  pl.BlockSpec((nseq, SUBLANES, RWKV_PROJ), prev_map)] + w_specs,
        out_specs=[pl.BlockSpec((nseq, tile, R_WIDTH), lambda b, i: (b, i, 0)),
                   pl.BlockSpec((nseq, R_HEADS, R_HEAD, R_HEAD), lambda b, i: (b, 0, 0, 0))],
        out_shape=[jax.ShapeDtypeStruct((batch, seq, R_WIDTH), F32),
                   jax.ShapeDtypeStruct((batch, R_HEADS, R_HEAD, R_HEAD), F32)],
        scratch_shapes=[pltpu.VMEM((nseq * (R_HEADS // 2), 2 * R_HEAD, 2 * R_HEAD), F32)],
        compiler_params=pltpu.CompilerParams(dimension_semantics=("arbitrary", "arbitrary"),
                                             vmem_limit_bytes=VMEM_LIMIT),
        name="rwkv_prompt",
    )(pr3d, pr3d, *wts)


def _rwkv_sample_kernel(pr_ref, prev_ref, mu_ref, rv_ref, wup_ref, aup_ref, gup_ref, bd_ref,
                        cols_ref, s_ref, o_ref, so_ref, vec_scr):
    h = pl.program_id(0)

    @pl.when(h == 0)
    def _():
        r, logw, k2, v, av, bv, g = _rwkv_prep(pr_ref[...], prev_ref[...], mu_ref[...], rv_ref[...],
                                               wup_ref[...], aup_ref[...], gup_ref[...], bd_ref[...])
        for i, x in enumerate((r, jnp.exp(logw), k2, v, av, bv, g)):
            vec_scr[i] = x.T

    rows = pl.ds(pl.multiple_of(h * R_HEAD, R_HEAD), R_HEAD)
    r, w, k, v, a, b, g = (vec_scr[i, rows, :] for i in range(7))
    ys = []
    for vi in range(R_HEAD):
        s = s_ref[0, vi]
        sa = jnp.sum(s * a, axis=0, keepdims=True)
        s2 = s * w + sa * b + v[vi:vi + 1, :] * k
        so_ref[0, vi] = s2
        ys.append(jnp.sum(s2 * r, axis=0, keepdims=True))
    y = jnp.concatenate(ys, axis=0)
    mu = jnp.mean(y, axis=0, keepdims=True)
    d = y - mu
    var = jnp.mean(d * d, axis=0, keepdims=True)
    yn = d * lax.rsqrt(var + LNX_EPS) * cols_ref[1] + cols_ref[2]
    bonus = jnp.sum(r * k * cols_ref[0], axis=0, keepdims=True)
    o_ref[...] = (yn + bonus * v) * g


def _rwkv_sample(pr, prev, state_t, prep_w, cols):
    nb = pr.shape[0]
    args = (pr, prev) + tuple(prep_w)
    head_rows = pl.BlockSpec((R_HEAD, nb), lambda h: (h, 0))
    st_spec = pl.BlockSpec((1, R_HEAD, R_HEAD, nb), lambda h: (h, 0, 0, 0))
    return pl.pallas_call(
        _rwkv_sample_kernel,
        grid=(R_HEADS,),
        in_specs=[_const_spec(a.shape) for a in args]
                 + [pl.BlockSpec((3, R_HEAD, nb), lambda h: (0, h, 0)), st_spec],
        out_specs=[head_rows, st_spec],
        out_shape=[jax.ShapeDtypeStruct((R_WIDTH, nb), F32),
                   jax.ShapeDtypeStruct(state_t.shape, F32)],
        scratch_shapes=[pltpu.VMEM((7, R_WIDTH, nb), F32)],
        compiler_params=pltpu.CompilerParams(dimension_semantics=("arbitrary",),
                                             vmem_limit_bytes=VMEM_LIMIT),
        name="rwkv_sample",
    )(*args, cols, state_t)


STEP_BATCH = 16


def _per_head_rows(x, n):
    return jnp.concatenate([jnp.broadcast_to(x[h:h + 1, :], (n, x.shape[1])) for h in range(x.shape[0])],
                           axis=0)


def _rows_from_columns(col_b, diag, n):
    picked = jnp.where(diag, col_b, 0.0)
    return jnp.concatenate([jnp.sum(picked[h * n:(h + 1) * n], axis=0, keepdims=True)
                            for h in range(col_b.shape[0] // n)], axis=0)


def _shift_rows(x, prev8, s):
    rolled = pltpu.roll(x, s, axis=0)
    top = jnp.where(_iota2((SUBLANES, 1), 0) < s, pltpu.roll(prev8, s, axis=0), rolled[:SUBLANES])
    return jnp.concatenate([top, rolled[SUBLANES:]], axis=0)


def _head_expand(lanes_per_head=M_HEADDIM):
    shape = (DT_PAD, M_HEADS * lanes_per_head)
    head_of_lane = _iota2(shape, 1) >> int(math.log2(lanes_per_head))
    return jnp.where(_iota2(shape, 0) == head_of_lane, 1.0, 0.0).astype(BF16)


def _gated_rmsnorm(y, z, gnorm_w):
    yz = y * _silu(z)
    gw = M_WIDTH // M_GROUPS
    outs = []
    for gi in range(M_GROUPS):
        yg = yz[:, gi * gw:(gi + 1) * gw]
        ms = jnp.mean(yg * yg, axis=-1, keepdims=True)
        outs.append(yg * lax.rsqrt(ms + RMS_EPS))
    return jnp.concatenate(outs, axis=1) * gnorm_w


def _mamba_tile_kernel(z_ref, xbc_ref, xprev_ref, dt_ref, cw_ref, dta_ref, mv_ref,
                       o_ref, hfin_ref, h_scr):
    tile = MAMBA_TILE
    q = SSD_CHUNK
    ti = pl.program_id(1)

    @pl.when(ti == 0)
    def _():
        h_scr[...] = jnp.zeros_like(h_scr)

    xb = xbc_ref[...]
    prev8 = jnp.where(ti == 0, 0.0, xprev_ref[...])
    conv = cw_ref[CONV_K:CONV_K + 1, :] + xb * cw_ref[CONV_K - 1:CONV_K, :]
    for s in range(1, CONV_K):
        conv = conv + _shift_rows(xb, prev8, s) * cw_ref[CONV_K - 1 - s:CONV_K - s, :]
    xact = _silu(conv)
    xs = xact[:, :M_WIDTH]
    gs = M_GROUPS * D_STATE
    bm = xact[:, M_WIDTH:M_WIDTH + gs]
    cm = xact[:, M_WIDTH + gs:]

    dt = _softplus(dt_ref[...] + dta_ref[0:1, :])
    adt = dt * (-jnp.exp(dta_ref[1:2, :]))
    x_dt = xs * _mm_sel_r(dt, _head_expand(), 3)

    sh_q = int(math.log2(q))
    ti_i = _iota2((tile, tile), 0)
    ti_j = _iota2((tile, tile), 1)
    chunk_tril = jnp.where((ti_i >= ti_j) & ((ti_i >> sh_q) == (ti_j >> sh_q)), 1.0, 0.0).astype(BF16)
    cs = _mm_sel_l(chunk_tril, adt, 3)
    cs_t = cs.T
    incl = _iota2((q, q), 0) >= _iota2((q, q), 1)
    lane = _iota2((1, 2 * M_HEADDIM), 1)
    first = lane < M_HEADDIM
    hpg = M_HEADS // M_GROUPS

    state = [h_scr[p] for p in range(M_HEADS // 2)]
    y_rows = []
    for j in range(tile // q):
        rs = slice(j * q, (j + 1) * q)
        y_pairs = []
        for gi in range(M_GROUPS):
            bm_g = bm[rs, gi * D_STATE:(gi + 1) * D_STATE]
            cm_g = cm[rs, gi * D_STATE:(gi + 1) * D_STATE]
            cb = _mm_nt(cm_g, bm_g)
            bm_t = bm_g.T
            for pp in range(hpg // 2):
                p = gi * (hpg // 2) + pp
                x_p = x_dt[rs, 2 * M_HEADDIM * p:2 * M_HEADDIM * (p + 1)]
                y_diag = jnp.zeros((q, 2 * M_HEADDIM), F32)
                contrib = jnp.zeros((D_STATE, 2 * M_HEADDIM), F32)
                scale_in = []
                scale_end = []
                for e in range(2):
                    head = 2 * p + e
                    m = first if e == 0 else jnp.logical_not(first)
                    a_col = cs[rs, head:head + 1]
                    a_row = cs_t[head:head + 1, rs]
                    a_end = cs_t[head:head + 1, (j + 1) * q - 1:(j + 1) * q]
                    lmat = jnp.exp(jnp.where(incl, a_col - a_row, -jnp.inf))
                    x_m = jnp.where(m, x_p, 0.0)
                    y_diag = y_diag + _mm(cb * lmat, x_m)
                    contrib = contrib + _mm(bm_t * jnp.exp(a_end - a_row), x_m)
                    scale_in.append(jnp.exp(a_col))
                    scale_end.append(jnp.exp(a_end))
                y_off = _mm(cm_g, state[p]) * jnp.where(first, scale_in[0], scale_in[1])
                state[p] = state[p] * jnp.where(first, scale_end[0], scale_end[1]) + contrib
                y_pairs.append(y_diag + y_off)
        y_rows.append(jnp.concatenate(y_pairs, axis=1))
    for p in range(M_HEADS // 2):
        h_scr[p] = state[p]

    y = jnp.concatenate(y_rows, axis=0) + mv_ref[0:1, :] * xs
    o_ref[...] = _gated_rmsnorm(y, z_ref[...], mv_ref[1:2, :])

    @pl.when(ti == pl.num_programs(1) - 1)
    def _():
        for p in range(M_HEADS // 2):
            hfin_ref[0, 2 * M_HEADDIM * p:2 * M_HEADDIM * (p + 1), :] = h_scr[p].T


def _mamba_prompt(z2d, xbc2d, dt2d, batch, seq, wts):
    tile = MAMBA_TILE
    ntile = seq // tile
    rows8 = seq // SUBLANES

    def prev_map(b, i):
        return (b * rows8 + jnp.maximum(i * (tile // SUBLANES) - 1, 0), 0)

    tok = lambda w: pl.BlockSpec((tile, w), lambda b, i: (b * ntile + i, 0))
    return pl.pallas_call(
        _mamba_tile_kernel,
        grid=(batch, ntile),
        in_specs=[tok(M_WIDTH), tok(CONV_DIM), pl.BlockSpec((SUBLANES, CONV_DIM), prev_map), tok(DT_PAD)]
                 + [_const_spec(w.shape) for w in wts],
        out_specs=[tok(M_WIDTH),
                   pl.BlockSpec((1, M_HEADS * M_HEADDIM, D_STATE), lambda b, i: (b, 0, 0))],
        out_shape=[jax.ShapeDtypeStruct((batch * seq, M_WIDTH), F32),
                   jax.ShapeDtypeStruct((batch, M_HEADS * M_HEADDIM, D_STATE), F32)],
        scratch_shapes=[pltpu.VMEM((M_HEADS // 2, D_STATE, 2 * M_HEADDIM), F32)],
        compiler_params=pltpu.CompilerParams(dimension_semantics=("arbitrary", "arbitrary"),
                                             vmem_limit_bytes=VMEM_LIMIT),
        name="mamba_prompt",
    )(z2d, xbc2d, xbc2d, dt2d, *wts)


def _mamba_prep_kernel(xbc_ref, cprev_ref, dt_ref, cw_ref, dta_ref,
                       xs_ref, xdt_ref, dec_ref, bm_ref, cm_ref, cnew_ref):
    xbc = xbc_ref[...]
    conv = cw_ref[CONV_K:CONV_K + 1, :] + xbc * cw_ref[CONV_K - 1:CONV_K, :]
    for j in range(CONV_K - 1):
        conv = conv + cprev_ref[j] * cw_ref[j:j + 1, :]
        cnew_ref[j] = cprev_ref[j + 1] if j + 1 < CONV_K - 1 else xbc
    xact = _silu(conv)
    xs = xact[:, :M_WIDTH]
    gs = M_GROUPS * D_STATE
    dt = _softplus(dt_ref[...] + dta_ref[0:1, :])
    adt = dt * (-jnp.exp(dta_ref[1:2, :]))
    expand = _head_expand()
    xs_ref[...] = xs
    xdt_ref[...] = xs * _mm_sel_r(dt, expand, 3)
    dec_ref[...] = jnp.exp(_mm_sel_r(adt, _head_expand(D_STATE), 3))
    hpg = M_HEADS // M_GROUPS
    per_head = lambda m: jnp.concatenate(
        [m[:, (e // hpg) * D_STATE:(e // hpg + 1) * D_STATE] for e in range(M_HEADS)], axis=1)
    bm_ref[...] = per_head(xact[:, M_WIDTH:M_WIDTH + gs])
    cm_ref[...] = per_head(xact[:, M_WIDTH + gs:])


def _mamba_sample_prep(xbc, cprev_t, dt, cw, dta):
    n = xbc.shape[0]
    args = (xbc, cprev_t, dt, cw, dta)
    widths = (M_WIDTH, M_WIDTH, M_HEADS * D_STATE, M_HEADS * D_STATE, M_HEADS * D_STATE)
    out_shapes = [(n, w) for w in widths] + [cprev_t.shape]
    return pl.pallas_call(
        _mamba_prep_kernel,
        grid=(1,),
        in_specs=[_const_spec(a.shape) for a in args],
        out_specs=[_const_spec(s) for s in out_shapes],
        out_shape=[jax.ShapeDtypeStruct(s, F32) for s in out_shapes],
        compiler_params=pltpu.CompilerParams(vmem_limit_bytes=VMEM_LIMIT),
        name="mamba_sample_prep",
    )(*args)


def _mamba_step_kernel(xs_ref, xdt_ref, dec_ref, bm_ref, cm_ref, h_ref, dskip_ref, y_ref, ho_ref):
    n = M_HEADS * M_HEADDIM
    ones_in = jnp.ones((M_HEADDIM, D_STATE), BF16)
    ones_out = jnp.ones((D_STATE, M_HEADDIM), BF16)
    diag = (_iota2((n, M_HEADDIM), 0) & (M_HEADDIM - 1)) == _iota2((n, M_HEADDIM), 1)

    def head_rows(ref, bi):
        return jnp.concatenate(
            [jnp.broadcast_to(ref[bi:bi + 1, e * D_STATE:(e + 1) * D_STATE], (M_HEADDIM, D_STATE))
             for e in range(M_HEADS)], axis=0)

    for bi in range(STEP_BATCH):
        rows = slice(bi * M_HEADS, (bi + 1) * M_HEADS)
        hrows = slice(bi * n, (bi + 1) * n)
        x_col = _mm(jnp.where(diag, _per_head_rows(xdt_ref[rows, :], M_HEADDIM), 0.0), ones_in)
        h2 = h_ref[hrows, :] * head_rows(dec_ref, bi) + x_col * head_rows(bm_ref, bi)
        ho_ref[hrows, :] = h2
        y = _rows_from_columns(_mm(h2 * head_rows(cm_ref, bi), ones_out), diag, M_HEADDIM)
        y_ref[rows, :] = y + dskip_ref[...] * xs_ref[rows, :]


def _mamba_sample_step(xs, xdt, dec, bm, cm, state, dskip_rows):
    nb = state.shape[0]
    rows = STEP_BATCH * M_HEADS
    vec_spec = pl.BlockSpec((rows, M_HEADDIM), lambda i: (i, 0))
    bc_spec = pl.BlockSpec((STEP_BATCH, M_HEADS * D_STATE), lambda i: (i, 0))
    st_spec = pl.BlockSpec((rows * M_HEADDIM, D_STATE), lambda i: (i, 0))
    flat = lambda x, w: x.reshape(nb * M_HEADS, w)
    return pl.pallas_call(
        _mamba_step_kernel,
        grid=(nb // STEP_BATCH,),
        in_specs=[vec_spec] * 2 + [bc_spec] * 3 + [st_spec, _const_spec((M_HEADS, M_HEADDIM))],
        out_specs=[vec_spec, st_spec],
        out_shape=[jax.ShapeDtypeStruct((nb * M_HEADS, M_HEADDIM), F32),
                   jax.ShapeDtypeStruct((nb * M_HEADS * M_HEADDIM, D_STATE), F32)],
        compiler_params=pltpu.CompilerParams(dimension_semantics=("arbitrary",),
                                             vmem_limit_bytes=VMEM_LIMIT),
        name="mamba_sample_step",
    )(flat(xs, M_HEADDIM), flat(xdt, M_HEADDIM), dec, bm, cm,
      state.reshape(nb * M_HEADS * M_HEADDIM, D_STATE), dskip_rows)


FF_SLAB = 1024


def _ffn_math(x, o_r, o_m, wout_ref, lnp_ref, w1_ref, w2_ref):
    mix = _mm(o_r, wout_ref[:R_WIDTH, :]) + _mm(o_m, wout_ref[R_WIDTH:, :])
    h = _layer_norm(ALPHA * x + mix, lnp_ref[0:1, :], lnp_ref[1:2, :], LN_EPS)
    hb = h.astype(BF16)
    ff = jnp.zeros_like(h)
    for j in range(D_FF // FF_SLAB):
        t = jnp.dot(hb, w1_ref[:, j * FF_SLAB:(j + 1) * FF_SLAB], preferred_element_type=F32)
        t = jnp.square(jnp.maximum(t, 0.0))
        ff = ff + jnp.dot(t.astype(BF16), w2_ref[j * FF_SLAB:(j + 1) * FF_SLAB, :],
                          preferred_element_type=F32)
    return _layer_norm(ALPHA * h + ff, lnp_ref[2:3, :], lnp_ref[3:4, :], LN_EPS)


def _ffn_kernel(x_ref, or_ref, ym_ref, z_ref, mv_ref, wout_ref, lnp_ref, w1_ref, w2_ref, y_ref):
    o_m = _gated_rmsnorm(ym_ref[...], z_ref[...], mv_ref[1:2, :])
    y_ref[...] = _ffn_math(x_ref[...], or_ref[...], o_m, wout_ref, lnp_ref, w1_ref, w2_ref)


def _ffn(x2d, o_r, y_m, z, mv, wout, lnp, w1, w2, tm):
    n = x2d.shape[0]
    tok = lambda w: pl.BlockSpec((tm, w), lambda i: (i, 0))
    once = lambda a: pl.BlockSpec(a.shape, lambda i: (0,) * a.ndim, pipeline_mode=pl.Buffered(1))
    wts = [mv, wout, lnp, w1, w2]
    return pl.pallas_call(
        _ffn_kernel,
        grid=(n // tm,),
        in_specs=[tok(D_MODEL), tok(R_WIDTH), tok(M_WIDTH), tok(M_WIDTH)] + [once(w) for w in wts],
        out_specs=tok(D_MODEL),
        out_shape=jax.ShapeDtypeStruct((n, D_MODEL), F32),
        compiler_params=pltpu.CompilerParams(dimension_semantics=("arbitrary",),
                                             vmem_limit_bytes=VMEM_LIMIT),
        name="ffn",
    )(x2d, o_r, y_m, z, *wts)


FFN_PREP = 8


def _ffn_w_kernel(x_ref, or_ref, om_ref, wout32_ref, w1_32_ref, w2_32_ref, lnp_ref,
                  y_ref, wout_ref, w1_ref, w2_ref):
    i = pl.program_id(0)

    @pl.when(i < FFN_PREP)
    def _():
        r_in = D_MODEL // FFN_PREP
        r_ff = D_FF // FFN_PREP
        rows_in = pl.ds(pl.multiple_of(i * r_in, r_in), r_in)
        wout_ref[rows_in, :] = wout32_ref[...].astype(BF16)
        w1_ref[rows_in, :] = w1_32_ref[...].astype(BF16)
        w2_ref[pl.ds(pl.multiple_of(i * r_ff, r_ff), r_ff), :] = w2_32_ref[...].astype(BF16)

    @pl.when(i >= FFN_PREP)
    def _():
        y_ref[...] = _ffn_math(x_ref[...], or_ref[...], om_ref[...], wout_ref, lnp_ref, w1_ref, w2_ref)


def _ffn_w(x2d, o_r, o_m, wout32, w1_32, w2_32, lnp, tm):
    n = x2d.shape[0]
    prep = FFN_PREP
    tok = lambda w: pl.BlockSpec((tm, w), lambda i: (jnp.maximum(i - prep, 0), 0))
    slab = lambda a: pl.BlockSpec((a.shape[0] // prep, a.shape[1]), lambda i: (jnp.minimum(i, prep - 1), 0))
    whole = lambda shape: pl.BlockSpec(shape, lambda i: (0, 0), pipeline_mode=pl.Buffered(1))
    w_shapes = [wout32.shape, w1_32.shape, w2_32.shape]
    return pl.pallas_call(
        _ffn_w_kernel,
        grid=(prep + n // tm,),
        in_specs=[tok(D_MODEL), tok(R_WIDTH), tok(M_WIDTH), slab(wout32), slab(w1_32), slab(w2_32),
                  whole(lnp.shape)],
        out_specs=[tok(D_MODEL)] + [whole(s) for s in w_shapes],
        out_shape=[jax.ShapeDtypeStruct((n, D_MODEL), F32)]
                  + [jax.ShapeDtypeStruct(s, BF16) for s in w_shapes],
        compiler_params=pltpu.CompilerParams(dimension_semantics=("arbitrary",),
                                             vmem_limit_bytes=PROJ_W_VMEM_LIMIT),
        name="ffn_w",
    )(x2d, o_r, o_m, wout32, w1_32, w2_32, lnp)


def _block_diag_ones():
    idx = np.arange(R_WIDTH) // R_HEAD
    return jnp.asarray(idx[:, None] == idx[None, :], dtype=BF16)


def kernel(x_prompt, x_sample, state_shift, state_wkv, state_conv, state_ssm, w_in, mu_shift, w0, w_up, a0, a_up, g_up, k_k, k_a, r_k, lnx_w, lnx_b, conv_w, conv_b, dt_bias, a_log, d_skip, gnorm_w, w_out, ln1_g, ln1_b, w_ff1, w_ff2, ln2_g, ln2_b):
    assert w_in.shape[0] == DEPTH and x_sample.shape[1] == 1
    batch, seq, _ = x_prompt.shape
    nb = x_sample.shape[0]
    assert seq % MAMBA_TILE == 0 and seq % RWKV_TILE == 0 and seq % DENSE_ROWS == 0
    assert nb % STEP_BATCH == 0 and batch % RWKV_ROWS == 0
    row = lambda a: a.reshape(1, -1)

    w_t = jnp.transpose(w_in[0])
    w_dt = w_t[PROJ_MAIN:]
    zl = jnp.zeros((D_DECAY_LORA, R_WIDTH), F32)
    wup_p = jnp.concatenate([w_up[0], zl], axis=0).astype(BF16)
    aup_p = jnp.concatenate([zl, a_up[0]], axis=0).astype(BF16)
    bd = _block_diag_ones()

    def rows8(*vecs):
        m = jnp.stack([v.reshape(-1) for v in vecs])
        return jnp.pad(m, ((0, SUBLANES - len(vecs)), (0, 0)))

    rv = rows8(w0[0], a0[0], k_k[0], k_a[0], r_k[0], lnx_w[0], lnx_b[0])
    prep_w = (row(mu_shift[0]), rv, wup_p, aup_p, g_up[0].astype(BF16), bd)
    cw = rows8(*conv_w[0], conv_b[0])
    dta = rows8(jnp.pad(dt_bias[0], (0, DT_PAD - M_HEADS)), jnp.pad(a_log[0], (0, DT_PAD - M_HEADS)))
    mv = rows8(jnp.repeat(d_skip[0], M_HEADDIM), gnorm_w[0])
    dskip_rows = jnp.broadcast_to(d_skip[0][:, None], (M_HEADS, M_HEADDIM))
    lnp = rows8(ln1_g[0], ln1_b[0], ln2_g[0], ln2_b[0])

    xp = x_prompt.reshape(batch * seq, D_MODEL)
    pr_p, z_p, xbc_p, dt_p, w_main = _proj_w(xp, w_t, w_dt, DENSE_ROWS)
    o_r_p, wkv_p = _rwkv_prompt(pr_p.reshape(batch, seq, RWKV_PROJ), prep_w)
    o_r_p = o_r_p.reshape(batch * seq, R_WIDTH)
    o_m_p, ssm_p = _mamba_prompt(z_p, xbc_p, dt_p, batch, seq, (cw, dta, mv))
    y_p, wout_bf, w1_bf, w2_bf = _ffn_w(xp, o_r_p, o_m_p, w_out[0], w_ff1[0], w_ff2[0], lnp,
                                        tm=DENSE_ROWS)
    shift_p = pr_p.reshape(batch, seq, RWKV_PROJ)[:, -1]
    conv_p = xbc_p.reshape(batch, seq, CONV_DIM)[:, seq - (CONV_K - 1):]

    xs2d = x_sample.reshape(nb, D_MODEL)
    pr_s, z_s, xbc_s, dt_s = _proj(xs2d, w_main, w_dt, nb)
    cols = jnp.broadcast_to(jnp.stack([r_k[0].reshape(-1), lnx_w[0], lnx_b[0]])[:, :, None],
                            (3, R_WIDTH, nb))
    o_r_t, wkv_t = _rwkv_sample(pr_s, state_shift[0], jnp.transpose(state_wkv[0], (1, 2, 3, 0)),
                                prep_w, cols)
    o_r_s = o_r_t.T
    wkv_s = jnp.transpose(wkv_t, (3, 0, 1, 2))
    xs_s, xdt_s, dec_s, bm_s, cm_s, conv_t = _mamba_sample_prep(
        xbc_s, jnp.transpose(state_conv[0], (1, 0, 2)), dt_s, cw, dta)
    conv_s = jnp.transpose(conv_t, (1, 0, 2))
    y_m_s, ssm_s = _mamba_sample_step(xs_s, xdt_s, dec_s, bm_s, cm_s, state_ssm[0], dskip_rows)
    y_s = _ffn(xs2d, o_r_s, y_m_s.reshape(nb, M_WIDTH), z_s, mv, wout_bf, lnp, w1_bf, w2_bf, tm=nb)

    return (y_p.reshape(batch, seq, D_MODEL), y_s.reshape(nb, 1, D_MODEL),
            shift_p[None], wkv_p[None], conv_p[None],
            ssm_p.reshape(1, batch, M_HEADS, M_HEADDIM, D_STATE),
            pr_s[None], wkv_s.reshape(1, nb, R_HEADS, R_HEAD, R_HEAD), conv_s[None],
            ssm_s.reshape(1, nb, M_HEADS, M_HEADDIM, D_STATE))
```

```python
import math

import numpy as np
import jax
import jax.numpy as jnp
from jax import lax
from jax.experimental import pallas as pl
from jax.experimental.pallas import tpu as pltpu

F32 = jnp.float32
BF16 = jnp.bfloat16

D_MODEL = 1024
R_WIDTH = 512
R_HEAD = 64
R_HEADS = 8
D_DECAY_LORA = 64
D_AAA_LORA = 64
D_GATE_LORA = 128
M_WIDTH = 512
M_HEADDIM = 64
M_HEADS = 8
M_GROUPS = 2
D_STATE = 128
CONV_K = 4
CONV_DIM = M_WIDTH + 2 * M_GROUPS * D_STATE
D_FF = 4 * D_MODEL
RWKV_PROJ = 3 * R_WIDTH + D_DECAY_LORA + D_AAA_LORA + D_GATE_LORA
DT_PAD = 128
PROJ_MAIN = RWKV_PROJ + M_WIDTH + CONV_DIM
DEPTH = 1
ALPHA = (2 * DEPTH) ** 0.25
LN_EPS = 1e-5
LNX_EPS = 64e-5
RMS_EPS = 1e-5

RWKV_CHUNK = 64
RWKV_TILE = 256
RWKV_ROWS = 2
SSD_CHUNK = 128
MAMBA_TILE = 256
SUBLANES = 8
DENSE_ROWS = 512
VMEM_LIMIT = 48 * 1024 * 1024
PROJ_W_VMEM_LIMIT = 56 * 1024 * 1024

_NT = (((1,), (1,)), ((), ()))


def _mm(a, b):
    return jnp.dot(a.astype(BF16), b.astype(BF16), preferred_element_type=F32)


def _mm_nt(a, b):
    return lax.dot_general(a.astype(BF16), b.astype(BF16), _NT, preferred_element_type=F32)


def _split(x, passes):
    parts = []
    for _ in range(passes):
        p = x.astype(BF16)
        parts.append(p)
        x = x - p.astype(F32)
    return parts


def _mm_sel_r(x, sel, passes):
    return sum(jnp.dot(p, sel, preferred_element_type=F32) for p in _split(x, passes))


def _mm_sel_l(sel, x, passes):
    return sum(jnp.dot(sel, p, preferred_element_type=F32) for p in _split(x, passes))


def _softplus(x):
    return jnp.maximum(x, 0.0) + jnp.log(1.0 + jnp.exp(-jnp.abs(x)))


def _sigmoid(x):
    return 0.5 + 0.5 * jnp.tanh(0.5 * x)


def _silu(x):
    return x * _sigmoid(x)


def _layer_norm(x, g, b, eps):
    mu = jnp.mean(x, axis=-1, keepdims=True)
    d = x - mu
    var = jnp.mean(d * d, axis=-1, keepdims=True)
    return d * lax.rsqrt(var + eps) * g + b


def _iota2(shape, axis):
    return lax.broadcasted_iota(jnp.int32, shape, axis)


def _dt_columns(xb, wdt_t):
    pad = jnp.zeros((DT_PAD - M_HEADS, D_MODEL), F32)
    return _mm_nt(xb, jnp.concatenate([wdt_t, pad], axis=0))


W_SLAB = 256


def _proj_kernel(xp_ref, xs_ref, wt_ref, wdt_ref,
                 prp_ref, zp_ref, xbcp_ref, dtp_ref, prs_ref, zs_ref, xbcs_ref, dts_ref, wbf_scr):
    i = pl.program_id(0)
    last = pl.num_programs(0) - 1

    @pl.when(i == 0)
    def _():
        for c in range(0, PROJ_MAIN, W_SLAB):
            wbf_scr[:, c:c + W_SLAB] = wt_ref[c:c + W_SLAB, :].T.astype(BF16)

    def project(x_ref, pr_ref, z_ref, xbc_ref, dt_ref):
        xb = x_ref[...].astype(BF16)
        res = jnp.dot(xb, wbf_scr[...], preferred_element_type=F32)
        c0 = RWKV_PROJ
        c1 = c0 + M_WIDTH
        pr_ref[...] = res[:, :c0]
        z_ref[...] = res[:, c0:c1]
        xbc_ref[...] = res[:, c1:]
        dt_ref[...] = _dt_columns(xb, wdt_ref[...])

    @pl.when(i < last)
    def _():
        project(xp_ref, prp_ref, zp_ref, xbcp_ref, dtp_ref)

    @pl.when(i == last)
    def _():
        project(xs_ref, prs_ref, zs_ref, xbcs_ref, dts_ref)


def _proj(xp2d, xs2d, w_t, wdt_t, tm):
    n = xp2d.shape[0]
    ns = xs2d.shape[0]
    ntile = n // tm
    widths = (RWKV_PROJ, M_WIDTH, CONV_DIM, DT_PAD)
    once = lambda shape: pl.BlockSpec(shape, lambda i: (0, 0), pipeline_mode=pl.Buffered(1))
    tile = lambda w: pl.BlockSpec((tm, w), lambda i: (jnp.minimum(i, ntile - 1), 0))
    return pl.pallas_call(
        _proj_kernel,
        grid=(ntile + 1,),
        in_specs=[tile(D_MODEL), once(xs2d.shape), once(w_t.shape), once(wdt_t.shape)],
        out_specs=[tile(w) for w in widths] + [once((ns, w)) for w in widths],
        out_shape=[jax.ShapeDtypeStruct((n, w), F32) for w in widths]
                  + [jax.ShapeDtypeStruct((ns, w), F32) for w in widths],
        scratch_shapes=[pltpu.VMEM((D_MODEL, PROJ_MAIN), BF16)],
        compiler_params=pltpu.CompilerParams(dimension_semantics=("arbitrary",),
                                             vmem_limit_bytes=PROJ_W_VMEM_LIMIT),
        name="proj",
    )(xp2d, xs2d, w_t, wdt_t)


def _rwkv_prep(pr, prev, mu, rv, wup, aup, gup, bd):
    w0, a0, k_k, k_a = (rv[i:i + 1, :] for i in range(4))
    u = pr + (prev - pr) * mu
    r = u[:, 0:R_WIDTH]
    k = u[:, R_WIDTH:2 * R_WIDTH]
    v = u[:, 2 * R_WIDTH:3 * R_WIDTH]
    lora = u[:, 3 * R_WIDTH:3 * R_WIDTH + D_DECAY_LORA + D_AAA_LORA]
    ug = u[:, 3 * R_WIDTH + D_DECAY_LORA + D_AAA_LORA:]
    w_log = -_softplus(-(w0 + _mm(jnp.tanh(lora), wup))) - 0.5
    logw = -jnp.exp(w_log)
    a = _sigmoid(a0 + _mm(lora, aup))
    g = _mm(_sigmoid(ug), gup)
    kk = k * k_k
    ss = _mm_sel_r(kk * kk, bd, 1)
    kkn = kk * lax.rsqrt(jnp.maximum(ss, 1e-24))
    k2 = k * (1.0 + (a - 1.0) * k_a)
    return r, logw, k2, v, -kkn, kkn * a, g


def _rwkv_post(y, r, k2, v, g, rv, bd):
    r_k, lnx_w, lnx_b = (rv[i:i + 1, :] for i in range(4, 7))
    inv = 1.0 / R_HEAD
    mu = _mm_sel_r(y, bd, 1) * inv
    d = y - mu
    var = _mm_sel_r(d * d, bd, 1) * inv
    yn = d * lax.rsqrt(var + LNX_EPS) * lnx_w + lnx_b
    bonus = _mm_sel_r(r * k2 * r_k, bd, 1)
    return (yn + bonus * v) * g


def _rwkv_tile_kernel(pr_ref, prevblk_ref, mu_ref, rv_ref, wup_ref, aup_ref, gup_ref, bd_ref,
                      o_ref, sfin_ref, g_scr):
    c = RWKV_CHUNK
    tile = RWKV_TILE
    nseq = RWKV_ROWS
    nck = tile // c
    hw = 2 * R_HEAD
    npair = R_HEADS // 2
    ti = pl.program_id(1)

    @pl.when(ti == 0)
    def _():
        g_scr[...] = jnp.zeros_like(g_scr)

    pr = jnp.concatenate([pr_ref[s] for s in range(nseq)], axis=0)
    prev = pltpu.roll(pr, 1, axis=0)
    row_id = _iota2((nseq * tile, 1), 0)
    for s in range(nseq):
        prev_row = jnp.where(ti == 0, 0.0, prevblk_ref[s, SUBLANES - 1:SUBLANES, :])
        prev = jnp.where(row_id == s * tile, prev_row, prev)
    head_ones = bd_ref[...]
    r, logw, k2, v, av, bv, g = _rwkv_prep(pr, prev, mu_ref[...], rv_ref[...], wup_ref[...],
                                           aup_ref[...], gup_ref[...], head_ones)

    sh_c = int(math.log2(c))
    ti_i = _iota2((tile, tile), 0)
    ti_j = _iota2((tile, tile), 1)
    chunk_tril = jnp.where((ti_i >= ti_j) & ((ti_i >> sh_c) == (ti_j >> sh_c)), 1.0, 0.0).astype(BF16)
    cum = jnp.concatenate([_mm_sel_l(chunk_tril, logw[s * tile:(s + 1) * tile], 3) for s in range(nseq)],
                          axis=0)
    cl = jnp.concatenate(
        [jnp.broadcast_to(cum[(q + 1) * c - 1:(q + 1) * c, :], (c, R_WIDTH)) for q in range(nseq * nck)],
        axis=0)
    p_out = jnp.exp(-cum)
    p_end = jnp.exp(cl - cum)
    rt = r * jnp.exp(cum)
    at = av * jnp.exp(cum - logw)
    bt = bv * p_out
    kt = k2 * p_out
    bh = bv * p_end
    kh = k2 * p_end
    pc = jnp.exp(cl)

    ii = _iota2((c, hw), 0)
    jj = _iota2((c, hw), 1) & (c - 1)
    strict = ii > jj
    incl = ii >= jj
    lane = _iota2((1, hw), 1)
    keep = (jnp.where(lane < R_HEAD, 1.0, 0.0).astype(BF16), jnp.where(lane >= R_HEAD, 1.0, 0.0).astype(BF16))
    e_r = _iota2((hw, hw), 0)
    e_c = _iota2((hw, hw), 1)
    eye_hw = e_r == e_c
    blockdiag = (e_r >= R_HEAD) == (e_c >= R_HEAD)

    def bd(x):
        xb = x.astype(BF16)
        return jnp.concatenate([xb * keep[0], xb * keep[1]], axis=0)

    def pdot(a, b_bd):
        return jnp.dot(a.astype(BF16), b_bd, preferred_element_type=F32)

    qp = [(q, p) for q in range(nseq * nck) for p in range(npair)]
    blk = lambda x, q, p: x[q * c:(q + 1) * c, hw * p:hw * (p + 1)]
    lhs = [jnp.concatenate([blk(at, *k), blk(rt, *k)], axis=0).astype(BF16) for k in qp]
    sc = [lax.dot_general(x, jnp.concatenate([bd(blk(bt, *k)), bd(blk(kt, *k))], axis=0), _NT,
                          preferred_element_type=F32) for x, k in zip(lhs, qp)]
    a_ab = [jnp.where(strict, s[:c, :hw], 0.0) for s in sc]
    a_rb = [jnp.where(incl, s[c:, :hw], 0.0) for s in sc]
    a_akrk = [jnp.concatenate([jnp.where(strict, s[:c, hw:], 0.0), jnp.where(incl, s[c:, hw:], 0.0)], axis=0)
              for s in sc]
    v_bd = [bd(blk(v, *k)) for k in qp]
    akrk_v = [pdot(a, x) for a, x in zip(a_akrk, v_bd)]

    sh = 4
    eye = jnp.where(ii == jj, 1.0, 0.0)
    npow = [jnp.where((ii >> sh) == (jj >> sh), n, 0.0) for n in a_ab]
    t_inv = [eye + n for n in npow]
    npow = [pdot(n, bd(n)) for n in npow]
    for _ in range(sh - 2):
        both = [pdot(jnp.concatenate([t, n], axis=0), bd(n)) for t, n in zip(t_inv, npow)]
        t_inv = [t + x[:c] for t, x in zip(t_inv, both)]
        npow = [x[c:] for x in both]
    t_inv = [t + pdot(t, bd(n)) for t, n in zip(t_inv, npow)]
    while (1 << sh) < c:
        off = ((ii >> (sh + 1)) == (jj >> (sh + 1))) & (((ii >> sh) & 1) > ((jj >> sh) & 1))
        tn = [pdot(t, bd(jnp.where(off, n, 0.0))) for t, n in zip(t_inv, a_ab)]
        t_inv = [t + pdot(x, bd(t)) for t, x in zip(t_inv, tn)]
        sh += 1

    tx = [pdot(t, jnp.concatenate([bd(blk(at, *k)), bd(x[:c])], axis=1))
          for t, x, k in zip(t_inv, akrk_v, qp)]
    rbx = [pdot(a, jnp.concatenate([bd(x[:, :hw]), bd(x[:, hw:])], axis=1)) for a, x in zip(a_rb, tx)]
    rp_pair = {k: blk(rt, *k) + x[:, :hw] for k, x in zip(qp, rbx)}
    y0_pair = {k: x[:, hw:] + y[c:] for k, x, y in zip(qp, rbx, akrk_v)}

    zeros_c = jnp.zeros((c, hw), F32)
    wt = {k: jnp.concatenate([blk(bh, *k), blk(kh, *k)], axis=0).T for k in qp}
    trans = {k: _mm(wt[k], jnp.concatenate([x, jnp.concatenate([zeros_c, blk(v, *k)], axis=1)], axis=0))
             for k, x in zip(qp, tx)}
    m_ba = {k: jnp.where(blockdiag, trans[k][:, :hw], 0.0) for k in qp}
    h0c = {k: jnp.where(blockdiag, trans[k][:, hw:], 0.0) for k in qp}
    pc_col = {k: jnp.sum(jnp.where(eye_hw, blk(pc, *k)[:1], 0.0), axis=1, keepdims=True) for k in qp}

    state = [g_scr[i] for i in range(nseq * npair)]
    y_rows = [None] * (nseq * nck)
    for q in range(nck):
        for s in range(nseq):
            qq = s * nck + q
            st = state[s * npair:(s + 1) * npair]
            on_st = [_mm(jnp.concatenate([rp_pair[qq, p], m_ba[qq, p]], axis=0), st[p]) for p in range(npair)]
            y_rows[qq] = jnp.concatenate([on_st[p][:c] + y0_pair[qq, p] for p in range(npair)], axis=1)
            state[s * npair:(s + 1) * npair] = [
                pc_col[qq, p] * st[p] + on_st[p][c:] + h0c[qq, p] for p in range(npair)]
    for i in range(nseq * npair):
        g_scr[i] = state[i]

    y = jnp.concatenate(y_rows, axis=0)
    out = _rwkv_post(y, r, k2, v, g, rv_ref[...], head_ones)
    for s in range(nseq):
        o_ref[s] = out[s * tile:(s + 1) * tile]

    @pl.when(ti == pl.num_programs(1) - 1)
    def _():
        for s in range(nseq):
            for p in range(npair):
                gt = g_scr[s * npair + p].T
                sfin_ref[s, 2 * p] = gt[:R_HEAD, :R_HEAD]
                sfin_ref[s, 2 * p + 1] = pltpu.roll(gt, R_HEAD, axis=1)[R_HEAD:, :R_HEAD]


def _const_spec(shape):
    return pl.BlockSpec(shape, lambda *_: (0,) * len(shape))


def _rwkv_prompt(pr3d, wts):
    batch, seq, _ = pr3d.shape
    tile = RWKV_TILE
    nseq = RWKV_ROWS

    def prev_map(b, i):
        return (b, jnp.maximum(i * (tile // SUBLANES) - 1, 0), 0)

    w_specs = [_const_spec(w.shape) for w in wts]
    return pl.pallas_call(
        _rwkv_tile_kernel,
        grid=(batch // nseq, seq // tile),
        in_specs=[pl.BlockSpec((nseq, tile, RWKV_PROJ), lambda b, i: (b, i, 0)),
                  pl.BlockSpec((nseq, SUBLANES, RWKV_PROJ), prev_map)] + w_specs,
        out_specs=[pl.BlockSpec((nseq, tile, R_WIDTH), lambda b, i: (b, i, 0)),
                   pl.BlockSpec((nseq, R_HEADS, R_HEAD, R_HEAD), lambda b, i: (b, 0, 0, 0))],
        out_shape=[jax.ShapeDtypeStruct((batch, seq, R_WIDTH), F32),
                   jax.ShapeDtypeStruct((batch, R_HEADS, R_HEAD, R_HEAD), F32)],
        scratch_shapes=[pltpu.VMEM((nseq * (R_HEADS // 2), 2 * R_HEAD, 2 * R_HEAD), F32)],
        compiler_params=pltpu.CompilerParams(dimension_semantics=("arbitrary", "arbitrary"),
                                             vmem_limit_bytes=VMEM_LIMIT),
        name="rwkv_prompt",
    )(pr3d, pr3d, *wts)


def _rwkv_sample_kernel(pr_ref, prev_ref, mu_ref, rv_ref, wup_ref, aup_ref, gup_ref, bd_ref,
                        cols_ref, s_ref, o_ref, so_ref, vec_scr):
    h = pl.program_id(0)

    @pl.when(h == 0)
    def _():
        r, logw, k2, v, av, bv, g = _rwkv_prep(pr_ref[...], prev_ref[...], mu_ref[...], rv_ref[...],
                                               wup_ref[...], aup_ref[...], gup_ref[...], bd_ref[...])
        for i, x in enumerate((r, jnp.exp(logw), k2, v, av, bv, g)):
            vec_scr[i] = x.T

    rows = pl.ds(pl.multiple_of(h * R_HEAD, R_HEAD), R_HEAD)
    r, w, k, v, a, b, g = (vec_scr[i, rows, :] for i in range(7))
    ys = []
    for vi in range(R_HEAD):
        s = s_ref[0, vi]
        sa = jnp.sum(s * a, axis=0, keepdims=True)
        s2 = s * w + sa * b + v[vi:vi + 1, :] * k
        so_ref[0, vi] = s2
        ys.append(jnp.sum(s2 * r, axis=0, keepdims=True))
    y = jnp.concatenate(ys, axis=0)
    mu = jnp.mean(y, axis=0, keepdims=True)
    d = y - mu
    var = jnp.mean(d * d, axis=0, keepdims=True)
    yn = d * lax.rsqrt(var + LNX_EPS) * cols_ref[1] + cols_ref[2]
    bonus = jnp.sum(r * k * cols_ref[0], axis=0, keepdims=True)
    o_ref[...] = (yn + bonus * v) * g


def _rwkv_sample(pr, prev, state_t, prep_w, cols):
    nb = pr.shape[0]
    args = (pr, prev) + tuple(prep_w)
    head_rows = pl.BlockSpec((R_HEAD, nb), lambda h: (h, 0))
    st_spec = pl.BlockSpec((1, R_HEAD, R_HEAD, nb), lambda h: (h, 0, 0, 0))
    return pl.pallas_call(
        _rwkv_sample_kernel,
        grid=(R_HEADS,),
        in_specs=[_const_spec(a.shape) for a in args]
                 + [pl.BlockSpec((3, R_HEAD, nb), lambda h: (0, h, 0)), st_spec],
        out_specs=[head_rows, st_spec],
        out_shape=[jax.ShapeDtypeStruct((R_WIDTH, nb), F32),
                   jax.ShapeDtypeStruct(state_t.shape, F32)],
        scratch_shapes=[pltpu.VMEM((7, R_WIDTH, nb), F32)],
        compiler_params=pltpu.CompilerParams(dimension_semantics=("arbitrary",),
                                             vmem_limit_bytes=VMEM_LIMIT),
        name="rwkv_sample",
    )(*args, cols, state_t)


STEP_BATCH = 16


def _per_head_rows(x, n):
    return jnp.concatenate([jnp.broadcast_to(x[h:h + 1, :], (n, x.shape[1])) for h in range(x.shape[0])],
                           axis=0)


def _rows_from_columns(col_b, diag, n):
    picked = jnp.where(diag, col_b, 0.0)
    return jnp.concatenate([jnp.sum(picked[h * n:(h + 1) * n], axis=0, keepdims=True)
                            for h in range(col_b.shape[0] // n)], axis=0)


def _shift_rows(x, prev8, s):
    rolled = pltpu.roll(x, s, axis=0)
    top = jnp.where(_iota2((SUBLANES, 1), 0) < s, pltpu.roll(prev8, s, axis=0), rolled[:SUBLANES])
    return jnp.concatenate([top, rolled[SUBLANES:]], axis=0)


def _head_expand(lanes_per_head=M_HEADDIM):
    shape = (DT_PAD, M_HEADS * lanes_per_head)
    head_of_lane = _iota2(shape, 1) >> int(math.log2(lanes_per_head))
    return jnp.where(_iota2(shape, 0) == head_of_lane, 1.0, 0.0).astype(BF16)


def _gated_rmsnorm(y, z, gnorm_w):
    yz = y * _silu(z)
    gw = M_WIDTH // M_GROUPS
    outs = []
    for gi in range(M_GROUPS):
        yg = yz[:, gi * gw:(gi + 1) * gw]
        ms = jnp.mean(yg * yg, axis=-1, keepdims=True)
        outs.append(yg * lax.rsqrt(ms + RMS_EPS))
    return jnp.concatenate(outs, axis=1) * gnorm_w


def _mamba_tile_kernel(z_ref, xbc_ref, xprev_ref, dt_ref, cw_ref, dta_ref, mv_ref,
                       o_ref, hfin_ref, h_scr):
    tile = MAMBA_TILE
    q = SSD_CHUNK
    ti = pl.program_id(1)

    @pl.when(ti == 0)
    def _():
        h_scr[...] = jnp.zeros_like(h_scr)

    xb = xbc_ref[...]
    prev8 = jnp.where(ti == 0, 0.0, xprev_ref[...])
    conv = cw_ref[CONV_K:CONV_K + 1, :] + xb * cw_ref[CONV_K - 1:CONV_K, :]
    for s in range(1, CONV_K):
        conv = conv + _shift_rows(xb, prev8, s) * cw_ref[CONV_K - 1 - s:CONV_K - s, :]
    xact = _silu(conv)
    xs = xact[:, :M_WIDTH]
    gs = M_GROUPS * D_STATE
    bm = xact[:, M_WIDTH:M_WIDTH + gs]
    cm = xact[:, M_WIDTH + gs:]

    dt = _softplus(dt_ref[...] + dta_ref[0:1, :])
    adt = dt * (-jnp.exp(dta_ref[1:2, :]))
    x_dt = xs * _mm_sel_r(dt, _head_expand(), 3)

    sh_q = int(math.log2(q))
    ti_i = _iota2((tile, tile), 0)
    ti_j = _iota2((tile, tile), 1)
    chunk_tril = jnp.where((ti_i >= ti_j) & ((ti_i >> sh_q) == (ti_j >> sh_q)), 1.0, 0.0).astype(BF16)
    cs = _mm_sel_l(chunk_tril, adt, 3)
    cs_t = cs.T
    incl = _iota2((q, q), 0) >= _iota2((q, q), 1)
    lane = _iota2((1, 2 * M_HEADDIM), 1)
    first = lane < M_HEADDIM
    hpg = M_HEADS // M_GROUPS

    state = [h_scr[p] for p in range(M_HEADS // 2)]
    y_rows = []
    for j in range(tile // q):
        rs = slice(j * q, (j + 1) * q)
        y_pairs = []
        for gi in range(M_GROUPS):
            bm_g = bm[rs, gi * D_STATE:(gi + 1) * D_STATE]
            cm_g = cm[rs, gi * D_STATE:(gi + 1) * D_STATE]
            cb = _mm_nt(cm_g, bm_g)
            bm_t = bm_g.T
            for pp in range(hpg // 2):
                p = gi * (hpg // 2) + pp
                x_p = x_dt[rs, 2 * M_HEADDIM * p:2 * M_HEADDIM * (p + 1)]
                y_diag = jnp.zeros((q, 2 * M_HEADDIM), F32)
                contrib = jnp.zeros((D_STATE, 2 * M_HEADDIM), F32)
                scale_in = []
                scale_end = []
                for e in range(2):
                    head = 2 * p + e
                    m = first if e == 0 else jnp.logical_not(first)
                    a_col = cs[rs, head:head + 1]
                    a_row = cs_t[head:head + 1, rs]
                    a_end = cs_t[head:head + 1, (j + 1) * q - 1:(j + 1) * q]
                    lmat = jnp.exp(jnp.where(incl, a_col - a_row, -jnp.inf))
                    x_m = jnp.where(m, x_p, 0.0)
                    y_diag = y_diag + _mm(cb * lmat, x_m)
                    contrib = contrib + _mm(bm_t * jnp.exp(a_end - a_row), x_m)
                    scale_in.append(jnp.exp(a_col))
                    scale_end.append(jnp.exp(a_end))
                y_off = _mm(cm_g, state[p]) * jnp.where(first, scale_in[0], scale_in[1])
                state[p] = state[p] * jnp.where(first, scale_end[0], scale_end[1]) + contrib
                y_pairs.append(y_diag + y_off)
        y_rows.append(jnp.concatenate(y_pairs, axis=1))
    for p in range(M_HEADS // 2):
        h_scr[p] = state[p]

    y = jnp.concatenate(y_rows, axis=0) + mv_ref[0:1, :] * xs
    o_ref[...] = _gated_rmsnorm(y, z_ref[...], mv_ref[1:2, :])

    @pl.when(ti == pl.num_programs(1) - 1)
    def _():
        for p in range(M_HEADS // 2):
            hfin_ref[0, 2 * M_HEADDIM * p:2 * M_HEADDIM * (p + 1), :] = h_scr[p].T


def _mamba_prompt(z2d, xbc2d, dt2d, batch, seq, wts):
    tile = MAMBA_TILE
    ntile = seq // tile
    rows8 = seq // SUBLANES

    def prev_map(b, i):
        return (b * rows8 + jnp.maximum(i * (tile // SUBLANES) - 1, 0), 0)

    tok = lambda w: pl.BlockSpec((tile, w), lambda b, i: (b * ntile + i, 0))
    return pl.pallas_call(
        _mamba_tile_kernel,
        grid=(batch, ntile),
        in_specs=[tok(M_WIDTH), tok(CONV_DIM), pl.BlockSpec((SUBLANES, CONV_DIM), prev_map), tok(DT_PAD)]
                 + [_const_spec(w.shape) for w in wts],
        out_specs=[tok(M_WIDTH),
                   pl.BlockSpec((1, M_HEADS * M_HEADDIM, D_STATE), lambda b, i: (b, 0, 0))],
        out_shape=[jax.ShapeDtypeStruct((batch * seq, M_WIDTH), F32),
                   jax.ShapeDtypeStruct((batch, M_HEADS * M_HEADDIM, D_STATE), F32)],
        scratch_shapes=[pltpu.VMEM((M_HEADS // 2, D_STATE, 2 * M_HEADDIM), F32)],
        compiler_params=pltpu.CompilerParams(dimension_semantics=("arbitrary", "arbitrary"),
                                             vmem_limit_bytes=VMEM_LIMIT),
        name="mamba_prompt",
    )(z2d, xbc2d, xbc2d, dt2d, *wts)


def _mamba_prep_kernel(xbc_ref, cprev_ref, dt_ref, cw_ref, dta_ref,
                       xs_ref, xdt_ref, dec_ref, bm_ref, cm_ref, cnew_ref):
    xbc = xbc_ref[...]
    conv = cw_ref[CONV_K:CONV_K + 1, :] + xbc * cw_ref[CONV_K - 1:CONV_K, :]
    for j in range(CONV_K - 1):
        conv = conv + cprev_ref[j] * cw_ref[j:j + 1, :]
        cnew_ref[j] = cprev_ref[j + 1] if j + 1 < CONV_K - 1 else xbc
    xact = _silu(conv)
    xs = xact[:, :M_WIDTH]
    gs = M_GROUPS * D_STATE
    dt = _softplus(dt_ref[...] + dta_ref[0:1, :])
    adt = dt * (-jnp.exp(dta_ref[1:2, :]))
    expand = _head_expand()
    xs_ref[...] = xs
    xdt_ref[...] = xs * _mm_sel_r(dt, expand, 3)
    dec_ref[...] = jnp.exp(_mm_sel_r(adt, _head_expand(D_STATE), 3))
    hpg = M_HEADS // M_GROUPS
    per_head = lambda m: jnp.concatenate(
        [m[:, (e // hpg) * D_STATE:(e // hpg + 1) * D_STATE] for e in range(M_HEADS)], axis=1)
    bm_ref[...] = per_head(xact[:, M_WIDTH:M_WIDTH + gs])
    cm_ref[...] = per_head(xact[:, M_WIDTH + gs:])


def _mamba_sample_prep(xbc, cprev_t, dt, cw, dta):
    n = xbc.shape[0]
    args = (xbc, cprev_t, dt, cw, dta)
    widths = (M_WIDTH, M_WIDTH, M_HEADS * D_STATE, M_HEADS * D_STATE, M_HEADS * D_STATE)
    out_shapes = [(n, w) for w in widths] + [cprev_t.shape]
    return pl.pallas_call(
        _mamba_prep_kernel,
        grid=(1,),
        in_specs=[_const_spec(a.shape) for a in args],
        out_specs=[_const_spec(s) for s in out_shapes],
        out_shape=[jax.ShapeDtypeStruct(s, F32) for s in out_shapes],
        compiler_params=pltpu.CompilerParams(vmem_limit_bytes=VMEM_LIMIT),
        name="mamba_sample_prep",
    )(*args)


def _mamba_step_kernel(xs_ref, xdt_ref, dec_ref, bm_ref, cm_ref, h_ref, dskip_ref, y_ref, ho_ref):
    n = M_HEADS * M_HEADDIM
    ones_in = jnp.ones((M_HEADDIM, D_STATE), BF16)
    ones_out = jnp.ones((D_STATE, M_HEADDIM), BF16)
    diag = (_iota2((n, M_HEADDIM), 0) & (M_HEADDIM - 1)) == _iota2((n, M_HEADDIM), 1)

    def head_rows(ref, bi):
        return jnp.concatenate(
            [jnp.broadcast_to(ref[bi:bi + 1, e * D_STATE:(e + 1) * D_STATE], (M_HEADDIM, D_STATE))
             for e in range(M_HEADS)], axis=0)

    for bi in range(STEP_BATCH):
        rows = slice(bi * M_HEADS, (bi + 1) * M_HEADS)
        hrows = slice(bi * n, (bi + 1) * n)
        x_col = _mm(jnp.where(diag, _per_head_rows(xdt_ref[rows, :], M_HEADDIM), 0.0), ones_in)
        h2 = h_ref[hrows, :] * head_rows(dec_ref, bi) + x_col * head_rows(bm_ref, bi)
        ho_ref[hrows, :] = h2
        y = _rows_from_columns(_mm(h2 * head_rows(cm_ref, bi), ones_out), diag, M_HEADDIM)
        y_ref[rows, :] = y + dskip_ref[...] * xs_ref[rows, :]


def _mamba_sample_step(xs, xdt, dec, bm, cm, state, dskip_rows):
    nb = state.shape[0]
    rows = STEP_BATCH * M_HEADS
    vec_spec = pl.BlockSpec((rows, M_HEADDIM), lambda i: (i, 0))
    bc_spec = pl.BlockSpec((STEP_BATCH, M_HEADS * D_STATE), lambda i: (i, 0))
    st_spec = pl.BlockSpec((rows * M_HEADDIM, D_STATE), lambda i: (i, 0))
    flat = lambda x, w: x.reshape(nb * M_HEADS, w)
    return pl.pallas_call(
        _mamba_step_kernel,
        grid=(nb // STEP_BATCH,),
        in_specs=[vec_spec] * 2 + [bc_spec] * 3 + [st_spec, _const_spec((M_HEADS, M_HEADDIM))],
        out_specs=[vec_spec, st_spec],
        out_shape=[jax.ShapeDtypeStruct((nb * M_HEADS, M_HEADDIM), F32),
                   jax.ShapeDtypeStruct((nb * M_HEADS * M_HEADDIM, D_STATE), F32)],
        compiler_params=pltpu.CompilerParams(dimension_semantics=("arbitrary",),
                                             vmem_limit_bytes=VMEM_LIMIT),
        name="mamba_sample_step",
    )(flat(xs, M_HEADDIM), flat(xdt, M_HEADDIM), dec, bm, cm,
      state.reshape(nb * M_HEADS * M_HEADDIM, D_STATE), dskip_rows)


FF_SLAB = 1024


def _ffn_math(x, o_r, o_m, wout_ref, lnp_ref, w1_ref, w2_ref):
    mix = _mm(o_r, wout_ref[:R_WIDTH, :]) + _mm(o_m, wout_ref[R_WIDTH:, :])
    h = _layer_norm(ALPHA * x + mix, lnp_ref[0:1, :], lnp_ref[1:2, :], LN_EPS)
    hb = h.astype(BF16)
    ff = jnp.zeros_like(h)
    for j in range(D_FF // FF_SLAB):
        t = jnp.dot(hb, w1_ref[:, j * FF_SLAB:(j + 1) * FF_SLAB], preferred_element_type=F32)
        t = jnp.square(jnp.maximum(t, 0.0))
        ff = ff + jnp.dot(t.astype(BF16), w2_ref[j * FF_SLAB:(j + 1) * FF_SLAB, :],
                          preferred_element_type=F32)
    return _layer_norm(ALPHA * h + ff, lnp_ref[2:3, :], lnp_ref[3:4, :], LN_EPS)


FFN_PREP = 8


def _ffn_kernel(xp_ref, orp_ref, omp_ref, xs_ref, ors_ref, yms_ref, zs_ref, mv_ref,
                wout32_ref, w1_32_ref, w2_32_ref, lnp_ref, yp_ref, ys_ref,
                wout_scr, w1_scr, w2_scr):
    i = pl.program_id(0)
    last = pl.num_programs(0) - 1

    @pl.when(i < FFN_PREP)
    def _():
        r_in = D_MODEL // FFN_PREP
        r_ff = D_FF // FFN_PREP
        rows_in = pl.ds(pl.multiple_of(i * r_in, r_in), r_in)
        wout_scr[rows_in, :] = wout32_ref[...].astype(BF16)
        w1_scr[rows_in, :] = w1_32_ref[...].astype(BF16)
        w2_scr[pl.ds(pl.multiple_of(i * r_ff, r_ff), r_ff), :] = w2_32_ref[...].astype(BF16)

    @pl.when((i >= FFN_PREP) & (i < last))
    def _():
        yp_ref[...] = _ffn_math(xp_ref[...], orp_ref[...], omp_ref[...], wout_scr, lnp_ref, w1_scr, w2_scr)

    @pl.when(i == last)
    def _():
        o_m = _gated_rmsnorm(yms_ref[...], zs_ref[...], mv_ref[1:2, :])
        ys_ref[...] = _ffn_math(xs_ref[...], ors_ref[...], o_m, wout_scr, lnp_ref, w1_scr, w2_scr)


def _ffn(prompt, sample, mv, wout32, w1_32, w2_32, lnp, tm):
    n = prompt[0].shape[0]
    ns = sample[0].shape[0]
    prep = FFN_PREP
    ntile = n // tm
    tok = lambda w: pl.BlockSpec((tm, w), lambda i: (jnp.clip(i - prep, 0, ntile - 1), 0))
    slab = lambda a: pl.BlockSpec((a.shape[0] // prep, a.shape[1]), lambda i: (jnp.minimum(i, prep - 1), 0))
    whole = lambda shape: pl.BlockSpec(shape, lambda i: (0, 0), pipeline_mode=pl.Buffered(1))
    return pl.pallas_call(
        _ffn_kernel,
        grid=(prep + ntile + 1,),
        in_specs=[tok(D_MODEL), tok(R_WIDTH), tok(M_WIDTH)]
                 + [whole(a.shape) for a in sample] + [whole(mv.shape)]
                 + [slab(wout32), slab(w1_32), slab(w2_32), whole(lnp.shape)],
        out_specs=[tok(D_MODEL), whole((ns, D_MODEL))],
        out_shape=[jax.ShapeDtypeStruct((n, D_MODEL), F32), jax.ShapeDtypeStruct((ns, D_MODEL), F32)],
        scratch_shapes=[pltpu.VMEM(w.shape, BF16) for w in (wout32, w1_32, w2_32)],
        compiler_params=pltpu.CompilerParams(dimension_semantics=("arbitrary",),
                                             vmem_limit_bytes=PROJ_W_VMEM_LIMIT),
        name="ffn",
    )(*prompt, *sample, mv, wout32, w1_32, w2_32, lnp)


def _block_diag_ones():
    idx = np.arange(R_WIDTH) // R_HEAD
    return jnp.asarray(idx[:, None] == idx[None, :], dtype=BF16)


def kernel(x_prompt, x_sample, state_shift, state_wkv, state_conv, state_ssm, w_in, mu_shift, w0, w_up, a0, a_up, g_up, k_k, k_a, r_k, lnx_w, lnx_b, conv_w, conv_b, dt_bias, a_log, d_skip, gnorm_w, w_out, ln1_g, ln1_b, w_ff1, w_ff2, ln2_g, ln2_b):
    assert w_in.shape[0] == DEPTH and x_sample.shape[1] == 1
    batch, seq, _ = x_prompt.shape
    nb = x_sample.shape[0]
    assert seq % MAMBA_TILE == 0 and seq % RWKV_TILE == 0 and seq % DENSE_ROWS == 0
    assert nb % STEP_BATCH == 0 and batch % RWKV_ROWS == 0
    row = lambda a: a.reshape(1, -1)

    w_t = jnp.transpose(w_in[0])
    w_dt = w_t[PROJ_MAIN:]
    zl = jnp.zeros((D_DECAY_LORA, R_WIDTH), F32)
    wup_p = jnp.concatenate([w_up[0], zl], axis=0).astype(BF16)
    aup_p = jnp.concatenate([zl, a_up[0]], axis=0).astype(BF16)
    bd = _block_diag_ones()

    def rows8(*vecs):
        m = jnp.stack([v.reshape(-1) for v in vecs])
        return jnp.pad(m, ((0, SUBLANES - len(vecs)), (0, 0)))

    rv = rows8(w0[0], a0[0], k_k[0], k_a[0], r_k[0], lnx_w[0], lnx_b[0])
    prep_w = (row(mu_shift[0]), rv, wup_p, aup_p, g_up[0].astype(BF16), bd)
    cw = rows8(*conv_w[0], conv_b[0])
    dta = rows8(jnp.pad(dt_bias[0], (0, DT_PAD - M_HEADS)), jnp.pad(a_log[0], (0, DT_PAD - M_HEADS)))
    mv = rows8(jnp.repeat(d_skip[0], M_HEADDIM), gnorm_w[0])
    dskip_rows = jnp.broadcast_to(d_skip[0][:, None], (M_HEADS, M_HEADDIM))
    lnp = rows8(ln1_g[0], ln1_b[0], ln2_g[0], ln2_b[0])

    xp = x_prompt.reshape(batch * seq, D_MODEL)
    xs2d = x_sample.reshape(nb, D_MODEL)
    pr_p, z_p, xbc_p, dt_p, pr_s, z_s, xbc_s, dt_s = _proj(xp, xs2d, w_t, w_dt, DENSE_ROWS)

    o_r_p, wkv_p = _rwkv_prompt(pr_p.reshape(batch, seq, RWKV_PROJ), prep_w)
    o_r_p = o_r_p.reshape(batch * seq, R_WIDTH)
    o_m_p, ssm_p = _mamba_prompt(z_p, xbc_p, dt_p, batch, seq, (cw, dta, mv))
    shift_p = pr_p.reshape(batch, seq, RWKV_PROJ)[:, -1]
    conv_p = xbc_p.reshape(batch, seq, CONV_DIM)[:, seq - (CONV_K - 1):]

    cols = jnp.broadcast_to(jnp.stack([r_k[0].reshape(-1), lnx_w[0], lnx_b[0]])[:, :, None],
                            (3, R_WIDTH, nb))
    o_r_t, wkv_t = _rwkv_sample(pr_s, state_shift[0], jnp.transpose(state_wkv[0], (1, 2, 3, 0)),
                                prep_w, cols)
    o_r_s = o_r_t.T
    wkv_s = jnp.transpose(wkv_t, (3, 0, 1, 2))
    xs_s, xdt_s, dec_s, bm_s, cm_s, conv_t = _mamba_sample_prep(
        xbc_s, jnp.transpose(state_conv[0], (1, 0, 2)), dt_s, cw, dta)
    conv_s = jnp.transpose(conv_t, (1, 0, 2))
    y_m_s, ssm_s = _mamba_sample_step(xs_s, xdt_s, dec_s, bm_s, cm_s, state_ssm[0], dskip_rows)

    y_p, y_s = _ffn((xp, o_r_p, o_m_p), (xs2d, o_r_s, y_m_s.reshape(nb, M_WIDTH), z_s), mv,
                    w_out[0], w_ff1[0], w_ff2[0], lnp, tm=DENSE_ROWS)

    return (y_p.reshape(batch, seq, D_MODEL), y_s.reshape(nb, 1, D_MODEL),
            shift_p[None], wkv_p[None], conv_p[None],
            ssm_p.reshape(1, batch, M_HEADS, M_HEADDIM, D_STATE),
            pr_s[None], wkv_s.reshape(1, nb, R_HEADS, R_HEAD, R_HEAD), conv_s[None],
            ssm_s.reshape(1, nb, M_HEADS, M_HEADDIM, D_STATE))
```

```python
import math

import numpy as np
import jax
import jax.numpy as jnp
from jax import lax
from jax.experimental import pallas as pl
from jax.experimental.pallas import tpu as pltpu

F32 = jnp.float32
BF16 = jnp.bfloat16

D_MODEL = 1024
R_WIDTH = 512
R_HEAD = 64
R_HEADS = 8
D_DECAY_LORA = 64
D_AAA_LORA = 64
D_GATE_LORA = 128
M_WIDTH = 512
M_HEADDIM = 64
M_HEADS = 8
M_GROUPS = 2
D_STATE = 128
CONV_K = 4
CONV_DIM = M_WIDTH + 2 * M_GROUPS * D_STATE
D_FF = 4 * D_MODEL
RWKV_PROJ = 3 * R_WIDTH + D_DECAY_LORA + D_AAA_LORA + D_GATE_LORA
DT_PAD = 128
PROJ_MAIN = RWKV_PROJ + M_WIDTH + CONV_DIM
DEPTH = 1
ALPHA = (2 * DEPTH) ** 0.25
LN_EPS = 1e-5
LNX_EPS = 64e-5
RMS_EPS = 1e-5

RWKV_CHUNK = 64
RWKV_TILE = 256
RWKV_ROWS = 2
SSD_CHUNK = 128
MAMBA_TILE = 256
SUBLANES = 8
DENSE_ROWS = 512
VMEM_LIMIT = 48 * 1024 * 1024
PROJ_W_VMEM_LIMIT = 56 * 1024 * 1024

_NT = (((1,), (1,)), ((), ()))


def _mm(a, b):
    return jnp.dot(a.astype(BF16), b.astype(BF16), preferred_element_type=F32)


def _mm_nt(a, b):
    return lax.dot_general(a.astype(BF16), b.astype(BF16), _NT, preferred_element_type=F32)


def _split(x, passes):
    parts = []
    for _ in range(passes):
        p = x.astype(BF16)
        parts.append(p)
        x = x - p.astype(F32)
    return parts


def _mm_sel_r(x, sel, passes):
    return sum(jnp.dot(p, sel, preferred_element_type=F32) for p in _split(x, passes))


def _mm_sel_l(sel, x, passes):
    return sum(jnp.dot(sel, p, preferred_element_type=F32) for p in _split(x, passes))


def _softplus(x):
    return jnp.maximum(x, 0.0) + jnp.log(1.0 + jnp.exp(-jnp.abs(x)))


def _sigmoid(x):
    return 0.5 + 0.5 * jnp.tanh(0.5 * x)


def _silu(x):
    return x * _sigmoid(x)


def _layer_norm(x, g, b, eps):
    mu = jnp.mean(x, axis=-1, keepdims=True)
    d = x - mu
    var = jnp.mean(d * d, axis=-1, keepdims=True)
    return d * lax.rsqrt(var + eps) * g + b


def _iota2(shape, axis):
    return lax.broadcasted_iota(jnp.int32, shape, axis)


def _dt_columns(xb, wdt_t):
    pad = jnp.zeros((DT_PAD - M_HEADS, D_MODEL), F32)
    return _mm_nt(xb, jnp.concatenate([wdt_t, pad], axis=0))


W_SLAB = 256


def _proj_kernel(xp_ref, xs_ref, wt_ref, wdt_ref, wout32_ref, w1_32_ref, w2_32_ref,
                 prp_ref, zp_ref, xbcp_ref, dtp_ref, prs_ref, zs_ref, xbcs_ref, dts_ref,
                 wout_ref, w1_ref, w2_ref, wbf_scr):
    i = pl.program_id(0)
    last = pl.num_programs(0) - 1

    @pl.when(i == 0)
    def _():
        for c in range(0, PROJ_MAIN, W_SLAB):
            wbf_scr[:, c:c + W_SLAB] = wt_ref[c:c + W_SLAB, :].T.astype(BF16)

    def project(x_ref, pr_ref, z_ref, xbc_ref, dt_ref):
        xb = x_ref[...].astype(BF16)
        res = jnp.dot(xb, wbf_scr[...], preferred_element_type=F32)
        c0 = RWKV_PROJ
        c1 = c0 + M_WIDTH
        pr_ref[...] = res[:, :c0]
        z_ref[...] = res[:, c0:c1]
        xbc_ref[...] = res[:, c1:]
        dt_ref[...] = _dt_columns(xb, wdt_ref[...])

    @pl.when(i < last)
    def _():
        project(xp_ref, prp_ref, zp_ref, xbcp_ref, dtp_ref)
        wout_ref[...] = wout32_ref[...].astype(BF16)
        w1_ref[...] = w1_32_ref[...].astype(BF16)
        w2_ref[...] = w2_32_ref[...].astype(BF16)

    @pl.when(i == last)
    def _():
        project(xs_ref, prs_ref, zs_ref, xbcs_ref, dts_ref)


def _proj(xp2d, xs2d, w_t, wdt_t, mlp_w, tm):
    n = xp2d.shape[0]
    ns = xs2d.shape[0]
    ntile = n // tm
    widths = (RWKV_PROJ, M_WIDTH, CONV_DIM, DT_PAD)
    once = lambda shape: pl.BlockSpec(shape, lambda i: (0, 0), pipeline_mode=pl.Buffered(1))
    tile = lambda w: pl.BlockSpec((tm, w), lambda i: (jnp.minimum(i, ntile - 1), 0))
    slab = lambda a: pl.BlockSpec((a.shape[0] // ntile, a.shape[1]), lambda i: (jnp.minimum(i, ntile - 1), 0))
    return pl.pallas_call(
        _proj_kernel,
        grid=(ntile + 1,),
        in_specs=[tile(D_MODEL), once(xs2d.shape), once(w_t.shape), once(wdt_t.shape)]
                 + [slab(w) for w in mlp_w],
        out_specs=[tile(w) for w in widths] + [once((ns, w)) for w in widths] + [slab(w) for w in mlp_w],
        out_shape=[jax.ShapeDtypeStruct((n, w), F32) for w in widths]
                  + [jax.ShapeDtypeStruct((ns, w), F32) for w in widths]
                  + [jax.ShapeDtypeStruct(w.shape, BF16) for w in mlp_w],
        scratch_shapes=[pltpu.VMEM((D_MODEL, PROJ_MAIN), BF16)],
        compiler_params=pltpu.CompilerParams(dimension_semantics=("arbitrary",),
                                             vmem_limit_bytes=PROJ_W_VMEM_LIMIT),
        name="proj",
    )(xp2d, xs2d, w_t, wdt_t, *mlp_w)


def _rwkv_prep(pr, prev, mu, rv, wup, aup, gup, bd):
    w0, a0, k_k, k_a = (rv[i:i + 1, :] for i in range(4))
    u = pr + (prev - pr) * mu
    r = u[:, 0:R_WIDTH]
    k = u[:, R_WIDTH:2 * R_WIDTH]
    v = u[:, 2 * R_WIDTH:3 * R_WIDTH]
    lora = u[:, 3 * R_WIDTH:3 * R_WIDTH + D_DECAY_LORA + D_AAA_LORA]
    ug = u[:, 3 * R_WIDTH + D_DECAY_LORA + D_AAA_LORA:]
    w_log = -_softplus(-(w0 + _mm(jnp.tanh(lora), wup))) - 0.5
    logw = -jnp.exp(w_log)
    a = _sigmoid(a0 + _mm(lora, aup))
    g = _mm(_sigmoid(ug), gup)
    kk = k * k_k
    ss = _mm_sel_r(kk * kk, bd, 1)
    kkn = kk * lax.rsqrt(jnp.maximum(ss, 1e-24))
    k2 = k * (1.0 + (a - 1.0) * k_a)
    return r, logw, k2, v, -kkn, kkn * a, g


def _rwkv_post(y, r, k2, v, g, rv, bd):
    r_k, lnx_w, lnx_b = (rv[i:i + 1, :] for i in range(4, 7))
    inv = 1.0 / R_HEAD
    mu = _mm_sel_r(y, bd, 1) * inv
    d = y - mu
    var = _mm_sel_r(d * d, bd, 1) * inv
    yn = d * lax.rsqrt(var + LNX_EPS) * lnx_w + lnx_b
    bonus = _mm_sel_r(r * k2 * r_k, bd, 1)
    return (yn + bonus * v) * g


def _rwkv_tile_kernel(pr_ref, prevblk_ref, mu_ref, rv_ref, wup_ref, aup_ref, gup_ref, bd_ref,
                      o_ref, sfin_ref, g_scr):
    c = RWKV_CHUNK
    tile = RWKV_TILE
    nseq = RWKV_ROWS
    nck = tile // c
    hw = 2 * R_HEAD
    npair = R_HEADS // 2
    ti = pl.program_id(1)

    @pl.when(ti == 0)
    def _():
        g_scr[...] = jnp.zeros_like(g_scr)

    pr = jnp.concatenate([pr_ref[s] for s in range(nseq)], axis=0)
    prev = pltpu.roll(pr, 1, axis=0)
    row_id = _iota2((nseq * tile, 1), 0)
    for s in range(nseq):
        prev_row = jnp.where(ti == 0, 0.0, prevblk_ref[s, SUBLANES - 1:SUBLANES, :])
        prev = jnp.where(row_id == s * tile, prev_row, prev)
    head_ones = bd_ref[...]
    r, logw, k2, v, av, bv, g = _rwkv_prep(pr, prev, mu_ref[...], rv_ref[...], wup_ref[...],
                                           aup_ref[...], gup_ref[...], head_ones)

    sh_c = int(math.log2(c))
    ti_i = _iota2((tile, tile), 0)
    ti_j = _iota2((tile, tile), 1)
    chunk_tril = jnp.where((ti_i >= ti_j) & ((ti_i >> sh_c) == (ti_j >> sh_c)), 1.0, 0.0).astype(BF16)
    cum = jnp.concatenate([_mm_sel_l(chunk_tril, logw[s * tile:(s + 1) * tile], 3) for s in range(nseq)],
                          axis=0)
    cl = jnp.concatenate(
        [jnp.broadcast_to(cum[(q + 1) * c - 1:(q + 1) * c, :], (c, R_WIDTH)) for q in range(nseq * nck)],
        axis=0)
    p_out = jnp.exp(-cum)
    p_end = jnp.exp(cl - cum)
    rt = r * jnp.exp(cum)
    at = av * jnp.exp(cum - logw)
    bt = bv * p_out
    kt = k2 * p_out
    bh = bv * p_end
    kh = k2 * p_end
    pc = jnp.exp(cl)

    ii = _iota2((c, hw), 0)
    jj = _iota2((c, hw), 1) & (c - 1)
    strict = ii > jj
    incl = ii >= jj
    lane = _iota2((1, hw), 1)
    keep = (jnp.where(lane < R_HEAD, 1.0, 0.0).astype(BF16), jnp.where(lane >= R_HEAD, 1.0, 0.0).astype(BF16))
    e_r = _iota2((hw, hw), 0)
    e_c = _iota2((hw, hw), 1)
    eye_hw = e_r == e_c
    blockdiag = (e_r >= R_HEAD) == (e_c >= R_HEAD)

    def bd(x):
        xb = x.astype(BF16)
        return jnp.concatenate([xb * keep[0], xb * keep[1]], axis=0)

    def pdot(a, b_bd):
        return jnp.dot(a.astype(BF16), b_bd, preferred_element_type=F32)

    qp = [(q, p) for q in range(nseq * nck) for p in range(npair)]
    blk = lambda x, q, p: x[q * c:(q + 1) * c, hw * p:hw * (p + 1)]
    lhs = [jnp.concatenate([blk(at, *k), blk(rt, *k)], axis=0).astype(BF16) for k in qp]
    sc = [lax.dot_general(x, jnp.concatenate([bd(blk(bt, *k)), bd(blk(kt, *k))], axis=0), _NT,
                          preferred_element_type=F32) for x, k in zip(lhs, qp)]
    a_ab = [jnp.where(strict, s[:c, :hw], 0.0) for s in sc]
    a_rb = [jnp.where(incl, s[c:, :hw], 0.0) for s in sc]
    a_akrk = [jnp.concatenate([jnp.where(strict, s[:c, hw:], 0.0), jnp.where(incl, s[c:, hw:], 0.0)], axis=0)
              for s in sc]
    v_bd = [bd(blk(v, *k)) for k in qp]
    akrk_v = [pdot(a, x) for a, x in zip(a_akrk, v_bd)]

    sh = 4
    eye = jnp.where(ii == jj, 1.0, 0.0)
    npow = [jnp.where((ii >> sh) == (jj >> sh), n, 0.0) for n in a_ab]
    t_inv = [eye + n for n in npow]
    npow = [pdot(n, bd(n)) for n in npow]
    for _ in range(sh - 2):
        both = [pdot(jnp.concatenate([t, n], axis=0), bd(n)) for t, n in zip(t_inv, npow)]
        t_inv = [t + x[:c] for t, x in zip(t_inv, both)]
        npow = [x[c:] for x in both]
    t_inv = [t + pdot(t, bd(n)) for t, n in zip(t_inv, npow)]
    while (1 << sh) < c:
        off = ((ii >> (sh + 1)) == (jj >> (sh + 1))) & (((ii >> sh) & 1) > ((jj >> sh) & 1))
        tn = [pdot(t, bd(jnp.where(off, n, 0.0))) for t, n in zip(t_inv, a_ab)]
        t_inv = [t + pdot(x, bd(t)) for t, x in zip(t_inv, tn)]
        sh += 1

    tx = [pdot(t, jnp.concatenate([bd(blk(at, *k)), bd(x[:c])], axis=1))
          for t, x, k in zip(t_inv, akrk_v, qp)]
    rbx = [pdot(a, jnp.concatenate([bd(x[:, :hw]), bd(x[:, hw:])], axis=1)) for a, x in zip(a_rb, tx)]
    rp_pair = {k: blk(rt, *k) + x[:, :hw] for k, x in zip(qp, rbx)}
    y0_pair = {k: x[:, hw:] + y[c:] for k, x, y in zip(qp, rbx, akrk_v)}

    zeros_c = jnp.zeros((c, hw), F32)
    wt = {k: jnp.concatenate([blk(bh, *k), blk(kh, *k)], axis=0).T for k in qp}
    trans = {k: _mm(wt[k], jnp.concatenate([x, jnp.concatenate([zeros_c, blk(v, *k)], axis=1)], axis=0))
             for k, x in zip(qp, tx)}
    m_ba = {k: jnp.where(blockdiag, trans[k][:, :hw], 0.0) for k in qp}
    h0c = {k: jnp.where(blockdiag, trans[k][:, hw:], 0.0) for k in qp}
    pc_col = {k: jnp.sum(jnp.where(eye_hw, blk(pc, *k)[:1], 0.0), axis=1, keepdims=True) for k in qp}

    state = [g_scr[i] for i in range(nseq * npair)]
    y_rows = [None] * (nseq * nck)
    for q in range(nck):
        for s in range(nseq):
            qq = s * nck + q
            st = state[s * npair:(s + 1) * npair]
            on_st = [_mm(jnp.concatenate([rp_pair[qq, p], m_ba[qq, p]], axis=0), st[p]) for p in range(npair)]
            y_rows[qq] = jnp.concatenate([on_st[p][:c] + y0_pair[qq, p] for p in range(npair)], axis=1)
            state[s * npair:(s + 1) * npair] = [
                pc_col[qq, p] * st[p] + on_st[p][c:] + h0c[qq, p] for p in range(npair)]
    for i in range(nseq * npair):
        g_scr[i] = state[i]

    y = jnp.concatenate(y_rows, axis=0)
    out = _rwkv_post(y, r, k2, v, g, rv_ref[...], head_ones)
    for s in range(nseq):
        o_ref[s] = out[s * tile:(s + 1) * tile]

    @pl.when(ti == pl.num_programs(1) - 1)
    def _():
        for s in range(nseq):
            for p in range(npair):
                gt = g_scr[s * npair + p].T
                sfin_ref[s, 2 * p] = gt[:R_HEAD, :R_HEAD]
                sfin_ref[s, 2 * p + 1] = pltpu.roll(gt, R_HEAD, axis=1)[R_HEAD:, :R_HEAD]


def _const_spec(shape):
    return pl.BlockSpec(shape, lambda *_: (0,) * len(shape))


def _rwkv_prompt(pr3d, wts):
    batch, seq, _ = pr3d.shape
    tile = RWKV_TILE
    nseq = RWKV_ROWS

    def prev_map(b, i):
        return (b, jnp.maximum(i * (tile // SUBLANES) - 1, 0), 0)

    w_specs = [_const_spec(w.shape) for w in wts]
    return pl.pallas_call(
        _rwkv_tile_kernel,
        grid=(batch // nseq, seq // tile),
        in_specs=[pl.BlockSpec((nseq, tile, RWKV_PROJ), lambda b, i: (b, i, 0)),
                  pl.BlockSpec((nseq, SUBLANES, RWKV_PROJ), prev_map)] + w_specs,
        out_specs=[pl.BlockSpec((nseq, tile, R_WIDTH), lambda b, i: (b, i, 0)),
                   pl.BlockSpec((nseq, R_HEADS, R_HEAD, R_HEAD), lambda b, i: (b, 0, 0, 0))],
        out_shape=[jax.ShapeDtypeStruct((batch, seq, R_WIDTH), F32),
                   jax.ShapeDtypeStruct((batch, R_HEADS, R_HEAD, R_HEAD), F32)],
        scratch_shapes=[pltpu.VMEM((nseq * (R_HEADS // 2), 2 * R_HEAD, 2 * R_HEAD), F32)],
        compiler_params=pltpu.CompilerParams(dimension_semantics=("arbitrary", "arbitrary"),
                                             vmem_limit_bytes=VMEM_LIMIT),
        name="rwkv_prompt",
    )(pr3d, pr3d, *wts)


def _rwkv_sample_kernel(pr_ref, prev_ref, mu_ref, rv_ref, wup_ref, aup_ref, gup_ref, bd_ref,
                        cols_ref, s_ref, o_ref, so_ref, vec_scr):
    h = pl.program_id(0)

    @pl.when(h == 0)
    def _():
        r, logw, k2, v, av, bv, g = _rwkv_prep(pr_ref[...], prev_ref[...], mu_ref[...], rv_ref[...],
                                               wup_ref[...], aup_ref[...], gup_ref[...], bd_ref[...])
        for i, x in enumerate((r, jnp.exp(logw), k2, v, av, bv, g)):
            vec_scr[i] = x.T

    rows = pl.ds(pl.multiple_of(h * R_HEAD, R_HEAD), R_HEAD)
    r, w, k, v, a, b, g = (vec_scr[i, rows, :] for i in range(7))
    ys = []
    for vi in range(R_HEAD):
        s = s_ref[0, vi]
        sa = jnp.sum(s * a, axis=0, keepdims=True)
        s2 = s * w + sa * b + v[vi:vi + 1, :] * k
        so_ref[0, vi] = s2
        ys.append(jnp.sum(s2 * r, axis=0, keepdims=True))
    y = jnp.concatenate(ys, axis=0)
    mu = jnp.mean(y, axis=0, keepdims=True)
    d = y - mu
    var = jnp.mean(d * d, axis=0, keepdims=True)
    yn = d * lax.rsqrt(var + LNX_EPS) * cols_ref[1] + cols_ref[2]
    bonus = jnp.sum(r * k * cols_ref[0], axis=0, keepdims=True)
    o_ref[...] = (yn + bonus * v) * g


def _rwkv_sample(pr, prev, state_t, prep_w, cols):
    nb = pr.shape[0]
    args = (pr, prev) + tuple(prep_w)
    head_rows = pl.BlockSpec((R_HEAD, nb), lambda h: (h, 0))
    st_spec = pl.BlockSpec((1, R_HEAD, R_HEAD, nb), lambda h: (h, 0, 0, 0))
    return pl.pallas_call(
        _rwkv_sample_kernel,
        grid=(R_HEADS,),
        in_specs=[_const_spec(a.shape) for a in args]
                 + [pl.BlockSpec((3, R_HEAD, nb), lambda h: (0, h, 0)), st_spec],
        out_specs=[head_rows, st_spec],
        out_shape=[jax.ShapeDtypeStruct((R_WIDTH, nb), F32),
                   jax.ShapeDtypeStruct(state_t.shape, F32)],
        scratch_shapes=[pltpu.VMEM((7, R_WIDTH, nb), F32)],
        compiler_params=pltpu.CompilerParams(dimension_semantics=("arbitrary",),
                                             vmem_limit_bytes=VMEM_LIMIT),
        name="rwkv_sample",
    )(*args, cols, state_t)


STEP_BATCH = 16


def _per_head_rows(x, n):
    return jnp.concatenate([jnp.broadcast_to(x[h:h + 1, :], (n, x.shape[1])) for h in range(x.shape[0])],
                           axis=0)


def _rows_from_columns(col_b, diag, n):
    picked = jnp.where(diag, col_b, 0.0)
    return jnp.concatenate([jnp.sum(picked[h * n:(h + 1) * n], axis=0, keepdims=True)
                            for h in range(col_b.shape[0] // n)], axis=0)


def _shift_rows(x, prev8, s):
    rolled = pltpu.roll(x, s, axis=0)
    top = jnp.where(_iota2((SUBLANES, 1), 0) < s, pltpu.roll(prev8, s, axis=0), rolled[:SUBLANES])
    return jnp.concatenate([top, rolled[SUBLANES:]], axis=0)


def _head_expand(lanes_per_head=M_HEADDIM):
    shape = (DT_PAD, M_HEADS * lanes_per_head)
    head_of_lane = _iota2(shape, 1) >> int(math.log2(lanes_per_head))
    return jnp.where(_iota2(shape, 0) == head_of_lane, 1.0, 0.0).astype(BF16)


def _gated_rmsnorm(y, z, gnorm_w):
    yz = y * _silu(z)
    gw = M_WIDTH // M_GROUPS
    outs = []
    for gi in range(M_GROUPS):
        yg = yz[:, gi * gw:(gi + 1) * gw]
        ms = jnp.mean(yg * yg, axis=-1, keepdims=True)
        outs.append(yg * lax.rsqrt(ms + RMS_EPS))
    return jnp.concatenate(outs, axis=1) * gnorm_w


def _mamba_tile_kernel(z_ref, xbc_ref, xprev_ref, dt_ref, cw_ref, dta_ref, mv_ref,
                       o_ref, hfin_ref, h_scr):
    tile = MAMBA_TILE
    q = SSD_CHUNK
    ti = pl.program_id(1)

    @pl.when(ti == 0)
    def _():
        h_scr[...] = jnp.zeros_like(h_scr)

    xb = xbc_ref[...]
    prev8 = jnp.where(ti == 0, 0.0, xprev_ref[...])
    conv = cw_ref[CONV_K:CONV_K + 1, :] + xb * cw_ref[CONV_K - 1:CONV_K, :]
    for s in range(1, CONV_K):
        conv = conv + _shift_rows(xb, prev8, s) * cw_ref[CONV_K - 1 - s:CONV_K - s, :]
    xact = _silu(conv)
    xs = xact[:, :M_WIDTH]
    gs = M_GROUPS * D_STATE
    bm = xact[:, M_WIDTH:M_WIDTH + gs]
    cm = xact[:, M_WIDTH + gs:]

    dt = _softplus(dt_ref[...] + dta_ref[0:1, :])
    adt = dt * (-jnp.exp(dta_ref[1:2, :]))
    x_dt = xs * _mm_sel_r(dt, _head_expand(), 3)

    sh_q = int(math.log2(q))
    ti_i = _iota2((tile, tile), 0)
    ti_j = _iota2((tile, tile), 1)
    chunk_tril = jnp.where((ti_i >= ti_j) & ((ti_i >> sh_q) == (ti_j >> sh_q)), 1.0, 0.0).astype(BF16)
    cs = _mm_sel_l(chunk_tril, adt, 3)
    cs_t = cs.T
    incl = _iota2((q, q), 0) >= _iota2((q, q), 1)
    lane = _iota2((1, 2 * M_HEADDIM), 1)
    first = lane < M_HEADDIM
    hpg = M_HEADS // M_GROUPS

    state = [h_scr[p] for p in range(M_HEADS // 2)]
    y_rows = []
    for j in range(tile // q):
        rs = slice(j * q, (j + 1) * q)
        y_pairs = []
        for gi in range(M_GROUPS):
            bm_g = bm[rs, gi * D_STATE:(gi + 1) * D_STATE]
            cm_g = cm[rs, gi * D_STATE:(gi + 1) * D_STATE]
            cb = _mm_nt(cm_g, bm_g)
            bm_t = bm_g.T
            for pp in range(hpg // 2):
                p = gi * (hpg // 2) + pp
                x_p = x_dt[rs, 2 * M_HEADDIM * p:2 * M_HEADDIM * (p + 1)]
                y_diag = jnp.zeros((q, 2 * M_HEADDIM), F32)
                contrib = jnp.zeros((D_STATE, 2 * M_HEADDIM), F32)
                scale_in = []
                scale_end = []
                for e in range(2):
                    head = 2 * p + e
                    m = first if e == 0 else jnp.logical_not(first)
                    a_col = cs[rs, head:head + 1]
                    a_row = cs_t[head:head + 1, rs]
                    a_end = cs_t[head:head + 1, (j + 1) * q - 1:(j + 1) * q]
                    lmat = jnp.exp(jnp.where(incl, a_col - a_row, -jnp.inf))
                    x_m = jnp.where(m, x_p, 0.0)
                    y_diag = y_diag + _mm(cb * lmat, x_m)
                    contrib = contrib + _mm(bm_t * jnp.exp(a_end - a_row), x_m)
                    scale_in.append(jnp.exp(a_col))
                    scale_end.append(jnp.exp(a_end))
                y_off = _mm(cm_g, state[p]) * jnp.where(first, scale_in[0], scale_in[1])
                state[p] = state[p] * jnp.where(first, scale_end[0], scale_end[1]) + contrib
                y_pairs.append(y_diag + y_off)
        y_rows.append(jnp.concatenate(y_pairs, axis=1))
    for p in range(M_HEADS // 2):
        h_scr[p] = state[p]

    y = jnp.concatenate(y_rows, axis=0) + mv_ref[0:1, :] * xs
    o_ref[...] = _gated_rmsnorm(y, z_ref[...], mv_ref[1:2, :])

    @pl.when(ti == pl.num_programs(1) - 1)
    def _():
        for p in range(M_HEADS // 2):
            hfin_ref[0, 2 * M_HEADDIM * p:2 * M_HEADDIM * (p + 1), :] = h_scr[p].T


def _mamba_prompt(z2d, xbc2d, dt2d, batch, seq, wts):
    tile = MAMBA_TILE
    ntile = seq // tile
    rows8 = seq // SUBLANES

    def prev_map(b, i):
        return (b * rows8 + jnp.maximum(i * (tile // SUBLANES) - 1, 0), 0)

    tok = lambda w: pl.BlockSpec((tile, w), lambda b, i: (b * ntile + i, 0))
    return pl.pallas_call(
        _mamba_tile_kernel,
        grid=(batch, ntile),
        in_specs=[tok(M_WIDTH), tok(CONV_DIM), pl.BlockSpec((SUBLANES, CONV_DIM), prev_map), tok(DT_PAD)]
                 + [_const_spec(w.shape) for w in wts],
        out_specs=[tok(M_WIDTH),
                   pl.BlockSpec((1, M_HEADS * M_HEADDIM, D_STATE), lambda b, i: (b, 0, 0))],
        out_shape=[jax.ShapeDtypeStruct((batch * seq, M_WIDTH), F32),
                   jax.ShapeDtypeStruct((batch, M_HEADS * M_HEADDIM, D_STATE), F32)],
        scratch_shapes=[pltpu.VMEM((M_HEADS // 2, D_STATE, 2 * M_HEADDIM), F32)],
        compiler_params=pltpu.CompilerParams(dimension_semantics=("arbitrary", "arbitrary"),
                                             vmem_limit_bytes=VMEM_LIMIT),
        name="mamba_prompt",
    )(z2d, xbc2d, xbc2d, dt2d, *wts)


def _mamba_prep_kernel(xbc_ref, cprev_ref, dt_ref, cw_ref, dta_ref,
                       xs_ref, xdt_ref, dec_ref, bm_ref, cm_ref, cnew_ref):
    xbc = xbc_ref[...]
    conv = cw_ref[CONV_K:CONV_K + 1, :] + xbc * cw_ref[CONV_K - 1:CONV_K, :]
    for j in range(CONV_K - 1):
        conv = conv + cprev_ref[j] * cw_ref[j:j + 1, :]
        cnew_ref[j] = cprev_ref[j + 1] if j + 1 < CONV_K - 1 else xbc
    xact = _silu(conv)
    xs = xact[:, :M_WIDTH]
    gs = M_GROUPS * D_STATE
    dt = _softplus(dt_ref[...] + dta_ref[0:1, :])
    adt = dt * (-jnp.exp(dta_ref[1:2, :]))
    expand = _head_expand()
    xs_ref[...] = xs
    xdt_ref[...] = xs * _mm_sel_r(dt, expand, 3)
    dec_ref[...] = jnp.exp(_mm_sel_r(adt, _head_expand(D_STATE), 3))
    hpg = M_HEADS // M_GROUPS
    per_head = lambda m: jnp.concatenate(
        [m[:, (e // hpg) * D_STATE:(e // hpg + 1) * D_STATE] for e in range(M_HEADS)], axis=1)
    bm_ref[...] = per_head(xact[:, M_WIDTH:M_WIDTH + gs])
    cm_ref[...] = per_head(xact[:, M_WIDTH + gs:])


def _mamba_sample_prep(xbc, cprev_t, dt, cw, dta):
    n = xbc.shape[0]
    args = (xbc, cprev_t, dt, cw, dta)
    widths = (M_WIDTH, M_WIDTH, M_HEADS * D_STATE, M_HEADS * D_STATE, M_HEADS * D_STATE)
    out_shapes = [(n, w) for w in widths] + [cprev_t.shape]
    return pl.pallas_call(
        _mamba_prep_kernel,
        grid=(1,),
        in_specs=[_const_spec(a.shape) for a in args],
        out_specs=[_const_spec(s) for s in out_shapes],
        out_shape=[jax.ShapeDtypeStruct(s, F32) for s in out_shapes],
        compiler_params=pltpu.CompilerParams(vmem_limit_bytes=VMEM_LIMIT),
        name="mamba_sample_prep",
    )(*args)


def _mamba_step_kernel(xs_ref, xdt_ref, dec_ref, bm_ref, cm_ref, h_ref, dskip_ref, y_ref, ho_ref):
    n = M_HEADS * M_HEADDIM
    ones_in = jnp.ones((M_HEADDIM, D_STATE), BF16)
    ones_out = jnp.ones((D_STATE, M_HEADDIM), BF16)
    diag = (_iota2((n, M_HEADDIM), 0) & (M_HEADDIM - 1)) == _iota2((n, M_HEADDIM), 1)

    def head_rows(ref, bi):
        return jnp.concatenate(
            [jnp.broadcast_to(ref[bi:bi + 1, e * D_STATE:(e + 1) * D_STATE], (M_HEADDIM, D_STATE))
             for e in range(M_HEADS)], axis=0)

    for bi in range(STEP_BATCH):
        rows = slice(bi * M_HEADS, (bi + 1) * M_HEADS)
        hrows = slice(bi * n, (bi + 1) * n)
        x_col = _mm(jnp.where(diag, _per_head_rows(xdt_ref[rows, :], M_HEADDIM), 0.0), ones_in)
        h2 = h_ref[hrows, :] * head_rows(dec_ref, bi) + x_col * head_rows(bm_ref, bi)
        ho_ref[hrows, :] = h2
        y = _rows_from_columns(_mm(h2 * head_rows(cm_ref, bi), ones_out), diag, M_HEADDIM)
        y_ref[rows, :] = y + dskip_ref[...] * xs_ref[rows, :]


def _mamba_sample_step(xs, xdt, dec, bm, cm, state, dskip_rows):
    nb = state.shape[0]
    rows = STEP_BATCH * M_HEADS
    vec_spec = pl.BlockSpec((rows, M_HEADDIM), lambda i: (i, 0))
    bc_spec = pl.BlockSpec((STEP_BATCH, M_HEADS * D_STATE), lambda i: (i, 0))
    st_spec = pl.BlockSpec((rows * M_HEADDIM, D_STATE), lambda i: (i, 0))
    flat = lambda x, w: x.reshape(nb * M_HEADS, w)
    return pl.pallas_call(
        _mamba_step_kernel,
        grid=(nb // STEP_BATCH,),
        in_specs=[vec_spec] * 2 + [bc_spec] * 3 + [st_spec, _const_spec((M_HEADS, M_HEADDIM))],
        out_specs=[vec_spec, st_spec],
        out_shape=[jax.ShapeDtypeStruct((nb * M_HEADS, M_HEADDIM), F32),
                   jax.ShapeDtypeStruct((nb * M_HEADS * M_HEADDIM, D_STATE), F32)],
        compiler_params=pltpu.CompilerParams(dimension_semantics=("arbitrary",),
                                             vmem_limit_bytes=VMEM_LIMIT),
        name="mamba_sample_step",
    )(flat(xs, M_HEADDIM), flat(xdt, M_HEADDIM), dec, bm, cm,
      state.reshape(nb * M_HEADS * M_HEADDIM, D_STATE), dskip_rows)


FF_SLAB = 1024


def _ffn_math(x, o_r, o_m, wout_ref, lnp_ref, w1_ref, w2_ref):
    mix = _mm(o_r, wout_ref[:R_WIDTH, :]) + _mm(o_m, wout_ref[R_WIDTH:, :])
    h = _layer_norm(ALPHA * x + mix, lnp_ref[0:1, :], lnp_ref[1:2, :], LN_EPS)
    hb = h.astype(BF16)
    ff = jnp.zeros_like(h)
    for j in range(D_FF // FF_SLAB):
        t = jnp.dot(hb, w1_ref[:, j * FF_SLAB:(j + 1) * FF_SLAB], preferred_element_type=F32)
        t = jnp.square(jnp.maximum(t, 0.0))
        ff = ff + jnp.dot(t.astype(BF16), w2_ref[j * FF_SLAB:(j + 1) * FF_SLAB, :],
                          preferred_element_type=F32)
    return _layer_norm(ALPHA * h + ff, lnp_ref[2:3, :], lnp_ref[3:4, :], LN_EPS)


def _ffn_kernel(xp_ref, orp_ref, omp_ref, xs_ref, ors_ref, yms_ref, zs_ref, mv_ref,
                wout_ref, w1_ref, w2_ref, lnp_ref, yp_ref, ys_ref):
    i = pl.program_id(0)
    last = pl.num_programs(0) - 1

    @pl.when(i < last)
    def _():
        yp_ref[...] = _ffn_math(xp_ref[...], orp_ref[...], omp_ref[...], wout_ref, lnp_ref, w1_ref, w2_ref)

    @pl.when(i == last)
    def _():
        o_m = _gated_rmsnorm(yms_ref[...], zs_ref[...], mv_ref[1:2, :])
        ys_ref[...] = _ffn_math(xs_ref[...], ors_ref[...], o_m, wout_ref, lnp_ref, w1_ref, w2_ref)


def _ffn(prompt, sample, mv, wout, w1, w2, lnp, tm):
    n = prompt[0].shape[0]
    ns = sample[0].shape[0]
    ntile = n // tm
    tok = lambda w: pl.BlockSpec((tm, w), lambda i: (jnp.minimum(i, ntile - 1), 0))
    whole = lambda shape: pl.BlockSpec(shape, lambda i: (0, 0), pipeline_mode=pl.Buffered(1))
    consts = list(sample) + [mv, wout, w1, w2, lnp]
    return pl.pallas_call(
        _ffn_kernel,
        grid=(ntile + 1,),
        in_specs=[tok(D_MODEL), tok(R_WIDTH), tok(M_WIDTH)] + [whole(a.shape) for a in consts],
        out_specs=[tok(D_MODEL), whole((ns, D_MODEL))],
        out_shape=[jax.ShapeDtypeStruct((n, D_MODEL), F32), jax.ShapeDtypeStruct((ns, D_MODEL), F32)],
        compiler_params=pltpu.CompilerParams(dimension_semantics=("arbitrary",),
                                             vmem_limit_bytes=VMEM_LIMIT),
        name="ffn",
    )(*prompt, *consts)


def _block_diag_ones():
    idx = np.arange(R_WIDTH) // R_HEAD
    return jnp.asarray(idx[:, None] == idx[None, :], dtype=BF16)


def kernel(x_prompt, x_sample, state_shift, state_wkv, state_conv, state_ssm, w_in, mu_shift, w0, w_up, a0, a_up, g_up, k_k, k_a, r_k, lnx_w, lnx_b, conv_w, conv_b, dt_bias, a_log, d_skip, gnorm_w, w_out, ln1_g, ln1_b, w_ff1, w_ff2, ln2_g, ln2_b):
    assert w_in.shape[0] == DEPTH and x_sample.shape[1] == 1
    batch, seq, _ = x_prompt.shape
    nb = x_sample.shape[0]
    assert seq % MAMBA_TILE == 0 and seq % RWKV_TILE == 0 and seq % DENSE_ROWS == 0
    assert nb % STEP_BATCH == 0 and batch % RWKV_ROWS == 0
    row = lambda a: a.reshape(1, -1)

    w_t = jnp.transpose(w_in[0])
    w_dt = w_t[PROJ_MAIN:]
    zl = jnp.zeros((D_DECAY_LORA, R_WIDTH), F32)
    wup_p = jnp.concatenate([w_up[0], zl], axis=0).astype(BF16)
    aup_p = jnp.concatenate([zl, a_up[0]], axis=0).astype(BF16)
    bd = _block_diag_ones()

    def rows8(*vecs):
        m = jnp.stack([v.reshape(-1) for v in vecs])
        return jnp.pad(m, ((0, SUBLANES - len(vecs)), (0, 0)))

    rv = rows8(w0[0], a0[0], k_k[0], k_a[0], r_k[0], lnx_w[0], lnx_b[0])
    prep_w = (row(mu_shift[0]), rv, wup_p, aup_p, g_up[0].astype(BF16), bd)
    cw = rows8(*conv_w[0], conv_b[0])
    dta = rows8(jnp.pad(dt_bias[0], (0, DT_PAD - M_HEADS)), jnp.pad(a_log[0], (0, DT_PAD - M_HEADS)))
    mv = rows8(jnp.repeat(d_skip[0], M_HEADDIM), gnorm_w[0])
    dskip_rows = jnp.broadcast_to(d_skip[0][:, None], (M_HEADS, M_HEADDIM))
    lnp = rows8(ln1_g[0], ln1_b[0], ln2_g[0], ln2_b[0])

    xp = x_prompt.reshape(batch * seq, D_MODEL)
    xs2d = x_sample.reshape(nb, D_MODEL)
    (pr_p, z_p, xbc_p, dt_p, pr_s, z_s, xbc_s, dt_s, wout_bf, w1_bf, w2_bf) = _proj(
        xp, xs2d, w_t, w_dt, (w_out[0], w_ff1[0], w_ff2[0]), DENSE_ROWS)

    o_r_p, wkv_p = _rwkv_prompt(pr_p.reshape(batch, seq, RWKV_PROJ), prep_w)
    o_r_p = o_r_p.reshape(batch * seq, R_WIDTH)
    o_m_p, ssm_p = _mamba_prompt(z_p, xbc_p, dt_p, batch, seq, (cw, dta, mv))
    shift_p = pr_p.reshape(batch, seq, RWKV_PROJ)[:, -1]
    conv_p = xbc_p.reshape(batch, seq, CONV_DIM)[:, seq - (CONV_K - 1):]

    cols = jnp.broadcast_to(jnp.stack([r_k[0].reshape(-1), lnx_w[0], lnx_b[0]])[:, :, None],
                            (3, R_WIDTH, nb))
    o_r_t, wkv_t = _rwkv_sample(pr_s, state_shift[0], jnp.transpose(state_wkv[0], (1, 2, 3, 0)),
                                prep_w, cols)
    o_r_s = o_r_t.T
    wkv_s = jnp.transpose(wkv_t, (3, 0, 1, 2))
    xs_s, xdt_s, dec_s, bm_s, cm_s, conv_t = _mamba_sample_prep(
        xbc_s, jnp.transpose(state_conv[0], (1, 0, 2)), dt_s, cw, dta)
    conv_s = jnp.transpose(conv_t, (1, 0, 2))
    y_m_s, ssm_s = _mamba_sample_step(xs_s, xdt_s, dec_s, bm_s, cm_s, state_ssm[0], dskip_rows)

    y_p, y_s = _ffn((xp, o_r_p, o_m_p), (xs2d, o_r_s, y_m_s.reshape(nb, M_WIDTH), z_s), mv,
                    wout_bf, w1_bf, w2_bf, lnp, tm=DENSE_ROWS)

    return (y_p.reshape(batch, seq, D_MODEL), y_s.reshape(nb, 1, D_MODEL),
            shift_p[None], wkv_p[None], conv_p[None],
            ssm_p.reshape(1, batch, M_HEADS, M_HEADDIM, D_STATE),
            pr_s[None], wkv_s.reshape(1, nb, R_HEADS, R_HEAD, R_HEAD), conv_s[None],
            ssm_s.reshape(1, nb, M_HEADS, M_HEADDIM, D_STATE))
```

```python
import math

import numpy as np
import jax
import jax.numpy as jnp
from jax import lax
from jax.experimental import pallas as pl
from jax.experimental.pallas import tpu as pltpu

F32 = jnp.float32
BF16 = jnp.bfloat16

D_MODEL = 1024
R_WIDTH = 512
R_HEAD = 64
R_HEADS = 8
D_DECAY_LORA = 64
D_AAA_LORA = 64
D_GATE_LORA = 128
M_WIDTH = 512
M_HEADDIM = 64
M_HEADS = 8
M_GROUPS = 2
D_STATE = 128
CONV_K = 4
CONV_DIM = M_WIDTH + 2 * M_GROUPS * D_STATE
D_FF = 4 * D_MODEL
RWKV_PROJ = 3 * R_WIDTH + D_DECAY_LORA + D_AAA_LORA + D_GATE_LORA
DT_PAD = 128
PROJ_MAIN = RWKV_PROJ + M_WIDTH + CONV_DIM
DEPTH = 1
ALPHA = (2 * DEPTH) ** 0.25
LN_EPS = 1e-5
LNX_EPS = 64e-5
RMS_EPS = 1e-5

RWKV_CHUNK = 64
RWKV_TILE = 256
RWKV_ROWS = 2
SSD_CHUNK = 128
MAMBA_TILE = 256
SUBLANES = 8
DENSE_ROWS = 512
VMEM_LIMIT = 48 * 1024 * 1024
PROJ_W_VMEM_LIMIT = 56 * 1024 * 1024

_NT = (((1,), (1,)), ((), ()))


def _mm(a, b):
    return jnp.dot(a.astype(BF16), b.astype(BF16), preferred_element_type=F32)


def _mm_nt(a, b):
    return lax.dot_general(a.astype(BF16), b.astype(BF16), _NT, preferred_element_type=F32)


def _split(x, passes):
    parts = []
    for _ in range(passes):
        p = x.astype(BF16)
        parts.append(p)
        x = x - p.astype(F32)
    return parts


def _mm_sel_r(x, sel, passes):
    return sum(jnp.dot(p, sel, preferred_element_type=F32) for p in _split(x, passes))


def _mm_sel_l(sel, x, passes):
    return sum(jnp.dot(sel, p, preferred_element_type=F32) for p in _split(x, passes))


def _softplus(x):
    return jnp.maximum(x, 0.0) + jnp.log(1.0 + jnp.exp(-jnp.abs(x)))


def _sigmoid(x):
    return 0.5 + 0.5 * jnp.tanh(0.5 * x)


def _silu(x):
    return x * _sigmoid(x)


def _layer_norm(x, g, b, eps):
    mu = jnp.mean(x, axis=-1, keepdims=True)
    d = x - mu
    var = jnp.mean(d * d, axis=-1, keepdims=True)
    return d * lax.rsqrt(var + eps) * g + b


def _iota2(shape, axis):
    return lax.broadcasted_iota(jnp.int32, shape, axis)


def _dt_columns(xb, wdt_t):
    pad = jnp.zeros((DT_PAD - M_HEADS, D_MODEL), F32)
    return _mm_nt(xb, jnp.concatenate([wdt_t, pad], axis=0))


W_SLAB = 256


def _proj_kernel(xp_ref, xs_ref, wt_ref, wdt_ref, wout32_ref, w1_32_ref, w2_32_ref,
                 prp_ref, prs_ref, zs_ref, xbcs_ref, dts_ref,
                 wm_ref, wout_ref, w1_ref, w2_ref, wbf_scr):
    i = pl.program_id(0)
    last = pl.num_programs(0) - 1

    @pl.when(i == 0)
    def _():
        for c in range(0, PROJ_MAIN, W_SLAB):
            wbf_scr[:, c:c + W_SLAB] = wt_ref[c:c + W_SLAB, :].T.astype(BF16)
        wm_ref[...] = wbf_scr[:, RWKV_PROJ:]

    @pl.when(i < last)
    def _():
        prp_ref[...] = jnp.dot(xp_ref[...].astype(BF16), wbf_scr[:, :RWKV_PROJ],
                               preferred_element_type=F32)
        wout_ref[...] = wout32_ref[...].astype(BF16)
        w1_ref[...] = w1_32_ref[...].astype(BF16)
        w2_ref[...] = w2_32_ref[...].astype(BF16)

    @pl.when(i == last)
    def _():
        xb = xs_ref[...].astype(BF16)
        res = jnp.dot(xb, wbf_scr[...], preferred_element_type=F32)
        c0 = RWKV_PROJ
        c1 = c0 + M_WIDTH
        prs_ref[...] = res[:, :c0]
        zs_ref[...] = res[:, c0:c1]
        xbcs_ref[...] = res[:, c1:]
        dts_ref[...] = _dt_columns(xb, wdt_ref[...])


def _proj(xp2d, xs2d, w_t, wdt_t, mlp_w, tm):
    n = xp2d.shape[0]
    ns = xs2d.shape[0]
    ntile = n // tm
    widths = (RWKV_PROJ, M_WIDTH, CONV_DIM, DT_PAD)
    wm_shape = (D_MODEL, M_WIDTH + CONV_DIM)
    once = lambda shape: pl.BlockSpec(shape, lambda i: (0, 0), pipeline_mode=pl.Buffered(1))
    tile = lambda w: pl.BlockSpec((tm, w), lambda i: (jnp.minimum(i, ntile - 1), 0))
    slab = lambda a: pl.BlockSpec((a.shape[0] // ntile, a.shape[1]), lambda i: (jnp.minimum(i, ntile - 1), 0))
    return pl.pallas_call(
        _proj_kernel,
        grid=(ntile + 1,),
        in_specs=[tile(D_MODEL), once(xs2d.shape), once(w_t.shape), once(wdt_t.shape)]
                 + [slab(w) for w in mlp_w],
        out_specs=[tile(RWKV_PROJ)] + [once((ns, w)) for w in widths] + [once(wm_shape)]
                  + [slab(w) for w in mlp_w],
        out_shape=[jax.ShapeDtypeStruct((n, RWKV_PROJ), F32)]
                  + [jax.ShapeDtypeStruct((ns, w), F32) for w in widths]
                  + [jax.ShapeDtypeStruct(wm_shape, BF16)]
                  + [jax.ShapeDtypeStruct(w.shape, BF16) for w in mlp_w],
        scratch_shapes=[pltpu.VMEM((D_MODEL, PROJ_MAIN), BF16)],
        compiler_params=pltpu.CompilerParams(dimension_semantics=("arbitrary",),
                                             vmem_limit_bytes=PROJ_W_VMEM_LIMIT),
        name="proj",
    )(xp2d, xs2d, w_t, wdt_t, *mlp_w)


def _rwkv_prep(pr, prev, mu, rv, wup, aup, gup, bd):
    w0, a0, k_k, k_a = (rv[i:i + 1, :] for i in range(4))
    u = pr + (prev - pr) * mu
    r = u[:, 0:R_WIDTH]
    k = u[:, R_WIDTH:2 * R_WIDTH]
    v = u[:, 2 * R_WIDTH:3 * R_WIDTH]
    lora = u[:, 3 * R_WIDTH:3 * R_WIDTH + D_DECAY_LORA + D_AAA_LORA]
    ug = u[:, 3 * R_WIDTH + D_DECAY_LORA + D_AAA_LORA:]
    w_log = -_softplus(-(w0 + _mm(jnp.tanh(lora), wup))) - 0.5
    logw = -jnp.exp(w_log)
    a = _sigmoid(a0 + _mm(lora, aup))
    g = _mm(_sigmoid(ug), gup)
    kk = k * k_k
    ss = _mm_sel_r(kk * kk, bd, 1)
    kkn = kk * lax.rsqrt(jnp.maximum(ss, 1e-24))
    k2 = k * (1.0 + (a - 1.0) * k_a)
    return r, logw, k2, v, -kkn, kkn * a, g


def _rwkv_post(y, r, k2, v, g, rv, bd):
    r_k, lnx_w, lnx_b = (rv[i:i + 1, :] for i in range(4, 7))
    inv = 1.0 / R_HEAD
    mu = _mm_sel_r(y, bd, 1) * inv
    d = y - mu
    var = _mm_sel_r(d * d, bd, 1) * inv
    yn = d * lax.rsqrt(var + LNX_EPS) * lnx_w + lnx_b
    bonus = _mm_sel_r(r * k2 * r_k, bd, 1)
    return (yn + bonus * v) * g


def _rwkv_tile_kernel(pr_ref, prevblk_ref, mu_ref, rv_ref, wup_ref, aup_ref, gup_ref, bd_ref,
                      o_ref, sfin_ref, g_scr):
    c = RWKV_CHUNK
    tile = RWKV_TILE
    nseq = RWKV_ROWS
    nck = tile // c
    hw = 2 * R_HEAD
    npair = R_HEADS // 2
    ti = pl.program_id(1)

    @pl.when(ti == 0)
    def _():
        g_scr[...] = jnp.zeros_like(g_scr)

    pr = jnp.concatenate([pr_ref[s] for s in range(nseq)], axis=0)
    prev = pltpu.roll(pr, 1, axis=0)
    row_id = _iota2((nseq * tile, 1), 0)
    for s in range(nseq):
        prev_row = jnp.where(ti == 0, 0.0, prevblk_ref[s, SUBLANES - 1:SUBLANES, :])
        prev = jnp.where(row_id == s * tile, prev_row, prev)
    head_ones = bd_ref[...]
    r, logw, k2, v, av, bv, g = _rwkv_prep(pr, prev, mu_ref[...], rv_ref[...], wup_ref[...],
                                           aup_ref[...], gup_ref[...], head_ones)

    sh_c = int(math.log2(c))
    ti_i = _iota2((tile, tile), 0)
    ti_j = _iota2((tile, tile), 1)
    chunk_tril = jnp.where((ti_i >= ti_j) & ((ti_i >> sh_c) == (ti_j >> sh_c)), 1.0, 0.0).astype(BF16)
    cum = jnp.concatenate([_mm_sel_l(chunk_tril, logw[s * tile:(s + 1) * tile], 3) for s in range(nseq)],
                          axis=0)
    cl = jnp.concatenate(
        [jnp.broadcast_to(cum[(q + 1) * c - 1:(q + 1) * c, :], (c, R_WIDTH)) for q in range(nseq * nck)],
        axis=0)
    p_out = jnp.exp(-cum)
    p_end = jnp.exp(cl - cum)
    rt = r * jnp.exp(cum)
    at = av * jnp.exp(cum - logw)
    bt = bv * p_out
    kt = k2 * p_out
    bh = bv * p_end
    kh = k2 * p_end
    pc = jnp.exp(cl)

    ii = _iota2((c, hw), 0)
    jj = _iota2((c, hw), 1) & (c - 1)
    strict = ii > jj
    incl = ii >= jj
    lane = _iota2((1, hw), 1)
    keep = (jnp.where(lane < R_HEAD, 1.0, 0.0).astype(BF16), jnp.where(lane >= R_HEAD, 1.0, 0.0).astype(BF16))
    e_r = _iota2((hw, hw), 0)
    e_c = _iota2((hw, hw), 1)
    eye_hw = e_r == e_c
    blockdiag = (e_r >= R_HEAD) == (e_c >= R_HEAD)

    def bd(x):
        xb = x.astype(BF16)
        return jnp.concatenate([xb * keep[0], xb * keep[1]], axis=0)

    def pdot(a, b_bd):
        return jnp.dot(a.astype(BF16), b_bd, preferred_element_type=F32)

    qp = [(q, p) for q in range(nseq * nck) for p in range(npair)]
    blk = lambda x, q, p: x[q * c:(q + 1) * c, hw * p:hw * (p + 1)]
    lhs = [jnp.concatenate([blk(at, *k), blk(rt, *k)], axis=0).astype(BF16) for k in qp]
    sc = [lax.dot_general(x, jnp.concatenate([bd(blk(bt, *k)), bd(blk(kt, *k))], axis=0), _NT,
                          preferred_element_type=F32) for x, k in zip(lhs, qp)]
    a_ab = [jnp.where(strict, s[:c, :hw], 0.0) for s in sc]
    a_rb = [jnp.where(incl, s[c:, :hw], 0.0) for s in sc]
    a_akrk = [jnp.concatenate([jnp.where(strict, s[:c, hw:], 0.0), jnp.where(incl, s[c:, hw:], 0.0)], axis=0)
              for s in sc]
    v_bd = [bd(blk(v, *k)) for k in qp]
    akrk_v = [pdot(a, x) for a, x in zip(a_akrk, v_bd)]

    sh = 4
    eye = jnp.where(ii == jj, 1.0, 0.0)
    npow = [jnp.where((ii >> sh) == (jj >> sh), n, 0.0) for n in a_ab]
    t_inv = [eye + n for n in npow]
    npow = [pdot(n, bd(n)) for n in npow]
    for _ in range(sh - 2):
        both = [pdot(jnp.concatenate([t, n], axis=0), bd(n)) for t, n in zip(t_inv, npow)]
        t_inv = [t + x[:c] for t, x in zip(t_inv, both)]
        npow = [x[c:] for x in both]
    t_inv = [t + pdot(t, bd(n)) for t, n in zip(t_inv, npow)]
    while (1 << sh) < c:
        off = ((ii >> (sh + 1)) == (jj >> (sh + 1))) & (((ii >> sh) & 1) > ((jj >> sh) & 1))
        tn = [pdot(t, bd(jnp.where(off, n, 0.0))) for t, n in zip(t_inv, a_ab)]
        t_inv = [t + pdot(x, bd(t)) for t, x in zip(t_inv, tn)]
        sh += 1

    tx = [pdot(t, jnp.concatenate([bd(blk(at, *k)), bd(x[:c])], axis=1))
          for t, x, k in zip(t_inv, akrk_v, qp)]
    rbx = [pdot(a, jnp.concatenate([bd(x[:, :hw]), bd(x[:, hw:])], axis=1)) for a, x in zip(a_rb, tx)]
    rp_pair = {k: blk(rt, *k) + x[:, :hw] for k, x in zip(qp, rbx)}
    y0_pair = {k: x[:, hw:] + y[c:] for k, x, y in zip(qp, rbx, akrk_v)}

    zeros_c = jnp.zeros((c, hw), F32)
    wt = {k: jnp.concatenate([blk(bh, *k), blk(kh, *k)], axis=0).T for k in qp}
    trans = {k: _mm(wt[k], jnp.concatenate([x, jnp.concatenate([zeros_c, blk(v, *k)], axis=1)], axis=0))
             for k, x in zip(qp, tx)}
    m_ba = {k: jnp.where(blockdiag, trans[k][:, :hw], 0.0) for k in qp}
    h0c = {k: jnp.where(blockdiag, trans[k][:, hw:], 0.0) for k in qp}
    pc_col = {k: jnp.sum(jnp.where(eye_hw, blk(pc, *k)[:1], 0.0), axis=1, keepdims=True) for k in qp}

    state = [g_scr[i] for i in range(nseq * npair)]
    y_rows = [None] * (nseq * nck)
    for q in range(nck):
        for s in range(nseq):
            qq = s * nck + q
            st = state[s * npair:(s + 1) * npair]
            on_st = [_mm(jnp.concatenate([rp_pair[qq, p], m_ba[qq, p]], axis=0), st[p]) for p in range(npair)]
            y_rows[qq] = jnp.concatenate([on_st[p][:c] + y0_pair[qq, p] for p in range(npair)], axis=1)
            state[s * npair:(s + 1) * npair] = [
                pc_col[qq, p] * st[p] + on_st[p][c:] + h0c[qq, p] for p in range(npair)]
    for i in range(nseq * npair):
        g_scr[i] = state[i]

    y = jnp.concatenate(y_rows, axis=0)
    out = _rwkv_post(y, r, k2, v, g, rv_ref[...], head_ones)
    for s in range(nseq):
        o_ref[s] = out[s * tile:(s + 1) * tile]

    @pl.when(ti == pl.num_programs(1) - 1)
    def _():
        for s in range(nseq):
            for p in range(npair):
                gt = g_scr[s * npair + p].T
                sfin_ref[s, 2 * p] = gt[:R_HEAD, :R_HEAD]
                sfin_ref[s, 2 * p + 1] = pltpu.roll(gt, R_HEAD, axis=1)[R_HEAD:, :R_HEAD]


def _const_spec(shape):
    return pl.BlockSpec(shape, lambda *_: (0,) * len(shape))


def _rwkv_prompt(pr3d, wts):
    batch, seq, _ = pr3d.shape
    tile = RWKV_TILE
    nseq = RWKV_ROWS

    def prev_map(b, i):
        return (b, jnp.maximum(i * (tile // SUBLANES) - 1, 0), 0)

    w_specs = [_const_spec(w.shape) for w in wts]
    return pl.pallas_call(
        _rwkv_tile_kernel,
        grid=(batch // nseq, seq // tile),
        in_specs=[pl.BlockSpec((nseq, tile, RWKV_PROJ), lambda b, i: (b, i, 0)),
                  pl.BlockSpec((nseq, SUBLANES, RWKV_PROJ), prev_map)] + w_specs,
        out_specs=[pl.BlockSpec((nseq, tile, R_WIDTH), lambda b, i: (b, i, 0)),
                   pl.BlockSpec((nseq, R_HEADS, R_HEAD, R_HEAD), lambda b, i: (b, 0, 0, 0))],
        out_shape=[jax.ShapeDtypeStruct((batch, seq, R_WIDTH), F32),
                   jax.ShapeDtypeStruct((batch, R_HEADS, R_HEAD, R_HEAD), F32)],
        scratch_shapes=[pltpu.VMEM((nseq * (R_HEADS // 2), 2 * R_HEAD, 2 * R_HEAD), F32)],
        compiler_params=pltpu.CompilerParams(dimension_semantics=("arbitrary", "arbitrary"),
                                             vmem_limit_bytes=VMEM_LIMIT),
        name="rwkv_prompt",
    )(pr3d, pr3d, *wts)


def _rwkv_sample_kernel(pr_ref, prev_ref, mu_ref, rv_ref, wup_ref, aup_ref, gup_ref, bd_ref,
                        cols_ref, s_ref, o_ref, so_ref, vec_scr):
    h = pl.program_id(0)

    @pl.when(h == 0)
    def _():
        r, logw, k2, v, av, bv, g = _rwkv_prep(pr_ref[...], prev_ref[...], mu_ref[...], rv_ref[...],
                                               wup_ref[...], aup_ref[...], gup_ref[...], bd_ref[...])
        for i, x in enumerate((r, jnp.exp(logw), k2, v, av, bv, g)):
            vec_scr[i] = x.T

    rows = pl.ds(pl.multiple_of(h * R_HEAD, R_HEAD), R_HEAD)
    r, w, k, v, a, b, g = (vec_scr[i, rows, :] for i in range(7))
    ys = []
    for vi in range(R_HEAD):
        s = s_ref[0, vi]
        sa = jnp.sum(s * a, axis=0, keepdims=True)
        s2 = s * w + sa * b + v[vi:vi + 1, :] * k
        so_ref[0, vi] = s2
        ys.append(jnp.sum(s2 * r, axis=0, keepdims=True))
    y = jnp.concatenate(ys, axis=0)
    mu = jnp.mean(y, axis=0, keepdims=True)
    d = y - mu
    var = jnp.mean(d * d, axis=0, keepdims=True)
    yn = d * lax.rsqrt(var + LNX_EPS) * cols_ref[1] + cols_ref[2]
    bonus = jnp.sum(r * k * cols_ref[0], axis=0, keepdims=True)
    o_ref[...] = (yn + bonus * v) * g


def _rwkv_sample(pr, prev, state_t, prep_w, cols):
    nb = pr.shape[0]
    args = (pr, prev) + tuple(prep_w)
    head_rows = pl.BlockSpec((R_HEAD, nb), lambda h: (h, 0))
    st_spec = pl.BlockSpec((1, R_HEAD, R_HEAD, nb), lambda h: (h, 0, 0, 0))
    return pl.pallas_call(
        _rwkv_sample_kernel,
        grid=(R_HEADS,),
        in_specs=[_const_spec(a.shape) for a in args]
                 + [pl.BlockSpec((3, R_HEAD, nb), lambda h: (0, h, 0)), st_spec],
        out_specs=[head_rows, st_spec],
        out_shape=[jax.ShapeDtypeStruct((R_WIDTH, nb), F32),
                   jax.ShapeDtypeStruct(state_t.shape, F32)],
        scratch_shapes=[pltpu.VMEM((7, R_WIDTH, nb), F32)],
        compiler_params=pltpu.CompilerParams(dimension_semantics=("arbitrary",),
                                             vmem_limit_bytes=VMEM_LIMIT),
        name="rwkv_sample",
    )(*args, cols, state_t)


STEP_BATCH = 16


def _per_head_rows(x, n):
    return jnp.concatenate([jnp.broadcast_to(x[h:h + 1, :], (n, x.shape[1])) for h in range(x.shape[0])],
                           axis=0)


def _rows_from_columns(col_b, diag, n):
    picked = jnp.where(diag, col_b, 0.0)
    return jnp.concatenate([jnp.sum(picked[h * n:(h + 1) * n], axis=0, keepdims=True)
                            for h in range(col_b.shape[0] // n)], axis=0)


def _shift_rows(x, prev8, s):
    rolled = pltpu.roll(x, s, axis=0)
    top = jnp.where(_iota2((SUBLANES, 1), 0) < s, pltpu.roll(prev8, s, axis=0), rolled[:SUBLANES])
    return jnp.concatenate([top, rolled[SUBLANES:]], axis=0)


def _head_expand(lanes_per_head=M_HEADDIM):
    shape = (DT_PAD, M_HEADS * lanes_per_head)
    head_of_lane = _iota2(shape, 1) >> int(math.log2(lanes_per_head))
    return jnp.where(_iota2(shape, 0) == head_of_lane, 1.0, 0.0).astype(BF16)


def _gated_rmsnorm(y, z, gnorm_w):
    yz = y * _silu(z)
    gw = M_WIDTH // M_GROUPS
    outs = []
    for gi in range(M_GROUPS):
        yg = yz[:, gi * gw:(gi + 1) * gw]
        ms = jnp.mean(yg * yg, axis=-1, keepdims=True)
        outs.append(yg * lax.rsqrt(ms + RMS_EPS))
    return jnp.concatenate(outs, axis=1) * gnorm_w


def _mamba_tile_kernel(x_ref, xprev_ref, wm_ref, wdt_ref, cw_ref, dta_ref, mv_ref,
                       o_ref, hfin_ref, tail_ref, h_scr):
    tile = MAMBA_TILE
    q = SSD_CHUNK
    ti = pl.program_id(1)

    @pl.when(ti == 0)
    def _():
        h_scr[...] = jnp.zeros_like(h_scr)

    x16 = x_ref[...].astype(BF16)
    proj = jnp.dot(x16, wm_ref[...], preferred_element_type=F32)
    z = proj[:, :M_WIDTH]
    xb = proj[:, M_WIDTH:]
    dt_raw = _dt_columns(x16, wdt_ref[...])
    tail_ref[...] = xb[tile - SUBLANES:, :]
    prev8 = jnp.where(ti == 0, 0.0,
                      jnp.dot(xprev_ref[...].astype(BF16), wm_ref[:, M_WIDTH:],
                              preferred_element_type=F32))
    conv = cw_ref[CONV_K:CONV_K + 1, :] + xb * cw_ref[CONV_K - 1:CONV_K, :]
    for s in range(1, CONV_K):
        conv = conv + _shift_rows(xb, prev8, s) * cw_ref[CONV_K - 1 - s:CONV_K - s, :]
    xact = _silu(conv)
    xs = xact[:, :M_WIDTH]
    gs = M_GROUPS * D_STATE
    bm = xact[:, M_WIDTH:M_WIDTH + gs]
    cm = xact[:, M_WIDTH + gs:]

    dt = _softplus(dt_raw + dta_ref[0:1, :])
    adt = dt * (-jnp.exp(dta_ref[1:2, :]))
    x_dt = xs * _mm_sel_r(dt, _head_expand(), 3)

    sh_q = int(math.log2(q))
    ti_i = _iota2((tile, tile), 0)
    ti_j = _iota2((tile, tile), 1)
    chunk_tril = jnp.where((ti_i >= ti_j) & ((ti_i >> sh_q) == (ti_j >> sh_q)), 1.0, 0.0).astype(BF16)
    cs = _mm_sel_l(chunk_tril, adt, 3)
    cs_t = cs.T
    incl = _iota2((q, q), 0) >= _iota2((q, q), 1)
    lane = _iota2((1, 2 * M_HEADDIM), 1)
    first = lane < M_HEADDIM
    hpg = M_HEADS // M_GROUPS

    state = [h_scr[p] for p in range(M_HEADS // 2)]
    y_rows = []
    for j in range(tile // q):
        rs = slice(j * q, (j + 1) * q)
        y_pairs = []
        for gi in range(M_GROUPS):
            bm_g = bm[rs, gi * D_STATE:(gi + 1) * D_STATE]
            cm_g = cm[rs, gi * D_STATE:(gi + 1) * D_STATE]
            cb = _mm_nt(cm_g, bm_g)
            bm_t = bm_g.T
            for pp in range(hpg // 2):
                p = gi * (hpg // 2) + pp
                x_p = x_dt[rs, 2 * M_HEADDIM * p:2 * M_HEADDIM * (p + 1)]
                y_diag = jnp.zeros((q, 2 * M_HEADDIM), F32)
                contrib = jnp.zeros((D_STATE, 2 * M_HEADDIM), F32)
                scale_in = []
                scale_end = []
                for e in range(2):
                    head = 2 * p + e
                    m = first if e == 0 else jnp.logical_not(first)
                    a_col = cs[rs, head:head + 1]
                    a_row = cs_t[head:head + 1, rs]
                    a_end = cs_t[head:head + 1, (j + 1) * q - 1:(j + 1) * q]
                    lmat = jnp.exp(jnp.where(incl, a_col - a_row, -jnp.inf))
                    x_m = jnp.where(m, x_p, 0.0)
                    y_diag = y_diag + _mm(cb * lmat, x_m)
                    contrib = contrib + _mm(bm_t * jnp.exp(a_end - a_row), x_m)
                    scale_in.append(jnp.exp(a_col))
                    scale_end.append(jnp.exp(a_end))
                y_off = _mm(cm_g, state[p]) * jnp.where(first, scale_in[0], scale_in[1])
                state[p] = state[p] * jnp.where(first, scale_end[0], scale_end[1]) + contrib
                y_pairs.append(y_diag + y_off)
        y_rows.append(jnp.concatenate(y_pairs, axis=1))
    for p in range(M_HEADS // 2):
        h_scr[p] = state[p]

    y = jnp.concatenate(y_rows, axis=0) + mv_ref[0:1, :] * xs
    o_ref[...] = _gated_rmsnorm(y, z, mv_ref[1:2, :])

    @pl.when(ti == pl.num_programs(1) - 1)
    def _():
        for p in range(M_HEADS // 2):
            hfin_ref[0, 2 * M_HEADDIM * p:2 * M_HEADDIM * (p + 1), :] = h_scr[p].T


def _mamba_prompt(x2d, batch, seq, wts):
    tile = MAMBA_TILE
    ntile = seq // tile
    rows8 = seq // SUBLANES

    def prev_map(b, i):
        return (b * rows8 + jnp.maximum(i * (tile // SUBLANES) - 1, 0), 0)

    tok = lambda w: pl.BlockSpec((tile, w), lambda b, i: (b * ntile + i, 0))
    return pl.pallas_call(
        _mamba_tile_kernel,
        grid=(batch, ntile),
        in_specs=[tok(D_MODEL), pl.BlockSpec((SUBLANES, D_MODEL), prev_map)]
                 + [_const_spec(w.shape) for w in wts],
        out_specs=[tok(M_WIDTH),
                   pl.BlockSpec((1, M_HEADS * M_HEADDIM, D_STATE), lambda b, i: (b, 0, 0)),
                   pl.BlockSpec((SUBLANES, CONV_DIM), lambda b, i: (b, 0))],
        out_shape=[jax.ShapeDtypeStruct((batch * seq, M_WIDTH), F32),
                   jax.ShapeDtypeStruct((batch, M_HEADS * M_HEADDIM, D_STATE), F32),
                   jax.ShapeDtypeStruct((batch * SUBLANES, CONV_DIM), F32)],
        scratch_shapes=[pltpu.VMEM((M_HEADS // 2, D_STATE, 2 * M_HEADDIM), F32)],
        compiler_params=pltpu.CompilerParams(dimension_semantics=("arbitrary", "arbitrary"),
                                             vmem_limit_bytes=VMEM_LIMIT),
        name="mamba_prompt",
    )(x2d, x2d, *wts)


def _mamba_prep_kernel(xbc_ref, cprev_ref, dt_ref, cw_ref, dta_ref,
                       xs_ref, xdt_ref, dec_ref, bm_ref, cm_ref, cnew_ref):
    xbc = xbc_ref[...]
    conv = cw_ref[CONV_K:CONV_K + 1, :] + xbc * cw_ref[CONV_K - 1:CONV_K, :]
    for j in range(CONV_K - 1):
        conv = conv + cprev_ref[j] * cw_ref[j:j + 1, :]
        cnew_ref[j] = cprev_ref[j + 1] if j + 1 < CONV_K - 1 else xbc
    xact = _silu(conv)
    xs = xact[:, :M_WIDTH]
    gs = M_GROUPS * D_STATE
    dt = _softplus(dt_ref[...] + dta_ref[0:1, :])
    adt = dt * (-jnp.exp(dta_ref[1:2, :]))
    expand = _head_expand()
    xs_ref[...] = xs
    xdt_ref[...] = xs * _mm_sel_r(dt, expand, 3)
    dec_ref[...] = jnp.exp(_mm_sel_r(adt, _head_expand(D_STATE), 3))
    hpg = M_HEADS // M_GROUPS
    per_head = lambda m: jnp.concatenate(
        [m[:, (e // hpg) * D_STATE:(e // hpg + 1) * D_STATE] for e in range(M_HEADS)], axis=1)
    bm_ref[...] = per_head(xact[:, M_WIDTH:M_WIDTH + gs])
    cm_ref[...] = per_head(xact[:, M_WIDTH + gs:])


def _mamba_sample_prep(xbc, cprev_t, dt, cw, dta):
    n = xbc.shape[0]
    args = (xbc, cprev_t, dt, cw, dta)
    widths = (M_WIDTH, M_WIDTH, M_HEADS * D_STATE, M_HEADS * D_STATE, M_HEADS * D_STATE)
    out_shapes = [(n, w) for w in widths] + [cprev_t.shape]
    return pl.pallas_call(
        _mamba_prep_kernel,
        grid=(1,),
        in_specs=[_const_spec(a.shape) for a in args],
        out_specs=[_const_spec(s) for s in out_shapes],
        out_shape=[jax.ShapeDtypeStruct(s, F32) for s in out_shapes],
        compiler_params=pltpu.CompilerParams(vmem_limit_bytes=VMEM_LIMIT),
        name="mamba_sample_prep",
    )(*args)


def _mamba_step_kernel(xs_ref, xdt_ref, dec_ref, bm_ref, cm_ref, h_ref, dskip_ref, y_ref, ho_ref):
    n = M_HEADS * M_HEADDIM
    ones_in = jnp.ones((M_HEADDIM, D_STATE), BF16)
    ones_out = jnp.ones((D_STATE, M_HEADDIM), BF16)
    diag = (_iota2((n, M_HEADDIM), 0) & (M_HEADDIM - 1)) == _iota2((n, M_HEADDIM), 1)

    def head_rows(ref, bi):
        return jnp.concatenate(
            [jnp.broadcast_to(ref[bi:bi + 1, e * D_STATE:(e + 1) * D_STATE], (M_HEADDIM, D_STATE))
             for e in range(M_HEADS)], axis=0)

    for bi in range(STEP_BATCH):
        rows = slice(bi * M_HEADS, (bi + 1) * M_HEADS)
        hrows = slice(bi * n, (bi + 1) * n)
        x_col = _mm(jnp.where(diag, _per_head_rows(xdt_ref[rows, :], M_HEADDIM), 0.0), ones_in)
        h2 = h_ref[hrows, :] * head_rows(dec_ref, bi) + x_col * head_rows(bm_ref, bi)
        ho_ref[hrows, :] = h2
        y = _rows_from_columns(_mm(h2 * head_rows(cm_ref, bi), ones_out), diag, M_HEADDIM)
        y_ref[rows, :] = y + dskip_ref[...] * xs_ref[rows, :]


def _mamba_sample_step(xs, xdt, dec, bm, cm, state, dskip_rows):
    nb = state.shape[0]
    rows = STEP_BATCH * M_HEADS
    vec_spec = pl.BlockSpec((rows, M_HEADDIM), lambda i: (i, 0))
    bc_spec = pl.BlockSpec((STEP_BATCH, M_HEADS * D_STATE), lambda i: (i, 0))
    st_spec = pl.BlockSpec((rows * M_HEADDIM, D_STATE), lambda i: (i, 0))
    flat = lambda x, w: x.reshape(nb * M_HEADS, w)
    return pl.pallas_call(
        _mamba_step_kernel,
        grid=(nb // STEP_BATCH,),
        in_specs=[vec_spec] * 2 + [bc_spec] * 3 + [st_spec, _const_spec((M_HEADS, M_HEADDIM))],
        out_specs=[vec_spec, st_spec],
        out_shape=[jax.ShapeDtypeStruct((nb * M_HEADS, M_HEADDIM), F32),
                   jax.ShapeDtypeStruct((nb * M_HEADS * M_HEADDIM, D_STATE), F32)],
        compiler_params=pltpu.CompilerParams(dimension_semantics=("arbitrary",),
                                             vmem_limit_bytes=VMEM_LIMIT),
        name="mamba_sample_step",
    )(flat(xs, M_HEADDIM), flat(xdt, M_HEADDIM), dec, bm, cm,
      state.reshape(nb * M_HEADS * M_HEADDIM, D_STATE), dskip_rows)


FF_SLAB = 2048


def _ffn_math(x, o_r, o_m, wout_ref, lnp_ref, w1_ref, w2_ref):
    mix = _mm(o_r, wout_ref[:R_WIDTH, :]) + _mm(o_m, wout_ref[R_WIDTH:, :])
    h = _layer_norm(ALPHA * x + mix, lnp_ref[0:1, :], lnp_ref[1:2, :], LN_EPS)
    hb = h.astype(BF16)
    ff = jnp.zeros_like(h)
    for j in range(D_FF // FF_SLAB):
        t = jnp.dot(hb, w1_ref[:, j * FF_SLAB:(j + 1) * FF_SLAB], preferred_element_type=F32)
        t = jnp.square(jnp.maximum(t, 0.0))
        ff = ff + jnp.dot(t.astype(BF16), w2_ref[j * FF_SLAB:(j + 1) * FF_SLAB, :],
                          preferred_element_type=F32)
    return _layer_norm(ALPHA * h + ff, lnp_ref[2:3, :], lnp_ref[3:4, :], LN_EPS)


def _ffn_kernel(xp_ref, orp_ref, omp_ref, xs_ref, ors_ref, yms_ref, zs_ref, mv_ref,
                wout_ref, w1_ref, w2_ref, lnp_ref, yp_ref, ys_ref):
    i = pl.program_id(0)
    last = pl.num_programs(0) - 1

    @pl.when(i < last)
    def _():
        yp_ref[...] = _ffn_math(xp_ref[...], orp_ref[...], omp_ref[...], wout_ref, lnp_ref, w1_ref, w2_ref)

    @pl.when(i == last)
    def _():
        o_m = _gated_rmsnorm(yms_ref[...], zs_ref[...], mv_ref[1:2, :])
        ys_ref[...] = _ffn_math(xs_ref[...], ors_ref[...], o_m, wout_ref, lnp_ref, w1_ref, w2_ref)


def _ffn(prompt, sample, mv, wout, w1, w2, lnp, tm):
    n = prompt[0].shape[0]
    ns = sample[0].shape[0]
    ntile = n // tm
    tok = lambda w: pl.BlockSpec((tm, w), lambda i: (jnp.minimum(i, ntile - 1), 0))
    whole = lambda shape: pl.BlockSpec(shape, lambda i: (0, 0), pipeline_mode=pl.Buffered(1))
    consts = list(sample) + [mv, wout, w1, w2, lnp]
    return pl.pallas_call(
        _ffn_kernel,
        grid=(ntile + 1,),
        in_specs=[tok(D_MODEL), tok(R_WIDTH), tok(M_WIDTH)] + [whole(a.shape) for a in consts],
        out_specs=[tok(D_MODEL), whole((ns, D_MODEL))],
        out_shape=[jax.ShapeDtypeStruct((n, D_MODEL), F32), jax.ShapeDtypeStruct((ns, D_MODEL), F32)],
        compiler_params=pltpu.CompilerParams(dimension_semantics=("arbitrary",),
                                             vmem_limit_bytes=VMEM_LIMIT),
        name="ffn",
    )(*prompt, *consts)


def _block_diag_ones():
    idx = np.arange(R_WIDTH) // R_HEAD
    return jnp.asarray(idx[:, None] == idx[None, :], dtype=BF16)


def kernel(x_prompt, x_sample, state_shift, state_wkv, state_conv, state_ssm, w_in, mu_shift, w0, w_up, a0, a_up, g_up, k_k, k_a, r_k, lnx_w, lnx_b, conv_w, conv_b, dt_bias, a_log, d_skip, gnorm_w, w_out, ln1_g, ln1_b, w_ff1, w_ff2, ln2_g, ln2_b):
    assert w_in.shape[0] == DEPTH and x_sample.shape[1] == 1
    batch, seq, _ = x_prompt.shape
    nb = x_sample.shape[0]
    assert seq % MAMBA_TILE == 0 and seq % RWKV_TILE == 0 and seq % DENSE_ROWS == 0
    assert nb % STEP_BATCH == 0 and batch % RWKV_ROWS == 0
    row = lambda a: a.reshape(1, -1)

    w_t = jnp.transpose(w_in[0])
    w_dt = w_t[PROJ_MAIN:]
    zl = jnp.zeros((D_DECAY_LORA, R_WIDTH), F32)
    wup_p = jnp.concatenate([w_up[0], zl], axis=0).astype(BF16)
    aup_p = jnp.concatenate([zl, a_up[0]], axis=0).astype(BF16)
    bd = _block_diag_ones()

    def rows8(*vecs):
        m = jnp.stack([v.reshape(-1) for v in vecs])
        return jnp.pad(m, ((0, SUBLANES - len(vecs)), (0, 0)))

    rv = rows8(w0[0], a0[0], k_k[0], k_a[0], r_k[0], lnx_w[0], lnx_b[0])
    prep_w = (row(mu_shift[0]), rv, wup_p, aup_p, g_up[0].astype(BF16), bd)
    cw = rows8(*conv_w[0], conv_b[0])
    dta = rows8(jnp.pad(dt_bias[0], (0, DT_PAD - M_HEADS)), jnp.pad(a_log[0], (0, DT_PAD - M_HEADS)))
    mv = rows8(jnp.repeat(d_skip[0], M_HEADDIM), gnorm_w[0])
    dskip_rows = jnp.broadcast_to(d_skip[0][:, None], (M_HEADS, M_HEADDIM))
    lnp = rows8(ln1_g[0], ln1_b[0], ln2_g[0], ln2_b[0])

    xp = x_prompt.reshape(batch * seq, D_MODEL)
    xs2d = x_sample.reshape(nb, D_MODEL)
    (pr_p, pr_s, z_s, xbc_s, dt_s, wm_bf, wout_bf, w1_bf, w2_bf) = _proj(
        xp, xs2d, w_t, w_dt, (w_out[0], w_ff1[0], w_ff2[0]), DENSE_ROWS)

    o_r_p, wkv_p = _rwkv_prompt(pr_p.reshape(batch, seq, RWKV_PROJ), prep_w)
    o_r_p = o_r_p.reshape(batch * seq, R_WIDTH)
    o_m_p, ssm_p, tail_p = _mamba_prompt(xp, batch, seq, (wm_bf, w_dt, cw, dta, mv))
    shift_p = pr_p.reshape(batch, seq, RWKV_PROJ)[:, -1]
    conv_p = tail_p.reshape(batch, SUBLANES, CONV_DIM)[:, SUBLANES - (CONV_K - 1):]

    cols = jnp.broadcast_to(jnp.stack([r_k[0].reshape(-1), lnx_w[0], lnx_b[0]])[:, :, None],
                            (3, R_WIDTH, nb))
    o_r_t, wkv_t = _rwkv_sample(pr_s, state_shift[0], jnp.transpose(state_wkv[0], (1, 2, 3, 0)),
                                prep_w, cols)
    o_r_s = o_r_t.T
    wkv_s = jnp.transpose(wkv_t, (3, 0, 1, 2))
    xs_s, xdt_s, dec_s, bm_s, cm_s, conv_t = _mamba_sample_prep(
        xbc_s, jnp.transpose(state_conv[0], (1, 0, 2)), dt_s, cw, dta)
    conv_s = jnp.transpose(conv_t, (1, 0, 2))
    y_m_s, ssm_s = _mamba_sample_step(xs_s, xdt_s, dec_s, bm_s, cm_s, state_ssm[0], dskip_rows)

    y_p, y_s = _ffn((xp, o_r_p, o_m_p), (xs2d, o_r_s, y_m_s.reshape(nb, M_WIDTH), z_s), mv,
                    wout_bf, w1_bf, w2_bf, lnp, tm=DENSE_ROWS)

    return (y_p.reshape(batch, seq, D_MODEL), y_s.reshape(nb, 1, D_MODEL),
            shift_p[None], wkv_p[None], conv_p[None],
            ssm_p.reshape(1, batch, M_HEADS, M_HEADDIM, D_STATE),
            pr_s[None], wkv_s.reshape(1, nb, R_HEADS, R_HEAD, R_HEAD), conv_s[None],
            ssm_s.reshape(1, nb, M_HEADS, M_HEADDIM, D_STATE))
```

```python
import math

import numpy as np
import jax
import jax.numpy as jnp
from jax import lax
from jax.experimental import pallas as pl
from jax.experimental.pallas import tpu as pltpu

F32 = jnp.float32
BF16 = jnp.bfloat16

D_MODEL = 1024
R_WIDTH = 512
R_HEAD = 64
R_HEADS = 8
D_DECAY_LORA = 64
D_AAA_LORA = 64
D_GATE_LORA = 128
M_WIDTH = 512
M_HEADDIM = 64
M_HEADS = 8
M_GROUPS = 2
D_STATE = 128
CONV_K = 4
CONV_DIM = M_WIDTH + 2 * M_GROUPS * D_STATE
D_FF = 4 * D_MODEL
RWKV_PROJ = 3 * R_WIDTH + D_DECAY_LORA + D_AAA_LORA + D_GATE_LORA
DT_PAD = 128
PROJ_MAIN = RWKV_PROJ + M_WIDTH + CONV_DIM
DEPTH = 1
ALPHA = (2 * DEPTH) ** 0.25
LN_EPS = 1e-5
LNX_EPS = 64e-5
RMS_EPS = 1e-5

RWKV_CHUNK = 64
RWKV_TILE = 256
RWKV_ROWS = 2
SSD_CHUNK = 128
MAMBA_TILE = 256
SUBLANES = 8
DENSE_ROWS = 512
VMEM_LIMIT = 48 * 1024 * 1024
PROJ_W_VMEM_LIMIT = 56 * 1024 * 1024

_NT = (((1,), (1,)), ((), ()))


def _mm(a, b):
    return jnp.dot(a.astype(BF16), b.astype(BF16), preferred_element_type=F32)


def _mm_nt(a, b):
    return lax.dot_general(a.astype(BF16), b.astype(BF16), _NT, preferred_element_type=F32)


def _split(x, passes):
    parts = []
    for _ in range(passes):
        p = x.astype(BF16)
        parts.append(p)
        x = x - p.astype(F32)
    return parts


def _mm_sel_r(x, sel, passes):
    return sum(jnp.dot(p, sel, preferred_element_type=F32) for p in _split(x, passes))


def _mm_sel_l(sel, x, passes):
    return sum(jnp.dot(sel, p, preferred_element_type=F32) for p in _split(x, passes))


def _softplus(x):
    return jnp.maximum(x, 0.0) + jnp.log(1.0 + jnp.exp(-jnp.abs(x)))


def _sigmoid(x):
    return 0.5 + 0.5 * jnp.tanh(0.5 * x)


def _silu(x):
    return x * _sigmoid(x)


def _layer_norm(x, g, b, eps):
    mu = jnp.mean(x, axis=-1, keepdims=True)
    d = x - mu
    var = jnp.mean(d * d, axis=-1, keepdims=True)
    return d * lax.rsqrt(var + eps) * g + b


def _iota2(shape, axis):
    return lax.broadcasted_iota(jnp.int32, shape, axis)


def _dt_columns(xb, wdt_t):
    pad = jnp.zeros((DT_PAD - M_HEADS, D_MODEL), F32)
    return _mm_nt(xb, jnp.concatenate([wdt_t, pad], axis=0))


W_SLAB = 256


def _proj_kernel(xp_ref, xs_ref, wt_ref, wdt_ref, wout32_ref, w1_32_ref, w2_32_ref,
                 prp_ref, zp_ref, xbcp_ref, dtp_ref, prs_ref, zs_ref, xbcs_ref, dts_ref,
                 wout_ref, w1_ref, w2_ref, wbf_scr):
    i = pl.program_id(0)
    last = pl.num_programs(0) - 1

    @pl.when(i == 0)
    def _():
        for c in range(0, PROJ_MAIN, W_SLAB):
            wbf_scr[:, c:c + W_SLAB] = wt_ref[c:c + W_SLAB, :].T.astype(BF16)

    def project(x_ref, pr_ref, z_ref, xbc_ref, dt_ref):
        xb = x_ref[...].astype(BF16)
        res = jnp.dot(xb, wbf_scr[...], preferred_element_type=F32)
        c0 = RWKV_PROJ
        c1 = c0 + M_WIDTH
        pr_ref[...] = res[:, :c0]
        z_ref[...] = res[:, c0:c1]
        xbc_ref[...] = res[:, c1:]
        dt_ref[...] = _dt_columns(xb, wdt_ref[...])

    @pl.when(i < last)
    def _():
        project(xp_ref, prp_ref, zp_ref, xbcp_ref, dtp_ref)
        wout_ref[...] = wout32_ref[...].astype(BF16)
        w1_ref[...] = w1_32_ref[...].astype(BF16)
        w2_ref[...] = w2_32_ref[...].astype(BF16)

    @pl.when(i == last)
    def _():
        project(xs_ref, prs_ref, zs_ref, xbcs_ref, dts_ref)


def _proj(xp2d, xs2d, w_t, wdt_t, mlp_w, tm):
    n = xp2d.shape[0]
    ns = xs2d.shape[0]
    ntile = n // tm
    widths = (RWKV_PROJ, M_WIDTH, CONV_DIM, DT_PAD)
    once = lambda shape: pl.BlockSpec(shape, lambda i: (0, 0), pipeline_mode=pl.Buffered(1))
    tile = lambda w: pl.BlockSpec((tm, w), lambda i: (jnp.minimum(i, ntile - 1), 0))
    slab = lambda a: pl.BlockSpec((a.shape[0] // ntile, a.shape[1]), lambda i: (jnp.minimum(i, ntile - 1), 0))
    return pl.pallas_call(
        _proj_kernel,
        grid=(ntile + 1,),
        in_specs=[tile(D_MODEL), once(xs2d.shape), once(w_t.shape), once(wdt_t.shape)]
                 + [slab(w) for w in mlp_w],
        out_specs=[tile(w) for w in widths] + [once((ns, w)) for w in widths] + [slab(w) for w in mlp_w],
        out_shape=[jax.ShapeDtypeStruct((n, w), F32) for w in widths]
                  + [jax.ShapeDtypeStruct((ns, w), F32) for w in widths]
                  + [jax.ShapeDtypeStruct(w.shape, BF16) for w in mlp_w],
        scratch_shapes=[pltpu.VMEM((D_MODEL, PROJ_MAIN), BF16)],
        compiler_params=pltpu.CompilerParams(dimension_semantics=("arbitrary",),
                                             vmem_limit_bytes=PROJ_W_VMEM_LIMIT),
        name="proj",
    )(xp2d, xs2d, w_t, wdt_t, *mlp_w)


def _rwkv_prep(pr, prev, mu, rv, wup, aup, gup, bd):
    w0, a0, k_k, k_a = (rv[i:i + 1, :] for i in range(4))
    u = pr + (prev - pr) * mu
    r = u[:, 0:R_WIDTH]
    k = u[:, R_WIDTH:2 * R_WIDTH]
    v = u[:, 2 * R_WIDTH:3 * R_WIDTH]
    lora = u[:, 3 * R_WIDTH:3 * R_WIDTH + D_DECAY_LORA + D_AAA_LORA]
    ug = u[:, 3 * R_WIDTH + D_DECAY_LORA + D_AAA_LORA:]
    w_log = -_softplus(-(w0 + _mm(jnp.tanh(lora), wup))) - 0.5
    logw = -jnp.exp(w_log)
    a = _sigmoid(a0 + _mm(lora, aup))
    g = _mm(_sigmoid(ug), gup)
    kk = k * k_k
    ss = _mm_sel_r(kk * kk, bd, 1)
    kkn = kk * lax.rsqrt(jnp.maximum(ss, 1e-24))
    k2 = k * (1.0 + (a - 1.0) * k_a)
    return r, logw, k2, v, -kkn, kkn * a, g


def _rwkv_post(y, r, k2, v, g, rv, bd):
    r_k, lnx_w, lnx_b = (rv[i:i + 1, :] for i in range(4, 7))
    inv = 1.0 / R_HEAD
    mu = _mm_sel_r(y, bd, 1) * inv
    d = y - mu
    var = _mm_sel_r(d * d, bd, 1) * inv
    yn = d * lax.rsqrt(var + LNX_EPS) * lnx_w + lnx_b
    bonus = _mm_sel_r(r * k2 * r_k, bd, 1)
    return (yn + bonus * v) * g


def _rwkv_tile_kernel(pr_ref, prevblk_ref, mu_ref, rv_ref, wup_ref, aup_ref, gup_ref, bd_ref,
                      o_ref, sfin_ref, g_scr):
    c = RWKV_CHUNK
    tile = RWKV_TILE
    nseq = RWKV_ROWS
    nck = tile // c
    hw = 2 * R_HEAD
    npair = R_HEADS // 2
    ti = pl.program_id(1)

    @pl.when(ti == 0)
    def _():
        g_scr[...] = jnp.zeros_like(g_scr)

    pr = jnp.concatenate([pr_ref[s] for s in range(nseq)], axis=0)
    prev = pltpu.roll(pr, 1, axis=0)
    row_id = _iota2((nseq * tile, 1), 0)
    for s in range(nseq):
        prev_row = jnp.where(ti == 0, 0.0, prevblk_ref[s, SUBLANES - 1:SUBLANES, :])
        prev = jnp.where(row_id == s * tile, prev_row, prev)
    head_ones = bd_ref[...]
    r, logw, k2, v, av, bv, g = _rwkv_prep(pr, prev, mu_ref[...], rv_ref[...], wup_ref[...],
                                           aup_ref[...], gup_ref[...], head_ones)

    sh_c = int(math.log2(c))
    ti_i = _iota2((tile, tile), 0)
    ti_j = _iota2((tile, tile), 1)
    chunk_tril = jnp.where((ti_i >= ti_j) & ((ti_i >> sh_c) == (ti_j >> sh_c)), 1.0, 0.0).astype(BF16)
    cum = jnp.concatenate([_mm_sel_l(chunk_tril, logw[s * tile:(s + 1) * tile], 3) for s in range(nseq)],
                          axis=0)
    cl = jnp.concatenate(
        [jnp.broadcast_to(cum[(q + 1) * c - 1:(q + 1) * c, :], (c, R_WIDTH)) for q in range(nseq * nck)],
        axis=0)
    p_out = jnp.exp(-cum)
    p_end = jnp.exp(cl - cum)
    rt = r * jnp.exp(cum)
    at = av * jnp.exp(cum - logw)
    bt = bv * p_out
    kt = k2 * p_out
    bh = bv * p_end
    kh = k2 * p_end
    pc = jnp.exp(cl)

    ii = _iota2((c, hw), 0)
    jj = _iota2((c, hw), 1) & (c - 1)
    strict = ii > jj
    incl = ii >= jj
    lane = _iota2((1, hw), 1)
    keep = (jnp.where(lane < R_HEAD, 1.0, 0.0).astype(BF16), jnp.where(lane >= R_HEAD, 1.0, 0.0).astype(BF16))
    e_r = _iota2((hw, hw), 0)
    e_c = _iota2((hw, hw), 1)
    eye_hw = e_r == e_c
    blockdiag = (e_r >= R_HEAD) == (e_c >= R_HEAD)

    def bd(x):
        xb = x.astype(BF16)
        return jnp.concatenate([xb * keep[0], xb * keep[1]], axis=0)

    def pdot(a, b_bd):
        return jnp.dot(a.astype(BF16), b_bd, preferred_element_type=F32)

    qp = [(q, p) for q in range(nseq * nck) for p in range(npair)]
    blk = lambda x, q, p: x[q * c:(q + 1) * c, hw * p:hw * (p + 1)]
    lhs = [jnp.concatenate([blk(at, *k), blk(rt, *k)], axis=0).astype(BF16) for k in qp]
    sc = [lax.dot_general(x, jnp.concatenate([bd(blk(bt, *k)), bd(blk(kt, *k))], axis=0), _NT,
                          preferred_element_type=F32) for x, k in zip(lhs, qp)]
    a_ab = [jnp.where(strict, s[:c, :hw], 0.0) for s in sc]
    a_rb = [jnp.where(incl, s[c:, :hw], 0.0) for s in sc]
    a_akrk = [jnp.concatenate([jnp.where(strict, s[:c, hw:], 0.0), jnp.where(incl, s[c:, hw:], 0.0)], axis=0)
              for s in sc]
    v_bd = [bd(blk(v, *k)) for k in qp]
    akrk_v = [pdot(a, x) for a, x in zip(a_akrk, v_bd)]

    sh = 4
    eye = jnp.where(ii == jj, 1.0, 0.0)
    npow = [jnp.where((ii >> sh) == (jj >> sh), n, 0.0) for n in a_ab]
    t_inv = [eye + n for n in npow]
    npow = [pdot(n, bd(n)) for n in npow]
    for _ in range(sh - 2):
        both = [pdot(jnp.concatenate([t, n], axis=0), bd(n)) for t, n in zip(t_inv, npow)]
        t_inv = [t + x[:c] for t, x in zip(t_inv, both)]
        npow = [x[c:] for x in both]
    t_inv = [t + pdot(t, bd(n)) for t, n in zip(t_inv, npow)]
    while (1 << sh) < c:
        off = ((ii >> (sh + 1)) == (jj >> (sh + 1))) & (((ii >> sh) & 1) > ((jj >> sh) & 1))
        tn = [pdot(t, bd(jnp.where(off, n, 0.0))) for t, n in zip(t_inv, a_ab)]
        t_inv = [t + pdot(x, bd(t)) for t, x in zip(t_inv, tn)]
        sh += 1

    tx = [pdot(t, jnp.concatenate([bd(blk(at, *k)), bd(x[:c])], axis=1))
          for t, x, k in zip(t_inv, akrk_v, qp)]
    rbx = [pdot(a, jnp.concatenate([bd(x[:, :hw]), bd(x[:, hw:])], axis=1)) for a, x in zip(a_rb, tx)]
    rp_pair = {k: blk(rt, *k) + x[:, :hw] for k, x in zip(qp, rbx)}
    y0_pair = {k: x[:, hw:] + y[c:] for k, x, y in zip(qp, rbx, akrk_v)}

    zeros_c = jnp.zeros((c, hw), F32)
    wt = {k: jnp.concatenate([blk(bh, *k), blk(kh, *k)], axis=0).T for k in qp}
    trans = {k: _mm(wt[k], jnp.concatenate([x, jnp.concatenate([zeros_c, blk(v, *k)], axis=1)], axis=0))
             for k, x in zip(qp, tx)}
    m_ba = {k: jnp.where(blockdiag, trans[k][:, :hw], 0.0) for k in qp}
    h0c = {k: jnp.where(blockdiag, trans[k][:, hw:], 0.0) for k in qp}
    pc_col = {k: jnp.sum(jnp.where(eye_hw, blk(pc, *k)[:1], 0.0), axis=1, keepdims=True) for k in qp}

    state = [g_scr[i] for i in range(nseq * npair)]
    y_rows = [None] * (nseq * nck)
    for q in range(nck):
        for s in range(nseq):
            qq = s * nck + q
            st = state[s * npair:(s + 1) * npair]
            on_st = [_mm(jnp.concatenate([rp_pair[qq, p], m_ba[qq, p]], axis=0), st[p]) for p in range(npair)]
            y_rows[qq] = jnp.concatenate([on_st[p][:c] + y0_pair[qq, p] for p in range(npair)], axis=1)
            state[s * npair:(s + 1) * npair] = [
                pc_col[qq, p] * st[p] + on_st[p][c:] + h0c[qq, p] for p in range(npair)]
    for i in range(nseq * npair):
        g_scr[i] = state[i]

    y = jnp.concatenate(y_rows, axis=0)
    out = _rwkv_post(y, r, k2, v, g, rv_ref[...], head_ones)
    for s in range(nseq):
        o_ref[s] = out[s * tile:(s + 1) * tile]

    @pl.when(ti == pl.num_programs(1) - 1)
    def _():
        for s in range(nseq):
            for p in range(npair):
                gt = g_scr[s * npair + p].T
                sfin_ref[s, 2 * p] = gt[:R_HEAD, :R_HEAD]
                sfin_ref[s, 2 * p + 1] = pltpu.roll(gt, R_HEAD, axis=1)[R_HEAD:, :R_HEAD]


def _const_spec(shape):
    return pl.BlockSpec(shape, lambda *_: (0,) * len(shape))


def _rwkv_prompt(pr3d, wts):
    batch, seq, _ = pr3d.shape
    tile = RWKV_TILE
    nseq = RWKV_ROWS

    def prev_map(b, i):
        return (b, jnp.maximum(i * (tile // SUBLANES) - 1, 0), 0)

    w_specs = [_const_spec(w.shape) for w in wts]
    return pl.pallas_call(
        _rwkv_tile_kernel,
        grid=(batch // nseq, seq // tile),
        in_specs=[pl.BlockSpec((nseq, tile, RWKV_PROJ), lambda b, i: (b, i, 0)),
                  pl.BlockSpec((nseq, SUBLANES, RWKV_PROJ), prev_map)] + w_specs,
        out_specs=[pl.BlockSpec((nseq, tile, R_WIDTH), lambda b, i: (b, i, 0)),
                   pl.BlockSpec((nseq, R_HEADS, R_HEAD, R_HEAD), lambda b, i: (b, 0, 0, 0))],
        out_shape=[jax.ShapeDtypeStruct((batch, seq, R_WIDTH), F32),
                   jax.ShapeDtypeStruct((batch, R_HEADS, R_HEAD, R_HEAD), F32)],
        scratch_shapes=[pltpu.VMEM((nseq * (R_HEADS // 2), 2 * R_HEAD, 2 * R_HEAD), F32)],
        compiler_params=pltpu.CompilerParams(dimension_semantics=("arbitrary", "arbitrary"),
                                             vmem_limit_bytes=VMEM_LIMIT),
        name="rwkv_prompt",
    )(pr3d, pr3d, *wts)


def _rwkv_sample_kernel(pr_ref, prev_ref, mu_ref, rv_ref, wup_ref, aup_ref, gup_ref, bd_ref,
                        cols_ref, s_ref, o_ref, so_ref, vec_scr):
    h = pl.program_id(0)

    @pl.when(h == 0)
    def _():
        r, logw, k2, v, av, bv, g = _rwkv_prep(pr_ref[...], prev_ref[...], mu_ref[...], rv_ref[...],
                                               wup_ref[...], aup_ref[...], gup_ref[...], bd_ref[...])
        for i, x in enumerate((r, jnp.exp(logw), k2, v, av, bv, g)):
            vec_scr[i] = x.T

    rows = pl.ds(pl.multiple_of(h * R_HEAD, R_HEAD), R_HEAD)
    r, w, k, v, a, b, g = (vec_scr[i, rows, :] for i in range(7))
    ys = []
    for vi in range(R_HEAD):
        s = s_ref[0, vi]
        sa = jnp.sum(s * a, axis=0, keepdims=True)
        s2 = s * w + sa * b + v[vi:vi + 1, :] * k
        so_ref[0, vi] = s2
        ys.append(jnp.sum(s2 * r, axis=0, keepdims=True))
    y = jnp.concatenate(ys, axis=0)
    mu = jnp.mean(y, axis=0, keepdims=True)
    d = y - mu
    var = jnp.mean(d * d, axis=0, keepdims=True)
    yn = d * lax.rsqrt(var + LNX_EPS) * cols_ref[1] + cols_ref[2]
    bonus = jnp.sum(r * k * cols_ref[0], axis=0, keepdims=True)
    o_ref[...] = (yn + bonus * v) * g


def _rwkv_sample(pr, prev, state_t, prep_w, cols):
    nb = pr.shape[0]
    args = (pr, prev) + tuple(prep_w)
    head_rows = pl.BlockSpec((R_HEAD, nb), lambda h: (h, 0))
    st_spec = pl.BlockSpec((1, R_HEAD, R_HEAD, nb), lambda h: (h, 0, 0, 0))
    return pl.pallas_call(
        _rwkv_sample_kernel,
        grid=(R_HEADS,),
        in_specs=[_const_spec(a.shape) for a in args]
                 + [pl.BlockSpec((3, R_HEAD, nb), lambda h: (0, h, 0)), st_spec],
        out_specs=[head_rows, st_spec],
        out_shape=[jax.ShapeDtypeStruct((R_WIDTH, nb), F32),
                   jax.ShapeDtypeStruct(state_t.shape, F32)],
        scratch_shapes=[pltpu.VMEM((7, R_WIDTH, nb), F32)],
        compiler_params=pltpu.CompilerParams(dimension_semantics=("arbitrary",),
                                             vmem_limit_bytes=VMEM_LIMIT),
        name="rwkv_sample",
    )(*args, cols, state_t)


STEP_BATCH = 16


def _per_head_rows(x, n):
    return jnp.concatenate([jnp.broadcast_to(x[h:h + 1, :], (n, x.shape[1])) for h in range(x.shape[0])],
                           axis=0)


def _rows_from_columns(col_b, diag, n):
    picked = jnp.where(diag, col_b, 0.0)
    return jnp.concatenate([jnp.sum(picked[h * n:(h + 1) * n], axis=0, keepdims=True)
                            for h in range(col_b.shape[0] // n)], axis=0)


def _shift_rows(x, prev8, s):
    rolled = pltpu.roll(x, s, axis=0)
    top = jnp.where(_iota2((SUBLANES, 1), 0) < s, pltpu.roll(prev8, s, axis=0), rolled[:SUBLANES])
    return jnp.concatenate([top, rolled[SUBLANES:]], axis=0)


def _head_expand(lanes_per_head=M_HEADDIM):
    shape = (DT_PAD, M_HEADS * lanes_per_head)
    head_of_lane = _iota2(shape, 1) >> int(math.log2(lanes_per_head))
    return jnp.where(_iota2(shape, 0) == head_of_lane, 1.0, 0.0).astype(BF16)


def _gated_rmsnorm(y, z, gnorm_w):
    yz = y * _silu(z)
    gw = M_WIDTH // M_GROUPS
    outs = []
    for gi in range(M_GROUPS):
        yg = yz[:, gi * gw:(gi + 1) * gw]
        ms = jnp.mean(yg * yg, axis=-1, keepdims=True)
        outs.append(yg * lax.rsqrt(ms + RMS_EPS))
    return jnp.concatenate(outs, axis=1) * gnorm_w


def _mamba_tile_kernel(z_ref, xbc_ref, xprev_ref, dt_ref, cw_ref, dta_ref, mv_ref,
                       o_ref, hfin_ref, h_scr):
    tile = MAMBA_TILE
    q = SSD_CHUNK
    ti = pl.program_id(1)

    @pl.when(ti == 0)
    def _():
        h_scr[...] = jnp.zeros_like(h_scr)

    xb = xbc_ref[...]
    prev8 = jnp.where(ti == 0, 0.0, xprev_ref[...])
    conv = cw_ref[CONV_K:CONV_K + 1, :] + xb * cw_ref[CONV_K - 1:CONV_K, :]
    for s in range(1, CONV_K):
        conv = conv + _shift_rows(xb, prev8, s) * cw_ref[CONV_K - 1 - s:CONV_K - s, :]
    xact = _silu(conv)
    xs = xact[:, :M_WIDTH]
    gs = M_GROUPS * D_STATE
    bm = xact[:, M_WIDTH:M_WIDTH + gs]
    cm = xact[:, M_WIDTH + gs:]

    dt = _softplus(dt_ref[...] + dta_ref[0:1, :])
    adt = dt * (-jnp.exp(dta_ref[1:2, :]))
    x_dt = xs * _mm_sel_r(dt, _head_expand(), 3)

    sh_q = int(math.log2(q))
    ti_i = _iota2((tile, tile), 0)
    ti_j = _iota2((tile, tile), 1)
    chunk_tril = jnp.where((ti_i >= ti_j) & ((ti_i >> sh_q) == (ti_j >> sh_q)), 1.0, 0.0).astype(BF16)
    cs = _mm_sel_l(chunk_tril, adt, 3)
    cs_t = cs.T
    incl = _iota2((q, q), 0) >= _iota2((q, q), 1)
    lane = _iota2((1, 2 * M_HEADDIM), 1)
    first = lane < M_HEADDIM
    hpg = M_HEADS // M_GROUPS

    state = [h_scr[p] for p in range(M_HEADS // 2)]
    y_rows = []
    for j in range(tile // q):
        rs = slice(j * q, (j + 1) * q)
        y_pairs = []
        for gi in range(M_GROUPS):
            bm_g = bm[rs, gi * D_STATE:(gi + 1) * D_STATE]
            cm_g = cm[rs, gi * D_STATE:(gi + 1) * D_STATE]
            cb = _mm_nt(cm_g, bm_g)
            bm_t = bm_g.T
            for pp in range(hpg // 2):
                p = gi * (hpg // 2) + pp
                x_p = x_dt[rs, 2 * M_HEADDIM * p:2 * M_HEADDIM * (p + 1)]
                y_diag = jnp.zeros((q, 2 * M_HEADDIM), F32)
                contrib = jnp.zeros((D_STATE, 2 * M_HEADDIM), F32)
                scale_in = []
                scale_end = []
                for e in range(2):
                    head = 2 * p + e
                    m = first if e == 0 else jnp.logical_not(first)
                    a_col = cs[rs, head:head + 1]
                    a_row = cs_t[head:head + 1, rs]
                    a_end = cs_t[head:head + 1, (j + 1) * q - 1:(j + 1) * q]
                    lmat = jnp.exp(jnp.where(incl, a_col - a_row, -jnp.inf))
                    x_m = jnp.where(m, x_p, 0.0)
                    y_diag = y_diag + _mm(cb * lmat, x_m)
                    contrib = contrib + _mm(bm_t * jnp.exp(a_end - a_row), x_m)
                    scale_in.append(jnp.exp(a_col))
                    scale_end.append(jnp.exp(a_end))
                y_off = _mm(cm_g, state[p]) * jnp.where(first, scale_in[0], scale_in[1])
                state[p] = state[p] * jnp.where(first, scale_end[0], scale_end[1]) + contrib
                y_pairs.append(y_diag + y_off)
        y_rows.append(jnp.concatenate(y_pairs, axis=1))
    for p in range(M_HEADS // 2):
        h_scr[p] = state[p]

    y = jnp.concatenate(y_rows, axis=0) + mv_ref[0:1, :] * xs
    o_ref[...] = _gated_rmsnorm(y, z_ref[...], mv_ref[1:2, :])

    @pl.when(ti == pl.num_programs(1) - 1)
    def _():
        for p in range(M_HEADS // 2):
            hfin_ref[0, 2 * M_HEADDIM * p:2 * M_HEADDIM * (p + 1), :] = h_scr[p].T


def _mamba_prompt(z2d, xbc2d, dt2d, batch, seq, wts):
    tile = MAMBA_TILE
    ntile = seq // tile
    rows8 = seq // SUBLANES

    def prev_map(b, i):
        return (b * rows8 + jnp.maximum(i * (tile // SUBLANES) - 1, 0), 0)

    tok = lambda w: pl.BlockSpec((tile, w), lambda b, i: (b * ntile + i, 0))
    return pl.pallas_call(
        _mamba_tile_kernel,
        grid=(batch, ntile),
        in_specs=[tok(M_WIDTH), tok(CONV_DIM), pl.BlockSpec((SUBLANES, CONV_DIM), prev_map), tok(DT_PAD)]
                 + [_const_spec(w.shape) for w in wts],
        out_specs=[tok(M_WIDTH),
                   pl.BlockSpec((1, M_HEADS * M_HEADDIM, D_STATE), lambda b, i: (b, 0, 0))],
        out_shape=[jax.ShapeDtypeStruct((batch * seq, M_WIDTH), F32),
                   jax.ShapeDtypeStruct((batch, M_HEADS * M_HEADDIM, D_STATE), F32)],
        scratch_shapes=[pltpu.VMEM((M_HEADS // 2, D_STATE, 2 * M_HEADDIM), F32)],
        compiler_params=pltpu.CompilerParams(dimension_semantics=("arbitrary", "arbitrary"),
                                             vmem_limit_bytes=VMEM_LIMIT),
        name="mamba_prompt",
    )(z2d, xbc2d, xbc2d, dt2d, *wts)


def _mamba_prep_kernel(xbc_ref, cprev_ref, dt_ref, cw_ref, dta_ref,
                       xs_ref, xdt_ref, dec_ref, bm_ref, cm_ref, cnew_ref):
    xbc = xbc_ref[...]
    conv = cw_ref[CONV_K:CONV_K + 1, :] + xbc * cw_ref[CONV_K - 1:CONV_K, :]
    for j in range(CONV_K - 1):
        conv = conv + cprev_ref[j] * cw_ref[j:j + 1, :]
        cnew_ref[j] = cprev_ref[j + 1] if j + 1 < CONV_K - 1 else xbc
    xact = _silu(conv)
    xs = xact[:, :M_WIDTH]
    gs = M_GROUPS * D_STATE
    dt = _softplus(dt_ref[...] + dta_ref[0:1, :])
    adt = dt * (-jnp.exp(dta_ref[1:2, :]))
    expand = _head_expand()
    xs_ref[...] = xs
    xdt_ref[...] = xs * _mm_sel_r(dt, expand, 3)
    dec_ref[...] = jnp.exp(_mm_sel_r(adt, _head_expand(D_STATE), 3))
    hpg = M_HEADS // M_GROUPS
    per_head = lambda m: jnp.concatenate(
        [m[:, (e // hpg) * D_STATE:(e // hpg + 1) * D_STATE] for e in range(M_HEADS)], axis=1)
    bm_ref[...] = per_head(xact[:, M_WIDTH:M_WIDTH + gs])
    cm_ref[...] = per_head(xact[:, M_WIDTH + gs:])


def _mamba_sample_prep(xbc, cprev_t, dt, cw, dta):
    n = xbc.shape[0]
    args = (xbc, cprev_t, dt, cw, dta)
    widths = (M_WIDTH, M_WIDTH, M_HEADS * D_STATE, M_HEADS * D_STATE, M_HEADS * D_STATE)
    out_shapes = [(n, w) for w in widths] + [cprev_t.shape]
    return pl.pallas_call(
        _mamba_prep_kernel,
        grid=(1,),
        in_specs=[_const_spec(a.shape) for a in args],
        out_specs=[_const_spec(s) for s in out_shapes],
        out_shape=[jax.ShapeDtypeStruct(s, F32) for s in out_shapes],
        compiler_params=pltpu.CompilerParams(vmem_limit_bytes=VMEM_LIMIT),
        name="mamba_sample_prep",
    )(*args)


def _mamba_step_kernel(xs_ref, xdt_ref, dec_ref, bm_ref, cm_ref, h_ref, dskip_ref, y_ref, ho_ref):
    n = M_HEADS * M_HEADDIM
    ones_in = jnp.ones((M_HEADDIM, D_STATE), BF16)
    ones_out = jnp.ones((D_STATE, M_HEADDIM), BF16)
    diag = (_iota2((n, M_HEADDIM), 0) & (M_HEADDIM - 1)) == _iota2((n, M_HEADDIM), 1)

    def head_rows(ref, bi):
        return jnp.concatenate(
            [jnp.broadcast_to(ref[bi:bi + 1, e * D_STATE:(e + 1) * D_STATE], (M_HEADDIM, D_STATE))
             for e in range(M_HEADS)], axis=0)

    for bi in range(STEP_BATCH):
        rows = slice(bi * M_HEADS, (bi + 1) * M_HEADS)
        hrows = slice(bi * n, (bi + 1) * n)
        x_col = _mm(jnp.where(diag, _per_head_rows(xdt_ref[rows, :], M_HEADDIM), 0.0), ones_in)
        h2 = h_ref[hrows, :] * head_rows(dec_ref, bi) + x_col * head_rows(bm_ref, bi)
        ho_ref[hrows, :] = h2
        y = _rows_from_columns(_mm(h2 * head_rows(cm_ref, bi), ones_out), diag, M_HEADDIM)
        y_ref[rows, :] = y + dskip_ref[...] * xs_ref[rows, :]


def _mamba_sample_step(xs, xdt, dec, bm, cm, state, dskip_rows):
    nb = state.shape[0]
    rows = STEP_BATCH * M_HEADS
    vec_spec = pl.BlockSpec((rows, M_HEADDIM), lambda i: (i, 0))
    bc_spec = pl.BlockSpec((STEP_BATCH, M_HEADS * D_STATE), lambda i: (i, 0))
    st_spec = pl.BlockSpec((rows * M_HEADDIM, D_STATE), lambda i: (i, 0))
    flat = lambda x, w: x.reshape(nb * M_HEADS, w)
    return pl.pallas_call(
        _mamba_step_kernel,
        grid=(nb // STEP_BATCH,),
        in_specs=[vec_spec] * 2 + [bc_spec] * 3 + [st_spec, _const_spec((M_HEADS, M_HEADDIM))],
        out_specs=[vec_spec, st_spec],
        out_shape=[jax.ShapeDtypeStruct((nb * M_HEADS, M_HEADDIM), F32),
                   jax.ShapeDtypeStruct((nb * M_HEADS * M_HEADDIM, D_STATE), F32)],
        compiler_params=pltpu.CompilerParams(dimension_semantics=("arbitrary",),
                                             vmem_limit_bytes=VMEM_LIMIT),
        name="mamba_sample_step",
    )(flat(xs, M_HEADDIM), flat(xdt, M_HEADDIM), dec, bm, cm,
      state.reshape(nb * M_HEADS * M_HEADDIM, D_STATE), dskip_rows)


FF_SLAB = 2048


def _ffn_math(x, o_r, o_m, wout_ref, lnp_ref, w1_ref, w2_ref):
    mix = _mm(o_r, wout_ref[:R_WIDTH, :]) + _mm(o_m, wout_ref[R_WIDTH:, :])
    h = _layer_norm(ALPHA * x + mix, lnp_ref[0:1, :], lnp_ref[1:2, :], LN_EPS)
    hb = h.astype(BF16)
    ff = jnp.zeros_like(h)
    for j in range(D_FF // FF_SLAB):
        t = jnp.dot(hb, w1_ref[:, j * FF_SLAB:(j + 1) * FF_SLAB], preferred_element_type=F32)
        t = jnp.square(jnp.maximum(t, 0.0))
        ff = ff + jnp.dot(t.astype(BF16), w2_ref[j * FF_SLAB:(j + 1) * FF_SLAB, :],
                          preferred_element_type=F32)
    return _layer_norm(ALPHA * h + ff, lnp_ref[2:3, :], lnp_ref[3:4, :], LN_EPS)


def _ffn_kernel(xp_ref, orp_ref, omp_ref, xs_ref, ors_ref, yms_ref, zs_ref, mv_ref,
                wout_ref, w1_ref, w2_ref, lnp_ref, yp_ref, ys_ref):
    i = pl.program_id(0)
    last = pl.num_programs(0) - 1

    @pl.when(i < last)
    def _():
        yp_ref[...] = _ffn_math(xp_ref[...], orp_ref[...], omp_ref[...], wout_ref, lnp_ref, w1_ref, w2_ref)

    @pl.when(i == last)
    def _():
        o_m = _gated_rmsnorm(yms_ref[...], zs_ref[...], mv_ref[1:2, :])
        ys_ref[...] = _ffn_math(xs_ref[...], ors_ref[...], o_m, wout_ref, lnp_ref, w1_ref, w2_ref)


def _ffn(prompt, sample, mv, wout, w1, w2, lnp, tm):
    n = prompt[0].shape[0]
    ns = sample[0].shape[0]
    ntile = n // tm
    tok = lambda w: pl.BlockSpec((tm, w), lambda i: (jnp.minimum(i, ntile - 1), 0))
    whole = lambda shape: pl.BlockSpec(shape, lambda i: (0, 0), pipeline_mode=pl.Buffered(1))
    consts = list(sample) + [mv, wout, w1, w2, lnp]
    return pl.pallas_call(
        _ffn_kernel,
        grid=(ntile + 1,),
        in_specs=[tok(D_MODEL), tok(R_WIDTH), tok(M_WIDTH)] + [whole(a.shape) for a in consts],
        out_specs=[tok(D_MODEL), whole((ns, D_MODEL))],
        out_shape=[jax.ShapeDtypeStruct((n, D_MODEL), F32), jax.ShapeDtypeStruct((ns, D_MODEL), F32)],
        compiler_params=pltpu.CompilerParams(dimension_semantics=("arbitrary",),
                                             vmem_limit_bytes=VMEM_LIMIT),
        name="ffn",
    )(*prompt, *consts)


def _block_diag_ones():
    idx = np.arange(R_WIDTH) // R_HEAD
    return jnp.asarray(idx[:, None] == idx[None, :], dtype=BF16)


def kernel(x_prompt, x_sample, state_shift, state_wkv, state_conv, state_ssm, w_in, mu_shift, w0, w_up, a0, a_up, g_up, k_k, k_a, r_k, lnx_w, lnx_b, conv_w, conv_b, dt_bias, a_log, d_skip, gnorm_w, w_out, ln1_g, ln1_b, w_ff1, w_ff2, ln2_g, ln2_b):
    assert w_in.shape[0] == DEPTH and x_sample.shape[1] == 1
    batch, seq, _ = x_prompt.shape
    nb = x_sample.shape[0]
    assert seq % MAMBA_TILE == 0 and seq % RWKV_TILE == 0 and seq % DENSE_ROWS == 0
    assert nb % STEP_BATCH == 0 and batch % RWKV_ROWS == 0
    row = lambda a: a.reshape(1, -1)

    w_t = jnp.transpose(w_in[0])
    w_dt = w_t[PROJ_MAIN:]
    zl = jnp.zeros((D_DECAY_LORA, R_WIDTH), F32)
    wup_p = jnp.concatenate([w_up[0], zl], axis=0).astype(BF16)
    aup_p = jnp.concatenate([zl, a_up[0]], axis=0).astype(BF16)
    bd = _block_diag_ones()

    def rows8(*vecs):
        m = jnp.stack([v.reshape(-1) for v in vecs])
        return jnp.pad(m, ((0, SUBLANES - len(vecs)), (0, 0)))

    rv = rows8(w0[0], a0[0], k_k[0], k_a[0], r_k[0], lnx_w[0], lnx_b[0])
    prep_w = (row(mu_shift[0]), rv, wup_p, aup_p, g_up[0].astype(BF16), bd)
    cw = rows8(*conv_w[0], conv_b[0])
    dta = rows8(jnp.pad(dt_bias[0], (0, DT_PAD - M_HEADS)), jnp.pad(a_log[0], (0, DT_PAD - M_HEADS)))
    mv = rows8(jnp.repeat(d_skip[0], M_HEADDIM), gnorm_w[0])
    dskip_rows = jnp.broadcast_to(d_skip[0][:, None], (M_HEADS, M_HEADDIM))
    lnp = rows8(ln1_g[0], ln1_b[0], ln2_g[0], ln2_b[0])

    xp = x_prompt.reshape(batch * seq, D_MODEL)
    xs2d = x_sample.reshape(nb, D_MODEL)
    (pr_p, z_p, xbc_p, dt_p, pr_s, z_s, xbc_s, dt_s, wout_bf, w1_bf, w2_bf) = _proj(
        xp, xs2d, w_t, w_dt, (w_out[0], w_ff1[0], w_ff2[0]), DENSE_ROWS)

    o_r_p, wkv_p = _rwkv_prompt(pr_p.reshape(batch, seq, RWKV_PROJ), prep_w)
    o_r_p = o_r_p.reshape(batch * seq, R_WIDTH)
    o_m_p, ssm_p = _mamba_prompt(z_p, xbc_p, dt_p, batch, seq, (cw, dta, mv))
    shift_p = pr_p.reshape(batch, seq, RWKV_PROJ)[:, -1]
    conv_p = xbc_p.reshape(batch, seq, CONV_DIM)[:, seq - (CONV_K - 1):]

    cols = jnp.broadcast_to(jnp.stack([r_k[0].reshape(-1), lnx_w[0], lnx_b[0]])[:, :, None],
                            (3, R_WIDTH, nb))
    o_r_t, wkv_t = _rwkv_sample(pr_s, state_shift[0], jnp.transpose(state_wkv[0], (1, 2, 3, 0)),
                                prep_w, cols)
    o_r_s = o_r_t.T
    wkv_s = jnp.transpose(wkv_t, (3, 0, 1, 2))
    xs_s, xdt_s, dec_s, bm_s, cm_s, conv_t = _mamba_sample_prep(
        xbc_s, jnp.transpose(state_conv[0], (1, 0, 2)), dt_s, cw, dta)
    conv_s = jnp.transpose(conv_t, (1, 0, 2))
    y_m_s, ssm_s = _mamba_sample_step(xs_s, xdt_s, dec_s, bm_s, cm_s, state_ssm[0], dskip_rows)

    y_p, y_s = _ffn((xp, o_r_p, o_m_p), (xs2d, o_r_s, y_m_s.reshape(nb, M_WIDTH), z_s), mv,
                    wout_bf, w1_bf, w2_bf, lnp, tm=DENSE_ROWS)

    return (y_p.reshape(batch, seq, D_MODEL), y_s.reshape(nb, 1, D_MODEL),
            shift_p[None], wkv_p[None], conv_p[None],
            ssm_p.reshape(1, batch, M_HEADS, M_HEADDIM, D_STATE),
            pr_s[None], wkv_s.reshape(1, nb, R_HEADS, R_HEAD, R_HEAD), conv_s[None],
            ssm_s.reshape(1, nb, M_HEADS, M_HEADDIM, D_STATE))
```

```python
import math

import numpy as np
import jax
import jax.numpy as jnp
from jax import lax
from jax.experimental import pallas as pl
from jax.experimental.pallas import tpu as pltpu

F32 = jnp.float32
BF16 = jnp.bfloat16

D_MODEL = 1024
R_WIDTH = 512
R_HEAD = 64
R_HEADS = 8
D_DECAY_LORA = 64
D_AAA_LORA = 64
D_GATE_LORA = 128
M_WIDTH = 512
M_HEADDIM = 64
M_HEADS = 8
M_GROUPS = 2
D_STATE = 128
CONV_K = 4
CONV_DIM = M_WIDTH + 2 * M_GROUPS * D_STATE
D_FF = 4 * D_MODEL
RWKV_PROJ = 3 * R_WIDTH + D_DECAY_LORA + D_AAA_LORA + D_GATE_LORA
DT_PAD = 128
PROJ_MAIN = RWKV_PROJ + M_WIDTH + CONV_DIM
DEPTH = 1
ALPHA = (2 * DEPTH) ** 0.25
LN_EPS = 1e-5
LNX_EPS = 64e-5
RMS_EPS = 1e-5

RWKV_CHUNK = 64
RWKV_TILE = 256
RWKV_ROWS = 4
RWKV_WAVES = 2
SSD_CHUNK = 128
MAMBA_TILE = 256
SUBLANES = 8
DENSE_ROWS = 512
VMEM_LIMIT = 48 * 1024 * 1024
PROJ_W_VMEM_LIMIT = 56 * 1024 * 1024

_NT = (((1,), (1,)), ((), ()))


def _mm(a, b):
    return jnp.dot(a.astype(BF16), b.astype(BF16), preferred_element_type=F32)


def _mm_nt(a, b):
    return lax.dot_general(a.astype(BF16), b.astype(BF16), _NT, preferred_element_type=F32)


def _split(x, passes):
    parts = []
    for _ in range(passes):
        p = x.astype(BF16)
        parts.append(p)
        x = x - p.astype(F32)
    return parts


def _mm_sel_r(x, sel, passes):
    return sum(jnp.dot(p, sel, preferred_element_type=F32) for p in _split(x, passes))


def _mm_sel_l(sel, x, passes):
    return sum(jnp.dot(sel, p, preferred_element_type=F32) for p in _split(x, passes))


def _softplus(x):
    return jnp.maximum(x, 0.0) + jnp.log(1.0 + jnp.exp(-jnp.abs(x)))


def _sigmoid(x):
    return 0.5 + 0.5 * jnp.tanh(0.5 * x)


def _silu(x):
    return x * _sigmoid(x)


def _layer_norm(x, g, b, eps):
    mu = jnp.mean(x, axis=-1, keepdims=True)
    d = x - mu
    var = jnp.mean(d * d, axis=-1, keepdims=True)
    return d * lax.rsqrt(var + eps) * g + b


def _iota2(shape, axis):
    return lax.broadcasted_iota(jnp.int32, shape, axis)


def _dt_columns(xb, wdt_t):
    pad = jnp.zeros((DT_PAD - M_HEADS, D_MODEL), F32)
    return _mm_nt(xb, jnp.concatenate([wdt_t, pad], axis=0))


W_SLAB = 256


def _proj_kernel(xp_ref, xs_ref, wt_ref, wdt_ref, wout32_ref, w1_32_ref, w2_32_ref,
                 prp_ref, zp_ref, xbcp_ref, dtp_ref, prs_ref, zs_ref, xbcs_ref, dts_ref,
                 wout_ref, w1_ref, w2_ref, wbf_scr):
    i = pl.program_id(0)
    last = pl.num_programs(0) - 1

    @pl.when(i == 0)
    def _():
        for c in range(0, PROJ_MAIN, W_SLAB):
            wbf_scr[:, c:c + W_SLAB] = wt_ref[c:c + W_SLAB, :].T.astype(BF16)

    def project(x_ref, pr_ref, z_ref, xbc_ref, dt_ref):
        xb = x_ref[...].astype(BF16)
        res = jnp.dot(xb, wbf_scr[...], preferred_element_type=F32)
        c0 = RWKV_PROJ
        c1 = c0 + M_WIDTH
        pr_ref[...] = res[:, :c0]
        z_ref[...] = res[:, c0:c1]
        xbc_ref[...] = res[:, c1:]
        dt_ref[...] = _dt_columns(xb, wdt_ref[...])

    @pl.when(i < last)
    def _():
        project(xp_ref, prp_ref, zp_ref, xbcp_ref, dtp_ref)
        wout_ref[...] = wout32_ref[...].astype(BF16)
        w1_ref[...] = w1_32_ref[...].astype(BF16)
        w2_ref[...] = w2_32_ref[...].astype(BF16)

    @pl.when(i == last)
    def _():
        project(xs_ref, prs_ref, zs_ref, xbcs_ref, dts_ref)


def _proj(xp2d, xs2d, w_t, wdt_t, mlp_w, tm):
    n = xp2d.shape[0]
    ns = xs2d.shape[0]
    ntile = n // tm
    widths = (RWKV_PROJ, M_WIDTH, CONV_DIM, DT_PAD)
    once = lambda shape: pl.BlockSpec(shape, lambda i: (0, 0), pipeline_mode=pl.Buffered(1))
    tile = lambda w: pl.BlockSpec((tm, w), lambda i: (jnp.minimum(i, ntile - 1), 0))
    slab = lambda a: pl.BlockSpec((a.shape[0] // ntile, a.shape[1]), lambda i: (jnp.minimum(i, ntile - 1), 0))
    return pl.pallas_call(
        _proj_kernel,
        grid=(ntile + 1,),
        in_specs=[tile(D_MODEL), once(xs2d.shape), once(w_t.shape), once(wdt_t.shape)]
                 + [slab(w) for w in mlp_w],
        out_specs=[tile(w) for w in widths] + [once((ns, w)) for w in widths] + [slab(w) for w in mlp_w],
        out_shape=[jax.ShapeDtypeStruct((n, w), F32) for w in widths]
                  + [jax.ShapeDtypeStruct((ns, w), F32) for w in widths]
                  + [jax.ShapeDtypeStruct(w.shape, BF16) for w in mlp_w],
        scratch_shapes=[pltpu.VMEM((D_MODEL, PROJ_MAIN), BF16)],
        compiler_params=pltpu.CompilerParams(dimension_semantics=("arbitrary",),
                                             vmem_limit_bytes=PROJ_W_VMEM_LIMIT),
        name="proj",
    )(xp2d, xs2d, w_t, wdt_t, *mlp_w)


def _rwkv_prep(pr, prev, mu, rv, wup, aup, gup, bd):
    w0, a0, k_k, k_a = (rv[i:i + 1, :] for i in range(4))
    u = pr + (prev - pr) * mu
    r = u[:, 0:R_WIDTH]
    k = u[:, R_WIDTH:2 * R_WIDTH]
    v = u[:, 2 * R_WIDTH:3 * R_WIDTH]
    lora = u[:, 3 * R_WIDTH:3 * R_WIDTH + D_DECAY_LORA + D_AAA_LORA]
    ug = u[:, 3 * R_WIDTH + D_DECAY_LORA + D_AAA_LORA:]
    w_log = -_softplus(-(w0 + _mm(jnp.tanh(lora), wup))) - 0.5
    logw = -jnp.exp(w_log)
    a = _sigmoid(a0 + _mm(lora, aup))
    g = _mm(_sigmoid(ug), gup)
    kk = k * k_k
    ss = _mm_sel_r(kk * kk, bd, 1)
    kkn = kk * lax.rsqrt(jnp.maximum(ss, 1e-24))
    k2 = k * (1.0 + (a - 1.0) * k_a)
    return r, logw, k2, v, -kkn, kkn * a, g


def _rwkv_post(y, r, k2, v, g, rv, bd):
    r_k, lnx_w, lnx_b = (rv[i:i + 1, :] for i in range(4, 7))
    inv = 1.0 / R_HEAD
    mu = _mm_sel_r(y, bd, 1) * inv
    d = y - mu
    var = _mm_sel_r(d * d, bd, 1) * inv
    yn = d * lax.rsqrt(var + LNX_EPS) * lnx_w + lnx_b
    bonus = _mm_sel_r(r * k2 * r_k, bd, 1)
    return (yn + bonus * v) * g


def _rwkv_tile_kernel(pr_ref, prevblk_ref, mu_ref, rv_ref, wup_ref, aup_ref, gup_ref, bd_ref,
                      o_ref, sfin_ref, g_scr):
    c = RWKV_CHUNK
    tile = RWKV_TILE
    nseq = RWKV_ROWS
    nck = tile // c
    hw = 2 * R_HEAD
    npair = R_HEADS // 2
    ti = pl.program_id(1)

    @pl.when(ti == 0)
    def _():
        g_scr[...] = jnp.zeros_like(g_scr)

    pr = jnp.concatenate([pr_ref[s] for s in range(nseq)], axis=0)
    prev = pltpu.roll(pr, 1, axis=0)
    row_id = _iota2((nseq * tile, 1), 0)
    for s in range(nseq):
        prev_row = jnp.where(ti == 0, 0.0, prevblk_ref[s, SUBLANES - 1:SUBLANES, :])
        prev = jnp.where(row_id == s * tile, prev_row, prev)
    head_ones = bd_ref[...]
    r, logw, k2, v, av, bv, g = _rwkv_prep(pr, prev, mu_ref[...], rv_ref[...], wup_ref[...],
                                           aup_ref[...], gup_ref[...], head_ones)

    sh_c = int(math.log2(c))
    ti_i = _iota2((tile, tile), 0)
    ti_j = _iota2((tile, tile), 1)
    chunk_tril = jnp.where((ti_i >= ti_j) & ((ti_i >> sh_c) == (ti_j >> sh_c)), 1.0, 0.0).astype(BF16)
    cum = jnp.concatenate([_mm_sel_l(chunk_tril, logw[s * tile:(s + 1) * tile], 3) for s in range(nseq)],
                          axis=0)
    cl = jnp.concatenate(
        [jnp.broadcast_to(cum[(q + 1) * c - 1:(q + 1) * c, :], (c, R_WIDTH)) for q in range(nseq * nck)],
        axis=0)
    p_out = jnp.exp(-cum)
    p_end = jnp.exp(cl - cum)
    rt = r * jnp.exp(cum)
    at = av * jnp.exp(cum - logw)
    bt = bv * p_out
    kt = k2 * p_out
    bh = bv * p_end
    kh = k2 * p_end
    pc = jnp.exp(cl)

    ii = _iota2((c, hw), 0)
    jj = _iota2((c, hw), 1) & (c - 1)
    strict = ii > jj
    incl = ii >= jj
    lane = _iota2((1, hw), 1)
    keep = (jnp.where(lane < R_HEAD, 1.0, 0.0).astype(BF16), jnp.where(lane >= R_HEAD, 1.0, 0.0).astype(BF16))
    e_r = _iota2((hw, hw), 0)
    e_c = _iota2((hw, hw), 1)
    eye_hw = e_r == e_c
    blockdiag = (e_r >= R_HEAD) == (e_c >= R_HEAD)

    def bd(x):
        xb = x.astype(BF16)
        return jnp.concatenate([xb * keep[0], xb * keep[1]], axis=0)

    def pdot(a, b_bd):
        return jnp.dot(a.astype(BF16), b_bd, preferred_element_type=F32)

    blk = lambda x, q, p: x[q * c:(q + 1) * c, hw * p:hw * (p + 1)]
    zeros_c = jnp.zeros((c, hw), F32)
    eye = jnp.where(ii == jj, 1.0, 0.0)

    def phase1(qp):
        lhs = [jnp.concatenate([blk(at, *k), blk(rt, *k)], axis=0).astype(BF16) for k in qp]
        sc = [lax.dot_general(x, jnp.concatenate([bd(blk(bt, *k)), bd(blk(kt, *k))], axis=0), _NT,
                              preferred_element_type=F32) for x, k in zip(lhs, qp)]
        a_ab = [jnp.where(strict, s[:c, :hw], 0.0) for s in sc]
        a_rb = [jnp.where(incl, s[c:, :hw], 0.0) for s in sc]
        a_akrk = [jnp.concatenate([jnp.where(strict, s[:c, hw:], 0.0), jnp.where(incl, s[c:, hw:], 0.0)], axis=0)
                  for s in sc]
        v_bd = [bd(blk(v, *k)) for k in qp]
        akrk_v = [pdot(a, x) for a, x in zip(a_akrk, v_bd)]

        sh = 4
        npow = [jnp.where((ii >> sh) == (jj >> sh), n, 0.0) for n in a_ab]
        t_inv = [eye + n for n in npow]
        npow = [pdot(n, bd(n)) for n in npow]
        for _ in range(sh - 2):
            both = [pdot(jnp.concatenate([t, n], axis=0), bd(n)) for t, n in zip(t_inv, npow)]
            t_inv = [t + x[:c] for t, x in zip(t_inv, both)]
            npow = [x[c:] for x in both]
        t_inv = [t + pdot(t, bd(n)) for t, n in zip(t_inv, npow)]
        while (1 << sh) < c:
            off = ((ii >> (sh + 1)) == (jj >> (sh + 1))) & (((ii >> sh) & 1) > ((jj >> sh) & 1))
            tn = [pdot(t, bd(jnp.where(off, n, 0.0))) for t, n in zip(t_inv, a_ab)]
            t_inv = [t + pdot(x, bd(t)) for t, x in zip(t_inv, tn)]
            sh += 1

        tx = [pdot(t, jnp.concatenate([bd(blk(at, *k)), bd(x[:c])], axis=1))
              for t, x, k in zip(t_inv, akrk_v, qp)]
        rbx = [pdot(a, jnp.concatenate([bd(x[:, :hw]), bd(x[:, hw:])], axis=1)) for a, x in zip(a_rb, tx)]
        rp_pair = {k: blk(rt, *k) + x[:, :hw] for k, x in zip(qp, rbx)}
        y0_pair = {k: x[:, hw:] + y[c:] for k, x, y in zip(qp, rbx, akrk_v)}

        wt = {k: jnp.concatenate([blk(bh, *k), blk(kh, *k)], axis=0).T for k in qp}
        trans = {k: _mm(wt[k], jnp.concatenate([x, jnp.concatenate([zeros_c, blk(v, *k)], axis=1)], axis=0))
                 for k, x in zip(qp, tx)}
        m_ba = {k: jnp.where(blockdiag, trans[k][:, :hw], 0.0) for k in qp}
        h0c = {k: jnp.where(blockdiag, trans[k][:, hw:], 0.0) for k in qp}
        pc_col = {k: jnp.sum(jnp.where(eye_hw, blk(pc, *k)[:1], 0.0), axis=1, keepdims=True) for k in qp}
        return rp_pair, y0_pair, m_ba, h0c, pc_col

    rp_pair, y0_pair, m_ba, h0c, pc_col = {}, {}, {}, {}, {}
    all_qp = [(q, p) for q in range(nseq * nck) for p in range(npair)]
    wave = len(all_qp) // RWKV_WAVES
    for w0 in range(0, len(all_qp), wave):
        for dst, src in zip((rp_pair, y0_pair, m_ba, h0c, pc_col), phase1(all_qp[w0:w0 + wave])):
            dst.update(src)


    state = [g_scr[i] for i in range(nseq * npair)]
    y_rows = [None] * (nseq * nck)
    for q in range(nck):
        for s in range(nseq):
            qq = s * nck + q
            st = state[s * npair:(s + 1) * npair]
            on_st = [_mm(jnp.concatenate([rp_pair[qq, p], m_ba[qq, p]], axis=0), st[p]) for p in range(npair)]
            y_rows[qq] = jnp.concatenate([on_st[p][:c] + y0_pair[qq, p] for p in range(npair)], axis=1)
            state[s * npair:(s + 1) * npair] = [
                pc_col[qq, p] * st[p] + on_st[p][c:] + h0c[qq, p] for p in range(npair)]
    for i in range(nseq * npair):
        g_scr[i] = state[i]

    y = jnp.concatenate(y_rows, axis=0)
    out = _rwkv_post(y, r, k2, v, g, rv_ref[...], head_ones)
    for s in range(nseq):
        o_ref[s] = out[s * tile:(s + 1) * tile]

    @pl.when(ti == pl.num_programs(1) - 1)
    def _():
        for s in range(nseq):
            for p in range(npair):
                gt = g_scr[s * npair + p].T
                sfin_ref[s, 2 * p] = gt[:R_HEAD, :R_HEAD]
                sfin_ref[s, 2 * p + 1] = pltpu.roll(gt, R_HEAD, axis=1)[R_HEAD:, :R_HEAD]


def _const_spec(shape):
    return pl.BlockSpec(shape, lambda *_: (0,) * len(shape))


def _rwkv_prompt(pr3d, wts):
    batch, seq, _ = pr3d.shape
    tile = RWKV_TILE
    nseq = RWKV_ROWS

    def prev_map(b, i):
        return (b, jnp.maximum(i * (tile // SUBLANES) - 1, 0), 0)

    w_specs = [_const_spec(w.shape) for w in wts]
    return pl.pallas_call(
        _rwkv_tile_kernel,
        grid=(batch // nseq, seq // tile),
        in_specs=[pl.BlockSpec((nseq, tile, RWKV_PROJ), lambda b, i: (b, i, 0)),
                  pl.BlockSpec((nseq, SUBLANES, RWKV_PROJ), prev_map)] + w_specs,
        out_specs=[pl.BlockSpec((nseq, tile, R_WIDTH), lambda b, i: (b, i, 0)),
                   pl.BlockSpec((nseq, R_HEADS, R_HEAD, R_HEAD), lambda b, i: (b, 0, 0, 0))],
        out_shape=[jax.ShapeDtypeStruct((batch, seq, R_WIDTH), F32),
                   jax.ShapeDtypeStruct((batch, R_HEADS, R_HEAD, R_HEAD), F32)],
        scratch_shapes=[pltpu.VMEM((nseq * (R_HEADS // 2), 2 * R_HEAD, 2 * R_HEAD), F32)],
        compiler_params=pltpu.CompilerParams(dimension_semantics=("arbitrary", "arbitrary"),
                                             vmem_limit_bytes=VMEM_LIMIT),
        name="rwkv_prompt",
    )(pr3d, pr3d, *wts)


def _rwkv_sample_kernel(pr_ref, prev_ref, mu_ref, rv_ref, wup_ref, aup_ref, gup_ref, bd_ref,
                        cols_ref, s_ref, o_ref, so_ref, vec_scr):
    h = pl.program_id(0)

    @pl.when(h == 0)
    def _():
        r, logw, k2, v, av, bv, g = _rwkv_prep(pr_ref[...], prev_ref[...], mu_ref[...], rv_ref[...],
                                               wup_ref[...], aup_ref[...], gup_ref[...], bd_ref[...])
        for i, x in enumerate((r, jnp.exp(logw), k2, v, av, bv, g)):
            vec_scr[i] = x.T

    rows = pl.ds(pl.multiple_of(h * R_HEAD, R_HEAD), R_HEAD)
    r, w, k, v, a, b, g = (vec_scr[i, rows, :] for i in range(7))
    ys = []
    for vi in range(R_HEAD):
        s = s_ref[0, vi]
        sa = jnp.sum(s * a, axis=0, keepdims=True)
        s2 = s * w + sa * b + v[vi:vi + 1, :] * k
        so_ref[0, vi] = s2
        ys.append(jnp.sum(s2 * r, axis=0, keepdims=True))
    y = jnp.concatenate(ys, axis=0)
    mu = jnp.mean(y, axis=0, keepdims=True)
    d = y - mu
    var = jnp.mean(d * d, axis=0, keepdims=True)
    yn = d * lax.rsqrt(var + LNX_EPS) * cols_ref[1] + cols_ref[2]
    bonus = jnp.sum(r * k * cols_ref[0], axis=0, keepdims=True)
    o_ref[...] = (yn + bonus * v) * g


def _rwkv_sample(pr, prev, state_t, prep_w, cols):
    nb = pr.shape[0]
    args = (pr, prev) + tuple(prep_w)
    head_rows = pl.BlockSpec((R_HEAD, nb), lambda h: (h, 0))
    st_spec = pl.BlockSpec((1, R_HEAD, R_HEAD, nb), lambda h: (h, 0, 0, 0))
    return pl.pallas_call(
        _rwkv_sample_kernel,
        grid=(R_HEADS,),
        in_specs=[_const_spec(a.shape) for a in args]
                 + [pl.BlockSpec((3, R_HEAD, nb), lambda h: (0, h, 0)), st_spec],
        out_specs=[head_rows, st_spec],
        out_shape=[jax.ShapeDtypeStruct((R_WIDTH, nb), F32),
                   jax.ShapeDtypeStruct(state_t.shape, F32)],
        scratch_shapes=[pltpu.VMEM((7, R_WIDTH, nb), F32)],
        compiler_params=pltpu.CompilerParams(dimension_semantics=("arbitrary",),
                                             vmem_limit_bytes=VMEM_LIMIT),
        name="rwkv_sample",
    )(*args, cols, state_t)


STEP_BATCH = 16


def _per_head_rows(x, n):
    return jnp.concatenate([jnp.broadcast_to(x[h:h + 1, :], (n, x.shape[1])) for h in range(x.shape[0])],
                           axis=0)


def _rows_from_columns(col_b, diag, n):
    picked = jnp.where(diag, col_b, 0.0)
    return jnp.concatenate([jnp.sum(picked[h * n:(h + 1) * n], axis=0, keepdims=True)
                            for h in range(col_b.shape[0] // n)], axis=0)


def _shift_rows(x, prev8, s):
    rolled = pltpu.roll(x, s, axis=0)
    top = jnp.where(_iota2((SUBLANES, 1), 0) < s, pltpu.roll(prev8, s, axis=0), rolled[:SUBLANES])
    return jnp.concatenate([top, rolled[SUBLANES:]], axis=0)


def _head_expand(lanes_per_head=M_HEADDIM):
    shape = (DT_PAD, M_HEADS * lanes_per_head)
    head_of_lane = _iota2(shape, 1) >> int(math.log2(lanes_per_head))
    return jnp.where(_iota2(shape, 0) == head_of_lane, 1.0, 0.0).astype(BF16)


def _gated_rmsnorm(y, z, gnorm_w):
    yz = y * _silu(z)
    gw = M_WIDTH // M_GROUPS
    outs = []
    for gi in range(M_GROUPS):
        yg = yz[:, gi * gw:(gi + 1) * gw]
        ms = jnp.mean(yg * yg, axis=-1, keepdims=True)
        outs.append(yg * lax.rsqrt(ms + RMS_EPS))
    return jnp.concatenate(outs, axis=1) * gnorm_w


def _mamba_tile_kernel(z_ref, xbc_ref, xprev_ref, dt_ref, cw_ref, dta_ref, mv_ref,
                       o_ref, hfin_ref, h_scr):
    tile = MAMBA_TILE
    q = SSD_CHUNK
    ti = pl.program_id(1)

    @pl.when(ti == 0)
    def _():
        h_scr[...] = jnp.zeros_like(h_scr)

    xb = xbc_ref[...]
    prev8 = jnp.where(ti == 0, 0.0, xprev_ref[...])
    conv = cw_ref[CONV_K:CONV_K + 1, :] + xb * cw_ref[CONV_K - 1:CONV_K, :]
    for s in range(1, CONV_K):
        conv = conv + _shift_rows(xb, prev8, s) * cw_ref[CONV_K - 1 - s:CONV_K - s, :]
    xact = _silu(conv)
    xs = xact[:, :M_WIDTH]
    gs = M_GROUPS * D_STATE
    bm = xact[:, M_WIDTH:M_WIDTH + gs]
    cm = xact[:, M_WIDTH + gs:]

    dt = _softplus(dt_ref[...] + dta_ref[0:1, :])
    adt = dt * (-jnp.exp(dta_ref[1:2, :]))
    x_dt = xs * _mm_sel_r(dt, _head_expand(), 3)

    sh_q = int(math.log2(q))
    ti_i = _iota2((tile, tile), 0)
    ti_j = _iota2((tile, tile), 1)
    chunk_tril = jnp.where((ti_i >= ti_j) & ((ti_i >> sh_q) == (ti_j >> sh_q)), 1.0, 0.0).astype(BF16)
    cs = _mm_sel_l(chunk_tril, adt, 3)
    cs_t = cs.T
    incl = _iota2((q, q), 0) >= _iota2((q, q), 1)
    lane = _iota2((1, 2 * M_HEADDIM), 1)
    first = lane < M_HEADDIM
    hpg = M_HEADS // M_GROUPS

    state = [h_scr[p] for p in range(M_HEADS // 2)]
    y_rows = []
    for j in range(tile // q):
        rs = slice(j * q, (j + 1) * q)
        y_pairs = []
        for gi in range(M_GROUPS):
            bm_g = bm[rs, gi * D_STATE:(gi + 1) * D_STATE]
            cm_g = cm[rs, gi * D_STATE:(gi + 1) * D_STATE]
            cb = _mm_nt(cm_g, bm_g)
            bm_t = bm_g.T
            for pp in range(hpg // 2):
                p = gi * (hpg // 2) + pp
                x_p = x_dt[rs, 2 * M_HEADDIM * p:2 * M_HEADDIM * (p + 1)]
                y_diag = jnp.zeros((q, 2 * M_HEADDIM), F32)
                contrib = jnp.zeros((D_STATE, 2 * M_HEADDIM), F32)
                scale_in = []
                scale_end = []
                for e in range(2):
                    head = 2 * p + e
                    m = first if e == 0 else jnp.logical_not(first)
                    a_col = cs[rs, head:head + 1]
                    a_row = cs_t[head:head + 1, rs]
                    a_end = cs_t[head:head + 1, (j + 1) * q - 1:(j + 1) * q]
                    lmat = jnp.exp(jnp.where(incl, a_col - a_row, -jnp.inf))
                    x_m = jnp.where(m, x_p, 0.0)
                    y_diag = y_diag + _mm(cb * lmat, x_m)
                    contrib = contrib + _mm(bm_t * jnp.exp(a_end - a_row), x_m)
                    scale_in.append(jnp.exp(a_col))
                    scale_end.append(jnp.exp(a_end))
                y_off = _mm(cm_g, state[p]) * jnp.where(first, scale_in[0], scale_in[1])
                state[p] = state[p] * jnp.where(first, scale_end[0], scale_end[1]) + contrib
                y_pairs.append(y_diag + y_off)
        y_rows.append(jnp.concatenate(y_pairs, axis=1))
    for p in range(M_HEADS // 2):
        h_scr[p] = state[p]

    y = jnp.concatenate(y_rows, axis=0) + mv_ref[0:1, :] * xs
    o_ref[...] = _gated_rmsnorm(y, z_ref[...], mv_ref[1:2, :])

    @pl.when(ti == pl.num_programs(1) - 1)
    def _():
        for p in range(M_HEADS // 2):
            hfin_ref[0, 2 * M_HEADDIM * p:2 * M_HEADDIM * (p + 1), :] = h_scr[p].T


def _mamba_prompt(z2d, xbc2d, dt2d, batch, seq, wts):
    tile = MAMBA_TILE
    ntile = seq // tile
    rows8 = seq // SUBLANES

    def prev_map(b, i):
        return (b * rows8 + jnp.maximum(i * (tile // SUBLANES) - 1, 0), 0)

    tok = lambda w: pl.BlockSpec((tile, w), lambda b, i: (b * ntile + i, 0))
    return pl.pallas_call(
        _mamba_tile_kernel,
        grid=(batch, ntile),
        in_specs=[tok(M_WIDTH), tok(CONV_DIM), pl.BlockSpec((SUBLANES, CONV_DIM), prev_map), tok(DT_PAD)]
                 + [_const_spec(w.shape) for w in wts],
        out_specs=[tok(M_WIDTH),
                   pl.BlockSpec((1, M_HEADS * M_HEADDIM, D_STATE), lambda b, i: (b, 0, 0))],
        out_shape=[jax.ShapeDtypeStruct((batch * seq, M_WIDTH), F32),
                   jax.ShapeDtypeStruct((batch, M_HEADS * M_HEADDIM, D_STATE), F32)],
        scratch_shapes=[pltpu.VMEM((M_HEADS // 2, D_STATE, 2 * M_HEADDIM), F32)],
        compiler_params=pltpu.CompilerParams(dimension_semantics=("arbitrary", "arbitrary"),
                                             vmem_limit_bytes=VMEM_LIMIT),
        name="mamba_prompt",
    )(z2d, xbc2d, xbc2d, dt2d, *wts)


def _mamba_prep_kernel(xbc_ref, cprev_ref, dt_ref, cw_ref, dta_ref,
                       xs_ref, xdt_ref, dec_ref, bm_ref, cm_ref, cnew_ref):
    xbc = xbc_ref[...]
    conv = cw_ref[CONV_K:CONV_K + 1, :] + xbc * cw_ref[CONV_K - 1:CONV_K, :]
    for j in range(CONV_K - 1):
        conv = conv + cprev_ref[j] * cw_ref[j:j + 1, :]
        cnew_ref[j] = cprev_ref[j + 1] if j + 1 < CONV_K - 1 else xbc
    xact = _silu(conv)
    xs = xact[:, :M_WIDTH]
    gs = M_GROUPS * D_STATE
    dt = _softplus(dt_ref[...] + dta_ref[0:1, :])
    adt = dt * (-jnp.exp(dta_ref[1:2, :]))
    expand = _head_expand()
    xs_ref[...] = xs
    xdt_ref[...] = xs * _mm_sel_r(dt, expand, 3)
    dec_ref[...] = jnp.exp(_mm_sel_r(adt, _head_expand(D_STATE), 3))
    hpg = M_HEADS // M_GROUPS
    per_head = lambda m: jnp.concatenate(
        [m[:, (e // hpg) * D_STATE:(e // hpg + 1) * D_STATE] for e in range(M_HEADS)], axis=1)
    bm_ref[...] = per_head(xact[:, M_WIDTH:M_WIDTH + gs])
    cm_ref[...] = per_head(xact[:, M_WIDTH + gs:])


def _mamba_sample_prep(xbc, cprev_t, dt, cw, dta):
    n = xbc.shape[0]
    args = (xbc, cprev_t, dt, cw, dta)
    widths = (M_WIDTH, M_WIDTH, M_HEADS * D_STATE, M_HEADS * D_STATE, M_HEADS * D_STATE)
    out_shapes = [(n, w) for w in widths] + [cprev_t.shape]
    return pl.pallas_call(
        _mamba_prep_kernel,
        grid=(1,),
        in_specs=[_const_spec(a.shape) for a in args],
        out_specs=[_const_spec(s) for s in out_shapes],
        out_shape=[jax.ShapeDtypeStruct(s, F32) for s in out_shapes],
        compiler_params=pltpu.CompilerParams(vmem_limit_bytes=VMEM_LIMIT),
        name="mamba_sample_prep",
    )(*args)


def _mamba_step_kernel(xs_ref, xdt_ref, dec_ref, bm_ref, cm_ref, h_ref, dskip_ref, y_ref, ho_ref):
    n = M_HEADS * M_HEADDIM
    ones_in = jnp.ones((M_HEADDIM, D_STATE), BF16)
    ones_out = jnp.ones((D_STATE, M_HEADDIM), BF16)
    diag = (_iota2((n, M_HEADDIM), 0) & (M_HEADDIM - 1)) == _iota2((n, M_HEADDIM), 1)

    def head_rows(ref, bi):
        return jnp.concatenate(
            [jnp.broadcast_to(ref[bi:bi + 1, e * D_STATE:(e + 1) * D_STATE], (M_HEADDIM, D_STATE))
             for e in range(M_HEADS)], axis=0)

    for bi in range(STEP_BATCH):
        rows = slice(bi * M_HEADS, (bi + 1) * M_HEADS)
        hrows = slice(bi * n, (bi + 1) * n)
        x_col = _mm(jnp.where(diag, _per_head_rows(xdt_ref[rows, :], M_HEADDIM), 0.0), ones_in)
        h2 = h_ref[hrows, :] * head_rows(dec_ref, bi) + x_col * head_rows(bm_ref, bi)
        ho_ref[hrows, :] = h2
        y = _rows_from_columns(_mm(h2 * head_rows(cm_ref, bi), ones_out), diag, M_HEADDIM)
        y_ref[rows, :] = y + dskip_ref[...] * xs_ref[rows, :]


def _mamba_sample_step(xs, xdt, dec, bm, cm, state, dskip_rows):
    nb = state.shape[0]
    rows = STEP_BATCH * M_HEADS
    vec_spec = pl.BlockSpec((rows, M_HEADDIM), lambda i: (i, 0))
    bc_spec = pl.BlockSpec((STEP_BATCH, M_HEADS * D_STATE), lambda i: (i, 0))
    st_spec = pl.BlockSpec((rows * M_HEADDIM, D_STATE), lambda i: (i, 0))
    flat = lambda x, w: x.reshape(nb * M_HEADS, w)
    return pl.pallas_call(
        _mamba_step_kernel,
        grid=(nb // STEP_BATCH,),
        in_specs=[vec_spec] * 2 + [bc_spec] * 3 + [st_spec, _const_spec((M_HEADS, M_HEADDIM))],
        out_specs=[vec_spec, st_spec],
        out_shape=[jax.ShapeDtypeStruct((nb * M_HEADS, M_HEADDIM), F32),
                   jax.ShapeDtypeStruct((nb * M_HEADS * M_HEADDIM, D_STATE), F32)],
        compiler_params=pltpu.CompilerParams(dimension_semantics=("arbitrary",),
                                             vmem_limit_bytes=VMEM_LIMIT),
        name="mamba_sample_step",
    )(flat(xs, M_HEADDIM), flat(xdt, M_HEADDIM), dec, bm, cm,
      state.reshape(nb * M_HEADS * M_HEADDIM, D_STATE), dskip_rows)


FF_SLAB = 2048


def _ffn_math(x, o_r, o_m, wout_ref, lnp_ref, w1_ref, w2_ref):
    mix = _mm(o_r, wout_ref[:R_WIDTH, :]) + _mm(o_m, wout_ref[R_WIDTH:, :])
    h = _layer_norm(ALPHA * x + mix, lnp_ref[0:1, :], lnp_ref[1:2, :], LN_EPS)
    hb = h.astype(BF16)
    ff = jnp.zeros_like(h)
    for j in range(D_FF // FF_SLAB):
        t = jnp.dot(hb, w1_ref[:, j * FF_SLAB:(j + 1) * FF_SLAB], preferred_element_type=F32)
        t = jnp.square(jnp.maximum(t, 0.0))
        ff = ff + jnp.dot(t.astype(BF16), w2_ref[j * FF_SLAB:(j + 1) * FF_SLAB, :],
                          preferred_element_type=F32)
    return _layer_norm(ALPHA * h + ff, lnp_ref[2:3, :], lnp_ref[3:4, :], LN_EPS)


def _ffn_kernel(xp_ref, orp_ref, omp_ref, xs_ref, ors_ref, yms_ref, zs_ref, mv_ref,
                wout_ref, w1_ref, w2_ref, lnp_ref, yp_ref, ys_ref):
    i = pl.program_id(0)
    last = pl.num_programs(0) - 1

    @pl.when(i < last)
    def _():
        yp_ref[...] = _ffn_math(xp_ref[...], orp_ref[...], omp_ref[...], wout_ref, lnp_ref, w1_ref, w2_ref)

    @pl.when(i == last)
    def _():
        o_m = _gated_rmsnorm(yms_ref[...], zs_ref[...], mv_ref[1:2, :])
        ys_ref[...] = _ffn_math(xs_ref[...], ors_ref[...], o_m, wout_ref, lnp_ref, w1_ref, w2_ref)


def _ffn(prompt, sample, mv, wout, w1, w2, lnp, tm):
    n = prompt[0].shape[0]
    ns = sample[0].shape[0]
    ntile = n // tm
    tok = lambda w: pl.BlockSpec((tm, w), lambda i: (jnp.minimum(i, ntile - 1), 0))
    whole = lambda shape: pl.BlockSpec(shape, lambda i: (0, 0), pipeline_mode=pl.Buffered(1))
    consts = list(sample) + [mv, wout, w1, w2, lnp]
    return pl.pallas_call(
        _ffn_kernel,
        grid=(ntile + 1,),
        in_specs=[tok(D_MODEL), tok(R_WIDTH), tok(M_WIDTH)] + [whole(a.shape) for a in consts],
        out_specs=[tok(D_MODEL), whole((ns, D_MODEL))],
        out_shape=[jax.ShapeDtypeStruct((n, D_MODEL), F32), jax.ShapeDtypeStruct((ns, D_MODEL), F32)],
        compiler_params=pltpu.CompilerParams(dimension_semantics=("arbitrary",),
                                             vmem_limit_bytes=VMEM_LIMIT),
        name="ffn",
    )(*prompt, *consts)


def _block_diag_ones():
    idx = np.arange(R_WIDTH) // R_HEAD
    return jnp.asarray(idx[:, None] == idx[None, :], dtype=BF16)


def kernel(x_prompt, x_sample, state_shift, state_wkv, state_conv, state_ssm, w_in, mu_shift, w0, w_up, a0, a_up, g_up, k_k, k_a, r_k, lnx_w, lnx_b, conv_w, conv_b, dt_bias, a_log, d_skip, gnorm_w, w_out, ln1_g, ln1_b, w_ff1, w_ff2, ln2_g, ln2_b):
    assert w_in.shape[0] == DEPTH and x_sample.shape[1] == 1
    batch, seq, _ = x_prompt.shape
    nb = x_sample.shape[0]
    assert seq % MAMBA_TILE == 0 and seq % RWKV_TILE == 0 and seq % DENSE_ROWS == 0
    assert nb % STEP_BATCH == 0 and batch % RWKV_ROWS == 0
    row = lambda a: a.reshape(1, -1)

    w_t = jnp.transpose(w_in[0])
    w_dt = w_t[PROJ_MAIN:]
    zl = jnp.zeros((D_DECAY_LORA, R_WIDTH), F32)
    wup_p = jnp.concatenate([w_up[0], zl], axis=0).astype(BF16)
    aup_p = jnp.concatenate([zl, a_up[0]], axis=0).astype(BF16)
    bd = _block_diag_ones()

    def rows8(*vecs):
        m = jnp.stack([v.reshape(-1) for v in vecs])
        return jnp.pad(m, ((0, SUBLANES - len(vecs)), (0, 0)))

    rv = rows8(w0[0], a0[0], k_k[0], k_a[0], r_k[0], lnx_w[0], lnx_b[0])
    prep_w = (row(mu_shift[0]), rv, wup_p, aup_p, g_up[0].astype(BF16), bd)
    cw = rows8(*conv_w[0], conv_b[0])
    dta = rows8(jnp.pad(dt_bias[0], (0, DT_PAD - M_HEADS)), jnp.pad(a_log[0], (0, DT_PAD - M_HEADS)))
    mv = rows8(jnp.repeat(d_skip[0], M_HEADDIM), gnorm_w[0])
    dskip_rows = jnp.broadcast_to(d_skip[0][:, None], (M_HEADS, M_HEADDIM))
    lnp = rows8(ln1_g[0], ln1_b[0], ln2_g[0], ln2_b[0])

    xp = x_prompt.reshape(batch * seq, D_MODEL)
    xs2d = x_sample.reshape(nb, D_MODEL)
    (pr_p, z_p, xbc_p, dt_p, pr_s, z_s, xbc_s, dt_s, wout_bf, w1_bf, w2_bf) = _proj(
        xp, xs2d, w_t, w_dt, (w_out[0], w_ff1[0], w_ff2[0]), DENSE_ROWS)

    o_r_p, wkv_p = _rwkv_prompt(pr_p.reshape(batch, seq, RWKV_PROJ), prep_w)
    o_r_p = o_r_p.reshape(batch * seq, R_WIDTH)
    o_m_p, ssm_p = _mamba_prompt(z_p, xbc_p, dt_p, batch, seq, (cw, dta, mv))
    shift_p = pr_p.reshape(batch, seq, RWKV_PROJ)[:, -1]
    conv_p = xbc_p.reshape(batch, seq, CONV_DIM)[:, seq - (CONV_K - 1):]

    cols = jnp.broadcast_to(jnp.stack([r_k[0].reshape(-1), lnx_w[0], lnx_b[0]])[:, :, None],
                            (3, R_WIDTH, nb))
    o_r_t, wkv_t = _rwkv_sample(pr_s, state_shift[0], jnp.transpose(state_wkv[0], (1, 2, 3, 0)),
                                prep_w, cols)
    o_r_s = o_r_t.T
    wkv_s = jnp.transpose(wkv_t, (3, 0, 1, 2))
    xs_s, xdt_s, dec_s, bm_s, cm_s, conv_t = _mamba_sample_prep(
        xbc_s, jnp.transpose(state_conv[0], (1, 0, 2)), dt_s, cw, dta)
    conv_s = jnp.transpose(conv_t, (1, 0, 2))
    y_m_s, ssm_s = _mamba_sample_step(xs_s, xdt_s, dec_s, bm_s, cm_s, state_ssm[0], dskip_rows)

    y_p, y_s = _ffn((xp, o_r_p, o_m_p), (xs2d, o_r_s, y_m_s.reshape(nb, M_WIDTH), z_s), mv,
                    wout_bf, w1_bf, w2_bf, lnp, tm=DENSE_ROWS)

    return (y_p.reshape(batch, seq, D_MODEL), y_s.reshape(nb, 1, D_MODEL),
            shift_p[None], wkv_p[None], conv_p[None],
            ssm_p.reshape(1, batch, M_HEADS, M_HEADDIM, D_STATE),
            pr_s[None], wkv_s.reshape(1, nb, R_HEADS, R_HEAD, R_HEAD), conv_s[None],
            ssm_s.reshape(1, nb, M_HEADS, M_HEADDIM, D_STATE))
```

```python
import math

import numpy as np
import jax
import jax.numpy as jnp
from jax import lax
from jax.experimental import pallas as pl
from jax.experimental.pallas import tpu as pltpu

F32 = jnp.float32
BF16 = jnp.bfloat16

D_MODEL = 1024
R_WIDTH = 512
R_HEAD = 64
R_HEADS = 8
D_DECAY_LORA = 64
D_AAA_LORA = 64
D_GATE_LORA = 128
M_WIDTH = 512
M_HEADDIM = 64
M_HEADS = 8
M_GROUPS = 2
D_STATE = 128
CONV_K = 4
CONV_DIM = M_WIDTH + 2 * M_GROUPS * D_STATE
D_FF = 4 * D_MODEL
RWKV_PROJ = 3 * R_WIDTH + D_DECAY_LORA + D_AAA_LORA + D_GATE_LORA
DT_PAD = 128
PROJ_MAIN = RWKV_PROJ + M_WIDTH + CONV_DIM
DEPTH = 1
ALPHA = (2 * DEPTH) ** 0.25
LN_EPS = 1e-5
LNX_EPS = 64e-5
RMS_EPS = 1e-5

RWKV_CHUNK = 64
RWKV_TILE = 256
RWKV_ROWS = 4
RWKV_WAVES = 2
SSD_CHUNK = 128
MAMBA_TILE = 256
SUBLANES = 8
DENSE_ROWS = 512
VMEM_LIMIT = 48 * 1024 * 1024
PROJ_W_VMEM_LIMIT = 56 * 1024 * 1024

_NT = (((1,), (1,)), ((), ()))


def _mm(a, b):
    return jnp.dot(a.astype(BF16), b.astype(BF16), preferred_element_type=F32)


def _mm_nt(a, b):
    return lax.dot_general(a.astype(BF16), b.astype(BF16), _NT, preferred_element_type=F32)


def _split(x, passes):
    parts = []
    for _ in range(passes):
        p = x.astype(BF16)
        parts.append(p)
        x = x - p.astype(F32)
    return parts


def _mm_sel_r(x, sel, passes):
    return sum(jnp.dot(p, sel, preferred_element_type=F32) for p in _split(x, passes))


def _mm_sel_l(sel, x, passes):
    return sum(jnp.dot(sel, p, preferred_element_type=F32) for p in _split(x, passes))


def _softplus(x):
    return jnp.maximum(x, 0.0) + jnp.log(1.0 + jnp.exp(-jnp.abs(x)))


def _sigmoid(x):
    return 0.5 + 0.5 * jnp.tanh(0.5 * x)


def _silu(x):
    return x * _sigmoid(x)


def _layer_norm(x, g, b, eps):
    mu = jnp.mean(x, axis=-1, keepdims=True)
    d = x - mu
    var = jnp.mean(d * d, axis=-1, keepdims=True)
    return d * lax.rsqrt(var + eps) * g + b


def _iota2(shape, axis):
    return lax.broadcasted_iota(jnp.int32, shape, axis)


def _dt_columns(xb, wdt_t):
    pad = jnp.zeros((DT_PAD - M_HEADS, D_MODEL), F32)
    return _mm_nt(xb, jnp.concatenate([wdt_t, pad], axis=0))


W_SLAB = 256


def _proj_kernel(xp_ref, xs_ref, wt_ref, wdt_ref,
                 prp_ref, zp_ref, xbcp_ref, dtp_ref, prs_ref, zs_ref, xbcs_ref, dts_ref, wbf_scr):
    i = pl.program_id(0)
    last = pl.num_programs(0) - 1

    @pl.when(i == 0)
    def _():
        for c in range(0, PROJ_MAIN, W_SLAB):
            wbf_scr[:, c:c + W_SLAB] = wt_ref[c:c + W_SLAB, :].T.astype(BF16)

    def project(x_ref, pr_ref, z_ref, xbc_ref, dt_ref):
        xb = x_ref[...].astype(BF16)
        res = jnp.dot(xb, wbf_scr[...], preferred_element_type=F32)
        c0 = RWKV_PROJ
        c1 = c0 + M_WIDTH
        pr_ref[...] = res[:, :c0]
        z_ref[...] = res[:, c0:c1]
        xbc_ref[...] = res[:, c1:]
        dt_ref[...] = _dt_columns(xb, wdt_ref[...])

    @pl.when(i < last)
    def _():
        project(xp_ref, prp_ref, zp_ref, xbcp_ref, dtp_ref)

    @pl.when(i == last)
    def _():
        project(xs_ref, prs_ref, zs_ref, xbcs_ref, dts_ref)


def _proj(xp2d, xs2d, w_t, wdt_t, tm):
    n = xp2d.shape[0]
    ns = xs2d.shape[0]
    ntile = n // tm
    widths = (RWKV_PROJ, M_WIDTH, CONV_DIM, DT_PAD)
    once = lambda shape: pl.BlockSpec(shape, lambda i: (0, 0), pipeline_mode=pl.Buffered(1))
    tile = lambda w: pl.BlockSpec((tm, w), lambda i: (jnp.minimum(i, ntile - 1), 0))
    return pl.pallas_call(
        _proj_kernel,
        grid=(ntile + 1,),
        in_specs=[tile(D_MODEL), once(xs2d.shape), once(w_t.shape), once(wdt_t.shape)],
        out_specs=[tile(w) for w in widths] + [once((ns, w)) for w in widths],
        out_shape=[jax.ShapeDtypeStruct((n, w), F32) for w in widths]
                  + [jax.ShapeDtypeStruct((ns, w), F32) for w in widths],
        scratch_shapes=[pltpu.VMEM((D_MODEL, PROJ_MAIN), BF16)],
        compiler_params=pltpu.CompilerParams(dimension_semantics=("arbitrary",),
                                             vmem_limit_bytes=PROJ_W_VMEM_LIMIT),
        name="proj",
    )(xp2d, xs2d, w_t, wdt_t)


def _rwkv_prep(pr, prev, mu, rv, wup, aup, gup, bd):
    w0, a0, k_k, k_a = (rv[i:i + 1, :] for i in range(4))
    u = pr + (prev - pr) * mu
    r = u[:, 0:R_WIDTH]
    k = u[:, R_WIDTH:2 * R_WIDTH]
    v = u[:, 2 * R_WIDTH:3 * R_WIDTH]
    lora = u[:, 3 * R_WIDTH:3 * R_WIDTH + D_DECAY_LORA + D_AAA_LORA]
    ug = u[:, 3 * R_WIDTH + D_DECAY_LORA + D_AAA_LORA:]
    w_log = -_softplus(-(w0 + _mm(jnp.tanh(lora), wup))) - 0.5
    logw = -jnp.exp(w_log)
    a = _sigmoid(a0 + _mm(lora, aup))
    g = _mm(_sigmoid(ug), gup)
    kk = k * k_k
    ss = _mm_sel_r(kk * kk, bd, 1)
    kkn = kk * lax.rsqrt(jnp.maximum(ss, 1e-24))
    k2 = k * (1.0 + (a - 1.0) * k_a)
    return r, logw, k2, v, -kkn, kkn * a, g


def _rwkv_post(y, r, k2, v, g, rv, bd):
    r_k, lnx_w, lnx_b = (rv[i:i + 1, :] for i in range(4, 7))
    inv = 1.0 / R_HEAD
    mu = _mm_sel_r(y, bd, 1) * inv
    d = y - mu
    var = _mm_sel_r(d * d, bd, 1) * inv
    yn = d * lax.rsqrt(var + LNX_EPS) * lnx_w + lnx_b
    bonus = _mm_sel_r(r * k2 * r_k, bd, 1)
    return (yn + bonus * v) * g


def _rwkv_tile_kernel(pr_ref, prevblk_ref, mu_ref, rv_ref, wup_ref, aup_ref, gup_ref, bd_ref,
                      o_ref, sfin_ref, g_scr):
    c = RWKV_CHUNK
    tile = RWKV_TILE
    nseq = RWKV_ROWS
    nck = tile // c
    hw = 2 * R_HEAD
    npair = R_HEADS // 2
    ti = pl.program_id(1)

    @pl.when(ti == 0)
    def _():
        g_scr[...] = jnp.zeros_like(g_scr)

    pr = jnp.concatenate([pr_ref[s] for s in range(nseq)], axis=0)
    prev = pltpu.roll(pr, 1, axis=0)
    row_id = _iota2((nseq * tile, 1), 0)
    for s in range(nseq):
        prev_row = jnp.where(ti == 0, 0.0, prevblk_ref[s, SUBLANES - 1:SUBLANES, :])
        prev = jnp.where(row_id == s * tile, prev_row, prev)
    head_ones = bd_ref[...]
    r, logw, k2, v, av, bv, g = _rwkv_prep(pr, prev, mu_ref[...], rv_ref[...], wup_ref[...],
                                           aup_ref[...], gup_ref[...], head_ones)

    sh_c = int(math.log2(c))
    ti_i = _iota2((tile, tile), 0)
    ti_j = _iota2((tile, tile), 1)
    chunk_tril = jnp.where((ti_i >= ti_j) & ((ti_i >> sh_c) == (ti_j >> sh_c)), 1.0, 0.0).astype(BF16)
    cum = jnp.concatenate([_mm_sel_l(chunk_tril, logw[s * tile:(s + 1) * tile], 3) for s in range(nseq)],
                          axis=0)
    cl = jnp.concatenate(
        [jnp.broadcast_to(cum[(q + 1) * c - 1:(q + 1) * c, :], (c, R_WIDTH)) for q in range(nseq * nck)],
        axis=0)
    p_out = jnp.exp(-cum)
    p_end = jnp.exp(cl - cum)
    rt = r * jnp.exp(cum)
    at = av * jnp.exp(cum - logw)
    bt = bv * p_out
    kt = k2 * p_out
    bh = bv * p_end
    kh = k2 * p_end
    pc = jnp.exp(cl)

    ii = _iota2((c, hw), 0)
    jj = _iota2((c, hw), 1) & (c - 1)
    strict = ii > jj
    incl = ii >= jj
    lane = _iota2((1, hw), 1)
    keep = (jnp.where(lane < R_HEAD, 1.0, 0.0).astype(BF16), jnp.where(lane >= R_HEAD, 1.0, 0.0).astype(BF16))
    e_r = _iota2((hw, hw), 0)
    e_c = _iota2((hw, hw), 1)
    eye_hw = e_r == e_c
    blockdiag = (e_r >= R_HEAD) == (e_c >= R_HEAD)

    def bd(x):
        xb = x.astype(BF16)
        return jnp.concatenate([xb * keep[0], xb * keep[1]], axis=0)

    def pdot(a, b_bd):
        return jnp.dot(a.astype(BF16), b_bd, preferred_element_type=F32)

    blk = lambda x, q, p: x[q * c:(q + 1) * c, hw * p:hw * (p + 1)]
    zeros_c = jnp.zeros((c, hw), F32)
    eye = jnp.where(ii == jj, 1.0, 0.0)

    def phase1(qp):
        lhs = [jnp.concatenate([blk(at, *k), blk(rt, *k)], axis=0).astype(BF16) for k in qp]
        sc = [lax.dot_general(x, jnp.concatenate([bd(blk(bt, *k)), bd(blk(kt, *k))], axis=0), _NT,
                              preferred_element_type=F32) for x, k in zip(lhs, qp)]
        a_ab = [jnp.where(strict, s[:c, :hw], 0.0) for s in sc]
        a_rb = [jnp.where(incl, s[c:, :hw], 0.0) for s in sc]
        a_akrk = [jnp.concatenate([jnp.where(strict, s[:c, hw:], 0.0), jnp.where(incl, s[c:, hw:], 0.0)], axis=0)
                  for s in sc]
        v_bd = [bd(blk(v, *k)) for k in qp]
        akrk_v = [pdot(a, x) for a, x in zip(a_akrk, v_bd)]

        sh = 4
        npow = [jnp.where((ii >> sh) == (jj >> sh), n, 0.0) for n in a_ab]
        t_inv = [eye + n for n in npow]
        npow = [pdot(n, bd(n)) for n in npow]
        for _ in range(sh - 2):
            both = [pdot(jnp.concatenate([t, n], axis=0), bd(n)) for t, n in zip(t_inv, npow)]
            t_inv = [t + x[:c] for t, x in zip(t_inv, both)]
            npow = [x[c:] for x in both]
        t_inv = [t + pdot(t, bd(n)) for t, n in zip(t_inv, npow)]
        while (1 << sh) < c:
            off = ((ii >> (sh + 1)) == (jj >> (sh + 1))) & (((ii >> sh) & 1) > ((jj >> sh) & 1))
            tn = [pdot(t, bd(jnp.where(off, n, 0.0))) for t, n in zip(t_inv, a_ab)]
            t_inv = [t + pdot(x, bd(t)) for t, x in zip(t_inv, tn)]
            sh += 1

        tx = [pdot(t, jnp.concatenate([bd(blk(at, *k)), bd(x[:c])], axis=1))
              for t, x, k in zip(t_inv, akrk_v, qp)]
        rbx = [pdot(a, jnp.concatenate([bd(x[:, :hw]), bd(x[:, hw:])], axis=1)) for a, x in zip(a_rb, tx)]
        rp_pair = {k: blk(rt, *k) + x[:, :hw] for k, x in zip(qp, rbx)}
        y0_pair = {k: x[:, hw:] + y[c:] for k, x, y in zip(qp, rbx, akrk_v)}

        wt = {k: jnp.concatenate([blk(bh, *k), blk(kh, *k)], axis=0).T for k in qp}
        trans = {k: _mm(wt[k], jnp.concatenate([x, jnp.concatenate([zeros_c, blk(v, *k)], axis=1)], axis=0))
                 for k, x in zip(qp, tx)}
        m_ba = {k: jnp.where(blockdiag, trans[k][:, :hw], 0.0) for k in qp}
        h0c = {k: jnp.where(blockdiag, trans[k][:, hw:], 0.0) for k in qp}
        pc_col = {k: jnp.sum(jnp.where(eye_hw, blk(pc, *k)[:1], 0.0), axis=1, keepdims=True) for k in qp}
        return rp_pair, y0_pair, m_ba, h0c, pc_col

    rp_pair, y0_pair, m_ba, h0c, pc_col = {}, {}, {}, {}, {}
    all_qp = [(q, p) for q in range(nseq * nck) for p in range(npair)]
    wave = len(all_qp) // RWKV_WAVES
    for w0 in range(0, len(all_qp), wave):
        for dst, src in zip((rp_pair, y0_pair, m_ba, h0c, pc_col), phase1(all_qp[w0:w0 + wave])):
            dst.update(src)


    state = [g_scr[i] for i in range(nseq * npair)]
    y_rows = [None] * (nseq * nck)
    for q in range(nck):
        for s in range(nseq):
            qq = s * nck + q
            st = state[s * npair:(s + 1) * npair]
            on_st = [_mm(jnp.concatenate([rp_pair[qq, p], m_ba[qq, p]], axis=0), st[p]) for p in range(npair)]
            y_rows[qq] = jnp.concatenate([on_st[p][:c] + y0_pair[qq, p] for p in range(npair)], axis=1)
            state[s * npair:(s + 1) * npair] = [
                pc_col[qq, p] * st[p] + on_st[p][c:] + h0c[qq, p] for p in range(npair)]
    for i in range(nseq * npair):
        g_scr[i] = state[i]

    y = jnp.concatenate(y_rows, axis=0)
    out = _rwkv_post(y, r, k2, v, g, rv_ref[...], head_ones)
    for s in range(nseq):
        o_ref[s] = out[s * tile:(s + 1) * tile]

    @pl.when(ti == pl.num_programs(1) - 1)
    def _():
        for s in range(nseq):
            for p in range(npair):
                gt = g_scr[s * npair + p].T
                sfin_ref[s, 2 * p] = gt[:R_HEAD, :R_HEAD]
                sfin_ref[s, 2 * p + 1] = pltpu.roll(gt, R_HEAD, axis=1)[R_HEAD:, :R_HEAD]


def _const_spec(shape):
    return pl.BlockSpec(shape, lambda *_: (0,) * len(shape))


def _rwkv_prompt(pr3d, wts):
    batch, seq, _ = pr3d.shape
    tile = RWKV_TILE
    nseq = RWKV_ROWS

    def prev_map(b, i):
        return (b, jnp.maximum(i * (tile // SUBLANES) - 1, 0), 0)

    w_specs = [_const_spec(w.shape) for w in wts]
    return pl.pallas_call(
        _rwkv_tile_kernel,
        grid=(batch // nseq, seq // tile),
        in_specs=[pl.BlockSpec((nseq, tile, RWKV_PROJ), lambda b, i: (b, i, 0)),
                  pl.BlockSpec((nseq, SUBLANES, RWKV_PROJ), prev_map)] + w_specs,
        out_specs=[pl.BlockSpec((nseq, tile, R_WIDTH), lambda b, i: (b, i, 0)),
                   pl.BlockSpec((nseq, R_HEADS, R_HEAD, R_HEAD), lambda b, i: (b, 0, 0, 0))],
        out_shape=[jax.ShapeDtypeStruct((batch, seq, R_WIDTH), F32),
                   jax.ShapeDtypeStruct((batch, R_HEADS, R_HEAD, R_HEAD), F32)],
        scratch_shapes=[pltpu.VMEM((nseq * (R_HEADS // 2), 2 * R_HEAD, 2 * R_HEAD), F32)],
        compiler_params=pltpu.CompilerParams(dimension_semantics=("arbitrary", "arbitrary"),
                                             vmem_limit_bytes=VMEM_LIMIT),
        name="rwkv_prompt",
    )(pr3d, pr3d, *wts)


def _rwkv_sample_kernel(pr_ref, prev_ref, mu_ref, rv_ref, wup_ref, aup_ref, gup_ref, bd_ref,
                        cols_ref, s_ref, o_ref, so_ref, vec_scr):
    h = pl.program_id(0)

    @pl.when(h == 0)
    def _():
        r, logw, k2, v, av, bv, g = _rwkv_prep(pr_ref[...], prev_ref[...], mu_ref[...], rv_ref[...],
                                               wup_ref[...], aup_ref[...], gup_ref[...], bd_ref[...])
        for i, x in enumerate((r, jnp.exp(logw), k2, v, av, bv, g)):
            vec_scr[i] = x.T

    rows = pl.ds(pl.multiple_of(h * R_HEAD, R_HEAD), R_HEAD)
    r, w, k, v, a, b, g = (vec_scr[i, rows, :] for i in range(7))
    ys = []
    for vi in range(R_HEAD):
        s = s_ref[0, vi]
        sa = jnp.sum(s * a, axis=0, keepdims=True)
        s2 = s * w + sa * b + v[vi:vi + 1, :] * k
        so_ref[0, vi] = s2
        ys.append(jnp.sum(s2 * r, axis=0, keepdims=True))
    y = jnp.concatenate(ys, axis=0)
    mu = jnp.mean(y, axis=0, keepdims=True)
    d = y - mu
    var = jnp.mean(d * d, axis=0, keepdims=True)
    yn = d * lax.rsqrt(var + LNX_EPS) * cols_ref[1] + cols_ref[2]
    bonus = jnp.sum(r * k * cols_ref[0], axis=0, keepdims=True)
    o_ref[...] = (yn + bonus * v) * g


def _rwkv_sample(pr, prev, state_t, prep_w, cols):
    nb = pr.shape[0]
    args = (pr, prev) + tuple(prep_w)
    head_rows = pl.BlockSpec((R_HEAD, nb), lambda h: (h, 0))
    st_spec = pl.BlockSpec((1, R_HEAD, R_HEAD, nb), lambda h: (h, 0, 0, 0))
    return pl.pallas_call(
        _rwkv_sample_kernel,
        grid=(R_HEADS,),
        in_specs=[_const_spec(a.shape) for a in args]
                 + [pl.BlockSpec((3, R_HEAD, nb), lambda h: (0, h, 0)), st_spec],
        out_specs=[head_rows, st_spec],
        out_shape=[jax.ShapeDtypeStruct((R_WIDTH, nb), F32),
                   jax.ShapeDtypeStruct(state_t.shape, F32)],
        scratch_shapes=[pltpu.VMEM((7, R_WIDTH, nb), F32)],
        compiler_params=pltpu.CompilerParams(dimension_semantics=("arbitrary",),
                                             vmem_limit_bytes=VMEM_LIMIT),
        name="rwkv_sample",
    )(*args, cols, state_t)


STEP_BATCH = 32


def _per_head_rows(x, n):
    return jnp.concatenate([jnp.broadcast_to(x[h:h + 1, :], (n, x.shape[1])) for h in range(x.shape[0])],
                           axis=0)


def _rows_from_columns(col_b, diag, n):
    picked = jnp.where(diag, col_b, 0.0)
    return jnp.concatenate([jnp.sum(picked[h * n:(h + 1) * n], axis=0, keepdims=True)
                            for h in range(col_b.shape[0] // n)], axis=0)


def _shift_rows(x, prev8, s):
    rolled = pltpu.roll(x, s, axis=0)
    top = jnp.where(_iota2((SUBLANES, 1), 0) < s, pltpu.roll(prev8, s, axis=0), rolled[:SUBLANES])
    return jnp.concatenate([top, rolled[SUBLANES:]], axis=0)


def _head_expand(lanes_per_head=M_HEADDIM):
    shape = (DT_PAD, M_HEADS * lanes_per_head)
    head_of_lane = _iota2(shape, 1) >> int(math.log2(lanes_per_head))
    return jnp.where(_iota2(shape, 0) == head_of_lane, 1.0, 0.0).astype(BF16)


def _gated_rmsnorm(y, z, gnorm_w):
    yz = y * _silu(z)
    gw = M_WIDTH // M_GROUPS
    outs = []
    for gi in range(M_GROUPS):
        yg = yz[:, gi * gw:(gi + 1) * gw]
        ms = jnp.mean(yg * yg, axis=-1, keepdims=True)
        outs.append(yg * lax.rsqrt(ms + RMS_EPS))
    return jnp.concatenate(outs, axis=1) * gnorm_w


def _mamba_tile_kernel(z_ref, xbc_ref, xprev_ref, dt_ref, cw_ref, dta_ref, mv_ref,
                       wout32_ref, w1_32_ref, w2_32_ref,
                       o_ref, hfin_ref, wout_ref, w1_ref, w2_ref, h_scr):
    tile = MAMBA_TILE
    q = SSD_CHUNK
    ti = pl.program_id(1)
    wout_ref[...] = wout32_ref[...].astype(BF16)
    w1_ref[...] = w1_32_ref[...].astype(BF16)
    w2_ref[...] = w2_32_ref[...].astype(BF16)

    @pl.when(ti == 0)
    def _():
        h_scr[...] = jnp.zeros_like(h_scr)

    xb = xbc_ref[...]
    prev8 = jnp.where(ti == 0, 0.0, xprev_ref[...])
    conv = cw_ref[CONV_K:CONV_K + 1, :] + xb * cw_ref[CONV_K - 1:CONV_K, :]
    for s in range(1, CONV_K):
        conv = conv + _shift_rows(xb, prev8, s) * cw_ref[CONV_K - 1 - s:CONV_K - s, :]
    xact = _silu(conv)
    xs = xact[:, :M_WIDTH]
    gs = M_GROUPS * D_STATE
    bm = xact[:, M_WIDTH:M_WIDTH + gs]
    cm = xact[:, M_WIDTH + gs:]

    dt = _softplus(dt_ref[...] + dta_ref[0:1, :])
    adt = dt * (-jnp.exp(dta_ref[1:2, :]))
    x_dt = xs * _mm_sel_r(dt, _head_expand(), 3)

    sh_q = int(math.log2(q))
    ti_i = _iota2((tile, tile), 0)
    ti_j = _iota2((tile, tile), 1)
    chunk_tril = jnp.where((ti_i >= ti_j) & ((ti_i >> sh_q) == (ti_j >> sh_q)), 1.0, 0.0).astype(BF16)
    cs = _mm_sel_l(chunk_tril, adt, 3)
    cs_t = cs.T
    incl = _iota2((q, q), 0) >= _iota2((q, q), 1)
    lane = _iota2((1, 2 * M_HEADDIM), 1)
    first = lane < M_HEADDIM
    hpg = M_HEADS // M_GROUPS

    state = [h_scr[p] for p in range(M_HEADS // 2)]
    y_rows = []
    for j in range(tile // q):
        rs = slice(j * q, (j + 1) * q)
        y_pairs = []
        for gi in range(M_GROUPS):
            bm_g = bm[rs, gi * D_STATE:(gi + 1) * D_STATE]
            cm_g = cm[rs, gi * D_STATE:(gi + 1) * D_STATE]
            cb = _mm_nt(cm_g, bm_g)
            bm_t = bm_g.T
            for pp in range(hpg // 2):
                p = gi * (hpg // 2) + pp
                x_p = x_dt[rs, 2 * M_HEADDIM * p:2 * M_HEADDIM * (p + 1)]
                y_diag = jnp.zeros((q, 2 * M_HEADDIM), F32)
                contrib = jnp.zeros((D_STATE, 2 * M_HEADDIM), F32)
                scale_in = []
                scale_end = []
                for e in range(2):
                    head = 2 * p + e
                    m = first if e == 0 else jnp.logical_not(first)
                    a_col = cs[rs, head:head + 1]
                    a_row = cs_t[head:head + 1, rs]
                    a_end = cs_t[head:head + 1, (j + 1) * q - 1:(j + 1) * q]
                    lmat = jnp.exp(jnp.where(incl, a_col - a_row, -jnp.inf))
                    x_m = jnp.where(m, x_p, 0.0)
                    y_diag = y_diag + _mm(cb * lmat, x_m)
                    contrib = contrib + _mm(bm_t * jnp.exp(a_end - a_row), x_m)
                    scale_in.append(jnp.exp(a_col))
                    scale_end.append(jnp.exp(a_end))
                y_off = _mm(cm_g, state[p]) * jnp.where(first, scale_in[0], scale_in[1])
                state[p] = state[p] * jnp.where(first, scale_end[0], scale_end[1]) + contrib
                y_pairs.append(y_diag + y_off)
        y_rows.append(jnp.concatenate(y_pairs, axis=1))
    for p in range(M_HEADS // 2):
        h_scr[p] = state[p]

    y = jnp.concatenate(y_rows, axis=0) + mv_ref[0:1, :] * xs
    o_ref[...] = _gated_rmsnorm(y, z_ref[...], mv_ref[1:2, :])

    @pl.when(ti == pl.num_programs(1) - 1)
    def _():
        for p in range(M_HEADS // 2):
            hfin_ref[0, 2 * M_HEADDIM * p:2 * M_HEADDIM * (p + 1), :] = h_scr[p].T


def _mamba_prompt(z2d, xbc2d, dt2d, batch, seq, wts, mlp_w):
    tile = MAMBA_TILE
    ntile = seq // tile
    rows8 = seq // SUBLANES
    nstep = batch * ntile
    slab = lambda a: pl.BlockSpec((a.shape[0] // nstep, a.shape[1]), lambda b, i: (b * ntile + i, 0))

    def prev_map(b, i):
        return (b * rows8 + jnp.maximum(i * (tile // SUBLANES) - 1, 0), 0)

    tok = lambda w: pl.BlockSpec((tile, w), lambda b, i: (b * ntile + i, 0))
    return pl.pallas_call(
        _mamba_tile_kernel,
        grid=(batch, ntile),
        in_specs=[tok(M_WIDTH), tok(CONV_DIM), pl.BlockSpec((SUBLANES, CONV_DIM), prev_map), tok(DT_PAD)]
                 + [_const_spec(w.shape) for w in wts] + [slab(w) for w in mlp_w],
        out_specs=[tok(M_WIDTH),
                   pl.BlockSpec((1, M_HEADS * M_HEADDIM, D_STATE), lambda b, i: (b, 0, 0))]
                  + [slab(w) for w in mlp_w],
        out_shape=[jax.ShapeDtypeStruct((batch * seq, M_WIDTH), F32),
                   jax.ShapeDtypeStruct((batch, M_HEADS * M_HEADDIM, D_STATE), F32)]
                  + [jax.ShapeDtypeStruct(w.shape, BF16) for w in mlp_w],
        scratch_shapes=[pltpu.VMEM((M_HEADS // 2, D_STATE, 2 * M_HEADDIM), F32)],
        compiler_params=pltpu.CompilerParams(dimension_semantics=("arbitrary", "arbitrary"),
                                             vmem_limit_bytes=VMEM_LIMIT),
        name="mamba_prompt",
    )(z2d, xbc2d, xbc2d, dt2d, *wts, *mlp_w)


def _mamba_prep_kernel(xbc_ref, cprev_ref, dt_ref, cw_ref, dta_ref,
                       xs_ref, xdt_ref, dec_ref, bm_ref, cm_ref, cnew_ref):
    xbc = xbc_ref[...]
    conv = cw_ref[CONV_K:CONV_K + 1, :] + xbc * cw_ref[CONV_K - 1:CONV_K, :]
    for j in range(CONV_K - 1):
        conv = conv + cprev_ref[j] * cw_ref[j:j + 1, :]
        cnew_ref[j] = cprev_ref[j + 1] if j + 1 < CONV_K - 1 else xbc
    xact = _silu(conv)
    xs = xact[:, :M_WIDTH]
    gs = M_GROUPS * D_STATE
    dt = _softplus(dt_ref[...] + dta_ref[0:1, :])
    adt = dt * (-jnp.exp(dta_ref[1:2, :]))
    expand = _head_expand()
    xs_ref[...] = xs
    xdt_ref[...] = xs * _mm_sel_r(dt, expand, 3)
    dec_ref[...] = jnp.exp(_mm_sel_r(adt, _head_expand(D_STATE), 3))
    hpg = M_HEADS // M_GROUPS
    per_head = lambda m: jnp.concatenate(
        [m[:, (e // hpg) * D_STATE:(e // hpg + 1) * D_STATE] for e in range(M_HEADS)], axis=1)
    bm_ref[...] = per_head(xact[:, M_WIDTH:M_WIDTH + gs])
    cm_ref[...] = per_head(xact[:, M_WIDTH + gs:])


def _mamba_sample_prep(xbc, cprev_t, dt, cw, dta):
    n = xbc.shape[0]
    args = (xbc, cprev_t, dt, cw, dta)
    widths = (M_WIDTH, M_WIDTH, M_HEADS * D_STATE, M_HEADS * D_STATE, M_HEADS * D_STATE)
    out_shapes = [(n, w) for w in widths] + [cprev_t.shape]
    return pl.pallas_call(
        _mamba_prep_kernel,
        grid=(1,),
        in_specs=[_const_spec(a.shape) for a in args],
        out_specs=[_const_spec(s) for s in out_shapes],
        out_shape=[jax.ShapeDtypeStruct(s, F32) for s in out_shapes],
        compiler_params=pltpu.CompilerParams(vmem_limit_bytes=VMEM_LIMIT),
        name="mamba_sample_prep",
    )(*args)


def _mamba_step_kernel(xs_ref, xdt_ref, dec_ref, bm_ref, cm_ref, h_ref, dskip_ref, y_ref, ho_ref):
    n = M_HEADS * M_HEADDIM
    ones_in = jnp.ones((M_HEADDIM, D_STATE), BF16)
    ones_out = jnp.ones((D_STATE, M_HEADDIM), BF16)
    diag = (_iota2((n, M_HEADDIM), 0) & (M_HEADDIM - 1)) == _iota2((n, M_HEADDIM), 1)

    def head_rows(ref, bi):
        return jnp.concatenate(
            [jnp.broadcast_to(ref[bi:bi + 1, e * D_STATE:(e + 1) * D_STATE], (M_HEADDIM, D_STATE))
             for e in range(M_HEADS)], axis=0)

    for bi in range(STEP_BATCH):
        rows = slice(bi * M_HEADS, (bi + 1) * M_HEADS)
        hrows = slice(bi * n, (bi + 1) * n)
        x_col = _mm(jnp.where(diag, _per_head_rows(xdt_ref[rows, :], M_HEADDIM), 0.0), ones_in)
        h2 = h_ref[hrows, :] * head_rows(dec_ref, bi) + x_col * head_rows(bm_ref, bi)
        ho_ref[hrows, :] = h2
        y = _rows_from_columns(_mm(h2 * head_rows(cm_ref, bi), ones_out), diag, M_HEADDIM)
        y_ref[rows, :] = y + dskip_ref[...] * xs_ref[rows, :]


def _mamba_sample_step(xs, xdt, dec, bm, cm, state, dskip_rows):
    nb = state.shape[0]
    rows = STEP_BATCH * M_HEADS
    vec_spec = pl.BlockSpec((rows, M_HEADDIM), lambda i: (i, 0))
    bc_spec = pl.BlockSpec((STEP_BATCH, M_HEADS * D_STATE), lambda i: (i, 0))
    st_spec = pl.BlockSpec((rows * M_HEADDIM, D_STATE), lambda i: (i, 0))
    flat = lambda x, w: x.reshape(nb * M_HEADS, w)
    return pl.pallas_call(
        _mamba_step_kernel,
        grid=(nb // STEP_BATCH,),
        in_specs=[vec_spec] * 2 + [bc_spec] * 3 + [st_spec, _const_spec((M_HEADS, M_HEADDIM))],
        out_specs=[vec_spec, st_spec],
        out_shape=[jax.ShapeDtypeStruct((nb * M_HEADS, M_HEADDIM), F32),
                   jax.ShapeDtypeStruct((nb * M_HEADS * M_HEADDIM, D_STATE), F32)],
        compiler_params=pltpu.CompilerParams(dimension_semantics=("arbitrary",),
                                             vmem_limit_bytes=VMEM_LIMIT),
        name="mamba_sample_step",
    )(flat(xs, M_HEADDIM), flat(xdt, M_HEADDIM), dec, bm, cm,
      state.reshape(nb * M_HEADS * M_HEADDIM, D_STATE), dskip_rows)


FF_SLAB = 2048


def _ffn_math(x, o_r, o_m, wout_ref, lnp_ref, w1_ref, w2_ref):
    mix = _mm(o_r, wout_ref[:R_WIDTH, :]) + _mm(o_m, wout_ref[R_WIDTH:, :])
    h = _layer_norm(ALPHA * x + mix, lnp_ref[0:1, :], lnp_ref[1:2, :], LN_EPS)
    hb = h.astype(BF16)
    ff = jnp.zeros_like(h)
    for j in range(D_FF // FF_SLAB):
        t = jnp.dot(hb, w1_ref[:, j * FF_SLAB:(j + 1) * FF_SLAB], preferred_element_type=F32)
        t = jnp.square(jnp.maximum(t, 0.0))
        ff = ff + jnp.dot(t.astype(BF16), w2_ref[j * FF_SLAB:(j + 1) * FF_SLAB, :],
                          preferred_element_type=F32)
    return _layer_norm(ALPHA * h + ff, lnp_ref[2:3, :], lnp_ref[3:4, :], LN_EPS)


def _ffn_kernel(xp_ref, orp_ref, omp_ref, xs_ref, ors_ref, yms_ref, zs_ref, mv_ref,
                wout_ref, w1_ref, w2_ref, lnp_ref, yp_ref, ys_ref):
    i = pl.program_id(0)
    last = pl.num_programs(0) - 1

    @pl.when(i < last)
    def _():
        yp_ref[...] = _ffn_math(xp_ref[...], orp_ref[...], omp_ref[...], wout_ref, lnp_ref, w1_ref, w2_ref)

    @pl.when(i == last)
    def _():
        o_m = _gated_rmsnorm(yms_ref[...], zs_ref[...], mv_ref[1:2, :])
        ys_ref[...] = _ffn_math(xs_ref[...], ors_ref[...], o_m, wout_ref, lnp_ref, w1_ref, w2_ref)


def _ffn(prompt, sample, mv, wout, w1, w2, lnp, tm):
    n = prompt[0].shape[0]
    ns = sample[0].shape[0]
    ntile = n // tm
    tok = lambda w: pl.BlockSpec((tm, w), lambda i: (jnp.minimum(i, ntile - 1), 0))
    whole = lambda shape: pl.BlockSpec(shape, lambda i: (0, 0), pipeline_mode=pl.Buffered(1))
    consts = list(sample) + [mv, wout, w1, w2, lnp]
    return pl.pallas_call(
        _ffn_kernel,
        grid=(ntile + 1,),
        in_specs=[tok(D_MODEL), tok(R_WIDTH), tok(M_WIDTH)] + [whole(a.shape) for a in consts],
        out_specs=[tok(D_MODEL), whole((ns, D_MODEL))],
        out_shape=[jax.ShapeDtypeStruct((n, D_MODEL), F32), jax.ShapeDtypeStruct((ns, D_MODEL), F32)],
        compiler_params=pltpu.CompilerParams(dimension_semantics=("arbitrary",),
                                             vmem_limit_bytes=VMEM_LIMIT),
        name="ffn",
    )(*prompt, *consts)


def _block_diag_ones():
    idx = np.arange(R_WIDTH) // R_HEAD
    return jnp.asarray(idx[:, None] == idx[None, :], dtype=BF16)


def kernel(x_prompt, x_sample, state_shift, state_wkv, state_conv, state_ssm, w_in, mu_shift, w0, w_up, a0, a_up, g_up, k_k, k_a, r_k, lnx_w, lnx_b, conv_w, conv_b, dt_bias, a_log, d_skip, gnorm_w, w_out, ln1_g, ln1_b, w_ff1, w_ff2, ln2_g, ln2_b):
    assert w_in.shape[0] == DEPTH and x_sample.shape[1] == 1
    batch, seq, _ = x_prompt.shape
    nb = x_sample.shape[0]
    assert seq % MAMBA_TILE == 0 and seq % RWKV_TILE == 0 and seq % DENSE_ROWS == 0
    assert nb % STEP_BATCH == 0 and batch % RWKV_ROWS == 0
    row = lambda a: a.reshape(1, -1)

    w_t = jnp.transpose(w_in[0])
    w_dt = w_t[PROJ_MAIN:]
    zl = jnp.zeros((D_DECAY_LORA, R_WIDTH), F32)
    wup_p = jnp.concatenate([w_up[0], zl], axis=0).astype(BF16)
    aup_p = jnp.concatenate([zl, a_up[0]], axis=0).astype(BF16)
    bd = _block_diag_ones()

    def rows8(*vecs):
        m = jnp.stack([v.reshape(-1) for v in vecs])
        return jnp.pad(m, ((0, SUBLANES - len(vecs)), (0, 0)))

    rv = rows8(w0[0], a0[0], k_k[0], k_a[0], r_k[0], lnx_w[0], lnx_b[0])
    prep_w = (row(mu_shift[0]), rv, wup_p, aup_p, g_up[0].astype(BF16), bd)
    cw = rows8(*conv_w[0], conv_b[0])
    dta = rows8(jnp.pad(dt_bias[0], (0, DT_PAD - M_HEADS)), jnp.pad(a_log[0], (0, DT_PAD - M_HEADS)))
    mv = rows8(jnp.repeat(d_skip[0], M_HEADDIM), gnorm_w[0])
    dskip_rows = jnp.broadcast_to(d_skip[0][:, None], (M_HEADS, M_HEADDIM))
    lnp = rows8(ln1_g[0], ln1_b[0], ln2_g[0], ln2_b[0])

    xp = x_prompt.reshape(batch * seq, D_MODEL)
    xs2d = x_sample.reshape(nb, D_MODEL)
    pr_p, z_p, xbc_p, dt_p, pr_s, z_s, xbc_s, dt_s = _proj(xp, xs2d, w_t, w_dt, DENSE_ROWS)

    o_r_p, wkv_p = _rwkv_prompt(pr_p.reshape(batch, seq, RWKV_PROJ), prep_w)
    o_r_p = o_r_p.reshape(batch * seq, R_WIDTH)
    o_m_p, ssm_p, wout_bf, w1_bf, w2_bf = _mamba_prompt(
        z_p, xbc_p, dt_p, batch, seq, (cw, dta, mv), (w_out[0], w_ff1[0], w_ff2[0]))
    shift_p = pr_p.reshape(batch, seq, RWKV_PROJ)[:, -1]
    conv_p = xbc_p.reshape(batch, seq, CONV_DIM)[:, seq - (CONV_K - 1):]

    cols = jnp.broadcast_to(jnp.stack([r_k[0].reshape(-1), lnx_w[0], lnx_b[0]])[:, :, None],
                            (3, R_WIDTH, nb))
    o_r_t, wkv_t = _rwkv_sample(pr_s, state_shift[0], jnp.transpose(state_wkv[0], (1, 2, 3, 0)),
                                prep_w, cols)
    o_r_s = o_r_t.T
    wkv_s = jnp.transpose(wkv_t, (3, 0, 1, 2))
    xs_s, xdt_s, dec_s, bm_s, cm_s, conv_t = _mamba_sample_prep(
        xbc_s, jnp.transpose(state_conv[0], (1, 0, 2)), dt_s, cw, dta)
    conv_s = jnp.transpose(conv_t, (1, 0, 2))
    y_m_s, ssm_s = _mamba_sample_step(xs_s, xdt_s, dec_s, bm_s, cm_s, state_ssm[0], dskip_rows)

    y_p, y_s = _ffn((xp, o_r_p, o_m_p), (xs2d, o_r_s, y_m_s.reshape(nb, M_WIDTH), z_s), mv,
                    wout_bf, w1_bf, w2_bf, lnp, tm=DENSE_ROWS)

    return (y_p.reshape(batch, seq, D_MODEL), y_s.reshape(nb, 1, D_MODEL),
            shift_p[None], wkv_p[None], conv_p[None],
            ssm_p.reshape(1, batch, M_HEADS, M_HEADDIM, D_STATE),
            pr_s[None], wkv_s.reshape(1, nb, R_HEADS, R_HEAD, R_HEAD), conv_s[None],
            ssm_s.reshape(1, nb, M_HEADS, M_HEADDIM, D_STATE))
```

```python
import math

import numpy as np
import jax
import jax.numpy as jnp
from jax import lax
from jax.experimental import pallas as pl
from jax.experimental.pallas import tpu as pltpu

F32 = jnp.float32
BF16 = jnp.bfloat16

D_MODEL = 1024
R_WIDTH = 512
R_HEAD = 64
R_HEADS = 8
D_DECAY_LORA = 64
D_AAA_LORA = 64
D_GATE_LORA = 128
M_WIDTH = 512
M_HEADDIM = 64
M_HEADS = 8
M_GROUPS = 2
D_STATE = 128
CONV_K = 4
CONV_DIM = M_WIDTH + 2 * M_GROUPS * D_STATE
D_FF = 4 * D_MODEL
RWKV_PROJ = 3 * R_WIDTH + D_DECAY_LORA + D_AAA_LORA + D_GATE_LORA
DT_PAD = 128
PROJ_MAIN = RWKV_PROJ + M_WIDTH + CONV_DIM
DEPTH = 1
ALPHA = (2 * DEPTH) ** 0.25
LN_EPS = 1e-5
LNX_EPS = 64e-5
RMS_EPS = 1e-5

RWKV_CHUNK = 64
RWKV_TILE = 256
RWKV_ROWS = 4
RWKV_WAVES = 2
SSD_CHUNK = 128
MAMBA_TILE = 256
SUBLANES = 8
DENSE_ROWS = 512
VMEM_LIMIT = 48 * 1024 * 1024
PROJ_W_VMEM_LIMIT = 56 * 1024 * 1024

_NT = (((1,), (1,)), ((), ()))


def _mm(a, b):
    return jnp.dot(a.astype(BF16), b.astype(BF16), preferred_element_type=F32)


def _mm_nt(a, b):
    return lax.dot_general(a.astype(BF16), b.astype(BF16), _NT, preferred_element_type=F32)


def _split(x, passes):
    parts = []
    for _ in range(passes):
        p = x.astype(BF16)
        parts.append(p)
        x = x - p.astype(F32)
    return parts


def _mm_sel_r(x, sel, passes):
    return sum(jnp.dot(p, sel, preferred_element_type=F32) for p in _split(x, passes))


def _mm_sel_l(sel, x, passes):
    return sum(jnp.dot(sel, p, preferred_element_type=F32) for p in _split(x, passes))


def _softplus(x):
    return jnp.maximum(x, 0.0) + jnp.log(1.0 + jnp.exp(-jnp.abs(x)))


def _sigmoid(x):
    return 0.5 + 0.5 * jnp.tanh(0.5 * x)


def _silu(x):
    return x * _sigmoid(x)


def _layer_norm(x, g, b, eps):
    mu = jnp.mean(x, axis=-1, keepdims=True)
    d = x - mu
    var = jnp.mean(d * d, axis=-1, keepdims=True)
    return d * lax.rsqrt(var + eps) * g + b


def _iota2(shape, axis):
    return lax.broadcasted_iota(jnp.int32, shape, axis)


def _dt_columns(xb, wdt_t):
    pad = jnp.zeros((DT_PAD - M_HEADS, D_MODEL), F32)
    return _mm_nt(xb, jnp.concatenate([wdt_t, pad], axis=0))


W_SLAB = 256


def _proj_kernel(xp_ref, xs_ref, wt_ref, wdt_ref,
                 prp_ref, zp_ref, xbcp_ref, dtp_ref, prs_ref, zs_ref, xbcs_ref, dts_ref, wbf_scr):
    i = pl.program_id(0)
    last = pl.num_programs(0) - 1

    @pl.when(i == 0)
    def _():
        for c in range(0, PROJ_MAIN, W_SLAB):
            wbf_scr[:, c:c + W_SLAB] = wt_ref[c:c + W_SLAB, :].T.astype(BF16)

    def project(x_ref, pr_ref, z_ref, xbc_ref, dt_ref):
        xb = x_ref[...].astype(BF16)
        res = jnp.dot(xb, wbf_scr[...], preferred_element_type=F32)
        c0 = RWKV_PROJ
        c1 = c0 + M_WIDTH
        pr_ref[...] = res[:, :c0]
        z_ref[...] = res[:, c0:c1]
        xbc_ref[...] = res[:, c1:]
        dt_ref[...] = _dt_columns(xb, wdt_ref[...])

    @pl.when(i < last)
    def _():
        project(xp_ref, prp_ref, zp_ref, xbcp_ref, dtp_ref)

    @pl.when(i == last)
    def _():
        project(xs_ref, prs_ref, zs_ref, xbcs_ref, dts_ref)


def _proj(xp2d, xs2d, w_t, wdt_t, tm):
    n = xp2d.shape[0]
    ns = xs2d.shape[0]
    ntile = n // tm
    widths = (RWKV_PROJ, M_WIDTH, CONV_DIM, DT_PAD)
    once = lambda shape: pl.BlockSpec(shape, lambda i: (0, 0), pipeline_mode=pl.Buffered(1))
    tile = lambda w: pl.BlockSpec((tm, w), lambda i: (jnp.minimum(i, ntile - 1), 0))
    return pl.pallas_call(
        _proj_kernel,
        grid=(ntile + 1,),
        in_specs=[tile(D_MODEL), once(xs2d.shape), once(w_t.shape), once(wdt_t.shape)],
        out_specs=[tile(w) for w in widths] + [once((ns, w)) for w in widths],
        out_shape=[jax.ShapeDtypeStruct((n, w), F32) for w in widths]
                  + [jax.ShapeDtypeStruct((ns, w), F32) for w in widths],
        scratch_shapes=[pltpu.VMEM((D_MODEL, PROJ_MAIN), BF16)],
        compiler_params=pltpu.CompilerParams(dimension_semantics=("arbitrary",),
                                             vmem_limit_bytes=PROJ_W_VMEM_LIMIT),
        name="proj",
    )(xp2d, xs2d, w_t, wdt_t)


def _rwkv_prep(pr, prev, mu, rv, wup, aup, gup, bd):
    w0, a0, k_k, k_a = (rv[i:i + 1, :] for i in range(4))
    u = pr + (prev - pr) * mu
    r = u[:, 0:R_WIDTH]
    k = u[:, R_WIDTH:2 * R_WIDTH]
    v = u[:, 2 * R_WIDTH:3 * R_WIDTH]
    lora = u[:, 3 * R_WIDTH:3 * R_WIDTH + D_DECAY_LORA + D_AAA_LORA]
    ug = u[:, 3 * R_WIDTH + D_DECAY_LORA + D_AAA_LORA:]
    w_log = -_softplus(-(w0 + _mm(jnp.tanh(lora), wup))) - 0.5
    logw = -jnp.exp(w_log)
    a = _sigmoid(a0 + _mm(lora, aup))
    g = _mm(_sigmoid(ug), gup)
    kk = k * k_k
    ss = _mm_sel_r(kk * kk, bd, 1)
    kkn = kk * lax.rsqrt(jnp.maximum(ss, 1e-24))
    k2 = k * (1.0 + (a - 1.0) * k_a)
    return r, logw, k2, v, -kkn, kkn * a, g


def _rwkv_post(y, r, k2, v, g, rv, bd):
    r_k, lnx_w, lnx_b = (rv[i:i + 1, :] for i in range(4, 7))
    inv = 1.0 / R_HEAD
    mu = _mm_sel_r(y, bd, 1) * inv
    d = y - mu
    var = _mm_sel_r(d * d, bd, 1) * inv
    yn = d * lax.rsqrt(var + LNX_EPS) * lnx_w + lnx_b
    bonus = _mm_sel_r(r * k2 * r_k, bd, 1)
    return (yn + bonus * v) * g


def _rwkv_tile_kernel(pr_ref, prevblk_ref, mu_ref, rv_ref, wup_ref, aup_ref, gup_ref, bd_ref,
                      o_ref, sfin_ref, g_scr):
    c = RWKV_CHUNK
    tile = RWKV_TILE
    nseq = RWKV_ROWS
    nck = tile // c
    hw = 2 * R_HEAD
    npair = R_HEADS // 2
    ti = pl.program_id(1)

    @pl.when(ti == 0)
    def _():
        g_scr[...] = jnp.zeros_like(g_scr)

    pr = jnp.concatenate([pr_ref[s] for s in range(nseq)], axis=0)
    prev = pltpu.roll(pr, 1, axis=0)
    row_id = _iota2((nseq * tile, 1), 0)
    for s in range(nseq):
        prev_row = jnp.where(ti == 0, 0.0, prevblk_ref[s, SUBLANES - 1:SUBLANES, :])
        prev = jnp.where(row_id == s * tile, prev_row, prev)
    head_ones = bd_ref[...]
    r, logw, k2, v, av, bv, g = _rwkv_prep(pr, prev, mu_ref[...], rv_ref[...], wup_ref[...],
                                           aup_ref[...], gup_ref[...], head_ones)

    sh_c = int(math.log2(c))
    ti_i = _iota2((tile, tile), 0)
    ti_j = _iota2((tile, tile), 1)
    chunk_tril = jnp.where((ti_i >= ti_j) & ((ti_i >> sh_c) == (ti_j >> sh_c)), 1.0, 0.0).astype(BF16)
    cum = jnp.concatenate([_mm_sel_l(chunk_tril, logw[s * tile:(s + 1) * tile], 3) for s in range(nseq)],
                          axis=0)
    cl = jnp.concatenate(
        [jnp.broadcast_to(cum[(q + 1) * c - 1:(q + 1) * c, :], (c, R_WIDTH)) for q in range(nseq * nck)],
        axis=0)
    p_out = jnp.exp(-cum)
    p_end = jnp.exp(cl - cum)
    rt = r * jnp.exp(cum)
    at = av * jnp.exp(cum - logw)
    bt = bv * p_out
    kt = k2 * p_out
    bh = bv * p_end
    kh = k2 * p_end
    pc = jnp.exp(cl)

    ii = _iota2((c, hw), 0)
    jj = _iota2((c, hw), 1) & (c - 1)
    strict = ii > jj
    incl = ii >= jj
    lane = _iota2((1, hw), 1)
    keep = (jnp.where(lane < R_HEAD, 1.0, 0.0).astype(BF16), jnp.where(lane >= R_HEAD, 1.0, 0.0).astype(BF16))
    e_r = _iota2((hw, hw), 0)
    e_c = _iota2((hw, hw), 1)
    eye_hw = e_r == e_c
    blockdiag = (e_r >= R_HEAD) == (e_c >= R_HEAD)

    def bd(x):
        xb = x.astype(BF16)
        return jnp.concatenate([xb * keep[0], xb * keep[1]], axis=0)

    def pdot(a, b_bd):
        return jnp.dot(a.astype(BF16), b_bd, preferred_element_type=F32)

    blk = lambda x, q, p: x[q * c:(q + 1) * c, hw * p:hw * (p + 1)]
    zeros_c = jnp.zeros((c, hw), F32)
    eye = jnp.where(ii == jj, 1.0, 0.0)

    def phase1(qp):
        lhs = [jnp.concatenate([blk(at, *k), blk(rt, *k)], axis=0).astype(BF16) for k in qp]
        sc = [lax.dot_general(x, jnp.concatenate([bd(blk(bt, *k)), bd(blk(kt, *k))], axis=0), _NT,
                              preferred_element_type=F32) for x, k in zip(lhs, qp)]
        a_ab = [jnp.where(strict, s[:c, :hw], 0.0) for s in sc]
        a_rb = [jnp.where(incl, s[c:, :hw], 0.0) for s in sc]
        a_akrk = [jnp.concatenate([jnp.where(strict, s[:c, hw:], 0.0), jnp.where(incl, s[c:, hw:], 0.0)], axis=0)
                  for s in sc]
        v_bd = [bd(blk(v, *k)) for k in qp]
        akrk_v = [pdot(a, x) for a, x in zip(a_akrk, v_bd)]

        sh = 4
        npow = [jnp.where((ii >> sh) == (jj >> sh), n, 0.0) for n in a_ab]
        t_inv = [eye + n for n in npow]
        npow = [pdot(n, bd(n)) for n in npow]
        for _ in range(sh - 2):
            both = [pdot(jnp.concatenate([t, n], axis=0), bd(n)) for t, n in zip(t_inv, npow)]
            t_inv = [t + x[:c] for t, x in zip(t_inv, both)]
            npow = [x[c:] for x in both]
        t_inv = [t + pdot(t, bd(n)) for t, n in zip(t_inv, npow)]
        while (1 << sh) < c:
            off = ((ii >> (sh + 1)) == (jj >> (sh + 1))) & (((ii >> sh) & 1) > ((jj >> sh) & 1))
            tn = [pdot(t, bd(jnp.where(off, n, 0.0))) for t, n in zip(t_inv, a_ab)]
            t_inv = [t + pdot(x, bd(t)) for t, x in zip(t_inv, tn)]
            sh += 1

        tx = [pdot(t, jnp.concatenate([bd(blk(at, *k)), bd(x[:c])], axis=1))
              for t, x, k in zip(t_inv, akrk_v, qp)]
        rbx = [pdot(a, jnp.concatenate([bd(x[:, :hw]), bd(x[:, hw:])], axis=1)) for a, x in zip(a_rb, tx)]
        rp_pair = {k: blk(rt, *k) + x[:, :hw] for k, x in zip(qp, rbx)}
        y0_pair = {k: x[:, hw:] + y[c:] for k, x, y in zip(qp, rbx, akrk_v)}

        wt = {k: jnp.concatenate([blk(bh, *k), blk(kh, *k)], axis=0).T for k in qp}
        trans = {k: _mm(wt[k], jnp.concatenate([x, jnp.concatenate([zeros_c, blk(v, *k)], axis=1)], axis=0))
                 for k, x in zip(qp, tx)}
        m_ba = {k: jnp.where(blockdiag, trans[k][:, :hw], 0.0) for k in qp}
        h0c = {k: jnp.where(blockdiag, trans[k][:, hw:], 0.0) for k in qp}
        pc_col = {k: jnp.sum(jnp.where(eye_hw, blk(pc, *k)[:1], 0.0), axis=1, keepdims=True) for k in qp}
        return rp_pair, y0_pair, m_ba, h0c, pc_col

    rp_pair, y0_pair, m_ba, h0c, pc_col = {}, {}, {}, {}, {}
    all_qp = [(q, p) for q in range(nseq * nck) for p in range(npair)]
    wave = len(all_qp) // RWKV_WAVES
    for w0 in range(0, len(all_qp), wave):
        for dst, src in zip((rp_pair, y0_pair, m_ba, h0c, pc_col), phase1(all_qp[w0:w0 + wave])):
            dst.update(src)


    state = [g_scr[i] for i in range(nseq * npair)]
    y_rows = [None] * (nseq * nck)
    for q in range(nck):
        for s in range(nseq):
            qq = s * nck + q
            st = state[s * npair:(s + 1) * npair]
            on_st = [_mm(jnp.concatenate([rp_pair[qq, p], m_ba[qq, p]], axis=0), st[p]) for p in range(npair)]
            y_rows[qq] = jnp.concatenate([on_st[p][:c] + y0_pair[qq, p] for p in range(npair)], axis=1)
            state[s * npair:(s + 1) * npair] = [
                pc_col[qq, p] * st[p] + on_st[p][c:] + h0c[qq, p] for p in range(npair)]
    for i in range(nseq * npair):
        g_scr[i] = state[i]

    y = jnp.concatenate(y_rows, axis=0)
    out = _rwkv_post(y, r, k2, v, g, rv_ref[...], head_ones)
    for s in range(nseq):
        o_ref[s] = out[s * tile:(s + 1) * tile]

    @pl.when(ti == pl.num_programs(1) - 1)
    def _():
        for s in range(nseq):
            for p in range(npair):
                gt = g_scr[s * npair + p].T
                sfin_ref[s, 2 * p] = gt[:R_HEAD, :R_HEAD]
                sfin_ref[s, 2 * p + 1] = pltpu.roll(gt, R_HEAD, axis=1)[R_HEAD:, :R_HEAD]


def _const_spec(shape):
    return pl.BlockSpec(shape, lambda *_: (0,) * len(shape))


def _rwkv_prompt(pr3d, wts):
    batch, seq, _ = pr3d.shape
    tile = RWKV_TILE
    nseq = RWKV_ROWS

    def prev_map(b, i):
        return (b, jnp.maximum(i * (tile // SUBLANES) - 1, 0), 0)

    w_specs = [_const_spec(w.shape) for w in wts]
    return pl.pallas_call(
        _rwkv_tile_kernel,
        grid=(batch // nseq, seq // tile),
        in_specs=[pl.BlockSpec((nseq, tile, RWKV_PROJ), lambda b, i: (b, i, 0)),
                  pl.BlockSpec((nseq, SUBLANES, RWKV_PROJ), prev_map)] + w_specs,
        out_specs=[pl.BlockSpec((nseq, tile, R_WIDTH), lambda b, i: (b, i, 0)),
                   pl.BlockSpec((nseq, R_HEADS, R_HEAD, R_HEAD), lambda b, i: (b, 0, 0, 0))],
        out_shape=[jax.ShapeDtypeStruct((batch, seq, R_WIDTH), F32),
                   jax.ShapeDtypeStruct((batch, R_HEADS, R_HEAD, R_HEAD), F32)],
        scratch_shapes=[pltpu.VMEM((nseq * (R_HEADS // 2), 2 * R_HEAD, 2 * R_HEAD), F32)],
        compiler_params=pltpu.CompilerParams(dimension_semantics=("arbitrary", "arbitrary"),
                                             vmem_limit_bytes=VMEM_LIMIT),
        name="rwkv_prompt",
    )(pr3d, pr3d, *wts)


def _rwkv_sample_kernel(pr_ref, prev_ref, mu_ref, rv_ref, wup_ref, aup_ref, gup_ref, bd_ref,
                        cols_ref, s_ref, o_ref, so_ref, vec_scr):
    h = pl.program_id(0)

    @pl.when(h == 0)
    def _():
        r, logw, k2, v, av, bv, g = _rwkv_prep(pr_ref[...], prev_ref[...], mu_ref[...], rv_ref[...],
                                               wup_ref[...], aup_ref[...], gup_ref[...], bd_ref[...])
        for i, x in enumerate((r, jnp.exp(logw), k2, v, av, bv, g)):
            vec_scr[i] = x.T

    rows = pl.ds(pl.multiple_of(h * R_HEAD, R_HEAD), R_HEAD)
    r, w, k, v, a, b, g = (vec_scr[i, rows, :] for i in range(7))
    ys = []
    for vi in range(R_HEAD):
        s = s_ref[0, vi]
        sa = jnp.sum(s * a, axis=0, keepdims=True)
        s2 = s * w + sa * b + v[vi:vi + 1, :] * k
        so_ref[0, vi] = s2
        ys.append(jnp.sum(s2 * r, axis=0, keepdims=True))
    y = jnp.concatenate(ys, axis=0)
    mu = jnp.mean(y, axis=0, keepdims=True)
    d = y - mu
    var = jnp.mean(d * d, axis=0, keepdims=True)
    yn = d * lax.rsqrt(var + LNX_EPS) * cols_ref[1] + cols_ref[2]
    bonus = jnp.sum(r * k * cols_ref[0], axis=0, keepdims=True)
    o_ref[...] = (yn + bonus * v) * g


STEP_BATCH = 16


def _per_head_rows(x, n):
    return jnp.concatenate([jnp.broadcast_to(x[h:h + 1, :], (n, x.shape[1])) for h in range(x.shape[0])],
                           axis=0)


def _rows_from_columns(col_b, diag, n):
    picked = jnp.where(diag, col_b, 0.0)
    return jnp.concatenate([jnp.sum(picked[h * n:(h + 1) * n], axis=0, keepdims=True)
                            for h in range(col_b.shape[0] // n)], axis=0)


def _shift_rows(x, prev8, s):
    rolled = pltpu.roll(x, s, axis=0)
    top = jnp.where(_iota2((SUBLANES, 1), 0) < s, pltpu.roll(prev8, s, axis=0), rolled[:SUBLANES])
    return jnp.concatenate([top, rolled[SUBLANES:]], axis=0)


def _head_expand(lanes_per_head=M_HEADDIM):
    shape = (DT_PAD, M_HEADS * lanes_per_head)
    head_of_lane = _iota2(shape, 1) >> int(math.log2(lanes_per_head))
    return jnp.where(_iota2(shape, 0) == head_of_lane, 1.0, 0.0).astype(BF16)


def _gated_rmsnorm(y, z, gnorm_w):
    yz = y * _silu(z)
    gw = M_WIDTH // M_GROUPS
    outs = []
    for gi in range(M_GROUPS):
        yg = yz[:, gi * gw:(gi + 1) * gw]
        ms = jnp.mean(yg * yg, axis=-1, keepdims=True)
        outs.append(yg * lax.rsqrt(ms + RMS_EPS))
    return jnp.concatenate(outs, axis=1) * gnorm_w


def _mamba_tile_kernel(z_ref, xbc_ref, xprev_ref, dt_ref, cw_ref, dta_ref, mv_ref,
                       wout32_ref, w1_32_ref, w2_32_ref,
                       o_ref, hfin_ref, wout_ref, w1_ref, w2_ref, h_scr):
    tile = MAMBA_TILE
    q = SSD_CHUNK
    ti = pl.program_id(1)
    wout_ref[...] = wout32_ref[...].astype(BF16)
    w1_ref[...] = w1_32_ref[...].astype(BF16)
    w2_ref[...] = w2_32_ref[...].astype(BF16)

    @pl.when(ti == 0)
    def _():
        h_scr[...] = jnp.zeros_like(h_scr)

    xb = xbc_ref[...]
    prev8 = jnp.where(ti == 0, 0.0, xprev_ref[...])
    conv = cw_ref[CONV_K:CONV_K + 1, :] + xb * cw_ref[CONV_K - 1:CONV_K, :]
    for s in range(1, CONV_K):
        conv = conv + _shift_rows(xb, prev8, s) * cw_ref[CONV_K - 1 - s:CONV_K - s, :]
    xact = _silu(conv)
    xs = xact[:, :M_WIDTH]
    gs = M_GROUPS * D_STATE
    bm = xact[:, M_WIDTH:M_WIDTH + gs]
    cm = xact[:, M_WIDTH + gs:]

    dt = _softplus(dt_ref[...] + dta_ref[0:1, :])
    adt = dt * (-jnp.exp(dta_ref[1:2, :]))
    x_dt = xs * _mm_sel_r(dt, _head_expand(), 3)

    sh_q = int(math.log2(q))
    ti_i = _iota2((tile, tile), 0)
    ti_j = _iota2((tile, tile), 1)
    chunk_tril = jnp.where((ti_i >= ti_j) & ((ti_i >> sh_q) == (ti_j >> sh_q)), 1.0, 0.0).astype(BF16)
    cs = _mm_sel_l(chunk_tril, adt, 3)
    cs_t = cs.T
    incl = _iota2((q, q), 0) >= _iota2((q, q), 1)
    lane = _iota2((1, 2 * M_HEADDIM), 1)
    first = lane < M_HEADDIM
    hpg = M_HEADS // M_GROUPS

    state = [h_scr[p] for p in range(M_HEADS // 2)]
    y_rows = []
    for j in range(tile // q):
        rs = slice(j * q, (j + 1) * q)
        y_pairs = []
        for gi in range(M_GROUPS):
            bm_g = bm[rs, gi * D_STATE:(gi + 1) * D_STATE]
            cm_g = cm[rs, gi * D_STATE:(gi + 1) * D_STATE]
            cb = _mm_nt(cm_g, bm_g)
            bm_t = bm_g.T
            for pp in range(hpg // 2):
                p = gi * (hpg // 2) + pp
                x_p = x_dt[rs, 2 * M_HEADDIM * p:2 * M_HEADDIM * (p + 1)]
                y_diag = jnp.zeros((q, 2 * M_HEADDIM), F32)
                contrib = jnp.zeros((D_STATE, 2 * M_HEADDIM), F32)
                scale_in = []
                scale_end = []
                for e in range(2):
                    head = 2 * p + e
                    m = first if e == 0 else jnp.logical_not(first)
                    a_col = cs[rs, head:head + 1]
                    a_row = cs_t[head:head + 1, rs]
                    a_end = cs_t[head:head + 1, (j + 1) * q - 1:(j + 1) * q]
                    lmat = jnp.exp(jnp.where(incl, a_col - a_row, -jnp.inf))
                    x_m = jnp.where(m, x_p, 0.0)
                    y_diag = y_diag + _mm(cb * lmat, x_m)
                    contrib = contrib + _mm(bm_t * jnp.exp(a_end - a_row), x_m)
                    scale_in.append(jnp.exp(a_col))
                    scale_end.append(jnp.exp(a_end))
                y_off = _mm(cm_g, state[p]) * jnp.where(first, scale_in[0], scale_in[1])
                state[p] = state[p] * jnp.where(first, scale_end[0], scale_end[1]) + contrib
                y_pairs.append(y_diag + y_off)
        y_rows.append(jnp.concatenate(y_pairs, axis=1))
    for p in range(M_HEADS // 2):
        h_scr[p] = state[p]

    y = jnp.concatenate(y_rows, axis=0) + mv_ref[0:1, :] * xs
    o_ref[...] = _gated_rmsnorm(y, z_ref[...], mv_ref[1:2, :])

    @pl.when(ti == pl.num_programs(1) - 1)
    def _():
        for p in range(M_HEADS // 2):
            hfin_ref[0, 2 * M_HEADDIM * p:2 * M_HEADDIM * (p + 1), :] = h_scr[p].T


def _mamba_prompt(z2d, xbc2d, dt2d, batch, seq, wts, mlp_w):
    tile = MAMBA_TILE
    ntile = seq // tile
    rows8 = seq // SUBLANES
    nstep = batch * ntile
    slab = lambda a: pl.BlockSpec((a.shape[0] // nstep, a.shape[1]), lambda b, i: (b * ntile + i, 0))

    def prev_map(b, i):
        return (b * rows8 + jnp.maximum(i * (tile // SUBLANES) - 1, 0), 0)

    tok = lambda w: pl.BlockSpec((tile, w), lambda b, i: (b * ntile + i, 0))
    return pl.pallas_call(
        _mamba_tile_kernel,
        grid=(batch, ntile),
        in_specs=[tok(M_WIDTH), tok(CONV_DIM), pl.BlockSpec((SUBLANES, CONV_DIM), prev_map), tok(DT_PAD)]
                 + [_const_spec(w.shape) for w in wts] + [slab(w) for w in mlp_w],
        out_specs=[tok(M_WIDTH),
                   pl.BlockSpec((1, M_HEADS * M_HEADDIM, D_STATE), lambda b, i: (b, 0, 0))]
                  + [slab(w) for w in mlp_w],
        out_shape=[jax.ShapeDtypeStruct((batch * seq, M_WIDTH), F32),
                   jax.ShapeDtypeStruct((batch, M_HEADS * M_HEADDIM, D_STATE), F32)]
                  + [jax.ShapeDtypeStruct(w.shape, BF16) for w in mlp_w],
        scratch_shapes=[pltpu.VMEM((M_HEADS // 2, D_STATE, 2 * M_HEADDIM), F32)],
        compiler_params=pltpu.CompilerParams(dimension_semantics=("arbitrary", "arbitrary"),
                                             vmem_limit_bytes=VMEM_LIMIT),
        name="mamba_prompt",
    )(z2d, xbc2d, xbc2d, dt2d, *wts, *mlp_w)


def _mamba_prep_kernel(xbc_ref, cprev_ref, dt_ref, cw_ref, dta_ref,
                       xs_ref, xdt_ref, dec_ref, bm_ref, cm_ref, cnew_ref):
    xbc = xbc_ref[...]
    conv = cw_ref[CONV_K:CONV_K + 1, :] + xbc * cw_ref[CONV_K - 1:CONV_K, :]
    for j in range(CONV_K - 1):
        conv = conv + cprev_ref[j] * cw_ref[j:j + 1, :]
        cnew_ref[j] = cprev_ref[j + 1] if j + 1 < CONV_K - 1 else xbc
    xact = _silu(conv)
    xs = xact[:, :M_WIDTH]
    gs = M_GROUPS * D_STATE
    dt = _softplus(dt_ref[...] + dta_ref[0:1, :])
    adt = dt * (-jnp.exp(dta_ref[1:2, :]))
    expand = _head_expand()
    xs_ref[...] = xs
    xdt_ref[...] = xs * _mm_sel_r(dt, expand, 3)
    dec_ref[...] = jnp.exp(_mm_sel_r(adt, _head_expand(D_STATE), 3))
    hpg = M_HEADS // M_GROUPS
    per_head = lambda m: jnp.concatenate(
        [m[:, (e // hpg) * D_STATE:(e // hpg + 1) * D_STATE] for e in range(M_HEADS)], axis=1)
    bm_ref[...] = per_head(xact[:, M_WIDTH:M_WIDTH + gs])
    cm_ref[...] = per_head(xact[:, M_WIDTH + gs:])


def _mamba_sample_prep(xbc, cprev_t, dt, cw, dta):
    n = xbc.shape[0]
    args = (xbc, cprev_t, dt, cw, dta)
    widths = (M_WIDTH, M_WIDTH, M_HEADS * D_STATE, M_HEADS * D_STATE, M_HEADS * D_STATE)
    out_shapes = [(n, w) for w in widths] + [cprev_t.shape]
    return pl.pallas_call(
        _mamba_prep_kernel,
        grid=(1,),
        in_specs=[_const_spec(a.shape) for a in args],
        out_specs=[_const_spec(s) for s in out_shapes],
        out_shape=[jax.ShapeDtypeStruct(s, F32) for s in out_shapes],
        compiler_params=pltpu.CompilerParams(vmem_limit_bytes=VMEM_LIMIT),
        name="mamba_sample_prep",
    )(*args)


def _mamba_step_kernel(xs_ref, xdt_ref, dec_ref, bm_ref, cm_ref, h_ref, dskip_ref, y_ref, ho_ref):
    n = M_HEADS * M_HEADDIM
    ones_in = jnp.ones((M_HEADDIM, D_STATE), BF16)
    ones_out = jnp.ones((D_STATE, M_HEADDIM), BF16)
    diag = (_iota2((n, M_HEADDIM), 0) & (M_HEADDIM - 1)) == _iota2((n, M_HEADDIM), 1)

    def head_rows(ref, bi):
        return jnp.concatenate(
            [jnp.broadcast_to(ref[bi:bi + 1, e * D_STATE:(e + 1) * D_STATE], (M_HEADDIM, D_STATE))
             for e in range(M_HEADS)], axis=0)

    for bi in range(STEP_BATCH):
        rows = slice(bi * M_HEADS, (bi + 1) * M_HEADS)
        hrows = slice(bi * n, (bi + 1) * n)
        x_col = _mm(jnp.where(diag, _per_head_rows(xdt_ref[rows, :], M_HEADDIM), 0.0), ones_in)
        h2 = h_ref[hrows, :] * head_rows(dec_ref, bi) + x_col * head_rows(bm_ref, bi)
        ho_ref[hrows, :] = h2
        y = _rows_from_columns(_mm(h2 * head_rows(cm_ref, bi), ones_out), diag, M_HEADDIM)
        y_ref[rows, :] = y + dskip_ref[...] * xs_ref[rows, :]


_N_RWKV_STEP_IN = 10
_N_MAMBA_STEP_IN = 7


def _sample_steps_kernel(*refs):
    a, b = _N_RWKV_STEP_IN, _N_RWKV_STEP_IN + _N_MAMBA_STEP_IN
    o_t_ref, so_ref, ym_ref, ho_ref, vec_scr = refs[b:]
    _rwkv_sample_kernel(*refs[:a], o_t_ref, so_ref, vec_scr)
    _mamba_step_kernel(*refs[a:b], ym_ref, ho_ref)


def _sample_steps(pr, prev, state_t, prep_w, cols, xs, xdt, dec, bm, cm, state_m, dskip_rows):
    nb = pr.shape[0]
    assert nb // STEP_BATCH == R_HEADS
    r_args = (pr, prev) + tuple(prep_w)
    head_rows = pl.BlockSpec((R_HEAD, nb), lambda i: (i, 0))
    rst_spec = pl.BlockSpec((1, R_HEAD, R_HEAD, nb), lambda i: (i, 0, 0, 0))
    rows = STEP_BATCH * M_HEADS
    vec_spec = pl.BlockSpec((rows, M_HEADDIM), lambda i: (i, 0))
    bc_spec = pl.BlockSpec((STEP_BATCH, M_HEADS * D_STATE), lambda i: (i, 0))
    mst_spec = pl.BlockSpec((rows * M_HEADDIM, D_STATE), lambda i: (i, 0))
    flat = lambda x, w: x.reshape(nb * M_HEADS, w)
    in_specs = ([_const_spec(a.shape) for a in r_args]
                + [pl.BlockSpec((3, R_HEAD, nb), lambda i: (0, i, 0)), rst_spec]
                + [vec_spec] * 2 + [bc_spec] * 3 + [mst_spec, _const_spec((M_HEADS, M_HEADDIM))])
    assert len(in_specs) == _N_RWKV_STEP_IN + _N_MAMBA_STEP_IN
    return pl.pallas_call(
        _sample_steps_kernel,
        grid=(R_HEADS,),
        in_specs=in_specs,
        out_specs=[head_rows, rst_spec, vec_spec, mst_spec],
        out_shape=[jax.ShapeDtypeStruct((R_WIDTH, nb), F32),
                   jax.ShapeDtypeStruct(state_t.shape, F32),
                   jax.ShapeDtypeStruct((nb * M_HEADS, M_HEADDIM), F32),
                   jax.ShapeDtypeStruct((nb * M_HEADS * M_HEADDIM, D_STATE), F32)],
        scratch_shapes=[pltpu.VMEM((7, R_WIDTH, nb), F32)],
        compiler_params=pltpu.CompilerParams(dimension_semantics=("arbitrary",),
                                             vmem_limit_bytes=VMEM_LIMIT),
        name="sample_steps",
    )(*r_args, cols, state_t, flat(xs, M_HEADDIM), flat(xdt, M_HEADDIM), dec, bm, cm,
      state_m.reshape(nb * M_HEADS * M_HEADDIM, D_STATE), dskip_rows)


FF_SLAB = 2048


def _ffn_math(x, o_r, o_m, wout_ref, lnp_ref, w1_ref, w2_ref):
    mix = _mm(o_r, wout_ref[:R_WIDTH, :]) + _mm(o_m, wout_ref[R_WIDTH:, :])
    h = _layer_norm(ALPHA * x + mix, lnp_ref[0:1, :], lnp_ref[1:2, :], LN_EPS)
    hb = h.astype(BF16)
    ff = jnp.zeros_like(h)
    for j in range(D_FF // FF_SLAB):
        t = jnp.dot(hb, w1_ref[:, j * FF_SLAB:(j + 1) * FF_SLAB], preferred_element_type=F32)
        t = jnp.square(jnp.maximum(t, 0.0))
        ff = ff + jnp.dot(t.astype(BF16), w2_ref[j * FF_SLAB:(j + 1) * FF_SLAB, :],
                          preferred_element_type=F32)
    return _layer_norm(ALPHA * h + ff, lnp_ref[2:3, :], lnp_ref[3:4, :], LN_EPS)


def _ffn_kernel(xp_ref, orp_ref, omp_ref, xs_ref, ors_ref, yms_ref, zs_ref, mv_ref,
                wout_ref, w1_ref, w2_ref, lnp_ref, yp_ref, ys_ref):
    i = pl.program_id(0)
    last = pl.num_programs(0) - 1

    @pl.when(i < last)
    def _():
        yp_ref[...] = _ffn_math(xp_ref[...], orp_ref[...], omp_ref[...], wout_ref, lnp_ref, w1_ref, w2_ref)

    @pl.when(i == last)
    def _():
        o_m = _gated_rmsnorm(yms_ref[...], zs_ref[...], mv_ref[1:2, :])
        ys_ref[...] = _ffn_math(xs_ref[...], ors_ref[...], o_m, wout_ref, lnp_ref, w1_ref, w2_ref)


def _ffn(prompt, sample, mv, wout, w1, w2, lnp, tm):
    n = prompt[0].shape[0]
    ns = sample[0].shape[0]
    ntile = n // tm
    tok = lambda w: pl.BlockSpec((tm, w), lambda i: (jnp.minimum(i, ntile - 1), 0))
    whole = lambda shape: pl.BlockSpec(shape, lambda i: (0, 0), pipeline_mode=pl.Buffered(1))
    consts = list(sample) + [mv, wout, w1, w2, lnp]
    return pl.pallas_call(
        _ffn_kernel,
        grid=(ntile + 1,),
        in_specs=[tok(D_MODEL), tok(R_WIDTH), tok(M_WIDTH)] + [whole(a.shape) for a in consts],
        out_specs=[tok(D_MODEL), whole((ns, D_MODEL))],
        out_shape=[jax.ShapeDtypeStruct((n, D_MODEL), F32), jax.ShapeDtypeStruct((ns, D_MODEL), F32)],
        compiler_params=pltpu.CompilerParams(dimension_semantics=("arbitrary",),
                                             vmem_limit_bytes=VMEM_LIMIT),
        name="ffn",
    )(*prompt, *consts)


def _block_diag_ones():
    idx = np.arange(R_WIDTH) // R_HEAD
    return jnp.asarray(idx[:, None] == idx[None, :], dtype=BF16)


def kernel(x_prompt, x_sample, state_shift, state_wkv, state_conv, state_ssm, w_in, mu_shift, w0, w_up, a0, a_up, g_up, k_k, k_a, r_k, lnx_w, lnx_b, conv_w, conv_b, dt_bias, a_log, d_skip, gnorm_w, w_out, ln1_g, ln1_b, w_ff1, w_ff2, ln2_g, ln2_b):
    assert w_in.shape[0] == DEPTH and x_sample.shape[1] == 1
    batch, seq, _ = x_prompt.shape
    nb = x_sample.shape[0]
    assert seq % MAMBA_TILE == 0 and seq % RWKV_TILE == 0 and seq % DENSE_ROWS == 0
    assert nb % STEP_BATCH == 0 and batch % RWKV_ROWS == 0
    row = lambda a: a.reshape(1, -1)

    w_t = jnp.transpose(w_in[0])
    w_dt = w_t[PROJ_MAIN:]
    zl = jnp.zeros((D_DECAY_LORA, R_WIDTH), F32)
    wup_p = jnp.concatenate([w_up[0], zl], axis=0).astype(BF16)
    aup_p = jnp.concatenate([zl, a_up[0]], axis=0).astype(BF16)
    bd = _block_diag_ones()

    def rows8(*vecs):
        m = jnp.stack([v.reshape(-1) for v in vecs])
        return jnp.pad(m, ((0, SUBLANES - len(vecs)), (0, 0)))

    rv = rows8(w0[0], a0[0], k_k[0], k_a[0], r_k[0], lnx_w[0], lnx_b[0])
    prep_w = (row(mu_shift[0]), rv, wup_p, aup_p, g_up[0].astype(BF16), bd)
    cw = rows8(*conv_w[0], conv_b[0])
    dta = rows8(jnp.pad(dt_bias[0], (0, DT_PAD - M_HEADS)), jnp.pad(a_log[0], (0, DT_PAD - M_HEADS)))
    mv = rows8(jnp.repeat(d_skip[0], M_HEADDIM), gnorm_w[0])
    dskip_rows = jnp.broadcast_to(d_skip[0][:, None], (M_HEADS, M_HEADDIM))
    lnp = rows8(ln1_g[0], ln1_b[0], ln2_g[0], ln2_b[0])

    xp = x_prompt.reshape(batch * seq, D_MODEL)
    xs2d = x_sample.reshape(nb, D_MODEL)
    pr_p, z_p, xbc_p, dt_p, pr_s, z_s, xbc_s, dt_s = _proj(xp, xs2d, w_t, w_dt, DENSE_ROWS)

    o_r_p, wkv_p = _rwkv_prompt(pr_p.reshape(batch, seq, RWKV_PROJ), prep_w)
    o_r_p = o_r_p.reshape(batch * seq, R_WIDTH)
    o_m_p, ssm_p, wout_bf, w1_bf, w2_bf = _mamba_prompt(
        z_p, xbc_p, dt_p, batch, seq, (cw, dta, mv), (w_out[0], w_ff1[0], w_ff2[0]))
    shift_p = pr_p.reshape(batch, seq, RWKV_PROJ)[:, -1]
    conv_p = xbc_p.reshape(batch, seq, CONV_DIM)[:, seq - (CONV_K - 1):]

    cols = jnp.broadcast_to(jnp.stack([r_k[0].reshape(-1), lnx_w[0], lnx_b[0]])[:, :, None],
                            (3, R_WIDTH, nb))
    xs_s, xdt_s, dec_s, bm_s, cm_s, conv_t = _mamba_sample_prep(
        xbc_s, jnp.transpose(state_conv[0], (1, 0, 2)), dt_s, cw, dta)
    conv_s = jnp.transpose(conv_t, (1, 0, 2))
    o_r_t, wkv_t, y_m_s, ssm_s = _sample_steps(
        pr_s, state_shift[0], jnp.transpose(state_wkv[0], (1, 2, 3, 0)), prep_w, cols,
        xs_s, xdt_s, dec_s, bm_s, cm_s, state_ssm[0], dskip_rows)
    o_r_s = o_r_t.T
    wkv_s = jnp.transpose(wkv_t, (3, 0, 1, 2))

    y_p, y_s = _ffn((xp, o_r_p, o_m_p), (xs2d, o_r_s, y_m_s.reshape(nb, M_WIDTH), z_s), mv,
                    wout_bf, w1_bf, w2_bf, lnp, tm=DENSE_ROWS)

    return (y_p.reshape(batch, seq, D_MODEL), y_s.reshape(nb, 1, D_MODEL),
            shift_p[None], wkv_p[None], conv_p[None],
            ssm_p.reshape(1, batch, M_HEADS, M_HEADDIM, D_STATE),
            pr_s[None], wkv_s.reshape(1, nb, R_HEADS, R_HEAD, R_HEAD), conv_s[None],
            ssm_s.reshape(1, nb, M_HEADS, M_HEADDIM, D_STATE))
```

```python
import math

import numpy as np
import jax
import jax.numpy as jnp
from jax import lax
from jax.experimental import pallas as pl
from jax.experimental.pallas import tpu as pltpu

F32 = jnp.float32
BF16 = jnp.bfloat16

D_MODEL = 1024
R_WIDTH = 512
R_HEAD = 64
R_HEADS = 8
D_DECAY_LORA = 64
D_AAA_LORA = 64
D_GATE_LORA = 128
M_WIDTH = 512
M_HEADDIM = 64
M_HEADS = 8
M_GROUPS = 2
D_STATE = 128
CONV_K = 4
CONV_DIM = M_WIDTH + 2 * M_GROUPS * D_STATE
D_FF = 4 * D_MODEL
RWKV_PROJ = 3 * R_WIDTH + D_DECAY_LORA + D_AAA_LORA + D_GATE_LORA
DT_PAD = 128
PROJ_MAIN = RWKV_PROJ + M_WIDTH + CONV_DIM
DEPTH = 1
ALPHA = (2 * DEPTH) ** 0.25
LN_EPS = 1e-5
LNX_EPS = 64e-5
RMS_EPS = 1e-5

RWKV_CHUNK = 64
RWKV_TILE = 256
RWKV_ROWS = 4
RWKV_WAVES = 2
SSD_CHUNK = 128
MAMBA_TILE = 256
SUBLANES = 8
DENSE_ROWS = 512
VMEM_LIMIT = 48 * 1024 * 1024
PROJ_W_VMEM_LIMIT = 56 * 1024 * 1024

_NT = (((1,), (1,)), ((), ()))


def _mm(a, b):
    return jnp.dot(a.astype(BF16), b.astype(BF16), preferred_element_type=F32)


def _mm_nt(a, b):
    return lax.dot_general(a.astype(BF16), b.astype(BF16), _NT, preferred_element_type=F32)


def _split(x, passes):
    parts = []
    for _ in range(passes):
        p = x.astype(BF16)
        parts.append(p)
        x = x - p.astype(F32)
    return parts


def _mm_sel_r(x, sel, passes):
    return sum(jnp.dot(p, sel, preferred_element_type=F32) for p in _split(x, passes))


def _mm_sel_l(sel, x, passes):
    return sum(jnp.dot(sel, p, preferred_element_type=F32) for p in _split(x, passes))


def _softplus(x):
    return jnp.maximum(x, 0.0) + jnp.log(1.0 + jnp.exp(-jnp.abs(x)))


def _sigmoid(x):
    return 0.5 + 0.5 * jnp.tanh(0.5 * x)


def _silu(x):
    return x * _sigmoid(x)


def _layer_norm(x, g, b, eps):
    mu = jnp.mean(x, axis=-1, keepdims=True)
    d = x - mu
    var = jnp.mean(d * d, axis=-1, keepdims=True)
    return d * lax.rsqrt(var + eps) * g + b


def _iota2(shape, axis):
    return lax.broadcasted_iota(jnp.int32, shape, axis)


def _dt_columns(xb, wdt_t):
    pad = jnp.zeros((DT_PAD - M_HEADS, D_MODEL), F32)
    return _mm_nt(xb, jnp.concatenate([wdt_t, pad], axis=0))


W_SLAB = 256


def _proj_kernel(xp_ref, xs_ref, wt_ref, wdt_ref,
                 prp_ref, zp_ref, xbcp_ref, dtp_ref, prs_ref, zs_ref, xbcs_ref, dts_ref, wbf_scr):
    i = pl.program_id(0)
    last = pl.num_programs(0) - 1

    @pl.when(i == 0)
    def _():
        for c in range(0, PROJ_MAIN, W_SLAB):
            wbf_scr[:, c:c + W_SLAB] = wt_ref[c:c + W_SLAB, :].T.astype(BF16)

    def project(x_ref, pr_ref, z_ref, xbc_ref, dt_ref):
        xb = x_ref[...].astype(BF16)
        res = jnp.dot(xb, wbf_scr[...], preferred_element_type=F32)
        c0 = RWKV_PROJ
        c1 = c0 + M_WIDTH
        pr_ref[...] = res[:, :c0]
        z_ref[...] = res[:, c0:c1]
        xbc_ref[...] = res[:, c1:]
        dt_ref[...] = _dt_columns(xb, wdt_ref[...])

    @pl.when(i < last)
    def _():
        project(xp_ref, prp_ref, zp_ref, xbcp_ref, dtp_ref)

    @pl.when(i == last)
    def _():
        project(xs_ref, prs_ref, zs_ref, xbcs_ref, dts_ref)


def _proj(xp2d, xs2d, w_t, wdt_t, tm):
    n = xp2d.shape[0]
    ns = xs2d.shape[0]
    ntile = n // tm
    widths = (RWKV_PROJ, M_WIDTH, CONV_DIM, DT_PAD)
    once = lambda shape: pl.BlockSpec(shape, lambda i: (0, 0), pipeline_mode=pl.Buffered(1))
    tile = lambda w: pl.BlockSpec((tm, w), lambda i: (jnp.minimum(i, ntile - 1), 0))
    return pl.pallas_call(
        _proj_kernel,
        grid=(ntile + 1,),
        in_specs=[tile(D_MODEL), once(xs2d.shape), once(w_t.shape), once(wdt_t.shape)],
        out_specs=[tile(w) for w in widths] + [once((ns, w)) for w in widths],
        out_shape=[jax.ShapeDtypeStruct((n, w), F32) for w in widths]
                  + [jax.ShapeDtypeStruct((ns, w), F32) for w in widths],
        scratch_shapes=[pltpu.VMEM((D_MODEL, PROJ_MAIN), BF16)],
        compiler_params=pltpu.CompilerParams(dimension_semantics=("arbitrary",),
                                             vmem_limit_bytes=PROJ_W_VMEM_LIMIT),
        name="proj",
    )(xp2d, xs2d, w_t, wdt_t)


def _rwkv_prep(pr, prev, mu, rv, wup, aup, gup, bd):
    w0, a0, k_k, k_a = (rv[i:i + 1, :] for i in range(4))
    u = pr + (prev - pr) * mu
    r = u[:, 0:R_WIDTH]
    k = u[:, R_WIDTH:2 * R_WIDTH]
    v = u[:, 2 * R_WIDTH:3 * R_WIDTH]
    lora = u[:, 3 * R_WIDTH:3 * R_WIDTH + D_DECAY_LORA + D_AAA_LORA]
    ug = u[:, 3 * R_WIDTH + D_DECAY_LORA + D_AAA_LORA:]
    w_log = -_softplus(-(w0 + _mm(jnp.tanh(lora), wup))) - 0.5
    logw = -jnp.exp(w_log)
    a = _sigmoid(a0 + _mm(lora, aup))
    g = _mm(_sigmoid(ug), gup)
    kk = k * k_k
    ss = _mm_sel_r(kk * kk, bd, 1)
    kkn = kk * lax.rsqrt(jnp.maximum(ss, 1e-24))
    k2 = k * (1.0 + (a - 1.0) * k_a)
    return r, logw, k2, v, -kkn, kkn * a, g


def _rwkv_post(y, r, k2, v, g, rv, bd):
    r_k, lnx_w, lnx_b = (rv[i:i + 1, :] for i in range(4, 7))
    inv = 1.0 / R_HEAD
    mu = _mm_sel_r(y, bd, 1) * inv
    d = y - mu
    var = _mm_sel_r(d * d, bd, 1) * inv
    yn = d * lax.rsqrt(var + LNX_EPS) * lnx_w + lnx_b
    bonus = _mm_sel_r(r * k2 * r_k, bd, 1)
    return (yn + bonus * v) * g


def _rwkv_tile_kernel(pr_ref, prevblk_ref, mu_ref, rv_ref, wup_ref, aup_ref, gup_ref, bd_ref,
                      o_ref, sfin_ref, g_scr):
    c = RWKV_CHUNK
    tile = RWKV_TILE
    nseq = RWKV_ROWS
    nck = tile // c
    hw = 2 * R_HEAD
    npair = R_HEADS // 2
    ti = pl.program_id(1)

    @pl.when(ti == 0)
    def _():
        g_scr[...] = jnp.zeros_like(g_scr)

    pr = jnp.concatenate([pr_ref[s] for s in range(nseq)], axis=0)
    prev = pltpu.roll(pr, 1, axis=0)
    row_id = _iota2((nseq * tile, 1), 0)
    for s in range(nseq):
        prev_row = jnp.where(ti == 0, 0.0, prevblk_ref[s, SUBLANES - 1:SUBLANES, :])
        prev = jnp.where(row_id == s * tile, prev_row, prev)
    head_ones = bd_ref[...]
    r, logw, k2, v, av, bv, g = _rwkv_prep(pr, prev, mu_ref[...], rv_ref[...], wup_ref[...],
                                           aup_ref[...], gup_ref[...], head_ones)

    sh_c = int(math.log2(c))
    ti_i = _iota2((tile, tile), 0)
    ti_j = _iota2((tile, tile), 1)
    chunk_tril = jnp.where((ti_i >= ti_j) & ((ti_i >> sh_c) == (ti_j >> sh_c)), 1.0, 0.0).astype(BF16)
    cum = jnp.concatenate([_mm_sel_l(chunk_tril, logw[s * tile:(s + 1) * tile], 3) for s in range(nseq)],
                          axis=0)
    cl = jnp.concatenate(
        [jnp.broadcast_to(cum[(q + 1) * c - 1:(q + 1) * c, :], (c, R_WIDTH)) for q in range(nseq * nck)],
        axis=0)
    p_out = jnp.exp(-cum)
    p_end = jnp.exp(cl - cum)
    rt = r * jnp.exp(cum)
    at = av * jnp.exp(cum - logw)
    bt = bv * p_out
    kt = k2 * p_out
    bh = bv * p_end
    kh = k2 * p_end
    pc = jnp.exp(cl)

    ii = _iota2((c, hw), 0)
    jj = _iota2((c, hw), 1) & (c - 1)
    strict = ii > jj
    incl = ii >= jj
    lane = _iota2((1, hw), 1)
    keep = (jnp.where(lane < R_HEAD, 1.0, 0.0).astype(BF16), jnp.where(lane >= R_HEAD, 1.0, 0.0).astype(BF16))
    e_r = _iota2((hw, hw), 0)
    e_c = _iota2((hw, hw), 1)
    eye_hw = e_r == e_c
    blockdiag = (e_r >= R_HEAD) == (e_c >= R_HEAD)

    def bd(x):
        xb = x.astype(BF16)
        return jnp.concatenate([xb * keep[0], xb * keep[1]], axis=0)

    def pdot(a, b_bd):
        return jnp.dot(a.astype(BF16), b_bd, preferred_element_type=F32)

    blk = lambda x, q, p: x[q * c:(q + 1) * c, hw * p:hw * (p + 1)]
    zeros_c = jnp.zeros((c, hw), F32)
    eye = jnp.where(ii == jj, 1.0, 0.0)

    def phase1(qp):
        lhs = [jnp.concatenate([blk(at, *k), blk(rt, *k)], axis=0).astype(BF16) for k in qp]
        sc = [lax.dot_general(x, jnp.concatenate([bd(blk(bt, *k)), bd(blk(kt, *k))], axis=0), _NT,
                              preferred_element_type=F32) for x, k in zip(lhs, qp)]
        a_ab = [jnp.where(strict, s[:c, :hw], 0.0) for s in sc]
        a_rb = [jnp.where(incl, s[c:, :hw], 0.0) for s in sc]
        a_akrk = [jnp.concatenate([jnp.where(strict, s[:c, hw:], 0.0), jnp.where(incl, s[c:, hw:], 0.0)], axis=0)
                  for s in sc]
        v_bd = [bd(blk(v, *k)) for k in qp]
        akrk_v = [pdot(a, x) for a, x in zip(a_akrk, v_bd)]

        sh = 4
        npow = [jnp.where((ii >> sh) == (jj >> sh), n, 0.0) for n in a_ab]
        t_inv = [eye + n for n in npow]
        npow = [pdot(n, bd(n)) for n in npow]
        for _ in range(sh - 2):
            both = [pdot(jnp.concatenate([t, n], axis=0), bd(n)) for t, n in zip(t_inv, npow)]
            t_inv = [t + x[:c] for t, x in zip(t_inv, both)]
            npow = [x[c:] for x in both]
        t_inv = [t + pdot(t, bd(n)) for t, n in zip(t_inv, npow)]
        while (1 << sh) < c:
            off = ((ii >> (sh + 1)) == (jj >> (sh + 1))) & (((ii >> sh) & 1) > ((jj >> sh) & 1))
            tn = [pdot(t, bd(jnp.where(off, n, 0.0))) for t, n in zip(t_inv, a_ab)]
            t_inv = [t + pdot(x, bd(t)) for t, x in zip(t_inv, tn)]
            sh += 1

        tx = [pdot(t, jnp.concatenate([bd(blk(at, *k)), bd(x[:c])], axis=1))
              for t, x, k in zip(t_inv, akrk_v, qp)]
        rbx = [pdot(a, jnp.concatenate([bd(x[:, :hw]), bd(x[:, hw:])], axis=1)) for a, x in zip(a_rb, tx)]
        rp_pair = {k: blk(rt, *k) + x[:, :hw] for k, x in zip(qp, rbx)}
        y0_pair = {k: x[:, hw:] + y[c:] for k, x, y in zip(qp, rbx, akrk_v)}

        wt = {k: jnp.concatenate([blk(bh, *k), blk(kh, *k)], axis=0).T for k in qp}
        trans = {k: _mm(wt[k], jnp.concatenate([x, jnp.concatenate([zeros_c, blk(v, *k)], axis=1)], axis=0))
                 for k, x in zip(qp, tx)}
        m_ba = {k: jnp.where(blockdiag, trans[k][:, :hw], 0.0) for k in qp}
        h0c = {k: jnp.where(blockdiag, trans[k][:, hw:], 0.0) for k in qp}
        pc_col = {k: jnp.sum(jnp.where(eye_hw, blk(pc, *k)[:1], 0.0), axis=1, keepdims=True) for k in qp}
        return rp_pair, y0_pair, m_ba, h0c, pc_col

    rp_pair, y0_pair, m_ba, h0c, pc_col = {}, {}, {}, {}, {}
    all_qp = [(q, p) for q in range(nseq * nck) for p in range(npair)]
    wave = len(all_qp) // RWKV_WAVES
    for w0 in range(0, len(all_qp), wave):
        for dst, src in zip((rp_pair, y0_pair, m_ba, h0c, pc_col), phase1(all_qp[w0:w0 + wave])):
            dst.update(src)


    state = [g_scr[i] for i in range(nseq * npair)]
    y_rows = [None] * (nseq * nck)
    for q in range(nck):
        for s in range(nseq):
            qq = s * nck + q
            st = state[s * npair:(s + 1) * npair]
            on_st = [_mm(jnp.concatenate([rp_pair[qq, p], m_ba[qq, p]], axis=0), st[p]) for p in range(npair)]
            y_rows[qq] = jnp.concatenate([on_st[p][:c] + y0_pair[qq, p] for p in range(npair)], axis=1)
            state[s * npair:(s + 1) * npair] = [
                pc_col[qq, p] * st[p] + on_st[p][c:] + h0c[qq, p] for p in range(npair)]
    for i in range(nseq * npair):
        g_scr[i] = state[i]

    y = jnp.concatenate(y_rows, axis=0)
    out = _rwkv_post(y, r, k2, v, g, rv_ref[...], head_ones)
    for s in range(nseq):
        o_ref[s] = out[s * tile:(s + 1) * tile].astype(BF16)

    @pl.when(ti == pl.num_programs(1) - 1)
    def _():
        for s in range(nseq):
            for p in range(npair):
                gt = g_scr[s * npair + p].T
                sfin_ref[s, 2 * p] = gt[:R_HEAD, :R_HEAD]
                sfin_ref[s, 2 * p + 1] = pltpu.roll(gt, R_HEAD, axis=1)[R_HEAD:, :R_HEAD]


def _const_spec(shape):
    return pl.BlockSpec(shape, lambda *_: (0,) * len(shape))


def _rwkv_prompt(pr3d, wts):
    batch, seq, _ = pr3d.shape
    tile = RWKV_TILE
    nseq = RWKV_ROWS

    def prev_map(b, i):
        return (b, jnp.maximum(i * (tile // SUBLANES) - 1, 0), 0)

    w_specs = [_const_spec(w.shape) for w in wts]
    return pl.pallas_call(
        _rwkv_tile_kernel,
        grid=(batch // nseq, seq // tile),
        in_specs=[pl.BlockSpec((nseq, tile, RWKV_PROJ), lambda b, i: (b, i, 0)),
                  pl.BlockSpec((nseq, SUBLANES, RWKV_PROJ), prev_map)] + w_specs,
        out_specs=[pl.BlockSpec((nseq, tile, R_WIDTH), lambda b, i: (b, i, 0)),
                   pl.BlockSpec((nseq, R_HEADS, R_HEAD, R_HEAD), lambda b, i: (b, 0, 0, 0))],
        out_shape=[jax.ShapeDtypeStruct((batch, seq, R_WIDTH), BF16),
                   jax.ShapeDtypeStruct((batch, R_HEADS, R_HEAD, R_HEAD), F32)],
        scratch_shapes=[pltpu.VMEM((nseq * (R_HEADS // 2), 2 * R_HEAD, 2 * R_HEAD), F32)],
        compiler_params=pltpu.CompilerParams(dimension_semantics=("arbitrary", "arbitrary"),
                                             vmem_limit_bytes=VMEM_LIMIT),
        name="rwkv_prompt",
    )(pr3d, pr3d, *wts)


def _rwkv_sample_kernel(pr_ref, prev_ref, mu_ref, rv_ref, wup_ref, aup_ref, gup_ref, bd_ref,
                        cols_ref, s_ref, o_ref, so_ref, vec_scr):
    h = pl.program_id(0)

    @pl.when(h == 0)
    def _():
        r, logw, k2, v, av, bv, g = _rwkv_prep(pr_ref[...], prev_ref[...], mu_ref[...], rv_ref[...],
                                               wup_ref[...], aup_ref[...], gup_ref[...], bd_ref[...])
        for i, x in enumerate((r, jnp.exp(logw), k2, v, av, bv, g)):
            vec_scr[i] = x.T

    rows = pl.ds(pl.multiple_of(h * R_HEAD, R_HEAD), R_HEAD)
    r, w, k, v, a, b, g = (vec_scr[i, rows, :] for i in range(7))
    ys = []
    for vi in range(R_HEAD):
        s = s_ref[0, vi]
        sa = jnp.sum(s * a, axis=0, keepdims=True)
        s2 = s * w + sa * b + v[vi:vi + 1, :] * k
        so_ref[0, vi] = s2
        ys.append(jnp.sum(s2 * r, axis=0, keepdims=True))
    y = jnp.concatenate(ys, axis=0)
    mu = jnp.mean(y, axis=0, keepdims=True)
    d = y - mu
    var = jnp.mean(d * d, axis=0, keepdims=True)
    yn = d * lax.rsqrt(var + LNX_EPS) * cols_ref[1] + cols_ref[2]
    bonus = jnp.sum(r * k * cols_ref[0], axis=0, keepdims=True)
    o_ref[...] = (yn + bonus * v) * g


STEP_BATCH = 16


def _per_head_rows(x, n):
    return jnp.concatenate([jnp.broadcast_to(x[h:h + 1, :], (n, x.shape[1])) for h in range(x.shape[0])],
                           axis=0)


def _rows_from_columns(col_b, diag, n):
    picked = jnp.where(diag, col_b, 0.0)
    return jnp.concatenate([jnp.sum(picked[h * n:(h + 1) * n], axis=0, keepdims=True)
                            for h in range(col_b.shape[0] // n)], axis=0)


def _shift_rows(x, prev8, s):
    rolled = pltpu.roll(x, s, axis=0)
    top = jnp.where(_iota2((SUBLANES, 1), 0) < s, pltpu.roll(prev8, s, axis=0), rolled[:SUBLANES])
    return jnp.concatenate([top, rolled[SUBLANES:]], axis=0)


def _head_expand(lanes_per_head=M_HEADDIM):
    shape = (DT_PAD, M_HEADS * lanes_per_head)
    head_of_lane = _iota2(shape, 1) >> int(math.log2(lanes_per_head))
    return jnp.where(_iota2(shape, 0) == head_of_lane, 1.0, 0.0).astype(BF16)


def _gated_rmsnorm(y, z, gnorm_w):
    yz = y * _silu(z)
    gw = M_WIDTH // M_GROUPS
    outs = []
    for gi in range(M_GROUPS):
        yg = yz[:, gi * gw:(gi + 1) * gw]
        ms = jnp.mean(yg * yg, axis=-1, keepdims=True)
        outs.append(yg * lax.rsqrt(ms + RMS_EPS))
    return jnp.concatenate(outs, axis=1) * gnorm_w


def _mamba_tile_kernel(z_ref, xbc_ref, xprev_ref, dt_ref, cw_ref, dta_ref, mv_ref,
                       wout32_ref, w1_32_ref, w2_32_ref,
                       o_ref, hfin_ref, wout_ref, w1_ref, w2_ref, h_scr):
    tile = MAMBA_TILE
    q = SSD_CHUNK
    ti = pl.program_id(1)
    wout_ref[...] = wout32_ref[...].astype(BF16)
    w1_ref[...] = w1_32_ref[...].astype(BF16)
    w2_ref[...] = w2_32_ref[...].astype(BF16)

    @pl.when(ti == 0)
    def _():
        h_scr[...] = jnp.zeros_like(h_scr)

    xb = xbc_ref[...]
    prev8 = jnp.where(ti == 0, 0.0, xprev_ref[...])
    conv = cw_ref[CONV_K:CONV_K + 1, :] + xb * cw_ref[CONV_K - 1:CONV_K, :]
    for s in range(1, CONV_K):
        conv = conv + _shift_rows(xb, prev8, s) * cw_ref[CONV_K - 1 - s:CONV_K - s, :]
    xact = _silu(conv)
    xs = xact[:, :M_WIDTH]
    gs = M_GROUPS * D_STATE
    bm = xact[:, M_WIDTH:M_WIDTH + gs]
    cm = xact[:, M_WIDTH + gs:]

    dt = _softplus(dt_ref[...] + dta_ref[0:1, :])
    adt = dt * (-jnp.exp(dta_ref[1:2, :]))
    x_dt = xs * _mm_sel_r(dt, _head_expand(), 3)

    sh_q = int(math.log2(q))
    ti_i = _iota2((tile, tile), 0)
    ti_j = _iota2((tile, tile), 1)
    chunk_tril = jnp.where((ti_i >= ti_j) & ((ti_i >> sh_q) == (ti_j >> sh_q)), 1.0, 0.0).astype(BF16)
    cs = _mm_sel_l(chunk_tril, adt, 3)
    cs_t = cs.T
    incl = _iota2((q, q), 0) >= _iota2((q, q), 1)
    lane = _iota2((1, 2 * M_HEADDIM), 1)
    first = lane < M_HEADDIM
    hpg = M_HEADS // M_GROUPS

    state = [h_scr[p] for p in range(M_HEADS // 2)]
    y_rows = []
    for j in range(tile // q):
        rs = slice(j * q, (j + 1) * q)
        y_pairs = []
        for gi in range(M_GROUPS):
            bm_g = bm[rs, gi * D_STATE:(gi + 1) * D_STATE]
            cm_g = cm[rs, gi * D_STATE:(gi + 1) * D_STATE]
            cb = _mm_nt(cm_g, bm_g)
            bm_t = bm_g.T
            for pp in range(hpg // 2):
                p = gi * (hpg // 2) + pp
                x_p = x_dt[rs, 2 * M_HEADDIM * p:2 * M_HEADDIM * (p + 1)]
                y_diag = jnp.zeros((q, 2 * M_HEADDIM), F32)
                contrib = jnp.zeros((D_STATE, 2 * M_HEADDIM), F32)
                scale_in = []
                scale_end = []
                for e in range(2):
                    head = 2 * p + e
                    m = first if e == 0 else jnp.logical_not(first)
                    a_col = cs[rs, head:head + 1]
                    a_row = cs_t[head:head + 1, rs]
                    a_end = cs_t[head:head + 1, (j + 1) * q - 1:(j + 1) * q]
                    lmat = jnp.exp(jnp.where(incl, a_col - a_row, -jnp.inf))
                    x_m = jnp.where(m, x_p, 0.0)
                    y_diag = y_diag + _mm(cb * lmat, x_m)
                    contrib = contrib + _mm(bm_t * jnp.exp(a_end - a_row), x_m)
                    scale_in.append(jnp.exp(a_col))
                    scale_end.append(jnp.exp(a_end))
                y_off = _mm(cm_g, state[p]) * jnp.where(first, scale_in[0], scale_in[1])
                state[p] = state[p] * jnp.where(first, scale_end[0], scale_end[1]) + contrib
                y_pairs.append(y_diag + y_off)
        y_rows.append(jnp.concatenate(y_pairs, axis=1))
    for p in range(M_HEADS // 2):
        h_scr[p] = state[p]

    y = jnp.concatenate(y_rows, axis=0) + mv_ref[0:1, :] * xs
    o_ref[...] = _gated_rmsnorm(y, z_ref[...], mv_ref[1:2, :]).astype(BF16)

    @pl.when(ti == pl.num_programs(1) - 1)
    def _():
        for p in range(M_HEADS // 2):
            hfin_ref[0, 2 * M_HEADDIM * p:2 * M_HEADDIM * (p + 1), :] = h_scr[p].T


def _mamba_prompt(z2d, xbc2d, dt2d, batch, seq, wts, mlp_w):
    tile = MAMBA_TILE
    ntile = seq // tile
    rows8 = seq // SUBLANES
    nstep = batch * ntile
    slab = lambda a: pl.BlockSpec((a.shape[0] // nstep, a.shape[1]), lambda b, i: (b * ntile + i, 0))

    def prev_map(b, i):
        return (b * rows8 + jnp.maximum(i * (tile // SUBLANES) - 1, 0), 0)

    tok = lambda w: pl.BlockSpec((tile, w), lambda b, i: (b * ntile + i, 0))
    return pl.pallas_call(
        _mamba_tile_kernel,
        grid=(batch, ntile),
        in_specs=[tok(M_WIDTH), tok(CONV_DIM), pl.BlockSpec((SUBLANES, CONV_DIM), prev_map), tok(DT_PAD)]
                 + [_const_spec(w.shape) for w in wts] + [slab(w) for w in mlp_w],
        out_specs=[tok(M_WIDTH),
                   pl.BlockSpec((1, M_HEADS * M_HEADDIM, D_STATE), lambda b, i: (b, 0, 0))]
                  + [slab(w) for w in mlp_w],
        out_shape=[jax.ShapeDtypeStruct((batch * seq, M_WIDTH), BF16),
                   jax.ShapeDtypeStruct((batch, M_HEADS * M_HEADDIM, D_STATE), F32)]
                  + [jax.ShapeDtypeStruct(w.shape, BF16) for w in mlp_w],
        scratch_shapes=[pltpu.VMEM((M_HEADS // 2, D_STATE, 2 * M_HEADDIM), F32)],
        compiler_params=pltpu.CompilerParams(dimension_semantics=("arbitrary", "arbitrary"),
                                             vmem_limit_bytes=VMEM_LIMIT),
        name="mamba_prompt",
    )(z2d, xbc2d, xbc2d, dt2d, *wts, *mlp_w)


def _mamba_prep_kernel(xbc_ref, cprev_ref, dt_ref, cw_ref, dta_ref,
                       xs_ref, xdt_ref, dec_ref, bm_ref, cm_ref, cnew_ref):
    xbc = xbc_ref[...]
    conv = cw_ref[CONV_K:CONV_K + 1, :] + xbc * cw_ref[CONV_K - 1:CONV_K, :]
    for j in range(CONV_K - 1):
        conv = conv + cprev_ref[j] * cw_ref[j:j + 1, :]
        cnew_ref[j] = cprev_ref[j + 1] if j + 1 < CONV_K - 1 else xbc
    xact = _silu(conv)
    xs = xact[:, :M_WIDTH]
    gs = M_GROUPS * D_STATE
    dt = _softplus(dt_ref[...] + dta_ref[0:1, :])
    adt = dt * (-jnp.exp(dta_ref[1:2, :]))
    expand = _head_expand()
    xs_ref[...] = xs
    xdt_ref[...] = xs * _mm_sel_r(dt, expand, 3)
    dec_ref[...] = jnp.exp(_mm_sel_r(adt, _head_expand(D_STATE), 3))
    hpg = M_HEADS // M_GROUPS
    per_head = lambda m: jnp.concatenate(
        [m[:, (e // hpg) * D_STATE:(e // hpg + 1) * D_STATE] for e in range(M_HEADS)], axis=1)
    bm_ref[...] = per_head(xact[:, M_WIDTH:M_WIDTH + gs])
    cm_ref[...] = per_head(xact[:, M_WIDTH + gs:])


def _mamba_sample_prep(xbc, cprev_t, dt, cw, dta):
    n = xbc.shape[0]
    args = (xbc, cprev_t, dt, cw, dta)
    widths = (M_WIDTH, M_WIDTH, M_HEADS * D_STATE, M_HEADS * D_STATE, M_HEADS * D_STATE)
    out_shapes = [(n, w) for w in widths] + [cprev_t.shape]
    return pl.pallas_call(
        _mamba_prep_kernel,
        grid=(1,),
        in_specs=[_const_spec(a.shape) for a in args],
        out_specs=[_const_spec(s) for s in out_shapes],
        out_shape=[jax.ShapeDtypeStruct(s, F32) for s in out_shapes],
        compiler_params=pltpu.CompilerParams(vmem_limit_bytes=VMEM_LIMIT),
        name="mamba_sample_prep",
    )(*args)


def _mamba_step_kernel(xs_ref, xdt_ref, dec_ref, bm_ref, cm_ref, h_ref, dskip_ref, y_ref, ho_ref):
    n = M_HEADS * M_HEADDIM
    ones_in = jnp.ones((M_HEADDIM, D_STATE), BF16)
    ones_out = jnp.ones((D_STATE, M_HEADDIM), BF16)
    diag = (_iota2((n, M_HEADDIM), 0) & (M_HEADDIM - 1)) == _iota2((n, M_HEADDIM), 1)

    def head_rows(ref, bi):
        return jnp.concatenate(
            [jnp.broadcast_to(ref[bi:bi + 1, e * D_STATE:(e + 1) * D_STATE], (M_HEADDIM, D_STATE))
             for e in range(M_HEADS)], axis=0)

    for bi in range(STEP_BATCH):
        rows = slice(bi * M_HEADS, (bi + 1) * M_HEADS)
        hrows = slice(bi * n, (bi + 1) * n)
        x_col = _mm(jnp.where(diag, _per_head_rows(xdt_ref[rows, :], M_HEADDIM), 0.0), ones_in)
        h2 = h_ref[hrows, :] * head_rows(dec_ref, bi) + x_col * head_rows(bm_ref, bi)
        ho_ref[hrows, :] = h2
        y = _rows_from_columns(_mm(h2 * head_rows(cm_ref, bi), ones_out), diag, M_HEADDIM)
        y_ref[rows, :] = y + dskip_ref[...] * xs_ref[rows, :]


_N_RWKV_STEP_IN = 10
_N_MAMBA_STEP_IN = 7


def _sample_steps_kernel(*refs):
    a, b = _N_RWKV_STEP_IN, _N_RWKV_STEP_IN + _N_MAMBA_STEP_IN
    o_t_ref, so_ref, ym_ref, ho_ref, vec_scr = refs[b:]
    _rwkv_sample_kernel(*refs[:a], o_t_ref, so_ref, vec_scr)
    _mamba_step_kernel(*refs[a:b], ym_ref, ho_ref)


def _sample_steps(pr, prev, state_t, prep_w, cols, xs, xdt, dec, bm, cm, state_m, dskip_rows):
    nb = pr.shape[0]
    assert nb // STEP_BATCH == R_HEADS
    r_args = (pr, prev) + tuple(prep_w)
    head_rows = pl.BlockSpec((R_HEAD, nb), lambda i: (i, 0))
    rst_spec = pl.BlockSpec((1, R_HEAD, R_HEAD, nb), lambda i: (i, 0, 0, 0))
    rows = STEP_BATCH * M_HEADS
    vec_spec = pl.BlockSpec((rows, M_HEADDIM), lambda i: (i, 0))
    bc_spec = pl.BlockSpec((STEP_BATCH, M_HEADS * D_STATE), lambda i: (i, 0))
    mst_spec = pl.BlockSpec((rows * M_HEADDIM, D_STATE), lambda i: (i, 0))
    flat = lambda x, w: x.reshape(nb * M_HEADS, w)
    in_specs = ([_const_spec(a.shape) for a in r_args]
                + [pl.BlockSpec((3, R_HEAD, nb), lambda i: (0, i, 0)), rst_spec]
                + [vec_spec] * 2 + [bc_spec] * 3 + [mst_spec, _const_spec((M_HEADS, M_HEADDIM))])
    assert len(in_specs) == _N_RWKV_STEP_IN + _N_MAMBA_STEP_IN
    return pl.pallas_call(
        _sample_steps_kernel,
        grid=(R_HEADS,),
        in_specs=in_specs,
        out_specs=[head_rows, rst_spec, vec_spec, mst_spec],
        out_shape=[jax.ShapeDtypeStruct((R_WIDTH, nb), F32),
                   jax.ShapeDtypeStruct(state_t.shape, F32),
                   jax.ShapeDtypeStruct((nb * M_HEADS, M_HEADDIM), F32),
                   jax.ShapeDtypeStruct((nb * M_HEADS * M_HEADDIM, D_STATE), F32)],
        scratch_shapes=[pltpu.VMEM((7, R_WIDTH, nb), F32)],
        compiler_params=pltpu.CompilerParams(dimension_semantics=("arbitrary",),
                                             vmem_limit_bytes=VMEM_LIMIT),
        name="sample_steps",
    )(*r_args, cols, state_t, flat(xs, M_HEADDIM), flat(xdt, M_HEADDIM), dec, bm, cm,
      state_m.reshape(nb * M_HEADS * M_HEADDIM, D_STATE), dskip_rows)


FF_SLAB = 2048


def _ffn_math(x, o_r, o_m, wout_ref, lnp_ref, w1_ref, w2_ref):
    mix = _mm(o_r, wout_ref[:R_WIDTH, :]) + _mm(o_m, wout_ref[R_WIDTH:, :])
    h = _layer_norm(ALPHA * x + mix, lnp_ref[0:1, :], lnp_ref[1:2, :], LN_EPS)
    hb = h.astype(BF16)
    ff = jnp.zeros_like(h)
    for j in range(D_FF // FF_SLAB):
        t = jnp.dot(hb, w1_ref[:, j * FF_SLAB:(j + 1) * FF_SLAB], preferred_element_type=F32)
        t = jnp.square(jnp.maximum(t, 0.0))
        ff = ff + jnp.dot(t.astype(BF16), w2_ref[j * FF_SLAB:(j + 1) * FF_SLAB, :],
                          preferred_element_type=F32)
    return _layer_norm(ALPHA * h + ff, lnp_ref[2:3, :], lnp_ref[3:4, :], LN_EPS)


def _ffn_kernel(xp_ref, orp_ref, omp_ref, xs_ref, ors_ref, yms_ref, zs_ref, mv_ref,
                wout_ref, w1_ref, w2_ref, lnp_ref, yp_ref, ys_ref):
    i = pl.program_id(0)
    last = pl.num_programs(0) - 1

    @pl.when(i < last)
    def _():
        yp_ref[...] = _ffn_math(xp_ref[...], orp_ref[...], omp_ref[...], wout_ref, lnp_ref, w1_ref, w2_ref)

    @pl.when(i == last)
    def _():
        o_m = _gated_rmsnorm(yms_ref[...], zs_ref[...], mv_ref[1:2, :])
        ys_ref[...] = _ffn_math(xs_ref[...], ors_ref[...], o_m, wout_ref, lnp_ref, w1_ref, w2_ref)


def _ffn(prompt, sample, mv, wout, w1, w2, lnp, tm):
    n = prompt[0].shape[0]
    ns = sample[0].shape[0]
    ntile = n // tm
    tok = lambda w: pl.BlockSpec((tm, w), lambda i: (jnp.minimum(i, ntile - 1), 0))
    whole = lambda shape: pl.BlockSpec(shape, lambda i: (0, 0), pipeline_mode=pl.Buffered(1))
    consts = list(sample) + [mv, wout, w1, w2, lnp]
    return pl.pallas_call(
        _ffn_kernel,
        grid=(ntile + 1,),
        in_specs=[tok(D_MODEL), tok(R_WIDTH), tok(M_WIDTH)] + [whole(a.shape) for a in consts],
        out_specs=[tok(D_MODEL), whole((ns, D_MODEL))],
        out_shape=[jax.ShapeDtypeStruct((n, D_MODEL), F32), jax.ShapeDtypeStruct((ns, D_MODEL), F32)],
        compiler_params=pltpu.CompilerParams(dimension_semantics=("arbitrary",),
                                             vmem_limit_bytes=VMEM_LIMIT),
        name="ffn",
    )(*prompt, *consts)


def _block_diag_ones():
    idx = np.arange(R_WIDTH) // R_HEAD
    return jnp.asarray(idx[:, None] == idx[None, :], dtype=BF16)


def kernel(x_prompt, x_sample, state_shift, state_wkv, state_conv, state_ssm, w_in, mu_shift, w0, w_up, a0, a_up, g_up, k_k, k_a, r_k, lnx_w, lnx_b, conv_w, conv_b, dt_bias, a_log, d_skip, gnorm_w, w_out, ln1_g, ln1_b, w_ff1, w_ff2, ln2_g, ln2_b):
    assert w_in.shape[0] == DEPTH and x_sample.shape[1] == 1
    batch, seq, _ = x_prompt.shape
    nb = x_sample.shape[0]
    assert seq % MAMBA_TILE == 0 and seq % RWKV_TILE == 0 and seq % DENSE_ROWS == 0
    assert nb % STEP_BATCH == 0 and batch % RWKV_ROWS == 0
    row = lambda a: a.reshape(1, -1)

    w_t = jnp.transpose(w_in[0])
    w_dt = w_t[PROJ_MAIN:]
    zl = jnp.zeros((D_DECAY_LORA, R_WIDTH), F32)
    wup_p = jnp.concatenate([w_up[0], zl], axis=0).astype(BF16)
    aup_p = jnp.concatenate([zl, a_up[0]], axis=0).astype(BF16)
    bd = _block_diag_ones()

    def rows8(*vecs):
        m = jnp.stack([v.reshape(-1) for v in vecs])
        return jnp.pad(m, ((0, SUBLANES - len(vecs)), (0, 0)))

    rv = rows8(w0[0], a0[0], k_k[0], k_a[0], r_k[0], lnx_w[0], lnx_b[0])
    prep_w = (row(mu_shift[0]), rv, wup_p, aup_p, g_up[0].astype(BF16), bd)
    cw = rows8(*conv_w[0], conv_b[0])
    dta = rows8(jnp.pad(dt_bias[0], (0, DT_PAD - M_HEADS)), jnp.pad(a_log[0], (0, DT_PAD - M_HEADS)))
    mv = rows8(jnp.repeat(d_skip[0], M_HEADDIM), gnorm_w[0])
    dskip_rows = jnp.broadcast_to(d_skip[0][:, None], (M_HEADS, M_HEADDIM))
    lnp = rows8(ln1_g[0], ln1_b[0], ln2_g[0], ln2_b[0])

    xp = x_prompt.reshape(batch * seq, D_MODEL)
    xs2d = x_sample.reshape(nb, D_MODEL)
    pr_p, z_p, xbc_p, dt_p, pr_s, z_s, xbc_s, dt_s = _proj(xp, xs2d, w_t, w_dt, DENSE_ROWS)

    o_r_p, wkv_p = _rwkv_prompt(pr_p.reshape(batch, seq, RWKV_PROJ), prep_w)
    o_r_p = o_r_p.reshape(batch * seq, R_WIDTH)
    o_m_p, ssm_p, wout_bf, w1_bf, w2_bf = _mamba_prompt(
        z_p, xbc_p, dt_p, batch, seq, (cw, dta, mv), (w_out[0], w_ff1[0], w_ff2[0]))
    shift_p = pr_p.reshape(batch, seq, RWKV_PROJ)[:, -1]
    conv_p = xbc_p.reshape(batch, seq, CONV_DIM)[:, seq - (CONV_K - 1):]

    cols = jnp.broadcast_to(jnp.stack([r_k[0].reshape(-1), lnx_w[0], lnx_b[0]])[:, :, None],
                            (3, R_WIDTH, nb))
    xs_s, xdt_s, dec_s, bm_s, cm_s, conv_t = _mamba_sample_prep(
        xbc_s, jnp.transpose(state_conv[0], (1, 0, 2)), dt_s, cw, dta)
    conv_s = jnp.transpose(conv_t, (1, 0, 2))
    o_r_t, wkv_t, y_m_s, ssm_s = _sample_steps(
        pr_s, state_shift[0], jnp.transpose(state_wkv[0], (1, 2, 3, 0)), prep_w, cols,
        xs_s, xdt_s, dec_s, bm_s, cm_s, state_ssm[0], dskip_rows)
    o_r_s = o_r_t.T
    wkv_s = jnp.transpose(wkv_t, (3, 0, 1, 2))

    y_p, y_s = _ffn((xp, o_r_p, o_m_p), (xs2d, o_r_s, y_m_s.reshape(nb, M_WIDTH), z_s), mv,
                    wout_bf, w1_bf, w2_bf, lnp, tm=DENSE_ROWS)

    return (y_p.reshape(batch, seq, D_MODEL), y_s.reshape(nb, 1, D_MODEL),
            shift_p[None], wkv_p[None], conv_p[None],
            ssm_p.reshape(1, batch, M_HEADS, M_HEADDIM, D_STATE),
            pr_s[None], wkv_s.reshape(1, nb, R_HEADS, R_HEAD, R_HEAD), conv_s[None],
            ssm_s.reshape(1, nb, M_HEADS, M_HEADDIM, D_STATE))
```
